```python
import math
import jax, jax.numpy as jnp
from jax import lax
import numpy as np

D_MODEL = 1024
BATCH = 8
SEQ = 2048
DEPTH = 1
DEC_BATCH = 128
DEC_SEQ = 1
PAST_LEN = 16384
PAGE_SIZE = 128

RET_HEADS = 8
RET_DV = D_MODEL // RET_HEADS
RET_DK = RET_DV
RET_WIDTH = RET_HEADS * RET_DV
RET_CHUNK = 128
ROPE_BASE = 10000.0
RET_GN_EPS = 1e-5
RWKV_HEAD = 64
RWKV_HEADS = D_MODEL // RWKV_HEAD
RWKV_WIDTH = RWKV_HEADS * RWKV_HEAD
LORA_DECAY = max(32, int(round(D_MODEL ** 0.5 * 1.8 / 32)) * 32)
LORA_A = max(32, int(round(D_MODEL ** 0.5 * 1.8 / 32)) * 32)
LORA_G = max(32, int(round(D_MODEL ** 0.8 * 0.6 / 32)) * 32)
RWKV_GN_EPS = 64e-5
OFF_Q = 0
OFF_K = OFF_Q + RET_HEADS * RET_DK
OFF_V = OFF_K + RET_HEADS * RET_DK
OFF_GSW = OFF_V + RET_WIDTH
OFF_SHIFT = OFF_GSW + RET_WIDTH
N_SHIFT = 3 * RWKV_WIDTH + LORA_DECAY + LORA_A + LORA_G
OFF_GATE_A = OFF_SHIFT + N_SHIFT
OFF_GATE_B = OFF_GATE_A + D_MODEL
N_IN = OFF_GATE_B + D_MODEL
SHIFT_SPLITS = (RWKV_WIDTH, 2 * RWKV_WIDTH, 3 * RWKV_WIDTH,
                3 * RWKV_WIDTH + LORA_DECAY, 3 * RWKV_WIDTH + LORA_DECAY + LORA_A)
N_GROUPS = 4
EXPERTS_PER_GROUP = 8
N_EXPERTS = N_GROUPS * EXPERTS_PER_GROUP
TOP_K = 2
D_EXPERT = 256
RMS_EPS = 1e-6

kernel_name = 'retnet_rwkv7_gated_hier_moe_step'

F32 = jnp.float32


def _rmsnorm(x, g):
    xf = x.astype(F32)
    y = xf * lax.rsqrt(jnp.mean(xf * xf, axis=-1, keepdims=True) + RMS_EPS)
    return (y * g.astype(F32)).astype(x.dtype)


def _head_norm(x, eps):
    mu = jnp.mean(x, axis=-1, keepdims=True)
    xc = x - mu
    return xc * lax.rsqrt(jnp.mean(xc * xc, axis=-1, keepdims=True) + eps)


def _rope(x, pos):
    half = x.shape[-1] // 2
    inv = ROPE_BASE ** (-jnp.arange(half, dtype=F32) / half)
    ang = pos[:, None] * inv[None, :]
    cos = jnp.cos(ang)[None, :, None, :]
    sin = jnp.sin(ang)[None, :, None, :]
    x1, x2 = x[..., :half], x[..., half:]
    return jnp.concatenate([x1 * cos - x2 * sin, x1 * sin + x2 * cos], axis=-1)


def _retention(q, k, v, s0):
    B, L, H, DK = q.shape
    DV = v.shape[-1]
    C = math.gcd(L, RET_CHUNK)
    n = L // C
    log_gamma = jnp.log1p(-jnp.exp2(-5.0 - jnp.arange(H, dtype=F32)))
    idx = jnp.arange(C, dtype=F32)
    rel = idx[:, None] - idx[None, :]
    intra = jnp.where(rel[None] >= 0, jnp.exp(jnp.maximum(rel, 0.0)[None] * log_gamma[:, None, None]), 0.0)
    q_dec = jnp.exp((idx[:, None] + 1.0) * log_gamma[None, :])
    k_dec = jnp.exp((C - 1.0 - idx)[None, :] * log_gamma[:, None])
    chunk_dec = jnp.exp(C * log_gamma)

    def blocks(t):
        return t.reshape(B, n, C, H, t.shape[-1]).transpose(1, 0, 2, 3, 4)

    def step(S, inp):
        qc, kc, vc = inp
        scores = jnp.einsum('bihd,bjhd->bhij', qc, kc) * intra[None]
        inner = jnp.einsum('bhij,bjhe->bihe', scores, vc)
        cross = jnp.einsum('bihd,bhde->bihe', qc, S) * q_dec[None, :, :, None]
        S_new = S * chunk_dec[None, :, None, None] + jnp.einsum('bjhd,bjhe,hj->bhde', kc, vc, k_dec)
        return S_new, inner + cross

    S, out = lax.scan(step, s0, (blocks(q), blocks(k), blocks(v)))
    return out.transpose(1, 0, 2, 3, 4).reshape(B, L, H, DV), S


def _rwkv7_recurrence(r, w, k, v, a, b, s0):
    def step(S, inp):
        r_t, w_t, k_t, v_t, a_t, b_t = inp
        sa = jnp.einsum('bhij,bhj->bhi', S, a_t)
        S = S * w_t[:, :, None, :] + sa[..., None] * b_t[:, :, None, :] + v_t[..., None] * k_t[:, :, None, :]
        return S, jnp.einsum('bhij,bhj->bhi', S, r_t)

    seq = (r.transpose(1, 0, 2, 3), w.transpose(1, 0, 2, 3), k.transpose(1, 0, 2, 3),
           v.transpose(1, 0, 2, 3), a.transpose(1, 0, 2, 3), b.transpose(1, 0, 2, 3))
    S, y = lax.scan(step, s0, seq)
    return y.transpose(1, 0, 2, 3), S


def _mixer(xn, prev_row, pos, s_ret, s_rwkv, p):
    B, L, _ = xn.shape
    proj = jnp.concatenate([prev_row[:, None, :].astype(xn.dtype), xn], axis=1) @ p['w_in']
    cur, prev = proj[:, 1:], proj[:, :-1]

    def heads(t, h):
        return t.reshape(B, L, h, -1).astype(F32)

    q = _rope(heads(cur[..., OFF_Q:OFF_K], RET_HEADS), pos)
    k = _rope(heads(cur[..., OFF_K:OFF_V], RET_HEADS), pos) * (RET_DK ** -0.5)
    v = heads(cur[..., OFF_V:OFF_GSW], RET_HEADS)
    o_ret, s_ret_new = _retention(q, k, v, s_ret.astype(F32))
    o_ret = (_head_norm(o_ret, RET_GN_EPS).reshape(B, L, RET_WIDTH) * p['ret_gn_w'].astype(F32)
             * jax.nn.silu(cur[..., OFF_GSW:OFF_SHIFT].astype(F32)))

    sh = cur[..., OFF_SHIFT:OFF_GATE_A].astype(F32)
    sh = sh + (prev[..., OFF_SHIFT:OFF_GATE_A].astype(F32) - sh) * p['mu_shift'].astype(F32)
    r_, k_, v_, w_lo, a_lo, g_lo = jnp.split(sh, SHIFT_SPLITS, axis=-1)
    w_log = -jax.nn.softplus(-(p['rwkv_w0'].astype(F32) + jnp.tanh(w_lo) @ p['rwkv_w2'].astype(F32))) - 0.5
    decay = jnp.exp(-jnp.exp(w_log))
    a = jax.nn.sigmoid(p['rwkv_a0'].astype(F32) + a_lo @ p['rwkv_a2'].astype(F32))
    g = jax.nn.sigmoid(g_lo) @ p['rwkv_g2'].astype(F32)
    kk = heads(k_ * p['rwkv_k_k'].astype(F32), RWKV_HEADS)
    kk = kk / jnp.maximum(jnp.sqrt(jnp.sum(kk * kk, axis=-1, keepdims=True)), 1e-12)
    kmod = k_ * (1.0 + (a - 1.0) * p['rwkv_k_a'].astype(F32))
    rh = heads(r_, RWKV_HEADS)
    kh = heads(kmod, RWKV_HEADS)
    vh = heads(v_, RWKV_HEADS)
    ah = heads(a, RWKV_HEADS)
    y, s_rwkv_new = _rwkv7_recurrence(rh, heads(decay, RWKV_HEADS), kh, vh, -kk, kk * ah, s_rwkv.astype(F32))
    y = (_head_norm(y, RWKV_GN_EPS).reshape(B, L, RWKV_WIDTH) * p['rwkv_lnx_w'].astype(F32)
         + p['rwkv_lnx_b'].astype(F32))
    bonus = (jnp.sum(rh * kh * p['rwkv_r_k'].astype(F32), axis=-1, keepdims=True) * vh).reshape(B, L, RWKV_WIDTH)
    o_rwkv = (y + bonus) * g

    gate_a = jax.nn.sigmoid(cur[..., OFF_GATE_A:OFF_GATE_B].astype(F32))
    gate_b = jax.nn.sigmoid(cur[..., OFF_GATE_B:N_IN].astype(F32))
    merged = (gate_a * o_ret + gate_b * o_rwkv).astype(xn.dtype)
    return merged @ p['w_out'], s_ret_new.astype(s_ret.dtype), s_rwkv_new.astype(s_rwkv.dtype)


def _hier_moe(x, p):
    B, L, D = x.shape
    T = B * L
    xf = x.reshape(T, D)
    g_logit = (xf @ p['router_group_w'] + p['router_group_b']).astype(F32)
    g_prob = jax.nn.softmax(g_logit, axis=-1)
    g_sel = jnp.argmax(g_logit, axis=-1)
    g_w = jnp.take_along_axis(g_prob, g_sel[:, None], axis=-1)
    e_logit = (xf @ p['router_expert_w'] + p['router_expert_b']).astype(F32).reshape(T, N_GROUPS, EXPERTS_PER_GROUP)
    e_logit = jnp.take_along_axis(e_logit, g_sel[:, None, None], axis=1)[:, 0]
    e_prob = jax.nn.softmax(e_logit, axis=-1)
    top_p, top_i = lax.top_k(e_prob, TOP_K)
    top_w = g_w * top_p / jnp.sum(top_p, axis=-1, keepdims=True)
    expert_id = g_sel[:, None] * EXPERTS_PER_GROUP + top_i
    combine = jnp.sum(jax.nn.one_hot(expert_id, N_EXPERTS, dtype=F32) * top_w[..., None], axis=1)
    h = (jax.nn.silu(jnp.einsum('td,edf->tef', xf, p['expert_w_gate']))
         * jnp.einsum('td,edf->tef', xf, p['expert_w_up']))
    h = h * combine[..., None].astype(h.dtype)
    y = jnp.einsum('tef,efd->td', h, p['expert_w_down'])
    return y.reshape(B, L, D).astype(x.dtype)


def _trunk(x, pos, s_ret, s_rwkv, s_shift, layers, norm_final):
    new_ret, new_rwkv, new_shift = [], [], []
    for l in range(DEPTH):
        p = layers[l]
        xn = _rmsnorm(x, p['norm_mix'])
        m, sr, sw = _mixer(xn, s_shift[l], pos, s_ret[l], s_rwkv[l], p)
        x = x + m.astype(x.dtype)
        x = x + _hier_moe(_rmsnorm(x, p['norm_ffn']), p)
        new_ret.append(sr)
        new_rwkv.append(sw)
        new_shift.append(xn[:, -1].astype(s_shift.dtype))
    return _rmsnorm(x, norm_final), jnp.stack(new_ret), jnp.stack(new_rwkv), jnp.stack(new_shift)


def setup_inputs(seed: int = 0) -> dict:
    key = jax.random.key(seed)
    ks = iter(jax.random.split(key, 32))
    L = DEPTH

    def nrm(shape, s):
        return jax.random.normal(next(ks), shape, F32) * s

    def unif(shape, lo, hi):
        return jax.random.uniform(next(ks), shape, F32, lo, hi)

    return {
        'x_prompt': nrm((BATCH, SEQ, D_MODEL), 1.0),
        'x_sample': nrm((DEC_BATCH, DEC_SEQ, D_MODEL), 1.0),
        'state_ret': nrm((L, DEC_BATCH, RET_HEADS, RET_DK, RET_DV), 0.5),
        'state_rwkv': nrm((L, DEC_BATCH, RWKV_HEADS, RWKV_HEAD, RWKV_HEAD), 0.1),
        'state_shift': nrm((L, DEC_BATCH, D_MODEL), 1.0),
        'norm_mix': 1.0 + nrm((L, D_MODEL), 0.1),
        'w_in': nrm((L, D_MODEL, N_IN), D_MODEL ** -0.5),
        'mu_shift': unif((L, N_SHIFT), 0.0, 1.0),
        'ret_gn_w': 1.0 + nrm((L, D_MODEL), 0.1),
        'rwkv_w0': unif((L, D_MODEL), -6.0, -1.0),
        'rwkv_w2': nrm((L, LORA_DECAY, D_MODEL), 0.1),
        'rwkv_a0': nrm((L, D_MODEL), 0.1),
        'rwkv_a2': nrm((L, LORA_A, D_MODEL), 0.1),
        'rwkv_g2': nrm((L, LORA_G, D_MODEL), LORA_G ** -0.5),
        'rwkv_k_k': 0.85 + nrm((L, D_MODEL), 0.05),
        'rwkv_k_a': 1.0 + nrm((L, D_MODEL), 0.05),
        'rwkv_r_k': nrm((L, RWKV_HEADS, RWKV_HEAD), 0.1),
        'rwkv_lnx_w': 1.0 + nrm((L, D_MODEL), 0.1),
        'rwkv_lnx_b': nrm((L, D_MODEL), 0.02),
        'w_out': nrm((L, D_MODEL, D_MODEL), D_MODEL ** -0.5),
        'norm_ffn': 1.0 + nrm((L, D_MODEL), 0.1),
        'router_group_w': nrm((L, D_MODEL, N_GROUPS), D_MODEL ** -0.5),
        'router_group_b': nrm((L, N_GROUPS), 0.01),
        'router_expert_w': nrm((L, D_MODEL, N_EXPERTS), D_MODEL ** -0.5),
        'router_expert_b': nrm((L, N_EXPERTS), 0.01),
        'expert_w_gate': nrm((L, N_EXPERTS, D_MODEL, D_EXPERT), D_MODEL ** -0.5),
        'expert_w_up': nrm((L, N_EXPERTS, D_MODEL, D_EXPERT), D_MODEL ** -0.5),
        'expert_w_down': nrm((L, N_EXPERTS, D_EXPERT, D_MODEL), D_EXPERT ** -0.5),
        'norm_final': 1.0 + nrm((D_MODEL,), 0.1),
    }


def reference(x_prompt, x_sample, state_ret, state_rwkv, state_shift,
              norm_mix, w_in, mu_shift, ret_gn_w,
              rwkv_w0, rwkv_w2, rwkv_a0, rwkv_a2, rwkv_g2, rwkv_k_k, rwkv_k_a, rwkv_r_k,
              rwkv_lnx_w, rwkv_lnx_b, w_out, norm_ffn,
              router_group_w, router_group_b, router_expert_w, router_expert_b,
              expert_w_gate, expert_w_up, expert_w_down, norm_final):
    layers = [dict(norm_mix=norm_mix[l], w_in=w_in[l], mu_shift=mu_shift[l], ret_gn_w=ret_gn_w[l],
                   rwkv_w0=rwkv_w0[l], rwkv_w2=rwkv_w2[l], rwkv_a0=rwkv_a0[l], rwkv_a2=rwkv_a2[l],
                   rwkv_g2=rwkv_g2[l], rwkv_k_k=rwkv_k_k[l], rwkv_k_a=rwkv_k_a[l], rwkv_r_k=rwkv_r_k[l],
                   rwkv_lnx_w=rwkv_lnx_w[l], rwkv_lnx_b=rwkv_lnx_b[l], w_out=w_out[l], norm_ffn=norm_ffn[l],
                   router_group_w=router_group_w[l], router_group_b=router_group_b[l],
                   router_expert_w=router_expert_w[l], router_expert_b=router_expert_b[l],
                   expert_w_gate=expert_w_gate[l], expert_w_up=expert_w_up[l],
                   expert_w_down=expert_w_down[l])
              for l in range(DEPTH)]
    Bp, Lp, _ = x_prompt.shape
    pos_p = jnp.arange(Lp, dtype=F32)
    ret0 = jnp.zeros((DEPTH, Bp, RET_HEADS, RET_DK, RET_DV), state_ret.dtype)
    rwkv0 = jnp.zeros((DEPTH, Bp, RWKV_HEADS, RWKV_HEAD, RWKV_HEAD), state_rwkv.dtype)
    shift0 = jnp.zeros((DEPTH, Bp, D_MODEL), state_shift.dtype)
    y_prompt, ret_p, rwkv_p, shift_p = _trunk(x_prompt, pos_p, ret0, rwkv0, shift0, layers, norm_final)
    pos_s = PAST_LEN + jnp.arange(x_sample.shape[1], dtype=F32)
    y_sample, ret_s, rwkv_s, shift_s = _trunk(x_sample, pos_s, state_ret, state_rwkv, state_shift, layers, norm_final)
    return (y_prompt, y_sample, ret_p, rwkv_p, shift_p, ret_s, rwkv_s, shift_s)
```

```python
import functools
import math

import jax
import jax.numpy as jnp
from jax import lax
from jax.experimental import pallas as pl
from jax.experimental.pallas import tpu as pltpu

F32 = jnp.float32
BF16 = jnp.bfloat16

D_MODEL = 1024
LANES = 128
PAST_LEN = 16384
RET_HEADS = 8
RET_D = D_MODEL // RET_HEADS
RET_CHUNK = 128
ROPE_BASE = 10000.0
RET_GN_EPS = 1e-5
RWKV_HEAD = 64
RWKV_HEADS = D_MODEL // RWKV_HEAD
RWKV_PAIRS = RWKV_HEADS // 2
RWKV_CHUNK = 64
LORA_DECAY = 64
LORA_A = 64
LORA_G = 160
RWKV_GN_EPS = 64e-5
N_GROUPS = 4
EXPERTS_PER_GROUP = 8
N_EXPERTS = N_GROUPS * EXPERTS_PER_GROUP
D_EXPERT = 256
RMS_EPS = 1e-6

SRC_SHIFT = 4 * D_MODEL
SRC_WLO = SRC_SHIFT + 3 * D_MODEL
SRC_ALO = SRC_WLO + LORA_DECAY
SRC_GLO = SRC_ALO + LORA_A
SRC_GATE_A = SRC_GLO + LORA_G
COL_Q, COL_K, COL_V, COL_GSW = 0, 1024, 2048, 3072
COL_R, COL_KK, COL_VV = 4096, 5120, 6144
COL_GATE_A, COL_GATE_B = 7168, 8192
COL_LORA = 9216
LORA_W = 512
N_PROJ = COL_LORA + LORA_W

VMEM_LIMIT = 48 * 1024 * 1024

_NN = (((1,), (0,)), ((), ()))
_NT = (((1,), (1,)), ((), ()))
_TN = (((0,), (0,)), ((), ()))


def _mm(a, b, dims=_NN, exact=False):
    if exact:
        return lax.dot_general(a.astype(F32), b.astype(F32), dims,
                               precision=lax.Precision.HIGHEST,
                               preferred_element_type=F32)
    return lax.dot_general(a.astype(BF16), b.astype(BF16), dims,
                           preferred_element_type=F32)


def _params(sem):
    return pltpu.CompilerParams(dimension_semantics=sem,
                                vmem_limit_bytes=VMEM_LIMIT)


def _rms(x, gain):
    return x * lax.rsqrt(jnp.mean(x * x, axis=-1, keepdims=True) + RMS_EPS) * gain


def _inproj_body(x_ref, g_ref, w_ref, o_ref, xn_ref, xb_ref, *, apply_norm):
    @pl.when(pl.program_id(1) == 0)
    def _():
        x = x_ref[...]
        if apply_norm:
            x = _rms(x, g_ref[...])
        xn_ref[...] = x
        xb_ref[...] = x.astype(BF16)

    o_ref[...] = jnp.dot(xb_ref[...], w_ref[...], preferred_element_type=F32)


def _inproj(x, gain, w, *, apply_norm):
    t = x.shape[0]
    n = w.shape[1]
    tm = math.gcd(t, 512)
    tn = n // 4
    return pl.pallas_call(
        functools.partial(_inproj_body, apply_norm=apply_norm),
        grid=(t // tm, n // tn),
        in_specs=[pl.BlockSpec((tm, D_MODEL), lambda i, j: (i, 0)),
                  pl.BlockSpec((1, D_MODEL), lambda i, j: (0, 0)),
                  pl.BlockSpec((D_MODEL, tn), lambda i, j: (0, j))],
        out_specs=[pl.BlockSpec((tm, tn), lambda i, j: (i, j)),
                   pl.BlockSpec((tm, D_MODEL), lambda i, j: (i, 0))],
        out_shape=[jax.ShapeDtypeStruct((t, n), F32),
                   jax.ShapeDtypeStruct((t, D_MODEL), F32)],
        scratch_shapes=[pltpu.VMEM((tm, D_MODEL), BF16)],
        compiler_params=_params(("parallel", "arbitrary")),
        name="inproj",
    )(x, gain, w)


def _head_norm(x, eps):
    mu = jnp.mean(x, axis=-1, keepdims=True)
    xc = x - mu
    return xc * lax.rsqrt(jnp.mean(xc * xc, axis=-1, keepdims=True) + eps)


def _silu(x):
    return x * jax.nn.sigmoid(x)


def _rope(x, cos, sin_signed):
    return x * cos + pltpu.roll(x, RET_D // 2, 1) * sin_signed


def _ret_tables(chunk):
    log_gamma = jnp.log1p(-jnp.exp2(-5.0 - jnp.arange(RET_HEADS, dtype=F32)))
    idx = jnp.arange(chunk, dtype=F32)
    rel = idx[:, None] - idx[None, :]
    intra = jnp.where(rel[None] >= 0,
                      jnp.exp(jnp.maximum(rel, 0.0)[None] * log_gamma[:, None, None]), 0.0)
    q_dec = jnp.exp((idx[:, None] + 1.0) * log_gamma[None, :])
    k_dec = jnp.exp((chunk - 1.0 - idx)[None, :] * log_gamma[:, None]).T
    chunk_dec = jnp.exp(chunk * log_gamma)[None, :]
    widen = lambda t: jnp.repeat(t, RET_D, axis=1)
    return intra, widen(q_dec), widen(k_dec), widen(chunk_dec)


def _rope_tables(pos):
    half = RET_D // 2
    inv = ROPE_BASE ** (-jnp.arange(half, dtype=F32) / half)
    ang = pos[:, None] * inv[None, :]
    cos, sin = jnp.cos(ang), jnp.sin(ang)
    return jnp.concatenate([cos, cos], axis=1), jnp.concatenate([-sin, sin], axis=1)


def _ret_body(q_ref, k_ref, v_ref, g_ref, cos_ref, sin_ref, qd_ref, kd_ref, cd_ref,
              intra_ref, gnw_ref, o_ref, sout_ref, s_ref):
    c = pl.program_id(1)

    @pl.when(c == 0)
    def _():
        s_ref[...] = jnp.zeros_like(s_ref)

    cos = cos_ref[...]
    sin = sin_ref[...]
    for h in range(RET_HEADS):
        sl = slice(h * RET_D, (h + 1) * RET_D)
        qr = _rope(q_ref[:, sl], cos, sin)
        kr = _rope(k_ref[:, sl], cos, sin) * (RET_D ** -0.5)
        v = v_ref[:, sl]
        s = s_ref[h]
        scores = _mm(qr, kr, _NT) * intra_ref[h]
        out = _mm(scores, v) + _mm(qr, s) * qd_ref[:, sl]
        s_ref[h] = s * cd_ref[:, sl] + _mm(kr * kd_ref[:, sl], v, _TN)
        o_ref[:, sl] = _head_norm(out, RET_GN_EPS) * gnw_ref[:, sl] * _silu(g_ref[:, sl])

    @pl.when(c == pl.num_programs(1) - 1)
    def _():
        sout_ref[0] = s_ref[...]


def _retention_prompt(proj, gn_w, batch, seq):
    chunk = math.gcd(seq, RET_CHUNK)
    n_chunks = seq // chunk
    intra, q_dec, k_dec, chunk_dec = _ret_tables(chunk)
    cos, sin = _rope_tables(jnp.arange(seq, dtype=F32))
    col = lambda j: pl.BlockSpec((chunk, D_MODEL), lambda b, c, j=j: (b * n_chunks + c, j))
    const2 = lambda shape: pl.BlockSpec(shape, lambda b, c: (0, 0))
    state = pl.BlockSpec((1, RET_HEADS, RET_D, RET_D), lambda b, c: (b, 0, 0, 0))
    return pl.pallas_call(
        _ret_body,
        grid=(batch, n_chunks),
        in_specs=[col(COL_Q // D_MODEL), col(COL_K // D_MODEL), col(COL_V // D_MODEL),
                  col(COL_GSW // D_MODEL),
                  pl.BlockSpec((chunk, RET_D), lambda b, c: (c, 0)),
                  pl.BlockSpec((chunk, RET_D), lambda b, c: (c, 0)),
                  const2((chunk, D_MODEL)), const2((chunk, D_MODEL)), const2((1, D_MODEL)),
                  pl.BlockSpec((RET_HEADS, chunk, chunk), lambda b, c: (0, 0, 0)),
                  const2((1, D_MODEL))],
        out_specs=[pl.BlockSpec((chunk, D_MODEL), lambda b, c: (b * n_chunks + c, 0)), state],
        out_shape=[jax.ShapeDtypeStruct((batch * seq, D_MODEL), F32),
                   jax.ShapeDtypeStruct((batch, RET_HEADS, RET_D, RET_D), F32)],
        scratch_shapes=[pltpu.VMEM((RET_HEADS, RET_D, RET_D), F32)],
        compiler_params=_params(("parallel", "arbitrary")),
        name="retention_prompt",
    )(proj, proj, proj, proj, cos, sin, q_dec, k_dec, chunk_dec, intra, gn_w)


def _ret_step_body(q_ref, k_ref, v_ref, g_ref, cos_ref, sin_ref, qd_ref, kd_ref, cd_ref,
                   gnw_ref, s_ref, o_ref, sout_ref, *, rows):
    cos = cos_ref[...]
    sin = sin_ref[...]
    row = lax.broadcasted_iota(jnp.int32, (rows, RET_D), 0)
    for h in range(RET_HEADS):
        sl = slice(h * RET_D, (h + 1) * RET_D)
        qr = _rope(q_ref[:, sl], cos, sin)
        kr = _rope(k_ref[:, sl], cos, sin) * (RET_D ** -0.5)
        v = v_ref[:, sl]
        kd = kr * kd_ref[:, sl]
        cd = cd_ref[:, sl]

        def one_seq(b, cross):
            s = s_ref[b, h]
            mine = row == b
            sout_ref[b, h] = s * cd + _mm(jnp.where(mine, kd, 0.0), v, _TN)
            return jnp.where(mine, _mm(qr, s), cross)

        cross = lax.fori_loop(0, rows, one_seq, jnp.zeros((rows, RET_D), F32))
        out = jnp.sum(qr * kr, axis=-1, keepdims=True) * v + cross * qd_ref[:, sl]
        o_ref[:, sl] = _head_norm(out, RET_GN_EPS) * gnw_ref[:, sl] * _silu(g_ref[:, sl])


def _retention_step(proj, s0, gn_w, pos):
    batch = proj.shape[0]
    rows = 8
    _, q_dec, k_dec, chunk_dec = _ret_tables(1)
    cos, sin = _rope_tables(pos)
    col = lambda j: pl.BlockSpec((rows, D_MODEL), lambda i, j=j: (i, j))
    const2 = lambda shape: pl.BlockSpec(shape, lambda i: (0, 0))
    state = pl.BlockSpec((rows, RET_HEADS, RET_D, RET_D), lambda i: (i, 0, 0, 0))
    return pl.pallas_call(
        functools.partial(_ret_step_body, rows=rows),
        grid=(batch // rows,),
        in_specs=[col(COL_Q // D_MODEL), col(COL_K // D_MODEL), col(COL_V // D_MODEL),
                  col(COL_GSW // D_MODEL),
                  const2((1, RET_D)), const2((1, RET_D)),
                  const2((1, D_MODEL)), const2((1, D_MODEL)), const2((1, D_MODEL)),
                  const2((1, D_MODEL)), state],
        out_specs=[pl.BlockSpec((rows, D_MODEL), lambda i: (i, 0)), state],
        out_shape=[jax.ShapeDtypeStruct((batch, D_MODEL), F32),
                   jax.ShapeDtypeStruct(s0.shape, F32)],
        compiler_params=_params(("parallel",)),
        name="retention_step",
    )(proj, proj, proj, proj, cos, sin, q_dec, k_dec, chunk_dec, gn_w, s0)


def _head_ones():
    row = lax.broadcasted_iota(jnp.int32, (LANES, LANES), 0)
    col = lax.broadcasted_iota(jnp.int32, (LANES, LANES), 1)
    return jnp.where((row // RWKV_HEAD) == (col // RWKV_HEAD), 1.0, 0.0)


def _head_sum(x, ones_bd):
    blocks = [_mm(x[:, j:j + LANES], ones_bd, exact=True) for j in range(0, x.shape[1], LANES)]
    return blocks[0] if len(blocks) == 1 else jnp.concatenate(blocks, axis=1)


def _softplus(x):
    return jnp.maximum(x, 0.0) + jnp.log1p(jnp.exp(-jnp.abs(x)))


def _lerp(cur, prev, mu):
    return cur + (prev - cur) * mu


def _rwkv_prep(r_, k_, v_, lora, w0, w2, a0, a2, g2, k_k, k_a, r_k, ones_bd):
    w_lo, a_lo, g_lo = lora[:, 0:128], lora[:, 128:256], lora[:, 256:512]
    w_log = -_softplus(-(w0 + _mm(jnp.tanh(w_lo), w2))) - 0.5
    log_decay = -jnp.exp(w_log)
    a_sig = jax.nn.sigmoid(a0 + _mm(a_lo, a2))
    g = _mm(jax.nn.sigmoid(g_lo), g2)
    kk = k_ * k_k
    kk = kk / jnp.maximum(jnp.sqrt(_head_sum(kk * kk, ones_bd)), 1e-12)
    kmod = k_ * (1.0 + (a_sig - 1.0) * k_a)
    bonus = _head_sum(r_ * kmod * r_k, ones_bd) * v_
    return log_decay, a_sig, g, kk, kmod, bonus


def _rwkv_out(y, bonus, g, lnx_w, lnx_b, ones_bd):
    inv_n = 1.0 / RWKV_HEAD
    yc = y - _head_sum(y, ones_bd) * inv_n
    var = _head_sum(yc * yc, ones_bd) * inv_n
    return (yc * lax.rsqrt(var + RWKV_GN_EPS) * lnx_w + lnx_b + bonus) * g


def _block_diag(x, lane_first):
    return jnp.concatenate([jnp.where(lane_first, x, 0.0), jnp.where(lane_first, 0.0, x)], axis=0)


def _pair_masks():
    n = 2 * RWKV_CHUNK
    row = lax.broadcasted_iota(jnp.int32, (n, n), 0)
    col = lax.broadcasted_iota(jnp.int32, (n, n), 1)
    same = (row // RWKV_CHUNK) == (col // RWKV_CHUNK)
    lane_first = lax.broadcasted_iota(jnp.int32, (1, LANES), 1) < RWKV_HEAD
    return lane_first, same & (col < row), same & (col <= row), row == col


def _rwkv_chunk(r, lw, k, v, a, b, cum, s_bd, masks):
    lane_first, strict, incl, eye = masks
    p_inc = jnp.exp(cum)
    p_inv = jnp.exp(-cum)
    p_exc = jnp.exp(cum - lw)
    p_end = p_inc[RWKV_CHUNK - 1:RWKV_CHUNK, :]
    a_t = _block_diag(a * p_exc, lane_first)
    r_t = _block_diag(r * p_inc, lane_first)
    b_t = b * p_inv
    k_t = k * p_inv
    b_hat = _block_diag(b_t * p_end, lane_first)
    k_hat = _block_diag(k_t * p_end, lane_first)
    b_t = _block_diag(b_t, lane_first)
    k_t = _block_diag(k_t, lane_first)
    v_bd = _block_diag(v, lane_first)

    ex = True
    a_ab = jnp.where(strict, _mm(a_t, b_t, _NT, ex), 0.0)
    a_ak = jnp.where(strict, _mm(a_t, k_t, _NT, ex), 0.0)
    r_b = jnp.where(incl, _mm(r_t, b_t, _NT, ex), 0.0)
    r_k = jnp.where(incl, _mm(r_t, k_t, _NT, ex), 0.0)

    t_inv = jnp.where(eye, 1.0, 0.0) + a_ab
    power = a_ab
    for _ in range(int(math.log2(RWKV_CHUNK)) - 1):
        power = _mm(power, power, _NN, ex)
        t_inv = t_inv + _mm(t_inv, power, _NN, ex)

    w_mat = _mm(t_inv, a_t, _NN, ex)
    u0 = _mm(t_inv, _mm(a_ak, v_bd, _NN, ex), _NN, ex)
    r_y = r_t + _mm(r_b, w_mat, _NN, ex)
    y0 = _mm(r_b, u0, _NN, ex) + _mm(r_k, v_bd, _NN, ex)
    psi = _mm(w_mat, b_hat, _TN, ex) + jnp.where(eye, p_end, 0.0)
    s_add = _mm(u0, b_hat, _TN, ex) + _mm(v_bd, k_hat, _TN, ex)

    y_bd = _mm(r_y, s_bd, _NT, ex) + y0
    s_new = _mm(s_bd, psi, _NN, ex) + s_add
    return y_bd[:RWKV_CHUNK] + y_bd[RWKV_CHUNK:], s_new


def _shift_rows(cur, first_row):
    rolled = pltpu.roll(cur, 1, 0)
    row = lax.broadcasted_iota(jnp.int32, cur.shape, 0)
    return jnp.where(row == 0, first_row, rolled)


def _rwkv_body(r_ref, k_ref, v_ref, lora_ref, mu_ref, mul_ref,
               w0_ref, w2_ref, a0_ref, a2_ref, g2_ref, kk_ref, ka_ref, rk_ref,
               lnw_ref, lnb_ref, o_ref, sout_ref,
               s_ref, carry_ref, carryl_ref, *, tile):
    t = pl.program_id(1)
    p = pl.program_id(2)
    masks = _pair_masks()
    ones_bd = _head_ones()

    @pl.when(t == 0)
    def _():
        s_ref[p] = jnp.zeros((LANES, LANES), F32)
        carry_ref[p] = jnp.zeros((8, LANES), F32)

        @pl.when(p == 0)
        def _():
            carryl_ref[...] = jnp.zeros_like(carryl_ref)

    prev = carry_ref[p]
    cur_r, cur_k, cur_v, cur_l = r_ref[...], k_ref[...], v_ref[...], lora_ref[...]
    mu = mu_ref[0]
    r_ = _lerp(cur_r, _shift_rows(cur_r, prev[0:1]), mu[0:1])
    k_ = _lerp(cur_k, _shift_rows(cur_k, prev[1:2]), mu[1:2])
    v_ = _lerp(cur_v, _shift_rows(cur_v, prev[2:3]), mu[2:3])
    lo = _lerp(cur_l, _shift_rows(cur_l, carryl_ref[0:1, :]), mul_ref[...])
    carry_ref[p, 0:1, :] = cur_r[tile - 1:tile]
    carry_ref[p, 1:2, :] = cur_k[tile - 1:tile]
    carry_ref[p, 2:3, :] = cur_v[tile - 1:tile]

    @pl.when(p == RWKV_PAIRS - 1)
    def _():
        carryl_ref[0:1, :] = cur_l[tile - 1:tile]

    log_decay, a_sig, g, kk, kmod, bonus = _rwkv_prep(
        r_, k_, v_, lo, w0_ref[...], w2_ref[...], a0_ref[...], a2_ref[...], g2_ref[...],
        kk_ref[...], ka_ref[...], rk_ref[...], ones_bd)

    row = lax.broadcasted_iota(jnp.int32, (tile, tile), 0)
    col = lax.broadcasted_iota(jnp.int32, (tile, tile), 1)
    tri = jnp.where(((row // RWKV_CHUNK) == (col // RWKV_CHUNK)) & (col <= row), 1.0, 0.0)
    cum = _mm(tri, log_decay, _NN, exact=True)

    neg_kk = -kk
    kk_a = kk * a_sig
    s_bd = s_ref[p]
    ys = []
    for c in range(tile // RWKV_CHUNK):
        sl = slice(c * RWKV_CHUNK, (c + 1) * RWKV_CHUNK)
        y, s_bd = _rwkv_chunk(r_[sl], log_decay[sl], kmod[sl], v_[sl], neg_kk[sl],
                              kk_a[sl], cum[sl], s_bd, masks)
        ys.append(y)
    s_ref[p] = s_bd
    y = ys[0] if len(ys) == 1 else jnp.concatenate(ys, axis=0)
    o_ref[...] = _rwkv_out(y, bonus, g, lnw_ref[...], lnb_ref[...], ones_bd)

    @pl.when(t == pl.num_programs(1) - 1)
    def _():
        sout_ref[0, 2 * p] = s_bd[:RWKV_HEAD, :RWKV_HEAD]
        sout_ref[0, 2 * p + 1] = s_bd[RWKV_HEAD:, RWKV_HEAD:]


def _rwkv_prompt(proj, wts, batch, seq):
    tile = math.gcd(seq, 256)
    n_tiles = seq // tile
    blk = lambda base: pl.BlockSpec((tile, LANES),
                                    lambda b, t, p, base=base: (b * n_tiles + t, base // LANES + p))
    per_pair = lambda rows: pl.BlockSpec((rows, LANES), lambda b, t, p: (0, p))
    in_specs = [
        blk(COL_R), blk(COL_KK), blk(COL_VV),
        pl.BlockSpec((tile, LORA_W), lambda b, t, p: (b * n_tiles + t, COL_LORA // LORA_W)),
        pl.BlockSpec((1, 3, LANES), lambda b, t, p: (0, 0, p)),
        pl.BlockSpec((1, LORA_W), lambda b, t, p: (0, 0)),
        per_pair(1), per_pair(LANES),
        per_pair(1), per_pair(LANES),
        per_pair(2 * LANES),
        per_pair(1), per_pair(1), per_pair(1),
        per_pair(1), per_pair(1),
    ]
    return pl.pallas_call(
        functools.partial(_rwkv_body, tile=tile),
        grid=(batch, n_tiles, RWKV_PAIRS),
        in_specs=in_specs,
        out_specs=[pl.BlockSpec((tile, LANES), lambda b, t, p: (b * n_tiles + t, p)),
                   pl.BlockSpec((1, RWKV_HEADS, RWKV_HEAD, RWKV_HEAD), lambda b, t, p: (b, 0, 0, 0))],
        out_shape=[jax.ShapeDtypeStruct((batch * seq, D_MODEL), F32),
                   jax.ShapeDtypeStruct((batch, RWKV_HEADS, RWKV_HEAD, RWKV_HEAD), F32)],
        scratch_shapes=[pltpu.VMEM((RWKV_PAIRS, LANES, LANES), F32),
                        pltpu.VMEM((RWKV_PAIRS, 8, LANES), F32),
                        pltpu.VMEM((8, LORA_W), F32)],
        compiler_params=_params(("parallel", "arbitrary", "arbitrary")),
        name="rwkv_prompt",
    )(proj, proj, proj, proj, wts["mu3"], wts["mu_lora"],
      wts["w0"], wts["w2"], wts["a0"], wts["a2"], wts["g2"],
      wts["k_k"], wts["k_a"], wts["r_k"], wts["lnx_w"], wts["lnx_b"])


def _rwkv_step_body(r_ref, k_ref, v_ref, lora_ref, pr_ref, pk_ref, pv_ref, plora_ref,
                    mu_ref, mul_ref, w0_ref, w2_ref, a0_ref, a2_ref, g2_ref,
                    kk_ref, ka_ref, rk_ref, lnw_ref, lnb_ref, s_ref,
                    o_ref, sout_ref, *, rows):
    ones_bd = _head_ones()
    mu = mu_ref[0]
    r_ = _lerp(r_ref[...], pr_ref[...], mu[0:1])
    k_ = _lerp(k_ref[...], pk_ref[...], mu[1:2])
    v_ = _lerp(v_ref[...], pv_ref[...], mu[2:3])
    lo = _lerp(lora_ref[...], plora_ref[...], mul_ref[...])
    log_decay, a_sig, g, kk, kmod, bonus = _rwkv_prep(
        r_, k_, v_, lo, w0_ref[...], w2_ref[...], a0_ref[...], a2_ref[...], g2_ref[...],
        kk_ref[...], ka_ref[...], rk_ref[...], ones_bd)
    decay = jnp.exp(log_decay)
    neg_kk = -kk
    kk_a = kk * a_sig
    decay_r = decay * r_
    row = lax.broadcasted_iota(jnp.int32, (rows, RWKV_HEAD), 0)

    ys = []
    for h in range(RWKV_HEADS):
        sl = slice(h * RWKV_HEAD, (h + 1) * RWKV_HEAD)
        w_h, r_h, v_h, k_h, b_h = decay[:, sl], r_[:, sl], v_[:, sl], kmod[:, sl], kk_a[:, sl]
        lhs = jnp.concatenate([neg_kk[:, sl], decay_r[:, sl]], axis=0)
        rhs = jnp.concatenate([b_h, k_h], axis=0)

        def one_seq(b, carry):
            sa_all, swr_all = carry
            s = s_ref[b, h]
            mine = row == b
            x = _mm(lhs, s, _NT, exact=True)
            sa, swr = x[:rows], x[rows:]
            left = jnp.concatenate([jnp.where(mine, sa, 0.0), jnp.where(mine, v_h, 0.0)], axis=0)
            w_row = jnp.sum(jnp.where(mine, w_h, 0.0), axis=0, keepdims=True)
            sout_ref[b, h] = s * w_row + _mm(left, rhs, _TN, exact=True)
            return jnp.where(mine, sa, sa_all), jnp.where(mine, swr, swr_all)

        zero = jnp.zeros((rows, RWKV_HEAD), F32)
        sa_all, swr_all = lax.fori_loop(0, rows, one_seq, (zero, zero))
        ys.append(swr_all + sa_all * jnp.sum(b_h * r_h, axis=-1, keepdims=True)
                  + v_h * jnp.sum(k_h * r_h, axis=-1, keepdims=True))
    y = jnp.concatenate(ys, axis=1)
    o_ref[...] = _rwkv_out(y, bonus, g, lnw_ref[...], lnb_ref[...], ones_bd)


def _rwkv_step(proj, proj_prev, s0, wts):
    batch = proj.shape[0]
    rows = 8
    wide = lambda base: pl.BlockSpec((rows, D_MODEL), lambda i, base=base: (i, base // D_MODEL))
    lora = pl.BlockSpec((rows, LORA_W), lambda i: (i, COL_LORA // LORA_W))
    full = lambda shape: pl.BlockSpec(shape, lambda i: (0,) * len(shape))
    state = pl.BlockSpec((rows, RWKV_HEADS, RWKV_HEAD, RWKV_HEAD), lambda i: (i, 0, 0, 0))
    vec = full((1, D_MODEL))
    return pl.pallas_call(
        functools.partial(_rwkv_step_body, rows=rows),
        grid=(batch // rows,),
        in_specs=[wide(COL_R), wide(COL_KK), wide(COL_VV), lora,
                  wide(COL_R), wide(COL_KK), wide(COL_VV), lora,
                  full((1, 3, D_MODEL)), full((1, LORA_W)),
                  vec, full((LANES, D_MODEL)), vec, full((LANES, D_MODEL)),
                  full((2 * LANES, D_MODEL)), vec, vec, vec, vec, vec, state],
        out_specs=[pl.BlockSpec((rows, D_MODEL), lambda i: (i, 0)), state],
        out_shape=[jax.ShapeDtypeStruct((batch, D_MODEL), F32),
                   jax.ShapeDtypeStruct(s0.shape, F32)],
        compiler_params=_params(("parallel",)),
        name="rwkv_step",
    )(proj, proj, proj, proj, proj_prev, proj_prev, proj_prev, proj_prev,
      wts["mu3"], wts["mu_lora"], wts["w0"], wts["w2"], wts["a0"], wts["a2"], wts["g2"],
      wts["k_k"], wts["k_a"], wts["r_k"], wts["lnx_w"], wts["lnx_b"], s0)


def _merge_body(x_ref, oret_ref, orwkv_ref, ga_ref, gb_ref, wout_ref, nffn_ref, rw_ref, rb_ref,
                x1_ref, xn_ref, comb_ref):
    merged = (jax.nn.sigmoid(ga_ref[...]) * oret_ref[...]
              + jax.nn.sigmoid(gb_ref[...]) * orwkv_ref[...])
    x1 = x_ref[...] + _mm(merged, wout_ref[...])
    x1_ref[...] = x1
    xn = _rms(x1, nffn_ref[...])
    xn_ref[...] = xn.astype(BF16)

    logits = _mm(xn, rw_ref[...], exact=True) + rb_ref[...]
    lane = lax.broadcasted_iota(jnp.int32, logits.shape, 1)
    neg_inf = -jnp.inf
    is_group = lane < N_GROUPS
    g_max = jnp.max(jnp.where(is_group, logits, neg_inf), axis=-1, keepdims=True)
    g_sel = jnp.min(jnp.where(is_group & (logits == g_max), lane, LANES), axis=-1, keepdims=True)
    g_w = 1.0 / jnp.sum(jnp.where(is_group, jnp.exp(logits - g_max), 0.0), axis=-1, keepdims=True)
    first = N_GROUPS + EXPERTS_PER_GROUP * g_sel
    in_group = (lane >= first) & (lane < first + EXPERTS_PER_GROUP)
    e_max = jnp.max(jnp.where(in_group, logits, neg_inf), axis=-1, keepdims=True)
    e_exp = jnp.where(in_group, jnp.exp(logits - e_max), 0.0)
    prob = e_exp / jnp.sum(e_exp, axis=-1, keepdims=True)
    prob = jnp.where(in_group, prob, -1.0)
    p1 = jnp.max(prob, axis=-1, keepdims=True)
    i1 = jnp.min(jnp.where(prob == p1, lane, LANES), axis=-1, keepdims=True)
    rest = jnp.where(lane == i1, -1.0, prob)
    p2 = jnp.max(rest, axis=-1, keepdims=True)
    i2 = jnp.min(jnp.where(rest == p2, lane, LANES), axis=-1, keepdims=True)
    denom = p1 + p2
    comb_ref[...] = (jnp.where(lane == i1 - N_GROUPS, g_w * p1 / denom, 0.0)
                     + jnp.where(lane == i2 - N_GROUPS, g_w * p2 / denom, 0.0))


def _merge(x, o_ret, o_rwkv, proj, wts):
    t = x.shape[0]
    tm = math.gcd(t, 512)
    row = lambda: pl.BlockSpec((tm, D_MODEL), lambda i: (i, 0))
    full = lambda shape: pl.BlockSpec(shape, lambda i: (0, 0))
    return pl.pallas_call(
        _merge_body,
        grid=(t // tm,),
        in_specs=[row(), row(), row(),
                  pl.BlockSpec((tm, D_MODEL), lambda i: (i, COL_GATE_A // D_MODEL)),
                  pl.BlockSpec((tm, D_MODEL), lambda i: (i, COL_GATE_B // D_MODEL)),
                  full((D_MODEL, D_MODEL)), full((1, D_MODEL)),
                  full((D_MODEL, LANES)), full((1, LANES))],
        out_specs=[row(), row(), pl.BlockSpec((tm, LANES), lambda i: (i, 0))],
        out_shape=[jax.ShapeDtypeStruct((t, D_MODEL), F32),
                   jax.ShapeDtypeStruct((t, D_MODEL), BF16),
                   jax.ShapeDtypeStruct((t, LANES), F32)],
        compiler_params=_params(("parallel",)),
        name="merge_router",
    )(x, o_ret, o_rwkv, proj, proj, wts["w_out"], wts["norm_ffn"], wts["router_w"], wts["router_b"])


def _moe_body(xn_ref, x1_ref, comb_ref, wg_ref, wu_ref, wd_ref, nf_ref, o_ref, acc_ref):
    e = pl.program_id(1)

    @pl.when(e == 0)
    def _():
        acc_ref[...] = jnp.zeros_like(acc_ref)

    comb = comb_ref[...]
    lane = lax.broadcasted_iota(jnp.int32, comb.shape, 1)
    weight = jnp.sum(jnp.where(lane == e, comb, 0.0), axis=-1, keepdims=True)
    xn = xn_ref[...]
    h = _silu(_mm(xn, wg_ref[0])) * _mm(xn, wu_ref[0]) * weight
    acc_ref[...] += _mm(h, wd_ref[0])

    @pl.when(e == pl.num_programs(1) - 1)
    def _():
        o_ref[...] = _rms(x1_ref[...] + acc_ref[...], nf_ref[...])


def _moe(xn, x1, comb, wts):
    t = xn.shape[0]
    tm = math.gcd(t, 1024)
    row = lambda w: pl.BlockSpec((tm, w), lambda i, e: (i, 0))
    return pl.pallas_call(
        _moe_body,
        grid=(t // tm, N_EXPERTS),
        in_specs=[row(D_MODEL), row(D_MODEL), row(LANES),
                  pl.BlockSpec((1, D_MODEL, D_EXPERT), lambda i, e: (e, 0, 0)),
                  pl.BlockSpec((1, D_MODEL, D_EXPERT), lambda i, e: (e, 0, 0)),
                  pl.BlockSpec((1, D_EXPERT, D_MODEL), lambda i, e: (e, 0, 0)),
                  pl.BlockSpec((1, D_MODEL), lambda i, e: (0, 0))],
        out_specs=row(D_MODEL),
        out_shape=jax.ShapeDtypeStruct((t, D_MODEL), F32),
        scratch_shapes=[pltpu.VMEM((tm, D_MODEL), F32)],
        compiler_params=_params(("parallel", "arbitrary")),
        name="moe",
    )(xn, x1, comb, wts["w_gate"], wts["w_up"], wts["w_down"], wts["norm_final"])


def _pad_to(x, size, axis):
    pad = [(0, 0)] * x.ndim
    pad[axis] = (0, size - x.shape[axis])
    return jnp.pad(x, pad)


def _prepare_weights(norm_mix, w_in, mu_shift, ret_gn_w, rwkv_w0, rwkv_w2, rwkv_a0, rwkv_a2,
                     rwkv_g2, rwkv_k_k, rwkv_k_a, rwkv_r_k, rwkv_lnx_w, rwkv_lnx_b, w_out,
                     norm_ffn, router_group_w, router_group_b, router_expert_w, router_expert_b,
                     expert_w_gate, expert_w_up, expert_w_down, norm_final):
    def lora_cols(t):
        return jnp.concatenate([_pad_to(t[..., SRC_WLO:SRC_ALO], 128, -1),
                                _pad_to(t[..., SRC_ALO:SRC_GLO], 128, -1),
                                _pad_to(t[..., SRC_GLO:SRC_GATE_A], 256, -1)], axis=-1)

    w = w_in[0]
    w_in_p = jnp.concatenate([w[:, :SRC_WLO], w[:, SRC_GATE_A:], lora_cols(w)], axis=1).astype(BF16)
    mu = jnp.concatenate([jnp.zeros((SRC_SHIFT,), F32), mu_shift[0]])
    row = lambda v: v.reshape(1, -1)
    router_w = _pad_to(jnp.concatenate([router_group_w[0], router_expert_w[0]], axis=1), LANES, 1)
    router_b = _pad_to(jnp.concatenate([router_group_b[0], router_expert_b[0]]), LANES, 0)
    return dict(
        w_in=w_in_p, norm_mix=row(norm_mix[0]), ret_gn_w=row(ret_gn_w[0]),
        mu3=mu[SRC_SHIFT:SRC_WLO].reshape(1, 3, D_MODEL), mu_lora=lora_cols(mu).reshape(1, LORA_W),
        w0=row(rwkv_w0[0]), w2=_pad_to(rwkv_w2[0], 128, 0),
        a0=row(rwkv_a0[0]), a2=_pad_to(rwkv_a2[0], 128, 0),
        g2=_pad_to(rwkv_g2[0], 256, 0),
        k_k=row(rwkv_k_k[0]), k_a=row(rwkv_k_a[0]), r_k=row(rwkv_r_k[0]),
        lnx_w=row(rwkv_lnx_w[0]), lnx_b=row(rwkv_lnx_b[0]),
        w_out=w_out[0].astype(BF16), norm_ffn=row(norm_ffn[0]),
        router_w=router_w, router_b=row(router_b),
        w_gate=expert_w_gate[0].astype(BF16), w_up=expert_w_up[0].astype(BF16),
        w_down=expert_w_down[0].astype(BF16), norm_final=row(norm_final))


def _finish(x, o_ret, o_rwkv, proj, wts):
    x1, xn2, comb = _merge(x, o_ret, o_rwkv, proj, wts)
    return _moe(xn2, x1, comb, wts)


def kernel(x_prompt, x_sample, state_ret, state_rwkv, state_shift, norm_mix, w_in, mu_shift, ret_gn_w, rwkv_w0, rwkv_w2, rwkv_a0, rwkv_a2, rwkv_g2, rwkv_k_k, rwkv_k_a, rwkv_r_k, rwkv_lnx_w, rwkv_lnx_b, w_out, norm_ffn, router_group_w, router_group_b, router_expert_w, router_expert_b, expert_w_gate, expert_w_up, expert_w_down, norm_final):
    assert norm_mix.shape[0] == 1, "single-layer step"
    wts = _prepare_weights(norm_mix, w_in, mu_shift, ret_gn_w, rwkv_w0, rwkv_w2, rwkv_a0, rwkv_a2,
                           rwkv_g2, rwkv_k_k, rwkv_k_a, rwkv_r_k, rwkv_lnx_w, rwkv_lnx_b, w_out,
                           norm_ffn, router_group_w, router_group_b, router_expert_w,
                           router_expert_b, expert_w_gate, expert_w_up, expert_w_down, norm_final)

    bp, lp, _ = x_prompt.shape
    xp = x_prompt.reshape(bp * lp, D_MODEL)
    proj_p, xn_p = _inproj(xp, wts["norm_mix"], wts["w_in"], apply_norm=True)
    o_ret_p, ret_p = _retention_prompt(proj_p, wts["ret_gn_w"], bp, lp)
    o_rwkv_p, rwkv_p = _rwkv_prompt(proj_p, wts, bp, lp)
    y_prompt = _finish(xp, o_ret_p, o_rwkv_p, proj_p, wts).reshape(bp, lp, D_MODEL)
    shift_p = xn_p.reshape(bp, lp, D_MODEL)[:, -1]

    bs, ls, _ = x_sample.shape
    assert ls == 1, "sample group advances one token"
    xs = x_sample.reshape(bs, D_MODEL)
    proj_s, xn_s = _inproj(xs, wts["norm_mix"], wts["w_in"], apply_norm=True)
    proj_prev, _ = _inproj(state_shift[0], wts["norm_mix"], wts["w_in"], apply_norm=False)
    pos_s = PAST_LEN + jnp.arange(ls, dtype=F32)
    o_ret_s, ret_s = _retention_step(proj_s, state_ret[0], wts["ret_gn_w"], pos_s)
    o_rwkv_s, rwkv_s = _rwkv_step(proj_s, proj_prev, state_rwkv[0], wts)
    y_sample = _finish(xs, o_ret_s, o_rwkv_s, proj_s, wts).reshape(bs, ls, D_MODEL)

    return (y_prompt, y_sample, ret_p[None], rwkv_p[None], shift_p[None],
            ret_s[None], rwkv_s[None], xn_s[None])
```

```python
import functools
import math

import jax
import jax.numpy as jnp
from jax import lax
from jax.experimental import pallas as pl
from jax.experimental.pallas import tpu as pltpu

F32 = jnp.float32
BF16 = jnp.bfloat16

D_MODEL = 1024
LANES = 128
PAST_LEN = 16384
RET_HEADS = 8
RET_D = D_MODEL // RET_HEADS
RET_CHUNK = 128
ROPE_BASE = 10000.0
RET_GN_EPS = 1e-5
RWKV_HEAD = 64
RWKV_HEADS = D_MODEL // RWKV_HEAD
RWKV_PAIRS = RWKV_HEADS // 2
RWKV_CHUNK = 64
LORA_DECAY = 64
LORA_A = 64
LORA_G = 160
RWKV_GN_EPS = 64e-5
N_GROUPS = 4
EXPERTS_PER_GROUP = 8
N_EXPERTS = N_GROUPS * EXPERTS_PER_GROUP
D_EXPERT = 256
RMS_EPS = 1e-6

SRC_SHIFT = 4 * D_MODEL
SRC_WLO = SRC_SHIFT + 3 * D_MODEL
SRC_ALO = SRC_WLO + LORA_DECAY
SRC_GLO = SRC_ALO + LORA_A
SRC_GATE_A = SRC_GLO + LORA_G
COL_Q, COL_K, COL_V, COL_GSW = 0, 1024, 2048, 3072
COL_R, COL_KK, COL_VV = 4096, 5120, 6144
COL_GATE_A, COL_GATE_B = 7168, 8192
COL_LORA = 9216
LORA_W = 512
N_PROJ = COL_LORA + LORA_W

VMEM_LIMIT = 48 * 1024 * 1024

_NN = (((1,), (0,)), ((), ()))
_NT = (((1,), (1,)), ((), ()))
_TN = (((0,), (0,)), ((), ()))


def _mm(a, b, dims=_NN, exact=False):
    if exact:
        return lax.dot_general(a.astype(F32), b.astype(F32), dims,
                               precision=lax.Precision.HIGHEST,
                               preferred_element_type=F32)
    return lax.dot_general(a.astype(BF16), b.astype(BF16), dims,
                           preferred_element_type=F32)


def _params(sem):
    return pltpu.CompilerParams(dimension_semantics=sem,
                                vmem_limit_bytes=VMEM_LIMIT)


def _rms(x, gain):
    return x * lax.rsqrt(jnp.mean(x * x, axis=-1, keepdims=True) + RMS_EPS) * gain


def _inproj_body(x_ref, g_ref, w_ref, o_ref, xn_ref, xb_ref, *, apply_norm):
    @pl.when(pl.program_id(1) == 0)
    def _():
        x = x_ref[...]
        if apply_norm:
            x = _rms(x, g_ref[...])
        xn_ref[...] = x
        xb_ref[...] = x.astype(BF16)

    o_ref[...] = jnp.dot(xb_ref[...], w_ref[...], preferred_element_type=F32)


def _inproj(x, gain, w, *, apply_norm):
    t = x.shape[0]
    n = w.shape[1]
    tm = math.gcd(t, 512)
    tn = n // 4
    return pl.pallas_call(
        functools.partial(_inproj_body, apply_norm=apply_norm),
        grid=(t // tm, n // tn),
        in_specs=[pl.BlockSpec((tm, D_MODEL), lambda i, j: (i, 0)),
                  pl.BlockSpec((1, D_MODEL), lambda i, j: (0, 0)),
                  pl.BlockSpec((D_MODEL, tn), lambda i, j: (0, j))],
        out_specs=[pl.BlockSpec((tm, tn), lambda i, j: (i, j)),
                   pl.BlockSpec((tm, D_MODEL), lambda i, j: (i, 0))],
        out_shape=[jax.ShapeDtypeStruct((t, n), F32),
                   jax.ShapeDtypeStruct((t, D_MODEL), F32)],
        scratch_shapes=[pltpu.VMEM((tm, D_MODEL), BF16)],
        compiler_params=_params(("parallel", "arbitrary")),
        name="inproj",
    )(x, gain, w)


def _head_norm(x, eps):
    mu = jnp.mean(x, axis=-1, keepdims=True)
    xc = x - mu
    return xc * lax.rsqrt(jnp.mean(xc * xc, axis=-1, keepdims=True) + eps)


def _silu(x):
    return x * jax.nn.sigmoid(x)


def _rope(x, cos, sin_signed):
    return x * cos + pltpu.roll(x, RET_D // 2, 1) * sin_signed


def _ret_tables(chunk):
    log_gamma = jnp.log1p(-jnp.exp2(-5.0 - jnp.arange(RET_HEADS, dtype=F32)))
    idx = jnp.arange(chunk, dtype=F32)
    rel = idx[:, None] - idx[None, :]
    intra = jnp.where(rel[None] >= 0,
                      jnp.exp(jnp.maximum(rel, 0.0)[None] * log_gamma[:, None, None]), 0.0)
    q_dec = jnp.exp((idx[:, None] + 1.0) * log_gamma[None, :])
    k_dec = jnp.exp((chunk - 1.0 - idx)[None, :] * log_gamma[:, None]).T
    chunk_dec = jnp.exp(chunk * log_gamma)[None, :]
    widen = lambda t: jnp.repeat(t, RET_D, axis=1)
    return intra, widen(q_dec), widen(k_dec), widen(chunk_dec)


def _rope_tables(pos):
    half = RET_D // 2
    inv = ROPE_BASE ** (-jnp.arange(half, dtype=F32) / half)
    ang = pos[:, None] * inv[None, :]
    cos, sin = jnp.cos(ang), jnp.sin(ang)
    return jnp.concatenate([cos, cos], axis=1), jnp.concatenate([-sin, sin], axis=1)


def _ret_body(q_ref, k_ref, v_ref, g_ref, cos_ref, sin_ref, qd_ref, kd_ref, cd_ref,
              intra_ref, gnw_ref, o_ref, sout_ref, s_ref):
    c = pl.program_id(1)

    @pl.when(c == 0)
    def _():
        s_ref[...] = jnp.zeros_like(s_ref)

    cos = cos_ref[...]
    sin = sin_ref[...]
    for h in range(RET_HEADS):
        sl = slice(h * RET_D, (h + 1) * RET_D)
        qr = _rope(q_ref[:, sl], cos, sin)
        kr = _rope(k_ref[:, sl], cos, sin) * (RET_D ** -0.5)
        v = v_ref[:, sl]
        s = s_ref[h]
        scores = _mm(qr, kr, _NT) * intra_ref[h]
        out = _mm(scores, v) + _mm(qr, s) * qd_ref[:, sl]
        s_ref[h] = s * cd_ref[:, sl] + _mm(kr * kd_ref[:, sl], v, _TN)
        o_ref[:, sl] = _head_norm(out, RET_GN_EPS) * gnw_ref[:, sl] * _silu(g_ref[:, sl])

    @pl.when(c == pl.num_programs(1) - 1)
    def _():
        sout_ref[0] = s_ref[...]


def _retention_prompt(proj, gn_w, batch, seq):
    chunk = math.gcd(seq, RET_CHUNK)
    n_chunks = seq // chunk
    intra, q_dec, k_dec, chunk_dec = _ret_tables(chunk)
    cos, sin = _rope_tables(jnp.arange(seq, dtype=F32))
    col = lambda j: pl.BlockSpec((chunk, D_MODEL), lambda b, c, j=j: (b * n_chunks + c, j))
    const2 = lambda shape: pl.BlockSpec(shape, lambda b, c: (0, 0))
    state = pl.BlockSpec((1, RET_HEADS, RET_D, RET_D), lambda b, c: (b, 0, 0, 0))
    return pl.pallas_call(
        _ret_body,
        grid=(batch, n_chunks),
        in_specs=[col(COL_Q // D_MODEL), col(COL_K // D_MODEL), col(COL_V // D_MODEL),
                  col(COL_GSW // D_MODEL),
                  pl.BlockSpec((chunk, RET_D), lambda b, c: (c, 0)),
                  pl.BlockSpec((chunk, RET_D), lambda b, c: (c, 0)),
                  const2((chunk, D_MODEL)), const2((chunk, D_MODEL)), const2((1, D_MODEL)),
                  pl.BlockSpec((RET_HEADS, chunk, chunk), lambda b, c: (0, 0, 0)),
                  const2((1, D_MODEL))],
        out_specs=[pl.BlockSpec((chunk, D_MODEL), lambda b, c: (b * n_chunks + c, 0)), state],
        out_shape=[jax.ShapeDtypeStruct((batch * seq, D_MODEL), F32),
                   jax.ShapeDtypeStruct((batch, RET_HEADS, RET_D, RET_D), F32)],
        scratch_shapes=[pltpu.VMEM((RET_HEADS, RET_D, RET_D), F32)],
        compiler_params=_params(("parallel", "arbitrary")),
        name="retention_prompt",
    )(proj, proj, proj, proj, cos, sin, q_dec, k_dec, chunk_dec, intra, gn_w)


def _ret_step_body(q_ref, k_ref, v_ref, g_ref, cos_ref, sin_ref, qd_ref, kd_ref, cd_ref,
                   gnw_ref, s_ref, o_ref, sout_ref, *, rows):
    cos = cos_ref[...]
    sin = sin_ref[...]
    row = lax.broadcasted_iota(jnp.int32, (rows, RET_D), 0)
    for h in range(RET_HEADS):
        sl = slice(h * RET_D, (h + 1) * RET_D)
        qr = _rope(q_ref[:, sl], cos, sin)
        kr = _rope(k_ref[:, sl], cos, sin) * (RET_D ** -0.5)
        v = v_ref[:, sl]
        kd = kr * kd_ref[:, sl]
        cd = cd_ref[:, sl]

        reads = [_mm(qr, s_ref[b, h]) for b in range(rows)]
        adds = [_mm(jnp.where(row == b, kd, 0.0), v, _TN) for b in range(rows)]
        cross = jnp.zeros((rows, RET_D), F32)
        for b in range(rows):
            sout_ref[b, h] = s_ref[b, h] * cd + adds[b]
            cross = jnp.where(row == b, reads[b], cross)
        out = jnp.sum(qr * kr, axis=-1, keepdims=True) * v + cross * qd_ref[:, sl]
        o_ref[:, sl] = _head_norm(out, RET_GN_EPS) * gnw_ref[:, sl] * _silu(g_ref[:, sl])


def _retention_step(proj, s0, gn_w, pos):
    batch = proj.shape[0]
    rows = 8
    _, q_dec, k_dec, chunk_dec = _ret_tables(1)
    cos, sin = _rope_tables(pos)
    col = lambda j: pl.BlockSpec((rows, D_MODEL), lambda i, j=j: (i, j))
    const2 = lambda shape: pl.BlockSpec(shape, lambda i: (0, 0))
    state = pl.BlockSpec((rows, RET_HEADS, RET_D, RET_D), lambda i: (i, 0, 0, 0))
    return pl.pallas_call(
        functools.partial(_ret_step_body, rows=rows),
        grid=(batch // rows,),
        in_specs=[col(COL_Q // D_MODEL), col(COL_K // D_MODEL), col(COL_V // D_MODEL),
                  col(COL_GSW // D_MODEL),
                  const2((1, RET_D)), const2((1, RET_D)),
                  const2((1, D_MODEL)), const2((1, D_MODEL)), const2((1, D_MODEL)),
                  const2((1, D_MODEL)), state],
        out_specs=[pl.BlockSpec((rows, D_MODEL), lambda i: (i, 0)), state],
        out_shape=[jax.ShapeDtypeStruct((batch, D_MODEL), F32),
                   jax.ShapeDtypeStruct(s0.shape, F32)],
        compiler_params=_params(("parallel",)),
        name="retention_step",
    )(proj, proj, proj, proj, cos, sin, q_dec, k_dec, chunk_dec, gn_w, s0)


def _head_ones():
    row = lax.broadcasted_iota(jnp.int32, (LANES, LANES), 0)
    col = lax.broadcasted_iota(jnp.int32, (LANES, LANES), 1)
    return jnp.where((row // RWKV_HEAD) == (col // RWKV_HEAD), 1.0, 0.0).astype(BF16)


def _split_bf16(x, terms):
    parts = []
    for _ in range(terms - 1):
        hi = x.astype(BF16)
        parts.append(hi)
        x = x - hi.astype(F32)
    parts.append(x.astype(BF16))
    return parts


def _head_sum(x, ones_bd):
    def one(blk):
        hi, lo = _split_bf16(blk, 2)
        return (jnp.dot(hi, ones_bd, preferred_element_type=F32)
                + jnp.dot(lo, ones_bd, preferred_element_type=F32))
    blocks = [one(x[:, j:j + LANES]) for j in range(0, x.shape[1], LANES)]
    return blocks[0] if len(blocks) == 1 else jnp.concatenate(blocks, axis=1)


def _softplus(x):
    return jnp.maximum(x, 0.0) + jnp.log1p(jnp.exp(-jnp.abs(x)))


def _lerp(cur, prev, mu):
    return cur + (prev - cur) * mu


def _rwkv_prep(r_, k_, v_, lora, w0, w2, a0, a2, g2, k_k, k_a, r_k, ones_bd):
    w_lo, a_lo, g_lo = lora[:, 0:128], lora[:, 128:256], lora[:, 256:512]
    w_log = -_softplus(-(w0 + _mm(jnp.tanh(w_lo), w2))) - 0.5
    log_decay = -jnp.exp(w_log)
    a_sig = jax.nn.sigmoid(a0 + _mm(a_lo, a2))
    g = _mm(jax.nn.sigmoid(g_lo), g2)
    kk = k_ * k_k
    kk = kk / jnp.maximum(jnp.sqrt(_head_sum(kk * kk, ones_bd)), 1e-12)
    kmod = k_ * (1.0 + (a_sig - 1.0) * k_a)
    bonus = _head_sum(r_ * kmod * r_k, ones_bd) * v_
    return log_decay, a_sig, g, kk, kmod, bonus


def _rwkv_out(y, bonus, g, lnx_w, lnx_b, ones_bd):
    inv_n = 1.0 / RWKV_HEAD
    yc = y - _head_sum(y, ones_bd) * inv_n
    var = _head_sum(yc * yc, ones_bd) * inv_n
    return (yc * lax.rsqrt(var + RWKV_GN_EPS) * lnx_w + lnx_b + bonus) * g


def _block_diag(x, lane_first):
    return jnp.concatenate([jnp.where(lane_first, x, 0.0), jnp.where(lane_first, 0.0, x)], axis=0)


def _pair_masks():
    n = 2 * RWKV_CHUNK
    row = lax.broadcasted_iota(jnp.int32, (n, n), 0)
    col = lax.broadcasted_iota(jnp.int32, (n, n), 1)
    same = (row // RWKV_CHUNK) == (col // RWKV_CHUNK)
    lane_first = lax.broadcasted_iota(jnp.int32, (1, LANES), 1) < RWKV_HEAD
    return lane_first, same & (col < row), same & (col <= row), row == col


def _rwkv_chunks(chunks, masks):
    lane_first, strict, incl, eye = masks
    n = 2 * RWKV_CHUNK
    rows = lambda x, y: jnp.concatenate([x, y], axis=0)
    cols = lambda x, y: jnp.concatenate([x, y], axis=1)
    bd = lambda x: _block_diag(x, lane_first)

    ops = []
    for r, lw, k, v, a, b, cum in chunks:
        p_inc = jnp.exp(cum)
        p_inv = jnp.exp(-cum)
        p_exc = jnp.exp(cum - lw)
        p_end = p_inc[RWKV_CHUNK - 1:RWKV_CHUNK, :]
        b_t = b * p_inv
        k_t = k * p_inv
        ops.append(dict(a_t=bd(a * p_exc), r_t=bd(r * p_inc), b_t=bd(b_t), k_t=bd(k_t),
                        b_hat=bd(b_t * p_end), k_hat=bd(k_t * p_end), v_bd=bd(v), p_end=p_end))

    for o in ops:
        prod = _mm(rows(o["a_t"], o["r_t"]), rows(o["b_t"], o["k_t"]), _NT)
        o["a_ab"] = jnp.where(strict, prod[:n, :n], 0.0)
        o["a_ak"] = jnp.where(strict, prod[:n, n:], 0.0)
        o["r_b"] = jnp.where(incl, prod[n:, :n], 0.0)
        o["r_k"] = jnp.where(incl, prod[n:, n:], 0.0)

    for o in ops:
        o["t_inv"] = jnp.where(eye, 1.0, 0.0) + o["a_ab"]
        o["power"] = _mm(o["a_ab"], o["a_ab"])
        o["av_rv"] = _mm(rows(o["a_ak"], o["r_k"]), o["v_bd"])
    for _ in range(int(math.log2(RWKV_CHUNK)) - 2):
        for o in ops:
            both = _mm(rows(o["power"], o["t_inv"]), o["power"])
            o["power"] = both[:n]
            o["t_inv"] = o["t_inv"] + both[n:]
    for o in ops:
        o["t_inv"] = o["t_inv"] + _mm(o["t_inv"], o["power"])
    for o in ops:
        o["wu"] = _mm(o["t_inv"], cols(o["a_t"], o["av_rv"][:n]))
    out = []
    for o in ops:
        wu = o["wu"]
        ry_y0 = _mm(o["r_b"], wu)
        r_y = o["r_t"] + ry_y0[:, :LANES]
        y0 = ry_y0[:, LANES:] + o["av_rv"][n:]
        psi = _mm(wu[:, :LANES], o["b_hat"], _TN) + jnp.where(eye, o["p_end"], 0.0)
        s_add = _mm(rows(wu[:, LANES:], o["v_bd"]), rows(o["b_hat"], o["k_hat"]), _TN)
        out.append((r_y, y0, psi, s_add))
    return out


def _shift_rows(cur, first_row):
    rolled = pltpu.roll(cur, 1, 0)
    row = lax.broadcasted_iota(jnp.int32, cur.shape, 0)
    return jnp.where(row == 0, first_row, rolled)


def _rwkv_body(r_ref, k_ref, v_ref, lora_ref, mu_ref, mul_ref,
               w0_ref, w2_ref, a0_ref, a2_ref, g2_ref, kk_ref, ka_ref, rk_ref,
               lnw_ref, lnb_ref, o_ref, sout_ref,
               s_ref, carry_ref, carryl_ref, *, tile):
    t = pl.program_id(1)
    p = pl.program_id(2)
    masks = _pair_masks()
    ones_bd = _head_ones()
    pairs = r_ref.shape[1] // LANES

    @pl.when(t == 0)
    def _():
        for q in range(pairs):
            s_ref[p * pairs + q] = jnp.zeros((LANES, LANES), F32)
        carry_ref[p] = jnp.zeros(carry_ref.shape[1:], F32)

        @pl.when(p == 0)
        def _():
            carryl_ref[...] = jnp.zeros_like(carryl_ref)

    prev = carry_ref[p]
    cur_r, cur_k, cur_v, cur_l = r_ref[...], k_ref[...], v_ref[...], lora_ref[...]
    mu = mu_ref[0]
    r_ = _lerp(cur_r, _shift_rows(cur_r, prev[0:1]), mu[0:1])
    k_ = _lerp(cur_k, _shift_rows(cur_k, prev[1:2]), mu[1:2])
    v_ = _lerp(cur_v, _shift_rows(cur_v, prev[2:3]), mu[2:3])
    lo = _lerp(cur_l, _shift_rows(cur_l, carryl_ref[0:1, :]), mul_ref[...])
    carry_ref[p, 0:1, :] = cur_r[tile - 1:tile]
    carry_ref[p, 1:2, :] = cur_k[tile - 1:tile]
    carry_ref[p, 2:3, :] = cur_v[tile - 1:tile]

    @pl.when(p == pl.num_programs(2) - 1)
    def _():
        carryl_ref[0:1, :] = cur_l[tile - 1:tile]

    log_decay, a_sig, g, kk, kmod, bonus = _rwkv_prep(
        r_, k_, v_, lo, w0_ref[...], w2_ref[...], a0_ref[...], a2_ref[...], g2_ref[...],
        kk_ref[...], ka_ref[...], rk_ref[...], ones_bd)

    row = lax.broadcasted_iota(jnp.int32, (tile, tile), 0)
    col = lax.broadcasted_iota(jnp.int32, (tile, tile), 1)
    tri = jnp.where(((row // RWKV_CHUNK) == (col // RWKV_CHUNK)) & (col <= row), 1.0, 0.0).astype(BF16)
    cum = sum(jnp.dot(tri, part, preferred_element_type=F32) for part in _split_bf16(log_decay, 3))

    neg_kk = -kk
    kk_a = kk * a_sig
    n_chunks = tile // RWKV_CHUNK
    chunks = []
    for c in range(n_chunks):
        for q in range(pairs):
            at = (slice(c * RWKV_CHUNK, (c + 1) * RWKV_CHUNK), slice(q * LANES, (q + 1) * LANES))
            chunks.append((r_[at], log_decay[at], kmod[at], v_[at], neg_kk[at], kk_a[at], cum[at]))
    parts = _rwkv_chunks(chunks, masks)

    states = [s_ref[p * pairs + q] for q in range(pairs)]
    ys = [[] for _ in range(pairs)]
    for c in range(n_chunks):
        for q in range(pairs):
            r_y, y0, psi, s_add = parts[c * pairs + q]
            y_bd = _mm(r_y, states[q], _NT) + y0
            states[q] = _mm(states[q], psi) + s_add
            ys[q].append(y_bd[:RWKV_CHUNK] + y_bd[RWKV_CHUNK:])
    for q in range(pairs):
        s_ref[p * pairs + q] = states[q]

    @pl.when(t == pl.num_programs(1) - 1)
    def _():
        for q in range(pairs):
            head = 2 * (p * pairs + q)
            sout_ref[0, head] = states[q][:RWKV_HEAD, :RWKV_HEAD]
            sout_ref[0, head + 1] = states[q][RWKV_HEAD:, RWKV_HEAD:]

    y = jnp.concatenate([jnp.concatenate(yq, axis=0) for yq in ys], axis=1)
    o_ref[...] = _rwkv_out(y, bonus, g, lnw_ref[...], lnb_ref[...], ones_bd)


RWKV_PAIRS_PER_STEP = 4


def _rwkv_prompt(proj, wts, batch, seq):
    tile = math.gcd(seq, 256)
    n_tiles = seq // tile
    width = RWKV_PAIRS_PER_STEP * LANES
    blk = lambda base: pl.BlockSpec((tile, width),
                                    lambda b, t, p, base=base: (b * n_tiles + t, base // width + p))
    per_pair = lambda rows: pl.BlockSpec((rows, width), lambda b, t, p: (0, p))
    in_specs = [
        blk(COL_R), blk(COL_KK), blk(COL_VV),
        pl.BlockSpec((tile, LORA_W), lambda b, t, p: (b * n_tiles + t, COL_LORA // LORA_W)),
        pl.BlockSpec((1, 3, width), lambda b, t, p: (0, 0, p)),
        pl.BlockSpec((1, LORA_W), lambda b, t, p: (0, 0)),
        per_pair(1), per_pair(LANES),
        per_pair(1), per_pair(LANES),
        per_pair(2 * LANES),
        per_pair(1), per_pair(1), per_pair(1),
        per_pair(1), per_pair(1),
    ]
    return pl.pallas_call(
        functools.partial(_rwkv_body, tile=tile),
        grid=(batch, n_tiles, RWKV_PAIRS // RWKV_PAIRS_PER_STEP),
        in_specs=in_specs,
        out_specs=[pl.BlockSpec((tile, width), lambda b, t, p: (b * n_tiles + t, p)),
                   pl.BlockSpec((1, RWKV_HEADS, RWKV_HEAD, RWKV_HEAD), lambda b, t, p: (b, 0, 0, 0))],
        out_shape=[jax.ShapeDtypeStruct((batch * seq, D_MODEL), F32),
                   jax.ShapeDtypeStruct((batch, RWKV_HEADS, RWKV_HEAD, RWKV_HEAD), F32)],
        scratch_shapes=[pltpu.VMEM((RWKV_PAIRS, LANES, LANES), F32),
                        pltpu.VMEM((RWKV_PAIRS // RWKV_PAIRS_PER_STEP, 8, width), F32),
                        pltpu.VMEM((8, LORA_W), F32)],
        compiler_params=_params(("parallel", "arbitrary", "arbitrary")),
        name="rwkv_prompt",
    )(proj, proj, proj, proj, wts["mu3"], wts["mu_lora"],
      wts["w0"], wts["w2"], wts["a0"], wts["a2"], wts["g2"],
      wts["k_k"], wts["k_a"], wts["r_k"], wts["lnx_w"], wts["lnx_b"])


def _rwkv_step_body(r_ref, k_ref, v_ref, lora_ref, pr_ref, pk_ref, pv_ref, plora_ref,
                    mu_ref, mul_ref, w0_ref, w2_ref, a0_ref, a2_ref, g2_ref,
                    kk_ref, ka_ref, rk_ref, lnw_ref, lnb_ref, s_ref,
                    o_ref, sout_ref, *, rows):
    ones_bd = _head_ones()
    mu = mu_ref[0]
    r_ = _lerp(r_ref[...], pr_ref[...], mu[0:1])
    k_ = _lerp(k_ref[...], pk_ref[...], mu[1:2])
    v_ = _lerp(v_ref[...], pv_ref[...], mu[2:3])
    lo = _lerp(lora_ref[...], plora_ref[...], mul_ref[...])
    log_decay, a_sig, g, kk, kmod, bonus = _rwkv_prep(
        r_, k_, v_, lo, w0_ref[...], w2_ref[...], a0_ref[...], a2_ref[...], g2_ref[...],
        kk_ref[...], ka_ref[...], rk_ref[...], ones_bd)
    decay = jnp.exp(log_decay)
    neg_kk = -kk
    kk_a = kk * a_sig
    decay_r = decay * r_
    row = lax.broadcasted_iota(jnp.int32, (rows, RWKV_HEAD), 0)

    heads = [slice(h * RWKV_HEAD, (h + 1) * RWKV_HEAD) for h in range(RWKV_HEADS)]
    reads = []
    for h, sl in enumerate(heads):
        lhs = jnp.concatenate([neg_kk[:, sl], decay_r[:, sl]], axis=0)
        reads.append([_mm(lhs, s_ref[b, h], _NT) for b in range(rows)])
    ys = []
    for h, sl in enumerate(heads):
        w_h, r_h, v_h, k_h, b_h = decay[:, sl], r_[:, sl], v_[:, sl], kmod[:, sl], kk_a[:, sl]
        rhs = jnp.concatenate([b_h, k_h], axis=0)
        sa_all = swr_all = jnp.zeros((rows, RWKV_HEAD), F32)
        adds = []
        for b in range(rows):
            mine = row == b
            sa, swr = reads[h][b][:rows], reads[h][b][rows:]
            left = jnp.concatenate([jnp.where(mine, sa, 0.0), jnp.where(mine, v_h, 0.0)], axis=0)
            adds.append(_mm(left, rhs, _TN))
            sa_all = jnp.where(mine, sa, sa_all)
            swr_all = jnp.where(mine, swr, swr_all)
        for b in range(rows):
            sout_ref[b, h] = s_ref[b, h] * w_h[b:b + 1] + adds[b]
        ys.append(swr_all + sa_all * jnp.sum(b_h * r_h, axis=-1, keepdims=True)
                  + v_h * jnp.sum(k_h * r_h, axis=-1, keepdims=True))
    y = jnp.concatenate(ys, axis=1)
    o_ref[...] = _rwkv_out(y, bonus, g, lnw_ref[...], lnb_ref[...], ones_bd)


def _rwkv_step(proj, proj_prev, s0, wts):
    batch = proj.shape[0]
    rows = 8
    wide = lambda base: pl.BlockSpec((rows, D_MODEL), lambda i, base=base: (i, base // D_MODEL))
    lora = pl.BlockSpec((rows, LORA_W), lambda i: (i, COL_LORA // LORA_W))
    full = lambda shape: pl.BlockSpec(shape, lambda i: (0,) * len(shape))
    state = pl.BlockSpec((rows, RWKV_HEADS, RWKV_HEAD, RWKV_HEAD), lambda i: (i, 0, 0, 0))
    vec = full((1, D_MODEL))
    return pl.pallas_call(
        functools.partial(_rwkv_step_body, rows=rows),
        grid=(batch // rows,),
        in_specs=[wide(COL_R), wide(COL_KK), wide(COL_VV), lora,
                  wide(COL_R), wide(COL_KK), wide(COL_VV), lora,
                  full((1, 3, D_MODEL)), full((1, LORA_W)),
                  vec, full((LANES, D_MODEL)), vec, full((LANES, D_MODEL)),
                  full((2 * LANES, D_MODEL)), vec, vec, vec, vec, vec, state],
        out_specs=[pl.BlockSpec((rows, D_MODEL), lambda i: (i, 0)), state],
        out_shape=[jax.ShapeDtypeStruct((batch, D_MODEL), F32),
                   jax.ShapeDtypeStruct(s0.shape, F32)],
        compiler_params=_params(("parallel",)),
        name="rwkv_step",
    )(proj, proj, proj, proj, proj_prev, proj_prev, proj_prev, proj_prev,
      wts["mu3"], wts["mu_lora"], wts["w0"], wts["w2"], wts["a0"], wts["a2"], wts["g2"],
      wts["k_k"], wts["k_a"], wts["r_k"], wts["lnx_w"], wts["lnx_b"], s0)


def _merge_body(x_ref, oret_ref, orwkv_ref, ga_ref, gb_ref, wout_ref, nffn_ref, rw_ref, rb_ref,
                x1_ref, xn_ref, comb_ref):
    merged = (jax.nn.sigmoid(ga_ref[...]) * oret_ref[...]
              + jax.nn.sigmoid(gb_ref[...]) * orwkv_ref[...])
    x1 = x_ref[...] + _mm(merged, wout_ref[...])
    x1_ref[...] = x1
    xn = _rms(x1, nffn_ref[...])
    xn_ref[...] = xn.astype(BF16)

    logits = _mm(xn, rw_ref[...]) + rb_ref[...]
    lane = lax.broadcasted_iota(jnp.int32, logits.shape, 1)
    neg_inf = -jnp.inf
    is_group = lane < N_GROUPS
    g_max = jnp.max(jnp.where(is_group, logits, neg_inf), axis=-1, keepdims=True)
    g_sel = jnp.min(jnp.where(is_group & (logits == g_max), lane, LANES), axis=-1, keepdims=True)
    g_w = 1.0 / jnp.sum(jnp.where(is_group, jnp.exp(logits - g_max), 0.0), axis=-1, keepdims=True)
    first = N_GROUPS + EXPERTS_PER_GROUP * g_sel
    in_group = (lane >= first) & (lane < first + EXPERTS_PER_GROUP)
    e_max = jnp.max(jnp.where(in_group, logits, neg_inf), axis=-1, keepdims=True)
    e_exp = jnp.where(in_group, jnp.exp(logits - e_max), 0.0)
    prob = e_exp / jnp.sum(e_exp, axis=-1, keepdims=True)
    prob = jnp.where(in_group, prob, -1.0)
    p1 = jnp.max(prob, axis=-1, keepdims=True)
    i1 = jnp.min(jnp.where(prob == p1, lane, LANES), axis=-1, keepdims=True)
    rest = jnp.where(lane == i1, -1.0, prob)
    p2 = jnp.max(rest, axis=-1, keepdims=True)
    i2 = jnp.min(jnp.where(rest == p2, lane, LANES), axis=-1, keepdims=True)
    denom = p1 + p2
    comb_ref[...] = (jnp.where(lane == i1 - N_GROUPS, g_w * p1 / denom, 0.0)
                     + jnp.where(lane == i2 - N_GROUPS, g_w * p2 / denom, 0.0))


def _merge(x, o_ret, o_rwkv, proj, wts):
    t = x.shape[0]
    tm = math.gcd(t, 512)
    row = lambda: pl.BlockSpec((tm, D_MODEL), lambda i: (i, 0))
    full = lambda shape: pl.BlockSpec(shape, lambda i: (0, 0))
    return pl.pallas_call(
        _merge_body,
        grid=(t // tm,),
        in_specs=[row(), row(), row(),
                  pl.BlockSpec((tm, D_MODEL), lambda i: (i, COL_GATE_A // D_MODEL)),
                  pl.BlockSpec((tm, D_MODEL), lambda i: (i, COL_GATE_B // D_MODEL)),
                  full((D_MODEL, D_MODEL)), full((1, D_MODEL)),
                  full((D_MODEL, LANES)), full((1, LANES))],
        out_specs=[row(), row(), pl.BlockSpec((tm, LANES), lambda i: (i, 0))],
        out_shape=[jax.ShapeDtypeStruct((t, D_MODEL), F32),
                   jax.ShapeDtypeStruct((t, D_MODEL), BF16),
                   jax.ShapeDtypeStruct((t, LANES), F32)],
        compiler_params=_params(("parallel",)),
        name="merge_router",
    )(x, o_ret, o_rwkv, proj, proj, wts["w_out"], wts["norm_ffn"], wts["router_w"], wts["router_b"])


def _moe_body(xn_ref, x1_ref, comb_ref, wg_ref, wu_ref, wd_ref, nf_ref, o_ref, acc_ref):
    e = pl.program_id(1)

    @pl.when(e == 0)
    def _():
        acc_ref[...] = jnp.zeros_like(acc_ref)

    comb = comb_ref[...]
    lane = lax.broadcasted_iota(jnp.int32, comb.shape, 1)
    weight = jnp.sum(jnp.where(lane == e, comb, 0.0), axis=-1, keepdims=True)
    xn = xn_ref[...]
    h = _silu(_mm(xn, wg_ref[0])) * _mm(xn, wu_ref[0]) * weight
    acc_ref[...] += _mm(h, wd_ref[0])

    @pl.when(e == pl.num_programs(1) - 1)
    def _():
        o_ref[...] = _rms(x1_ref[...] + acc_ref[...], nf_ref[...])


def _moe(xn, x1, comb, wts):
    t = xn.shape[0]
    tm = math.gcd(t, 1024)
    row = lambda w: pl.BlockSpec((tm, w), lambda i, e: (i, 0))
    return pl.pallas_call(
        _moe_body,
        grid=(t // tm, N_EXPERTS),
        in_specs=[row(D_MODEL), row(D_MODEL), row(LANES),
                  pl.BlockSpec((1, D_MODEL, D_EXPERT), lambda i, e: (e, 0, 0)),
                  pl.BlockSpec((1, D_MODEL, D_EXPERT), lambda i, e: (e, 0, 0)),
                  pl.BlockSpec((1, D_EXPERT, D_MODEL), lambda i, e: (e, 0, 0)),
                  pl.BlockSpec((1, D_MODEL), lambda i, e: (0, 0))],
        out_specs=row(D_MODEL),
        out_shape=jax.ShapeDtypeStruct((t, D_MODEL), F32),
        scratch_shapes=[pltpu.VMEM((tm, D_MODEL), F32)],
        compiler_params=_params(("parallel", "arbitrary")),
        name="moe",
    )(xn, x1, comb, wts["w_gate"], wts["w_up"], wts["w_down"], wts["norm_final"])


def _pad_to(x, size, axis):
    pad = [(0, 0)] * x.ndim
    pad[axis] = (0, size - x.shape[axis])
    return jnp.pad(x, pad)


def _prepare_weights(norm_mix, w_in, mu_shift, ret_gn_w, rwkv_w0, rwkv_w2, rwkv_a0, rwkv_a2,
                     rwkv_g2, rwkv_k_k, rwkv_k_a, rwkv_r_k, rwkv_lnx_w, rwkv_lnx_b, w_out,
                     norm_ffn, router_group_w, router_group_b, router_expert_w, router_expert_b,
                     expert_w_gate, expert_w_up, expert_w_down, norm_final):
    def lora_cols(t):
        return jnp.concatenate([_pad_to(t[..., SRC_WLO:SRC_ALO], 128, -1),
                                _pad_to(t[..., SRC_ALO:SRC_GLO], 128, -1),
                                _pad_to(t[..., SRC_GLO:SRC_GATE_A], 256, -1)], axis=-1)

    w = w_in[0]
    w_in_p = jnp.concatenate([w[:, :SRC_WLO], w[:, SRC_GATE_A:], lora_cols(w)], axis=1).astype(BF16)
    mu = jnp.concatenate([jnp.zeros((SRC_SHIFT,), F32), mu_shift[0]])
    row = lambda v: v.reshape(1, -1)
    router_w = _pad_to(jnp.concatenate([router_group_w[0], router_expert_w[0]], axis=1), LANES, 1)
    router_b = _pad_to(jnp.concatenate([router_group_b[0], router_expert_b[0]]), LANES, 0)
    return dict(
        w_in=w_in_p, norm_mix=row(norm_mix[0]), ret_gn_w=row(ret_gn_w[0]),
        mu3=mu[SRC_SHIFT:SRC_WLO].reshape(1, 3, D_MODEL), mu_lora=lora_cols(mu).reshape(1, LORA_W),
        w0=row(rwkv_w0[0]), w2=_pad_to(rwkv_w2[0], 128, 0),
        a0=row(rwkv_a0[0]), a2=_pad_to(rwkv_a2[0], 128, 0),
        g2=_pad_to(rwkv_g2[0], 256, 0),
        k_k=row(rwkv_k_k[0]), k_a=row(rwkv_k_a[0]), r_k=row(rwkv_r_k[0]),
        lnx_w=row(rwkv_lnx_w[0]), lnx_b=row(rwkv_lnx_b[0]),
        w_out=w_out[0].astype(BF16), norm_ffn=row(norm_ffn[0]),
        router_w=router_w, router_b=row(router_b),
        w_gate=expert_w_gate[0].astype(BF16), w_up=expert_w_up[0].astype(BF16),
        w_down=expert_w_down[0].astype(BF16), norm_final=row(norm_final))


def _finish(x, o_ret, o_rwkv, proj, wts):
    x1, xn2, comb = _merge(x, o_ret, o_rwkv, proj, wts)
    return _moe(xn2, x1, comb, wts)


def kernel(x_prompt, x_sample, state_ret, state_rwkv, state_shift, norm_mix, w_in, mu_shift, ret_gn_w, rwkv_w0, rwkv_w2, rwkv_a0, rwkv_a2, rwkv_g2, rwkv_k_k, rwkv_k_a, rwkv_r_k, rwkv_lnx_w, rwkv_lnx_b, w_out, norm_ffn, router_group_w, router_group_b, router_expert_w, router_expert_b, expert_w_gate, expert_w_up, expert_w_down, norm_final):
    assert norm_mix.shape[0] == 1, "single-layer step"
    wts = _prepare_weights(norm_mix, w_in, mu_shift, ret_gn_w, rwkv_w0, rwkv_w2, rwkv_a0, rwkv_a2,
                           rwkv_g2, rwkv_k_k, rwkv_k_a, rwkv_r_k, rwkv_lnx_w, rwkv_lnx_b, w_out,
                           norm_ffn, router_group_w, router_group_b, router_expert_w,
                           router_expert_b, expert_w_gate, expert_w_up, expert_w_down, norm_final)

    bp, lp, _ = x_prompt.shape
    xp = x_prompt.reshape(bp * lp, D_MODEL)
    proj_p, xn_p = _inproj(xp, wts["norm_mix"], wts["w_in"], apply_norm=True)
    o_ret_p, ret_p = _retention_prompt(proj_p, wts["ret_gn_w"], bp, lp)
    o_rwkv_p, rwkv_p = _rwkv_prompt(proj_p, wts, bp, lp)
    y_prompt = _finish(xp, o_ret_p, o_rwkv_p, proj_p, wts).reshape(bp, lp, D_MODEL)
    shift_p = xn_p.reshape(bp, lp, D_MODEL)[:, -1]

    bs, ls, _ = x_sample.shape
    assert ls == 1, "sample group advances one token"
    xs = x_sample.reshape(bs, D_MODEL)
    proj_s, xn_s = _inproj(xs, wts["norm_mix"], wts["w_in"], apply_norm=True)
    proj_prev, _ = _inproj(state_shift[0], wts["norm_mix"], wts["w_in"], apply_norm=False)
    pos_s = PAST_LEN + jnp.arange(ls, dtype=F32)
    o_ret_s, ret_s = _retention_step(proj_s, state_ret[0], wts["ret_gn_w"], pos_s)
    o_rwkv_s, rwkv_s = _rwkv_step(proj_s, proj_prev, state_rwkv[0], wts)
    y_sample = _finish(xs, o_ret_s, o_rwkv_s, proj_s, wts).reshape(bs, ls, D_MODEL)

    return (y_prompt, y_sample, ret_p[None], rwkv_p[None], shift_p[None],
            ret_s[None], rwkv_s[None], xn_s[None])
```

```python
import functools
import math

import jax
import jax.numpy as jnp
from jax import lax
from jax.experimental import pallas as pl
from jax.experimental.pallas import tpu as pltpu

F32 = jnp.float32
BF16 = jnp.bfloat16

D_MODEL = 1024
LANES = 128
PAST_LEN = 16384
RET_HEADS = 8
RET_D = D_MODEL // RET_HEADS
RET_CHUNK = 128
ROPE_BASE = 10000.0
RET_GN_EPS = 1e-5
RWKV_HEAD = 64
RWKV_HEADS = D_MODEL // RWKV_HEAD
RWKV_PAIRS = RWKV_HEADS // 2
RWKV_CHUNK = 64
LORA_DECAY = 64
LORA_A = 64
LORA_G = 160
RWKV_GN_EPS = 64e-5
N_GROUPS = 4
EXPERTS_PER_GROUP = 8
N_EXPERTS = N_GROUPS * EXPERTS_PER_GROUP
D_EXPERT = 256
RMS_EPS = 1e-6

SRC_SHIFT = 4 * D_MODEL
SRC_WLO = SRC_SHIFT + 3 * D_MODEL
SRC_ALO = SRC_WLO + LORA_DECAY
SRC_GLO = SRC_ALO + LORA_A
SRC_GATE_A = SRC_GLO + LORA_G
COL_Q, COL_K, COL_V, COL_GSW = 0, 1024, 2048, 3072
COL_R, COL_KK, COL_VV = 4096, 5120, 6144
COL_GATE_A, COL_GATE_B = 7168, 8192
COL_LORA = 9216
LORA_W = 512
N_PROJ = COL_LORA + LORA_W

VMEM_LIMIT = 48 * 1024 * 1024

_NN = (((1,), (0,)), ((), ()))
_NT = (((1,), (1,)), ((), ()))
_TN = (((0,), (0,)), ((), ()))


def _mm(a, b, dims=_NN, exact=False):
    if exact:
        return lax.dot_general(a.astype(F32), b.astype(F32), dims,
                               precision=lax.Precision.HIGHEST,
                               preferred_element_type=F32)
    return lax.dot_general(a.astype(BF16), b.astype(BF16), dims,
                           preferred_element_type=F32)


def _params(sem):
    return pltpu.CompilerParams(dimension_semantics=sem,
                                vmem_limit_bytes=VMEM_LIMIT)


def _rms(x, gain):
    return x * lax.rsqrt(jnp.mean(x * x, axis=-1, keepdims=True) + RMS_EPS) * gain


def _norm_body(x_ref, g_ref, xb_ref, tail_ref, *, keep):
    x = _rms(x_ref[...], g_ref[...])
    xb_ref[...] = x.astype(BF16)
    tail_ref[0] = x[x.shape[0] - keep:, :]


def _mixer_norm(x, gain, tm, keep):
    t = x.shape[0]
    return pl.pallas_call(
        functools.partial(_norm_body, keep=keep),
        grid=(t // tm,),
        in_specs=[pl.BlockSpec((tm, D_MODEL), lambda i: (i, 0)),
                  pl.BlockSpec((1, D_MODEL), lambda i: (0, 0))],
        out_specs=[pl.BlockSpec((tm, D_MODEL), lambda i: (i, 0)),
                   pl.BlockSpec((1, keep, D_MODEL), lambda i: (i, 0, 0))],
        out_shape=[jax.ShapeDtypeStruct((t, D_MODEL), BF16),
                   jax.ShapeDtypeStruct((t // tm, keep, D_MODEL), F32)],
        compiler_params=_params(("parallel",)),
        name="mixer_norm",
    )(x, gain)


def _inproj_body(x_ref, w_ref, o_ref):
    o_ref[...] = jnp.dot(x_ref[...], w_ref[...], preferred_element_type=F32)


def _inproj(xb, w):
    t = xb.shape[0]
    n = w.shape[1]
    tm = math.gcd(t, 1024)
    tn = n // 4
    return pl.pallas_call(
        _inproj_body,
        grid=(n // tn, t // tm),
        in_specs=[pl.BlockSpec((tm, D_MODEL), lambda j, i: (i, 0)),
                  pl.BlockSpec((D_MODEL, tn), lambda j, i: (0, j), pipeline_mode=pl.Buffered(1))],
        out_specs=pl.BlockSpec((tm, tn), lambda j, i: (i, j)),
        out_shape=jax.ShapeDtypeStruct((t, n), F32),
        compiler_params=_params(("arbitrary", "arbitrary")),
        name="inproj",
    )(xb, w)


def _head_norm(x, eps):
    mu = jnp.mean(x, axis=-1, keepdims=True)
    xc = x - mu
    return xc * lax.rsqrt(jnp.mean(xc * xc, axis=-1, keepdims=True) + eps)


def _silu(x):
    return x * jax.nn.sigmoid(x)


def _rope(x, cos, sin_signed):
    return x * cos + pltpu.roll(x, RET_D // 2, 1) * sin_signed


def _ret_tables(chunk):
    log_gamma = jnp.log1p(-jnp.exp2(-5.0 - jnp.arange(RET_HEADS, dtype=F32)))
    idx = jnp.arange(chunk, dtype=F32)
    rel = idx[:, None] - idx[None, :]
    intra = jnp.where(rel[None] >= 0,
                      jnp.exp(jnp.maximum(rel, 0.0)[None] * log_gamma[:, None, None]), 0.0)
    q_dec = jnp.exp((idx[:, None] + 1.0) * log_gamma[None, :])
    k_dec = jnp.exp((chunk - 1.0 - idx)[None, :] * log_gamma[:, None]).T
    chunk_dec = jnp.exp(chunk * log_gamma)[None, :]
    widen = lambda t: jnp.repeat(t, RET_D, axis=1)
    return intra, widen(q_dec), widen(k_dec), widen(chunk_dec)


def _rope_tables(pos):
    half = RET_D // 2
    inv = ROPE_BASE ** (-jnp.arange(half, dtype=F32) / half)
    ang = pos[:, None] * inv[None, :]
    cos, sin = jnp.cos(ang), jnp.sin(ang)
    return jnp.concatenate([cos, cos], axis=1), jnp.concatenate([-sin, sin], axis=1)


def _ret_body(q_ref, k_ref, v_ref, g_ref, cos_ref, sin_ref, qd_ref, kd_ref, cd_ref,
              intra_ref, gnw_ref, o_ref, sout_ref, s_ref):
    c = pl.program_id(1)

    @pl.when(c == 0)
    def _():
        s_ref[...] = jnp.zeros_like(s_ref)

    cos = cos_ref[...]
    sin = sin_ref[...]
    heads = [slice(h * RET_D, (h + 1) * RET_D) for h in range(RET_HEADS)]
    qr = [_rope(q_ref[:, sl], cos, sin).astype(BF16) for sl in heads]
    kr = [_rope(k_ref[:, sl], cos, sin) * (RET_D ** -0.5) for sl in heads]
    vb = [v_ref[:, sl].astype(BF16) for sl in heads]
    scores = [_mm(qr[h], kr[h], _NT) * intra_ref[h] for h in range(RET_HEADS)]
    cross = [_mm(qr[h], s_ref[h]) * qd_ref[:, sl] for h, sl in enumerate(heads)]
    adds = [_mm(kr[h] * kd_ref[:, sl], vb[h], _TN) for h, sl in enumerate(heads)]
    inner = [_mm(scores[h], vb[h]) for h in range(RET_HEADS)]
    for h, sl in enumerate(heads):
        s_ref[h] = s_ref[h] * cd_ref[:, sl] + adds[h]
        o_ref[:, sl] = (_head_norm(inner[h] + cross[h], RET_GN_EPS) * gnw_ref[:, sl]
                        * _silu(g_ref[:, sl]))

    @pl.when(c == pl.num_programs(1) - 1)
    def _():
        sout_ref[0] = s_ref[...]


def _retention_prompt(proj, gn_w, batch, seq):
    chunk = math.gcd(seq, RET_CHUNK)
    n_chunks = seq // chunk
    intra, q_dec, k_dec, chunk_dec = _ret_tables(chunk)
    cos, sin = _rope_tables(jnp.arange(seq, dtype=F32))
    col = lambda j: pl.BlockSpec((chunk, D_MODEL), lambda b, c, j=j: (b * n_chunks + c, j))
    const2 = lambda shape: pl.BlockSpec(shape, lambda b, c: (0, 0))
    state = pl.BlockSpec((1, RET_HEADS, RET_D, RET_D), lambda b, c: (b, 0, 0, 0))
    return pl.pallas_call(
        _ret_body,
        grid=(batch, n_chunks),
        in_specs=[col(COL_Q // D_MODEL), col(COL_K // D_MODEL), col(COL_V // D_MODEL),
                  col(COL_GSW // D_MODEL),
                  pl.BlockSpec((chunk, RET_D), lambda b, c: (c, 0)),
                  pl.BlockSpec((chunk, RET_D), lambda b, c: (c, 0)),
                  const2((chunk, D_MODEL)), const2((chunk, D_MODEL)), const2((1, D_MODEL)),
                  pl.BlockSpec((RET_HEADS, chunk, chunk), lambda b, c: (0, 0, 0)),
                  const2((1, D_MODEL))],
        out_specs=[pl.BlockSpec((chunk, D_MODEL), lambda b, c: (b * n_chunks + c, 0)), state],
        out_shape=[jax.ShapeDtypeStruct((batch * seq, D_MODEL), F32),
                   jax.ShapeDtypeStruct((batch, RET_HEADS, RET_D, RET_D), F32)],
        scratch_shapes=[pltpu.VMEM((RET_HEADS, RET_D, RET_D), F32)],
        compiler_params=_params(("parallel", "arbitrary")),
        name="retention_prompt",
    )(proj, proj, proj, proj, cos, sin, q_dec, k_dec, chunk_dec, intra, gn_w)


def _ret_step_body(q_ref, k_ref, v_ref, g_ref, cos_ref, sin_ref, qd_ref, kd_ref, cd_ref,
                   gnw_ref, s_ref, o_ref, sout_ref, *, rows):
    cos = cos_ref[...]
    sin = sin_ref[...]
    row = lax.broadcasted_iota(jnp.int32, (rows, RET_D), 0)
    for h in range(RET_HEADS):
        sl = slice(h * RET_D, (h + 1) * RET_D)
        qr = _rope(q_ref[:, sl], cos, sin)
        kr = _rope(k_ref[:, sl], cos, sin) * (RET_D ** -0.5)
        v = v_ref[:, sl]
        kd = kr * kd_ref[:, sl]
        cd = cd_ref[:, sl]

        reads = [_mm(qr, s_ref[b, h]) for b in range(rows)]
        adds = [_mm(jnp.where(row == b, kd, 0.0), v, _TN) for b in range(rows)]
        cross = jnp.zeros((rows, RET_D), F32)
        for b in range(rows):
            sout_ref[b, h] = s_ref[b, h] * cd + adds[b]
            cross = jnp.where(row == b, reads[b], cross)
        out = jnp.sum(qr * kr, axis=-1, keepdims=True) * v + cross * qd_ref[:, sl]
        o_ref[:, sl] = _head_norm(out, RET_GN_EPS) * gnw_ref[:, sl] * _silu(g_ref[:, sl])


def _retention_step(proj, s0, gn_w, pos):
    batch = proj.shape[0]
    rows = 8
    _, q_dec, k_dec, chunk_dec = _ret_tables(1)
    cos, sin = _rope_tables(pos)
    col = lambda j: pl.BlockSpec((rows, D_MODEL), lambda i, j=j: (i, j))
    const2 = lambda shape: pl.BlockSpec(shape, lambda i: (0, 0))
    state = pl.BlockSpec((rows, RET_HEADS, RET_D, RET_D), lambda i: (i, 0, 0, 0))
    return pl.pallas_call(
        functools.partial(_ret_step_body, rows=rows),
        grid=(batch // rows,),
        in_specs=[col(COL_Q // D_MODEL), col(COL_K // D_MODEL), col(COL_V // D_MODEL),
                  col(COL_GSW // D_MODEL),
                  const2((1, RET_D)), const2((1, RET_D)),
                  const2((1, D_MODEL)), const2((1, D_MODEL)), const2((1, D_MODEL)),
                  const2((1, D_MODEL)), state],
        out_specs=[pl.BlockSpec((rows, D_MODEL), lambda i: (i, 0)), state],
        out_shape=[jax.ShapeDtypeStruct((batch, D_MODEL), F32),
                   jax.ShapeDtypeStruct(s0.shape, F32)],
        compiler_params=_params(("parallel",)),
        name="retention_step",
    )(proj, proj, proj, proj, cos, sin, q_dec, k_dec, chunk_dec, gn_w, s0)


def _head_ones():
    row = lax.broadcasted_iota(jnp.int32, (LANES, LANES), 0)
    col = lax.broadcasted_iota(jnp.int32, (LANES, LANES), 1)
    return jnp.where((row // RWKV_HEAD) == (col // RWKV_HEAD), 1.0, 0.0).astype(BF16)


def _split_bf16(x, terms):
    parts = []
    for _ in range(terms - 1):
        hi = x.astype(BF16)
        parts.append(hi)
        x = x - hi.astype(F32)
    parts.append(x.astype(BF16))
    return parts


def _head_sum(x, ones_bd):
    def one(blk):
        hi, lo = _split_bf16(blk, 2)
        return (jnp.dot(hi, ones_bd, preferred_element_type=F32)
                + jnp.dot(lo, ones_bd, preferred_element_type=F32))
    blocks = [one(x[:, j:j + LANES]) for j in range(0, x.shape[1], LANES)]
    return blocks[0] if len(blocks) == 1 else jnp.concatenate(blocks, axis=1)


def _softplus(x):
    return jnp.maximum(x, 0.0) + jnp.log1p(jnp.exp(-jnp.abs(x)))


def _lerp(cur, prev, mu):
    return cur + (prev - cur) * mu


def _rwkv_prep(r_, k_, v_, lora, w0, w2, a0, a2, g2, k_k, k_a, r_k, ones_bd):
    w_lo, a_lo, g_lo = lora[:, 0:128], lora[:, 128:256], lora[:, 256:512]
    w_log = -_softplus(-(w0 + _mm(jnp.tanh(w_lo), w2))) - 0.5
    log_decay = -jnp.exp(w_log)
    a_sig = jax.nn.sigmoid(a0 + _mm(a_lo, a2))
    g = _mm(jax.nn.sigmoid(g_lo), g2)
    kk = k_ * k_k
    kk = kk / jnp.maximum(jnp.sqrt(_head_sum(kk * kk, ones_bd)), 1e-12)
    kmod = k_ * (1.0 + (a_sig - 1.0) * k_a)
    bonus = _head_sum(r_ * kmod * r_k, ones_bd) * v_
    return log_decay, a_sig, g, kk, kmod, bonus


def _rwkv_out(y, bonus, g, lnx_w, lnx_b, ones_bd):
    inv_n = 1.0 / RWKV_HEAD
    yc = y - _head_sum(y, ones_bd) * inv_n
    var = _head_sum(yc * yc, ones_bd) * inv_n
    return (yc * lax.rsqrt(var + RWKV_GN_EPS) * lnx_w + lnx_b + bonus) * g


def _block_diag(x, lane_first):
    return jnp.concatenate([jnp.where(lane_first, x, 0.0), jnp.where(lane_first, 0.0, x)], axis=0)


def _pair_masks():
    n = 2 * RWKV_CHUNK
    row = lax.broadcasted_iota(jnp.int32, (n, n), 0)
    col = lax.broadcasted_iota(jnp.int32, (n, n), 1)
    same = (row // RWKV_CHUNK) == (col // RWKV_CHUNK)
    lane_first = lax.broadcasted_iota(jnp.int32, (1, LANES), 1) < RWKV_HEAD
    return lane_first, same & (col < row), same & (col <= row), row == col


def _rwkv_chunks(chunks, masks):
    lane_first, strict, incl, eye = masks
    n = 2 * RWKV_CHUNK
    rows = lambda x, y: jnp.concatenate([x, y], axis=0)
    cols = lambda x, y: jnp.concatenate([x, y], axis=1)
    bd = lambda x: _block_diag(x, lane_first)

    ops = []
    for r, lw, k, v, a, b, cum in chunks:
        p_inc = jnp.exp(cum)
        p_inv = jnp.exp(-cum)
        p_exc = jnp.exp(cum - lw)
        p_end = p_inc[RWKV_CHUNK - 1:RWKV_CHUNK, :]
        b_t = b * p_inv
        k_t = k * p_inv
        ops.append(dict(a_t=bd(a * p_exc), r_t=bd(r * p_inc), b_t=bd(b_t), k_t=bd(k_t),
                        b_hat=bd(b_t * p_end), k_hat=bd(k_t * p_end), v_bd=bd(v), p_end=p_end))

    for o in ops:
        prod = _mm(rows(o["a_t"], o["r_t"]), rows(o["b_t"], o["k_t"]), _NT)
        o["a_ab"] = jnp.where(strict, prod[:n, :n], 0.0)
        o["a_ak"] = jnp.where(strict, prod[:n, n:], 0.0)
        o["r_b"] = jnp.where(incl, prod[n:, :n], 0.0)
        o["r_k"] = jnp.where(incl, prod[n:, n:], 0.0)

    for o in ops:
        o["t_inv"] = jnp.where(eye, 1.0, 0.0) + o["a_ab"]
        o["power"] = _mm(o["a_ab"], o["a_ab"])
        o["av_rv"] = _mm(rows(o["a_ak"], o["r_k"]), o["v_bd"])
    for _ in range(int(math.log2(RWKV_CHUNK)) - 2):
        for o in ops:
            both = _mm(rows(o["power"], o["t_inv"]), o["power"])
            o["power"] = both[:n]
            o["t_inv"] = o["t_inv"] + both[n:]
    for o in ops:
        o["t_inv"] = o["t_inv"] + _mm(o["t_inv"], o["power"])
    for o in ops:
        o["wu"] = _mm(o["t_inv"], cols(o["a_t"], o["av_rv"][:n]))
    out = []
    for o in ops:
        wu = o["wu"]
        ry_y0 = _mm(o["r_b"], wu)
        r_y = o["r_t"] + ry_y0[:, :LANES]
        y0 = ry_y0[:, LANES:] + o["av_rv"][n:]
        psi = _mm(wu[:, :LANES], o["b_hat"], _TN) + jnp.where(eye, o["p_end"], 0.0)
        s_add = _mm(rows(wu[:, LANES:], o["v_bd"]), rows(o["b_hat"], o["k_hat"]), _TN)
        out.append((r_y, y0, psi, s_add))
    return out


def _shift_rows(cur, first_row):
    rolled = pltpu.roll(cur, 1, 0)
    row = lax.broadcasted_iota(jnp.int32, cur.shape, 0)
    return jnp.where(row == 0, first_row, rolled)


def _rwkv_body(r_ref, k_ref, v_ref, lora_ref, mu_ref, mul_ref,
               w0_ref, w2_ref, a0_ref, a2_ref, g2_ref, kk_ref, ka_ref, rk_ref,
               lnw_ref, lnb_ref, o_ref, sout_ref,
               s_ref, carry_ref, carryl_ref, *, tile):
    t = pl.program_id(1)
    p = pl.program_id(2)
    masks = _pair_masks()
    ones_bd = _head_ones()
    pairs = r_ref.shape[1] // LANES

    @pl.when(t == 0)
    def _():
        for q in range(pairs):
            s_ref[p * pairs + q] = jnp.zeros((LANES, LANES), F32)
        carry_ref[p] = jnp.zeros(carry_ref.shape[1:], F32)

        @pl.when(p == 0)
        def _():
            carryl_ref[...] = jnp.zeros_like(carryl_ref)

    prev = carry_ref[p]
    cur_r, cur_k, cur_v, cur_l = r_ref[...], k_ref[...], v_ref[...], lora_ref[...]
    mu = mu_ref[0]
    r_ = _lerp(cur_r, _shift_rows(cur_r, prev[0:1]), mu[0:1])
    k_ = _lerp(cur_k, _shift_rows(cur_k, prev[1:2]), mu[1:2])
    v_ = _lerp(cur_v, _shift_rows(cur_v, prev[2:3]), mu[2:3])
    lo = _lerp(cur_l, _shift_rows(cur_l, carryl_ref[0:1, :]), mul_ref[...])
    carry_ref[p, 0:1, :] = cur_r[tile - 1:tile]
    carry_ref[p, 1:2, :] = cur_k[tile - 1:tile]
    carry_ref[p, 2:3, :] = cur_v[tile - 1:tile]

    @pl.when(p == pl.num_programs(2) - 1)
    def _():
        carryl_ref[0:1, :] = cur_l[tile - 1:tile]

    log_decay, a_sig, g, kk, kmod, bonus = _rwkv_prep(
        r_, k_, v_, lo, w0_ref[...], w2_ref[...], a0_ref[...], a2_ref[...], g2_ref[...],
        kk_ref[...], ka_ref[...], rk_ref[...], ones_bd)

    row = lax.broadcasted_iota(jnp.int32, (tile, tile), 0)
    col = lax.broadcasted_iota(jnp.int32, (tile, tile), 1)
    tri = jnp.where(((row // RWKV_CHUNK) == (col // RWKV_CHUNK)) & (col <= row), 1.0, 0.0).astype(BF16)
    cum = sum(jnp.dot(tri, part, preferred_element_type=F32) for part in _split_bf16(log_decay, 3))

    neg_kk = -kk
    kk_a = kk * a_sig
    n_chunks = tile // RWKV_CHUNK
    chunks = []
    for c in range(n_chunks):
        for q in range(pairs):
            at = (slice(c * RWKV_CHUNK, (c + 1) * RWKV_CHUNK), slice(q * LANES, (q + 1) * LANES))
            chunks.append((r_[at], log_decay[at], kmod[at], v_[at], neg_kk[at], kk_a[at], cum[at]))
    parts = _rwkv_chunks(chunks, masks)

    states = [s_ref[p * pairs + q] for q in range(pairs)]
    ys = [[] for _ in range(pairs)]
    for c in range(n_chunks):
        for q in range(pairs):
            r_y, y0, psi, s_add = parts[c * pairs + q]
            y_bd = _mm(r_y, states[q], _NT) + y0
            states[q] = _mm(states[q], psi) + s_add
            ys[q].append(y_bd[:RWKV_CHUNK] + y_bd[RWKV_CHUNK:])
    for q in range(pairs):
        s_ref[p * pairs + q] = states[q]

    @pl.when(t == pl.num_programs(1) - 1)
    def _():
        for q in range(pairs):
            head = 2 * (p * pairs + q)
            sout_ref[0, head] = states[q][:RWKV_HEAD, :RWKV_HEAD]
            sout_ref[0, head + 1] = states[q][RWKV_HEAD:, RWKV_HEAD:]

    y = jnp.concatenate([jnp.concatenate(yq, axis=0) for yq in ys], axis=1)
    o_ref[...] = _rwkv_out(y, bonus, g, lnw_ref[...], lnb_ref[...], ones_bd)


RWKV_PAIRS_PER_STEP = 4


def _rwkv_prompt(proj, wts, batch, seq):
    tile = math.gcd(seq, 256)
    n_tiles = seq // tile
    width = RWKV_PAIRS_PER_STEP * LANES
    blk = lambda base: pl.BlockSpec((tile, width),
                                    lambda b, t, p, base=base: (b * n_tiles + t, base // width + p))
    per_pair = lambda rows: pl.BlockSpec((rows, width), lambda b, t, p: (0, p))
    in_specs = [
        blk(COL_R), blk(COL_KK), blk(COL_VV),
        pl.BlockSpec((tile, LORA_W), lambda b, t, p: (b * n_tiles + t, COL_LORA // LORA_W)),
        pl.BlockSpec((1, 3, width), lambda b, t, p: (0, 0, p)),
        pl.BlockSpec((1, LORA_W), lambda b, t, p: (0, 0)),
        per_pair(1), per_pair(LANES),
        per_pair(1), per_pair(LANES),
        per_pair(2 * LANES),
        per_pair(1), per_pair(1), per_pair(1),
        per_pair(1), per_pair(1),
    ]
    return pl.pallas_call(
        functools.partial(_rwkv_body, tile=tile),
        grid=(batch, n_tiles, RWKV_PAIRS // RWKV_PAIRS_PER_STEP),
        in_specs=in_specs,
        out_specs=[pl.BlockSpec((tile, width), lambda b, t, p: (b * n_tiles + t, p)),
                   pl.BlockSpec((1, RWKV_HEADS, RWKV_HEAD, RWKV_HEAD), lambda b, t, p: (b, 0, 0, 0))],
        out_shape=[jax.ShapeDtypeStruct((batch * seq, D_MODEL), F32),
                   jax.ShapeDtypeStruct((batch, RWKV_HEADS, RWKV_HEAD, RWKV_HEAD), F32)],
        scratch_shapes=[pltpu.VMEM((RWKV_PAIRS, LANES, LANES), F32),
                        pltpu.VMEM((RWKV_PAIRS // RWKV_PAIRS_PER_STEP, 8, width), F32),
                        pltpu.VMEM((8, LORA_W), F32)],
        compiler_params=_params(("parallel", "arbitrary", "arbitrary")),
        name="rwkv_prompt",
    )(proj, proj, proj, proj, wts["mu3"], wts["mu_lora"],
      wts["w0"], wts["w2"], wts["a0"], wts["a2"], wts["g2"],
      wts["k_k"], wts["k_a"], wts["r_k"], wts["lnx_w"], wts["lnx_b"])


def _rwkv_step_body(r_ref, k_ref, v_ref, lora_ref, pr_ref, pk_ref, pv_ref, plora_ref,
                    mu_ref, mul_ref, w0_ref, w2_ref, a0_ref, a2_ref, g2_ref,
                    kk_ref, ka_ref, rk_ref, lnw_ref, lnb_ref, s_ref,
                    o_ref, sout_ref, *, rows):
    ones_bd = _head_ones()
    mu = mu_ref[0]
    r_ = _lerp(r_ref[...], pr_ref[...], mu[0:1])
    k_ = _lerp(k_ref[...], pk_ref[...], mu[1:2])
    v_ = _lerp(v_ref[...], pv_ref[...], mu[2:3])
    lo = _lerp(lora_ref[...], plora_ref[...], mul_ref[...])
    log_decay, a_sig, g, kk, kmod, bonus = _rwkv_prep(
        r_, k_, v_, lo, w0_ref[...], w2_ref[...], a0_ref[...], a2_ref[...], g2_ref[...],
        kk_ref[...], ka_ref[...], rk_ref[...], ones_bd)
    decay = jnp.exp(log_decay)
    neg_kk = -kk
    kk_a = kk * a_sig
    decay_r = decay * r_
    row = lax.broadcasted_iota(jnp.int32, (rows, RWKV_HEAD), 0)

    heads = [slice(h * RWKV_HEAD, (h + 1) * RWKV_HEAD) for h in range(RWKV_HEADS)]
    reads = []
    for h, sl in enumerate(heads):
        lhs = jnp.concatenate([neg_kk[:, sl], decay_r[:, sl]], axis=0)
        reads.append([_mm(lhs, s_ref[b, h], _NT) for b in range(rows)])
    ys = []
    for h, sl in enumerate(heads):
        w_h, r_h, v_h, k_h, b_h = decay[:, sl], r_[:, sl], v_[:, sl], kmod[:, sl], kk_a[:, sl]
        rhs = jnp.concatenate([b_h, k_h], axis=0)
        sa_all = swr_all = jnp.zeros((rows, RWKV_HEAD), F32)
        adds = []
        for b in range(rows):
            mine = row == b
            sa, swr = reads[h][b][:rows], reads[h][b][rows:]
            left = jnp.concatenate([jnp.where(mine, sa, 0.0), jnp.where(mine, v_h, 0.0)], axis=0)
            adds.append(_mm(left, rhs, _TN))
            sa_all = jnp.where(mine, sa, sa_all)
            swr_all = jnp.where(mine, swr, swr_all)
        for b in range(rows):
            sout_ref[b, h] = s_ref[b, h] * w_h[b:b + 1] + adds[b]
        ys.append(swr_all + sa_all * jnp.sum(b_h * r_h, axis=-1, keepdims=True)
                  + v_h * jnp.sum(k_h * r_h, axis=-1, keepdims=True))
    y = jnp.concatenate(ys, axis=1)
    o_ref[...] = _rwkv_out(y, bonus, g, lnw_ref[...], lnb_ref[...], ones_bd)


def _rwkv_step(proj, proj_prev, s0, wts):
    batch = proj.shape[0]
    rows = 8
    wide = lambda base: pl.BlockSpec((rows, D_MODEL), lambda i, base=base: (i, base // D_MODEL))
    lora = pl.BlockSpec((rows, LORA_W), lambda i: (i, COL_LORA // LORA_W))
    full = lambda shape: pl.BlockSpec(shape, lambda i: (0,) * len(shape))
    state = pl.BlockSpec((rows, RWKV_HEADS, RWKV_HEAD, RWKV_HEAD), lambda i: (i, 0, 0, 0))
    vec = full((1, D_MODEL))
    return pl.pallas_call(
        functools.partial(_rwkv_step_body, rows=rows),
        grid=(batch // rows,),
        in_specs=[wide(COL_R), wide(COL_KK), wide(COL_VV), lora,
                  wide(COL_R), wide(COL_KK), wide(COL_VV), lora,
                  full((1, 3, D_MODEL)), full((1, LORA_W)),
                  vec, full((LANES, D_MODEL)), vec, full((LANES, D_MODEL)),
                  full((2 * LANES, D_MODEL)), vec, vec, vec, vec, vec, state],
        out_specs=[pl.BlockSpec((rows, D_MODEL), lambda i: (i, 0)), state],
        out_shape=[jax.ShapeDtypeStruct((batch, D_MODEL), F32),
                   jax.ShapeDtypeStruct(s0.shape, F32)],
        compiler_params=_params(("parallel",)),
        name="rwkv_step",
    )(proj, proj, proj, proj, proj_prev, proj_prev, proj_prev, proj_prev,
      wts["mu3"], wts["mu_lora"], wts["w0"], wts["w2"], wts["a0"], wts["a2"], wts["g2"],
      wts["k_k"], wts["k_a"], wts["r_k"], wts["lnx_w"], wts["lnx_b"], s0)


def _merge_body(x_ref, oret_ref, orwkv_ref, ga_ref, gb_ref, wout_ref, nffn_ref, rw_ref, rb_ref,
                x1_ref, xn_ref, comb_ref):
    merged = (jax.nn.sigmoid(ga_ref[...]) * oret_ref[...]
              + jax.nn.sigmoid(gb_ref[...]) * orwkv_ref[...])
    x1 = x_ref[...] + _mm(merged, wout_ref[...])
    x1_ref[...] = x1
    xn = _rms(x1, nffn_ref[...])
    xn_ref[...] = xn.astype(BF16)

    logits = _mm(xn, rw_ref[...]) + rb_ref[...]
    lane = lax.broadcasted_iota(jnp.int32, logits.shape, 1)
    neg_inf = -jnp.inf
    is_group = lane < N_GROUPS
    g_max = jnp.max(jnp.where(is_group, logits, neg_inf), axis=-1, keepdims=True)
    g_sel = jnp.min(jnp.where(is_group & (logits == g_max), lane, LANES), axis=-1, keepdims=True)
    g_w = 1.0 / jnp.sum(jnp.where(is_group, jnp.exp(logits - g_max), 0.0), axis=-1, keepdims=True)
    first = N_GROUPS + EXPERTS_PER_GROUP * g_sel
    in_group = (lane >= first) & (lane < first + EXPERTS_PER_GROUP)
    e_max = jnp.max(jnp.where(in_group, logits, neg_inf), axis=-1, keepdims=True)
    e_exp = jnp.where(in_group, jnp.exp(logits - e_max), 0.0)
    prob = e_exp / jnp.sum(e_exp, axis=-1, keepdims=True)
    prob = jnp.where(in_group, prob, -1.0)
    p1 = jnp.max(prob, axis=-1, keepdims=True)
    i1 = jnp.min(jnp.where(prob == p1, lane, LANES), axis=-1, keepdims=True)
    rest = jnp.where(lane == i1, -1.0, prob)
    p2 = jnp.max(rest, axis=-1, keepdims=True)
    i2 = jnp.min(jnp.where(rest == p2, lane, LANES), axis=-1, keepdims=True)
    denom = p1 + p2
    comb_ref[...] = (jnp.where(lane == i1 - N_GROUPS, g_w * p1 / denom, 0.0)
                     + jnp.where(lane == i2 - N_GROUPS, g_w * p2 / denom, 0.0))


def _merge(x, o_ret, o_rwkv, proj, wts):
    t = x.shape[0]
    tm = math.gcd(t, 512)
    row = lambda: pl.BlockSpec((tm, D_MODEL), lambda i: (i, 0))
    full = lambda shape: pl.BlockSpec(shape, lambda i: (0, 0))
    return pl.pallas_call(
        _merge_body,
        grid=(t // tm,),
        in_specs=[row(), row(), row(),
                  pl.BlockSpec((tm, D_MODEL), lambda i: (i, COL_GATE_A // D_MODEL)),
                  pl.BlockSpec((tm, D_MODEL), lambda i: (i, COL_GATE_B // D_MODEL)),
                  full((D_MODEL, D_MODEL)), full((1, D_MODEL)),
                  full((D_MODEL, LANES)), full((1, LANES))],
        out_specs=[row(), row(), pl.BlockSpec((tm, LANES), lambda i: (i, 0))],
        out_shape=[jax.ShapeDtypeStruct((t, D_MODEL), F32),
                   jax.ShapeDtypeStruct((t, D_MODEL), BF16),
                   jax.ShapeDtypeStruct((t, LANES), F32)],
        compiler_params=_params(("parallel",)),
        name="merge_router",
    )(x, o_ret, o_rwkv, proj, proj, wts["w_out"], wts["norm_ffn"], wts["router_w"], wts["router_b"])


def _moe_body(xn_ref, x1_ref, comb_ref, wg_ref, wu_ref, wd_ref, nf_ref, o_ref, acc_ref):
    e = pl.program_id(1)

    @pl.when(e == 0)
    def _():
        acc_ref[...] = jnp.zeros_like(acc_ref)

    comb = comb_ref[...]
    lane = lax.broadcasted_iota(jnp.int32, comb.shape, 1)
    weight = jnp.sum(jnp.where(lane == e, comb, 0.0), axis=-1, keepdims=True)
    xn = xn_ref[...]
    h = _silu(_mm(xn, wg_ref[0])) * _mm(xn, wu_ref[0]) * weight
    acc_ref[...] += _mm(h, wd_ref[0])

    @pl.when(e == pl.num_programs(1) - 1)
    def _():
        o_ref[...] = _rms(x1_ref[...] + acc_ref[...], nf_ref[...])


def _moe(xn, x1, comb, wts):
    t = xn.shape[0]
    tm = math.gcd(t, 1024)
    row = lambda w: pl.BlockSpec((tm, w), lambda i, e: (i, 0))
    return pl.pallas_call(
        _moe_body,
        grid=(t // tm, N_EXPERTS),
        in_specs=[row(D_MODEL), row(D_MODEL), row(LANES),
                  pl.BlockSpec((1, D_MODEL, D_EXPERT), lambda i, e: (e, 0, 0)),
                  pl.BlockSpec((1, D_MODEL, D_EXPERT), lambda i, e: (e, 0, 0)),
                  pl.BlockSpec((1, D_EXPERT, D_MODEL), lambda i, e: (e, 0, 0)),
                  pl.BlockSpec((1, D_MODEL), lambda i, e: (0, 0))],
        out_specs=row(D_MODEL),
        out_shape=jax.ShapeDtypeStruct((t, D_MODEL), F32),
        scratch_shapes=[pltpu.VMEM((tm, D_MODEL), F32)],
        compiler_params=_params(("parallel", "arbitrary")),
        name="moe",
    )(xn, x1, comb, wts["w_gate"], wts["w_up"], wts["w_down"], wts["norm_final"])


def _pad_to(x, size, axis):
    pad = [(0, 0)] * x.ndim
    pad[axis] = (0, size - x.shape[axis])
    return jnp.pad(x, pad)


def _prepare_weights(norm_mix, w_in, mu_shift, ret_gn_w, rwkv_w0, rwkv_w2, rwkv_a0, rwkv_a2,
                     rwkv_g2, rwkv_k_k, rwkv_k_a, rwkv_r_k, rwkv_lnx_w, rwkv_lnx_b, w_out,
                     norm_ffn, router_group_w, router_group_b, router_expert_w, router_expert_b,
                     expert_w_gate, expert_w_up, expert_w_down, norm_final):
    def lora_cols(t):
        return jnp.concatenate([_pad_to(t[..., SRC_WLO:SRC_ALO], 128, -1),
                                _pad_to(t[..., SRC_ALO:SRC_GLO], 128, -1),
                                _pad_to(t[..., SRC_GLO:SRC_GATE_A], 256, -1)], axis=-1)

    w = w_in[0]
    w_in_p = jnp.concatenate([w[:, :SRC_WLO], w[:, SRC_GATE_A:], lora_cols(w)], axis=1).astype(BF16)
    mu = jnp.concatenate([jnp.zeros((SRC_SHIFT,), F32), mu_shift[0]])
    row = lambda v: v.reshape(1, -1)
    router_w = _pad_to(jnp.concatenate([router_group_w[0], router_expert_w[0]], axis=1), LANES, 1)
    router_b = _pad_to(jnp.concatenate([router_group_b[0], router_expert_b[0]]), LANES, 0)
    return dict(
        w_in=w_in_p, norm_mix=row(norm_mix[0]), ret_gn_w=row(ret_gn_w[0]),
        mu3=mu[SRC_SHIFT:SRC_WLO].reshape(1, 3, D_MODEL), mu_lora=lora_cols(mu).reshape(1, LORA_W),
        w0=row(rwkv_w0[0]), w2=_pad_to(rwkv_w2[0], 128, 0),
        a0=row(rwkv_a0[0]), a2=_pad_to(rwkv_a2[0], 128, 0),
        g2=_pad_to(rwkv_g2[0], 256, 0),
        k_k=row(rwkv_k_k[0]), k_a=row(rwkv_k_a[0]), r_k=row(rwkv_r_k[0]),
        lnx_w=row(rwkv_lnx_w[0]), lnx_b=row(rwkv_lnx_b[0]),
        w_out=w_out[0].astype(BF16), norm_ffn=row(norm_ffn[0]),
        router_w=router_w, router_b=row(router_b),
        w_gate=expert_w_gate[0].astype(BF16), w_up=expert_w_up[0].astype(BF16),
        w_down=expert_w_down[0].astype(BF16), norm_final=row(norm_final))


def _finish(x, o_ret, o_rwkv, proj, wts):
    x1, xn2, comb = _merge(x, o_ret, o_rwkv, proj, wts)
    return _moe(xn2, x1, comb, wts)


def kernel(x_prompt, x_sample, state_ret, state_rwkv, state_shift, norm_mix, w_in, mu_shift, ret_gn_w, rwkv_w0, rwkv_w2, rwkv_a0, rwkv_a2, rwkv_g2, rwkv_k_k, rwkv_k_a, rwkv_r_k, rwkv_lnx_w, rwkv_lnx_b, w_out, norm_ffn, router_group_w, router_group_b, router_expert_w, router_expert_b, expert_w_gate, expert_w_up, expert_w_down, norm_final):
    assert norm_mix.shape[0] == 1, "single-layer step"
    wts = _prepare_weights(norm_mix, w_in, mu_shift, ret_gn_w, rwkv_w0, rwkv_w2, rwkv_a0, rwkv_a2,
                           rwkv_g2, rwkv_k_k, rwkv_k_a, rwkv_r_k, rwkv_lnx_w, rwkv_lnx_b, w_out,
                           norm_ffn, router_group_w, router_group_b, router_expert_w,
                           router_expert_b, expert_w_gate, expert_w_up, expert_w_down, norm_final)

    bp, lp, _ = x_prompt.shape
    xp = x_prompt.reshape(bp * lp, D_MODEL)
    tile_p = math.gcd(lp, 512)
    xb_p, tail_p = _mixer_norm(xp, wts["norm_mix"], tile_p, 8)
    proj_p = _inproj(xb_p, wts["w_in"])
    o_ret_p, ret_p = _retention_prompt(proj_p, wts["ret_gn_w"], bp, lp)
    o_rwkv_p, rwkv_p = _rwkv_prompt(proj_p, wts, bp, lp)
    y_prompt = _finish(xp, o_ret_p, o_rwkv_p, proj_p, wts).reshape(bp, lp, D_MODEL)
    shift_p = tail_p.reshape(bp, lp // tile_p, 8, D_MODEL)[:, -1, -1]

    bs, ls, _ = x_sample.shape
    assert ls == 1, "sample group advances one token"
    xs = x_sample.reshape(bs, D_MODEL)
    xb_s, xn_s = _mixer_norm(xs, wts["norm_mix"], bs, bs)
    proj_s = _inproj(xb_s, wts["w_in"])
    proj_prev = _inproj(state_shift[0].astype(BF16), wts["w_in"])
    pos_s = PAST_LEN + jnp.arange(ls, dtype=F32)
    o_ret_s, ret_s = _retention_step(proj_s, state_ret[0], wts["ret_gn_w"], pos_s)
    o_rwkv_s, rwkv_s = _rwkv_step(proj_s, proj_prev, state_rwkv[0], wts)
    y_sample = _finish(xs, o_ret_s, o_rwkv_s, proj_s, wts).reshape(bs, ls, D_MODEL)

    return (y_prompt, y_sample, ret_p[None], rwkv_p[None], shift_p[None],
            ret_s[None], rwkv_s[None], xn_s)
```

```python
import functools
import math

import jax
import jax.numpy as jnp
from jax import lax
from jax.experimental import pallas as pl
from jax.experimental.pallas import tpu as pltpu

F32 = jnp.float32
BF16 = jnp.bfloat16

D_MODEL = 1024
LANES = 128
PAST_LEN = 16384
RET_HEADS = 8
RET_D = D_MODEL // RET_HEADS
RET_CHUNK = 128
ROPE_BASE = 10000.0
RET_GN_EPS = 1e-5
RWKV_HEAD = 64
RWKV_HEADS = D_MODEL // RWKV_HEAD
RWKV_PAIRS = RWKV_HEADS // 2
RWKV_CHUNK = 64
LORA_DECAY = 64
LORA_A = 64
LORA_G = 160
RWKV_GN_EPS = 64e-5
N_GROUPS = 4
EXPERTS_PER_GROUP = 8
N_EXPERTS = N_GROUPS * EXPERTS_PER_GROUP
D_EXPERT = 256
RMS_EPS = 1e-6

SRC_SHIFT = 4 * D_MODEL
SRC_WLO = SRC_SHIFT + 3 * D_MODEL
SRC_ALO = SRC_WLO + LORA_DECAY
SRC_GLO = SRC_ALO + LORA_A
SRC_GATE_A = SRC_GLO + LORA_G
COL_Q, COL_K, COL_V, COL_GSW = 0, 1024, 2048, 3072
COL_R, COL_KK, COL_VV = 4096, 5120, 6144
COL_GATE_A, COL_GATE_B = 7168, 8192
COL_LORA = 9216
LORA_W = 512
N_PROJ = COL_LORA + LORA_W

VMEM_LIMIT = 48 * 1024 * 1024

_NN = (((1,), (0,)), ((), ()))
_NT = (((1,), (1,)), ((), ()))
_TN = (((0,), (0,)), ((), ()))


def _mm(a, b, dims=_NN, exact=False):
    if exact:
        return lax.dot_general(a.astype(F32), b.astype(F32), dims,
                               precision=lax.Precision.HIGHEST,
                               preferred_element_type=F32)
    return lax.dot_general(a.astype(BF16), b.astype(BF16), dims,
                           preferred_element_type=F32)


def _params(sem):
    return pltpu.CompilerParams(dimension_semantics=sem,
                                vmem_limit_bytes=VMEM_LIMIT)


def _rms(x, gain):
    return x * lax.rsqrt(jnp.mean(x * x, axis=-1, keepdims=True) + RMS_EPS) * gain


def _norm_body(x_ref, g_ref, xb_ref, tail_ref, *, keep):
    x = _rms(x_ref[...], g_ref[...])
    xb_ref[...] = x.astype(BF16)
    tail_ref[0] = x[x.shape[0] - keep:, :]


def _mixer_norm(x, gain, tm, keep):
    t = x.shape[0]
    return pl.pallas_call(
        functools.partial(_norm_body, keep=keep),
        grid=(t // tm,),
        in_specs=[pl.BlockSpec((tm, D_MODEL), lambda i: (i, 0)),
                  pl.BlockSpec((1, D_MODEL), lambda i: (0, 0))],
        out_specs=[pl.BlockSpec((tm, D_MODEL), lambda i: (i, 0)),
                   pl.BlockSpec((1, keep, D_MODEL), lambda i: (i, 0, 0))],
        out_shape=[jax.ShapeDtypeStruct((t, D_MODEL), BF16),
                   jax.ShapeDtypeStruct((t // tm, keep, D_MODEL), F32)],
        compiler_params=_params(("parallel",)),
        name="mixer_norm",
    )(x, gain)


def _inproj_body(x_ref, w_ref, o_ref):
    o_ref[...] = jnp.dot(x_ref[...], w_ref[...], preferred_element_type=F32)


def _inproj(xb, w):
    t = xb.shape[0]
    n = w.shape[1]
    tm = math.gcd(t, 1024)
    tn = n // 4
    return pl.pallas_call(
        _inproj_body,
        grid=(n // tn, t // tm),
        in_specs=[pl.BlockSpec((tm, D_MODEL), lambda j, i: (i, 0)),
                  pl.BlockSpec((D_MODEL, tn), lambda j, i: (0, j), pipeline_mode=pl.Buffered(1))],
        out_specs=pl.BlockSpec((tm, tn), lambda j, i: (i, j)),
        out_shape=jax.ShapeDtypeStruct((t, n), F32),
        compiler_params=_params(("arbitrary", "arbitrary")),
        name="inproj",
    )(xb, w)


def _head_norm(x, eps):
    mu = jnp.mean(x, axis=-1, keepdims=True)
    xc = x - mu
    return xc * lax.rsqrt(jnp.mean(xc * xc, axis=-1, keepdims=True) + eps)


def _silu(x):
    return x * jax.nn.sigmoid(x)


def _rope(x, cos, sin_signed):
    return x * cos + pltpu.roll(x, RET_D // 2, 1) * sin_signed


def _ret_tables(chunk):
    log_gamma = jnp.log1p(-jnp.exp2(-5.0 - jnp.arange(RET_HEADS, dtype=F32)))
    idx = jnp.arange(chunk, dtype=F32)
    rel = idx[:, None] - idx[None, :]
    intra = jnp.where(rel[None] >= 0,
                      jnp.exp(jnp.maximum(rel, 0.0)[None] * log_gamma[:, None, None]), 0.0)
    q_dec = jnp.exp((idx[:, None] + 1.0) * log_gamma[None, :])
    k_dec = jnp.exp((chunk - 1.0 - idx)[None, :] * log_gamma[:, None]).T
    chunk_dec = jnp.exp(chunk * log_gamma)[None, :]
    widen = lambda t: jnp.repeat(t, RET_D, axis=1)
    return intra, widen(q_dec), widen(k_dec), widen(chunk_dec)


def _rope_tables(pos):
    half = RET_D // 2
    inv = ROPE_BASE ** (-jnp.arange(half, dtype=F32) / half)
    ang = pos[:, None] * inv[None, :]
    cos, sin = jnp.cos(ang), jnp.sin(ang)
    return jnp.concatenate([cos, cos], axis=1), jnp.concatenate([-sin, sin], axis=1)


def _ret_body(q_ref, k_ref, v_ref, g_ref, cos_ref, sin_ref, qd_ref, kd_ref, cd_ref,
              intra_ref, gnw_ref, o_ref, sout_ref, s_ref):
    c = pl.program_id(1)

    @pl.when(c == 0)
    def _():
        s_ref[...] = jnp.zeros_like(s_ref)

    cos = cos_ref[...]
    sin = sin_ref[...]
    heads = [slice(h * RET_D, (h + 1) * RET_D) for h in range(RET_HEADS)]
    qr = [_rope(q_ref[:, sl], cos, sin).astype(BF16) for sl in heads]
    kr = [_rope(k_ref[:, sl], cos, sin) * (RET_D ** -0.5) for sl in heads]
    vb = [v_ref[:, sl].astype(BF16) for sl in heads]
    scores = [_mm(qr[h], kr[h], _NT) * intra_ref[h] for h in range(RET_HEADS)]
    cross = [_mm(qr[h], s_ref[h]) * qd_ref[:, sl] for h, sl in enumerate(heads)]
    adds = [_mm(kr[h] * kd_ref[:, sl], vb[h], _TN) for h, sl in enumerate(heads)]
    inner = [_mm(scores[h], vb[h]) for h in range(RET_HEADS)]
    for h, sl in enumerate(heads):
        s_ref[h] = s_ref[h] * cd_ref[:, sl] + adds[h]
        o_ref[:, sl] = (_head_norm(inner[h] + cross[h], RET_GN_EPS) * gnw_ref[:, sl]
                        * _silu(g_ref[:, sl]))

    @pl.when(c == pl.num_programs(1) - 1)
    def _():
        sout_ref[0] = s_ref[...]


def _retention_prompt(proj, gn_w, batch, seq):
    chunk = math.gcd(seq, RET_CHUNK)
    n_chunks = seq // chunk
    intra, q_dec, k_dec, chunk_dec = _ret_tables(chunk)
    cos, sin = _rope_tables(jnp.arange(seq, dtype=F32))
    col = lambda j: pl.BlockSpec((chunk, D_MODEL), lambda b, c, j=j: (b * n_chunks + c, j))
    const2 = lambda shape: pl.BlockSpec(shape, lambda b, c: (0, 0))
    state = pl.BlockSpec((1, RET_HEADS, RET_D, RET_D), lambda b, c: (b, 0, 0, 0))
    return pl.pallas_call(
        _ret_body,
        grid=(batch, n_chunks),
        in_specs=[col(COL_Q // D_MODEL), col(COL_K // D_MODEL), col(COL_V // D_MODEL),
                  col(COL_GSW // D_MODEL),
                  pl.BlockSpec((chunk, RET_D), lambda b, c: (c, 0)),
                  pl.BlockSpec((chunk, RET_D), lambda b, c: (c, 0)),
                  const2((chunk, D_MODEL)), const2((chunk, D_MODEL)), const2((1, D_MODEL)),
                  pl.BlockSpec((RET_HEADS, chunk, chunk), lambda b, c: (0, 0, 0)),
                  const2((1, D_MODEL))],
        out_specs=[pl.BlockSpec((chunk, D_MODEL), lambda b, c: (b * n_chunks + c, 0)), state],
        out_shape=[jax.ShapeDtypeStruct((batch * seq, D_MODEL), F32),
                   jax.ShapeDtypeStruct((batch, RET_HEADS, RET_D, RET_D), F32)],
        scratch_shapes=[pltpu.VMEM((RET_HEADS, RET_D, RET_D), F32)],
        compiler_params=_params(("parallel", "arbitrary")),
        name="retention_prompt",
    )(proj, proj, proj, proj, cos, sin, q_dec, k_dec, chunk_dec, intra, gn_w)


def _ret_step_body(q_ref, k_ref, v_ref, g_ref, cos_ref, sin_ref, qd_ref, kd_ref, cd_ref,
                   gnw_ref, s_ref, o_ref, sout_ref, *, rows):
    cos = cos_ref[...]
    sin = sin_ref[...]
    row = lax.broadcasted_iota(jnp.int32, (rows, RET_D), 0)
    for h in range(RET_HEADS):
        sl = slice(h * RET_D, (h + 1) * RET_D)
        qr = _rope(q_ref[:, sl], cos, sin)
        kr = _rope(k_ref[:, sl], cos, sin) * (RET_D ** -0.5)
        v = v_ref[:, sl]
        kd = kr * kd_ref[:, sl]
        cd = cd_ref[:, sl]

        reads = [_mm(qr, s_ref[b, h]) for b in range(rows)]
        adds = [_mm(jnp.where(row == b, kd, 0.0), v, _TN) for b in range(rows)]
        cross = jnp.zeros((rows, RET_D), F32)
        for b in range(rows):
            sout_ref[b, h] = s_ref[b, h] * cd + adds[b]
            cross = jnp.where(row == b, reads[b], cross)
        out = jnp.sum(qr * kr, axis=-1, keepdims=True) * v + cross * qd_ref[:, sl]
        o_ref[:, sl] = _head_norm(out, RET_GN_EPS) * gnw_ref[:, sl] * _silu(g_ref[:, sl])


def _retention_step(proj, s0, gn_w, pos):
    batch = proj.shape[0]
    rows = 8
    _, q_dec, k_dec, chunk_dec = _ret_tables(1)
    cos, sin = _rope_tables(pos)
    col = lambda j: pl.BlockSpec((rows, D_MODEL), lambda i, j=j: (i, j))
    const2 = lambda shape: pl.BlockSpec(shape, lambda i: (0, 0))
    state = pl.BlockSpec((rows, RET_HEADS, RET_D, RET_D), lambda i: (i, 0, 0, 0))
    return pl.pallas_call(
        functools.partial(_ret_step_body, rows=rows),
        grid=(batch // rows,),
        in_specs=[col(COL_Q // D_MODEL), col(COL_K // D_MODEL), col(COL_V // D_MODEL),
                  col(COL_GSW // D_MODEL),
                  const2((1, RET_D)), const2((1, RET_D)),
                  const2((1, D_MODEL)), const2((1, D_MODEL)), const2((1, D_MODEL)),
                  const2((1, D_MODEL)), state],
        out_specs=[pl.BlockSpec((rows, D_MODEL), lambda i: (i, 0)), state],
        out_shape=[jax.ShapeDtypeStruct((batch, D_MODEL), F32),
                   jax.ShapeDtypeStruct(s0.shape, F32)],
        compiler_params=_params(("parallel",)),
        name="retention_step",
    )(proj, proj, proj, proj, cos, sin, q_dec, k_dec, chunk_dec, gn_w, s0)


def _head_ones():
    row = lax.broadcasted_iota(jnp.int32, (LANES, LANES), 0)
    col = lax.broadcasted_iota(jnp.int32, (LANES, LANES), 1)
    return jnp.where((row // RWKV_HEAD) == (col // RWKV_HEAD), 1.0, 0.0).astype(BF16)


def _split_bf16(x, terms):
    parts = []
    for _ in range(terms - 1):
        hi = x.astype(BF16)
        parts.append(hi)
        x = x - hi.astype(F32)
    parts.append(x.astype(BF16))
    return parts


def _head_sum(x, ones_bd):
    def one(blk):
        hi, lo = _split_bf16(blk, 2)
        return (jnp.dot(hi, ones_bd, preferred_element_type=F32)
                + jnp.dot(lo, ones_bd, preferred_element_type=F32))
    blocks = [one(x[:, j:j + LANES]) for j in range(0, x.shape[1], LANES)]
    return blocks[0] if len(blocks) == 1 else jnp.concatenate(blocks, axis=1)


def _softplus(x):
    return jnp.maximum(x, 0.0) + jnp.log1p(jnp.exp(-jnp.abs(x)))


def _lerp(cur, prev, mu):
    return cur + (prev - cur) * mu


def _rwkv_prep(r_, k_, v_, lora, w0, w2, a0, a2, g2, k_k, k_a, r_k, ones_bd):
    w_lo, a_lo, g_lo = lora[:, 0:128], lora[:, 128:256], lora[:, 256:512]
    w_log = -_softplus(-(w0 + _mm(jnp.tanh(w_lo), w2))) - 0.5
    log_decay = -jnp.exp(w_log)
    a_sig = jax.nn.sigmoid(a0 + _mm(a_lo, a2))
    g = _mm(jax.nn.sigmoid(g_lo), g2)
    kk = k_ * k_k
    kk = kk / jnp.maximum(jnp.sqrt(_head_sum(kk * kk, ones_bd)), 1e-12)
    kmod = k_ * (1.0 + (a_sig - 1.0) * k_a)
    bonus = _head_sum(r_ * kmod * r_k, ones_bd) * v_
    return log_decay, a_sig, g, kk, kmod, bonus


def _rwkv_out(y, bonus, g, lnx_w, lnx_b, ones_bd):
    inv_n = 1.0 / RWKV_HEAD
    yc = y - _head_sum(y, ones_bd) * inv_n
    var = _head_sum(yc * yc, ones_bd) * inv_n
    return (yc * lax.rsqrt(var + RWKV_GN_EPS) * lnx_w + lnx_b + bonus) * g


def _block_diag(x, lane_first):
    return jnp.concatenate([jnp.where(lane_first, x, 0.0), jnp.where(lane_first, 0.0, x)], axis=0)


def _pair_masks():
    n = 2 * RWKV_CHUNK
    row = lax.broadcasted_iota(jnp.int32, (n, n), 0)
    col = lax.broadcasted_iota(jnp.int32, (n, n), 1)
    same = (row // RWKV_CHUNK) == (col // RWKV_CHUNK)
    lane_first = lax.broadcasted_iota(jnp.int32, (1, LANES), 1) < RWKV_HEAD
    return lane_first, same & (col < row), same & (col <= row), row == col


def _rwkv_chunks(chunks, masks):
    lane_first, strict, incl, eye = masks
    n = 2 * RWKV_CHUNK
    rows = lambda x, y: jnp.concatenate([x, y], axis=0)
    cols = lambda x, y: jnp.concatenate([x, y], axis=1)
    bd = lambda x: _block_diag(x, lane_first)

    ops = []
    for r, lw, k, v, a, b, cum in chunks:
        p_inc = jnp.exp(cum)
        p_inv = jnp.exp(-cum)
        p_exc = jnp.exp(cum - lw)
        p_end = p_inc[RWKV_CHUNK - 1:RWKV_CHUNK, :]
        b_t = b * p_inv
        k_t = k * p_inv
        ops.append(dict(a_t=bd(a * p_exc), r_t=bd(r * p_inc), b_t=bd(b_t), k_t=bd(k_t),
                        b_hat=bd(b_t * p_end), k_hat=bd(k_t * p_end), v_bd=bd(v), p_end=p_end))

    for o in ops:
        prod = _mm(rows(o["a_t"], o["r_t"]), rows(o["b_t"], o["k_t"]), _NT)
        o["a_ab"] = jnp.where(strict, prod[:n, :n], 0.0)
        o["a_ak"] = jnp.where(strict, prod[:n, n:], 0.0)
        o["r_b"] = jnp.where(incl, prod[n:, :n], 0.0)
        o["r_k"] = jnp.where(incl, prod[n:, n:], 0.0)

    for o in ops:
        o["t_inv"] = jnp.where(eye, 1.0, 0.0) + o["a_ab"]
        o["power"] = _mm(o["a_ab"], o["a_ab"])
        o["av_rv"] = _mm(rows(o["a_ak"], o["r_k"]), o["v_bd"])
    for _ in range(int(math.log2(RWKV_CHUNK)) - 2):
        for o in ops:
            both = _mm(rows(o["power"], o["t_inv"]), o["power"])
            o["power"] = both[:n]
            o["t_inv"] = o["t_inv"] + both[n:]
    for o in ops:
        o["t_inv"] = o["t_inv"] + _mm(o["t_inv"], o["power"])
    for o in ops:
        o["wu"] = _mm(o["t_inv"], cols(o["a_t"], o["av_rv"][:n]))
    out = []
    for o in ops:
        wu = o["wu"]
        ry_y0 = _mm(o["r_b"], wu)
        r_y = o["r_t"] + ry_y0[:, :LANES]
        y0 = ry_y0[:, LANES:] + o["av_rv"][n:]
        psi = _mm(wu[:, :LANES], o["b_hat"], _TN) + jnp.where(eye, o["p_end"], 0.0)
        s_add = _mm(rows(wu[:, LANES:], o["v_bd"]), rows(o["b_hat"], o["k_hat"]), _TN)
        out.append((r_y, y0, psi, s_add))
    return out


def _shift_rows(cur, first_row):
    rolled = pltpu.roll(cur, 1, 0)
    row = lax.broadcasted_iota(jnp.int32, cur.shape, 0)
    return jnp.where(row == 0, first_row, rolled)


def _rwkv_body(r_ref, k_ref, v_ref, lora_ref, mu_ref, mul_ref,
               w0_ref, w2_ref, a0_ref, a2_ref, g2_ref, kk_ref, ka_ref, rk_ref,
               lnw_ref, lnb_ref, o_ref, sout_ref,
               s_ref, carry_ref, carryl_ref, *, tile):
    t = pl.program_id(1)
    p = pl.program_id(2)
    masks = _pair_masks()
    ones_bd = _head_ones()
    pairs = r_ref.shape[1] // LANES

    @pl.when(t == 0)
    def _():
        for q in range(pairs):
            s_ref[p * pairs + q] = jnp.zeros((LANES, LANES), F32)
        carry_ref[p] = jnp.zeros(carry_ref.shape[1:], F32)

        @pl.when(p == 0)
        def _():
            carryl_ref[...] = jnp.zeros_like(carryl_ref)

    prev = carry_ref[p]
    cur_r, cur_k, cur_v, cur_l = r_ref[...], k_ref[...], v_ref[...], lora_ref[...]
    mu = mu_ref[0]
    r_ = _lerp(cur_r, _shift_rows(cur_r, prev[0:1]), mu[0:1])
    k_ = _lerp(cur_k, _shift_rows(cur_k, prev[1:2]), mu[1:2])
    v_ = _lerp(cur_v, _shift_rows(cur_v, prev[2:3]), mu[2:3])
    lo = _lerp(cur_l, _shift_rows(cur_l, carryl_ref[0:1, :]), mul_ref[...])
    carry_ref[p, 0:1, :] = cur_r[tile - 1:tile]
    carry_ref[p, 1:2, :] = cur_k[tile - 1:tile]
    carry_ref[p, 2:3, :] = cur_v[tile - 1:tile]

    @pl.when(p == pl.num_programs(2) - 1)
    def _():
        carryl_ref[0:1, :] = cur_l[tile - 1:tile]

    log_decay, a_sig, g, kk, kmod, bonus = _rwkv_prep(
        r_, k_, v_, lo, w0_ref[...], w2_ref[...], a0_ref[...], a2_ref[...], g2_ref[...],
        kk_ref[...], ka_ref[...], rk_ref[...], ones_bd)

    row = lax.broadcasted_iota(jnp.int32, (tile, tile), 0)
    col = lax.broadcasted_iota(jnp.int32, (tile, tile), 1)
    tri = jnp.where(((row // RWKV_CHUNK) == (col // RWKV_CHUNK)) & (col <= row), 1.0, 0.0).astype(BF16)
    cum = sum(jnp.dot(tri, part, preferred_element_type=F32) for part in _split_bf16(log_decay, 3))

    neg_kk = -kk
    kk_a = kk * a_sig
    n_chunks = tile // RWKV_CHUNK
    chunks = []
    for c in range(n_chunks):
        for q in range(pairs):
            at = (slice(c * RWKV_CHUNK, (c + 1) * RWKV_CHUNK), slice(q * LANES, (q + 1) * LANES))
            chunks.append((r_[at], log_decay[at], kmod[at], v_[at], neg_kk[at], kk_a[at], cum[at]))
    parts = _rwkv_chunks(chunks, masks)

    states = [s_ref[p * pairs + q] for q in range(pairs)]
    ys = [[] for _ in range(pairs)]
    for c in range(n_chunks):
        for q in range(pairs):
            r_y, y0, psi, s_add = parts[c * pairs + q]
            y_bd = _mm(r_y, states[q], _NT) + y0
            states[q] = _mm(states[q], psi) + s_add
            ys[q].append(y_bd[:RWKV_CHUNK] + y_bd[RWKV_CHUNK:])
    for q in range(pairs):
        s_ref[p * pairs + q] = states[q]

    @pl.when(t == pl.num_programs(1) - 1)
    def _():
        for q in range(pairs):
            head = 2 * (p * pairs + q)
            sout_ref[0, head] = states[q][:RWKV_HEAD, :RWKV_HEAD]
            sout_ref[0, head + 1] = states[q][RWKV_HEAD:, RWKV_HEAD:]

    y = jnp.concatenate([jnp.concatenate(yq, axis=0) for yq in ys], axis=1)
    o_ref[...] = _rwkv_out(y, bonus, g, lnw_ref[...], lnb_ref[...], ones_bd)


RWKV_PAIRS_PER_STEP = 4


def _rwkv_prompt(proj, wts, batch, seq):
    tile = math.gcd(seq, 256)
    n_tiles = seq // tile
    width = RWKV_PAIRS_PER_STEP * LANES
    blk = lambda base: pl.BlockSpec((tile, width),
                                    lambda b, t, p, base=base: (b * n_tiles + t, base // width + p))
    per_pair = lambda rows: pl.BlockSpec((rows, width), lambda b, t, p: (0, p))
    in_specs = [
        blk(COL_R), blk(COL_KK), blk(COL_VV),
        pl.BlockSpec((tile, LORA_W), lambda b, t, p: (b * n_tiles + t, COL_LORA // LORA_W)),
        pl.BlockSpec((1, 3, width), lambda b, t, p: (0, 0, p)),
        pl.BlockSpec((1, LORA_W), lambda b, t, p: (0, 0)),
        per_pair(1), per_pair(LANES),
        per_pair(1), per_pair(LANES),
        per_pair(2 * LANES),
        per_pair(1), per_pair(1), per_pair(1),
        per_pair(1), per_pair(1),
    ]
    return pl.pallas_call(
        functools.partial(_rwkv_body, tile=tile),
        grid=(batch, n_tiles, RWKV_PAIRS // RWKV_PAIRS_PER_STEP),
        in_specs=in_specs,
        out_specs=[pl.BlockSpec((tile, width), lambda b, t, p: (b * n_tiles + t, p)),
                   pl.BlockSpec((1, RWKV_HEADS, RWKV_HEAD, RWKV_HEAD), lambda b, t, p: (b, 0, 0, 0))],
        out_shape=[jax.ShapeDtypeStruct((batch * seq, D_MODEL), F32),
                   jax.ShapeDtypeStruct((batch, RWKV_HEADS, RWKV_HEAD, RWKV_HEAD), F32)],
        scratch_shapes=[pltpu.VMEM((RWKV_PAIRS, LANES, LANES), F32),
                        pltpu.VMEM((RWKV_PAIRS // RWKV_PAIRS_PER_STEP, 8, width), F32),
                        pltpu.VMEM((8, LORA_W), F32)],
        compiler_params=_params(("parallel", "arbitrary", "arbitrary")),
        name="rwkv_prompt",
    )(proj, proj, proj, proj, wts["mu3"], wts["mu_lora"],
      wts["w0"], wts["w2"], wts["a0"], wts["a2"], wts["g2"],
      wts["k_k"], wts["k_a"], wts["r_k"], wts["lnx_w"], wts["lnx_b"])


def _rwkv_step_body(r_ref, k_ref, v_ref, lora_ref, pr_ref, pk_ref, pv_ref, plora_ref,
                    mu_ref, mul_ref, w0_ref, w2_ref, a0_ref, a2_ref, g2_ref,
                    kk_ref, ka_ref, rk_ref, lnw_ref, lnb_ref, s_ref,
                    o_ref, sout_ref, *, rows):
    ones_bd = _head_ones()
    mu = mu_ref[0]
    r_ = _lerp(r_ref[...], pr_ref[...], mu[0:1])
    k_ = _lerp(k_ref[...], pk_ref[...], mu[1:2])
    v_ = _lerp(v_ref[...], pv_ref[...], mu[2:3])
    lo = _lerp(lora_ref[...], plora_ref[...], mul_ref[...])
    log_decay, a_sig, g, kk, kmod, bonus = _rwkv_prep(
        r_, k_, v_, lo, w0_ref[...], w2_ref[...], a0_ref[...], a2_ref[...], g2_ref[...],
        kk_ref[...], ka_ref[...], rk_ref[...], ones_bd)
    decay = jnp.exp(log_decay)
    neg_kk = -kk
    kk_a = kk * a_sig
    decay_r = decay * r_
    row = lax.broadcasted_iota(jnp.int32, (rows, RWKV_HEAD), 0)

    heads = [slice(h * RWKV_HEAD, (h + 1) * RWKV_HEAD) for h in range(RWKV_HEADS)]
    reads = []
    for h, sl in enumerate(heads):
        lhs = jnp.concatenate([neg_kk[:, sl], decay_r[:, sl]], axis=0)
        reads.append([_mm(lhs, s_ref[b, h], _NT) for b in range(rows)])
    ys = []
    for h, sl in enumerate(heads):
        w_h, r_h, v_h, k_h, b_h = decay[:, sl], r_[:, sl], v_[:, sl], kmod[:, sl], kk_a[:, sl]
        rhs = jnp.concatenate([b_h, k_h], axis=0)
        sa_all = swr_all = jnp.zeros((rows, RWKV_HEAD), F32)
        adds = []
        for b in range(rows):
            mine = row == b
            sa, swr = reads[h][b][:rows], reads[h][b][rows:]
            left = jnp.concatenate([jnp.where(mine, sa, 0.0), jnp.where(mine, v_h, 0.0)], axis=0)
            adds.append(_mm(left, rhs, _TN))
            sa_all = jnp.where(mine, sa, sa_all)
            swr_all = jnp.where(mine, swr, swr_all)
        for b in range(rows):
            sout_ref[b, h] = s_ref[b, h] * w_h[b:b + 1] + adds[b]
        ys.append(swr_all + sa_all * jnp.sum(b_h * r_h, axis=-1, keepdims=True)
                  + v_h * jnp.sum(k_h * r_h, axis=-1, keepdims=True))
    y = jnp.concatenate(ys, axis=1)
    o_ref[...] = _rwkv_out(y, bonus, g, lnw_ref[...], lnb_ref[...], ones_bd)


def _rwkv_step(proj, proj_prev, s0, wts):
    batch = proj.shape[0]
    rows = 8
    wide = lambda base: pl.BlockSpec((rows, D_MODEL), lambda i, base=base: (i, base // D_MODEL))
    lora = pl.BlockSpec((rows, LORA_W), lambda i: (i, COL_LORA // LORA_W))
    full = lambda shape: pl.BlockSpec(shape, lambda i: (0,) * len(shape))
    state = pl.BlockSpec((rows, RWKV_HEADS, RWKV_HEAD, RWKV_HEAD), lambda i: (i, 0, 0, 0))
    vec = full((1, D_MODEL))
    return pl.pallas_call(
        functools.partial(_rwkv_step_body, rows=rows),
        grid=(batch // rows,),
        in_specs=[wide(COL_R), wide(COL_KK), wide(COL_VV), lora,
                  wide(COL_R), wide(COL_KK), wide(COL_VV), lora,
                  full((1, 3, D_MODEL)), full((1, LORA_W)),
                  vec, full((LANES, D_MODEL)), vec, full((LANES, D_MODEL)),
                  full((2 * LANES, D_MODEL)), vec, vec, vec, vec, vec, state],
        out_specs=[pl.BlockSpec((rows, D_MODEL), lambda i: (i, 0)), state],
        out_shape=[jax.ShapeDtypeStruct((batch, D_MODEL), F32),
                   jax.ShapeDtypeStruct(s0.shape, F32)],
        compiler_params=_params(("parallel",)),
        name="rwkv_step",
    )(proj, proj, proj, proj, proj_prev, proj_prev, proj_prev, proj_prev,
      wts["mu3"], wts["mu_lora"], wts["w0"], wts["w2"], wts["a0"], wts["a2"], wts["g2"],
      wts["k_k"], wts["k_a"], wts["r_k"], wts["lnx_w"], wts["lnx_b"], s0)


def _merge_body(x_ref, oret_ref, orwkv_ref, ga_ref, gb_ref, wout_ref, nffn_ref, rw_ref, rb_ref,
                x1_ref, xn_ref, comb_ref, cnt_ref):
    merged = (jax.nn.sigmoid(ga_ref[...]) * oret_ref[...]
              + jax.nn.sigmoid(gb_ref[...]) * orwkv_ref[...])
    x1 = x_ref[...] + _mm(merged, wout_ref[...])
    x1_ref[...] = x1
    xn = _rms(x1, nffn_ref[...])
    xn_ref[...] = xn.astype(BF16)

    logits = _mm(xn, rw_ref[...]) + rb_ref[...]
    lane = lax.broadcasted_iota(jnp.int32, logits.shape, 1)
    neg_inf = -jnp.inf
    is_group = lane < N_GROUPS
    g_max = jnp.max(jnp.where(is_group, logits, neg_inf), axis=-1, keepdims=True)
    g_sel = jnp.min(jnp.where(is_group & (logits == g_max), lane, LANES), axis=-1, keepdims=True)
    g_w = 1.0 / jnp.sum(jnp.where(is_group, jnp.exp(logits - g_max), 0.0), axis=-1, keepdims=True)
    first = N_GROUPS + EXPERTS_PER_GROUP * g_sel
    in_group = (lane >= first) & (lane < first + EXPERTS_PER_GROUP)
    e_max = jnp.max(jnp.where(in_group, logits, neg_inf), axis=-1, keepdims=True)
    e_exp = jnp.where(in_group, jnp.exp(logits - e_max), 0.0)
    prob = e_exp / jnp.sum(e_exp, axis=-1, keepdims=True)
    prob = jnp.where(in_group, prob, -1.0)
    p1 = jnp.max(prob, axis=-1, keepdims=True)
    i1 = jnp.min(jnp.where(prob == p1, lane, LANES), axis=-1, keepdims=True)
    rest = jnp.where(lane == i1, -1.0, prob)
    p2 = jnp.max(rest, axis=-1, keepdims=True)
    i2 = jnp.min(jnp.where(rest == p2, lane, LANES), axis=-1, keepdims=True)
    denom = p1 + p2
    comb = (jnp.where(lane == i1 - N_GROUPS, g_w * p1 / denom, 0.0)
            + jnp.where(lane == i2 - N_GROUPS, g_w * p2 / denom, 0.0))
    comb_ref[...] = jnp.where(lane == N_EXPERTS, g_sel.astype(F32), comb)
    lane8 = lax.broadcasted_iota(jnp.int32, (8, LANES), 1)
    sizes = jnp.zeros((8, LANES), jnp.int32)
    for g in range(N_GROUPS):
        n_g = jnp.sum(jnp.where(g_sel == g, 1, 0), axis=0, keepdims=True)
        sizes = jnp.where(lane8 == g, n_g, sizes)
    cnt_ref[0] = sizes


MERGE_TILE = 512


def _merge(x, o_ret, o_rwkv, proj, wts):
    t = x.shape[0]
    tm = math.gcd(t, MERGE_TILE)
    row = lambda: pl.BlockSpec((tm, D_MODEL), lambda i: (i, 0))
    full = lambda shape: pl.BlockSpec(shape, lambda i: (0, 0))
    return pl.pallas_call(
        _merge_body,
        grid=(t // tm,),
        in_specs=[row(), row(), row(),
                  pl.BlockSpec((tm, D_MODEL), lambda i: (i, COL_GATE_A // D_MODEL)),
                  pl.BlockSpec((tm, D_MODEL), lambda i: (i, COL_GATE_B // D_MODEL)),
                  full((D_MODEL, D_MODEL)), full((1, D_MODEL)),
                  full((D_MODEL, LANES)), full((1, LANES))],
        out_specs=[row(), row(), pl.BlockSpec((tm, LANES), lambda i: (i, 0)),
                   pl.BlockSpec((1, 8, LANES), lambda i: (i, 0, 0))],
        out_shape=[jax.ShapeDtypeStruct((t, D_MODEL), F32),
                   jax.ShapeDtypeStruct((t, D_MODEL), BF16),
                   jax.ShapeDtypeStruct((t, LANES), F32),
                   jax.ShapeDtypeStruct((t // tm, 8, LANES), jnp.int32)],
        compiler_params=_params(("parallel",)),
        name="merge_router",
    )(x, o_ret, o_rwkv, proj, proj, wts["w_out"], wts["norm_ffn"], wts["router_w"], wts["router_b"])


MOE_TILE = 1024
MOE_BLOCK = 128
EXPERTS_PER_STEP = 4


def _moe_body(cnt_ref, xn_ref, x1_ref, comb_ref, wg_ref, wu_ref, wd_ref, nf_ref, o_ref,
              acc_ref, xs_ref, cs_ref, pt_ref, *, ratio):
    i = pl.program_id(0)
    s = pl.program_id(1)
    tm = xn_ref.shape[0]
    group = s // (EXPERTS_PER_GROUP // EXPERTS_PER_STEP)

    sizes = []
    for g in range(N_GROUPS):
        n = cnt_ref[i * ratio * N_GROUPS + g]
        for k in range(1, ratio):
            n = n + cnt_ref[(i * ratio + k) * N_GROUPS + g]
        sizes.append(n)
    starts = [jnp.int32(0)]
    for g in range(N_GROUPS - 1):
        starts.append(starts[-1] + sizes[g])

    @pl.when(s == 0)
    def _():
        comb = comb_ref[...]
        g_row = comb.T[N_EXPERTS:N_EXPERTS + 1, :]
        gid = lax.broadcasted_iota(jnp.int32, (8, tm), 0)
        onehot = jnp.where(g_row.astype(jnp.int32) == gid, 1.0, 0.0)
        step = math.gcd(tm, 256)
        row = lax.broadcasted_iota(jnp.int32, (step, tm), 0)
        col = lax.broadcasted_iota(jnp.int32, (step, tm), 1)
        rank = jnp.zeros((8, tm), F32)
        for r0 in range(0, tm, step):
            earlier = jnp.where(row + r0 < col, 1.0, 0.0).astype(BF16)
            rank = rank + jnp.dot(onehot[:, r0:r0 + step].astype(BF16), earlier,
                                  preferred_element_type=F32)
        start_col = jnp.zeros((8, 1), F32)
        gid_col = lax.broadcasted_iota(jnp.int32, (8, 1), 0)
        for g in range(1, N_GROUPS):
            start_col = jnp.where(gid_col == g, starts[g].astype(F32), start_col)
        pos_row = jnp.sum(onehot * (rank + start_col), axis=0, keepdims=True)
        pos_row_i = pos_row.astype(jnp.int32)
        xn = xn_ref[...]
        comb_parts = _split_bf16(comb, 3)
        for r0 in range(0, tm, step):
            perm = jnp.where(row + r0 == pos_row_i, 1.0, 0.0).astype(BF16)
            xs_ref[r0:r0 + step, :] = jnp.dot(perm, xn, preferred_element_type=F32).astype(BF16)
            cs_ref[r0:r0 + step, :] = sum(jnp.dot(perm, part, preferred_element_type=F32)
                                          for part in comb_parts)
        pos_col_i = jnp.broadcast_to(pos_row, (LANES, tm)).T.astype(jnp.int32)
        lane = lax.broadcasted_iota(jnp.int32, (tm, LANES), 1)
        for j in range(tm // LANES):
            pt_ref[:, j * LANES:(j + 1) * LANES] = jnp.where(
                lane + j * LANES == pos_col_i, 1.0, 0.0).astype(BF16)
        acc_ref[...] = jnp.zeros_like(acc_ref)

    start, size = starts[N_GROUPS - 1], sizes[N_GROUPS - 1]
    for g in range(N_GROUPS - 2, -1, -1):
        start = jnp.where(group == g, starts[g], start)
        size = jnp.where(group == g, sizes[g], size)
    blk = math.gcd(tm, MOE_BLOCK)
    first = start // blk
    last = jnp.where(size > 0, (start + size + blk - 1) // blk, first)
    lane = lax.broadcasted_iota(jnp.int32, (blk, LANES), 1)

    def block(j, carry):
        r0 = pl.multiple_of(j * blk, blk)
        xb = xs_ref[pl.ds(r0, blk), :]
        cw = cs_ref[pl.ds(r0, blk), :]
        hs = []
        for k in range(EXPERTS_PER_STEP):
            weight = jnp.sum(jnp.where(lane == s * EXPERTS_PER_STEP + k, cw, 0.0),
                             axis=-1, keepdims=True)
            hs.append(_silu(_mm(xb, wg_ref[k])) * _mm(xb, wu_ref[k]) * weight)
        total = _mm(hs[0], wd_ref[0])
        for k in range(1, EXPERTS_PER_STEP):
            total = total + _mm(hs[k], wd_ref[k])
        acc_ref[pl.ds(r0, blk), :] += total
        return carry

    lax.fori_loop(first, last, block, 0)

    @pl.when(s == pl.num_programs(1) - 1)
    def _():
        parts = _split_bf16(acc_ref[...], 2)
        step = math.gcd(tm, 256)
        for r0 in range(0, tm, step):
            moe = sum(jnp.dot(pt_ref[r0:r0 + step, :], part, preferred_element_type=F32)
                      for part in parts)
            o_ref[r0:r0 + step, :] = _rms(x1_ref[r0:r0 + step, :] + moe, nf_ref[...])


def _moe(xn, x1, comb, counts, wts):
    t = xn.shape[0]
    tm = math.gcd(t, MOE_TILE)
    assert tm % LANES == 0, "token count must be a multiple of the lane width"
    ratio = tm // math.gcd(t, MERGE_TILE)
    row = lambda w: pl.BlockSpec((tm, w), lambda i, s, cnt: (i, 0))
    once = lambda w: pl.BlockSpec((tm, w), lambda i, s, cnt: (i, 0), pipeline_mode=pl.Buffered(1))
    expert = lambda a, b: pl.BlockSpec((EXPERTS_PER_STEP, a, b), lambda i, s, cnt: (s, 0, 0))
    grid_spec = pltpu.PrefetchScalarGridSpec(
        num_scalar_prefetch=1,
        grid=(t // tm, N_EXPERTS // EXPERTS_PER_STEP),
        in_specs=[once(D_MODEL), once(D_MODEL), once(LANES),
                  expert(D_MODEL, D_EXPERT), expert(D_MODEL, D_EXPERT), expert(D_EXPERT, D_MODEL),
                  pl.BlockSpec((1, D_MODEL), lambda i, s, cnt: (0, 0))],
        out_specs=row(D_MODEL),
        scratch_shapes=[pltpu.VMEM((tm, D_MODEL), F32),
                        pltpu.VMEM((tm, D_MODEL), BF16),
                        pltpu.VMEM((tm, LANES), F32),
                        pltpu.VMEM((tm, tm), BF16)])
    return pl.pallas_call(
        functools.partial(_moe_body, ratio=ratio),
        grid_spec=grid_spec,
        out_shape=jax.ShapeDtypeStruct((t, D_MODEL), F32),
        compiler_params=_params(("parallel", "arbitrary")),
        name="moe",
    )(counts, xn, x1, comb, wts["w_gate"], wts["w_up"], wts["w_down"], wts["norm_final"])


def _pad_to(x, size, axis):
    pad = [(0, 0)] * x.ndim
    pad[axis] = (0, size - x.shape[axis])
    return jnp.pad(x, pad)


def _prepare_weights(norm_mix, w_in, mu_shift, ret_gn_w, rwkv_w0, rwkv_w2, rwkv_a0, rwkv_a2,
                     rwkv_g2, rwkv_k_k, rwkv_k_a, rwkv_r_k, rwkv_lnx_w, rwkv_lnx_b, w_out,
                     norm_ffn, router_group_w, router_group_b, router_expert_w, router_expert_b,
                     expert_w_gate, expert_w_up, expert_w_down, norm_final):
    def lora_cols(t):
        return jnp.concatenate([_pad_to(t[..., SRC_WLO:SRC_ALO], 128, -1),
                                _pad_to(t[..., SRC_ALO:SRC_GLO], 128, -1),
                                _pad_to(t[..., SRC_GLO:SRC_GATE_A], 256, -1)], axis=-1)

    w = w_in[0]
    w_in_p = jnp.concatenate([w[:, :SRC_WLO], w[:, SRC_GATE_A:], lora_cols(w)], axis=1).astype(BF16)
    mu = jnp.concatenate([jnp.zeros((SRC_SHIFT,), F32), mu_shift[0]])
    row = lambda v: v.reshape(1, -1)
    router_w = _pad_to(jnp.concatenate([router_group_w[0], router_expert_w[0]], axis=1), LANES, 1)
    router_b = _pad_to(jnp.concatenate([router_group_b[0], router_expert_b[0]]), LANES, 0)
    return dict(
        w_in=w_in_p, norm_mix=row(norm_mix[0]), ret_gn_w=row(ret_gn_w[0]),
        mu3=mu[SRC_SHIFT:SRC_WLO].reshape(1, 3, D_MODEL), mu_lora=lora_cols(mu).reshape(1, LORA_W),
        w0=row(rwkv_w0[0]), w2=_pad_to(rwkv_w2[0], 128, 0),
        a0=row(rwkv_a0[0]), a2=_pad_to(rwkv_a2[0], 128, 0),
        g2=_pad_to(rwkv_g2[0], 256, 0),
        k_k=row(rwkv_k_k[0]), k_a=row(rwkv_k_a[0]), r_k=row(rwkv_r_k[0]),
        lnx_w=row(rwkv_lnx_w[0]), lnx_b=row(rwkv_lnx_b[0]),
        w_out=w_out[0].astype(BF16), norm_ffn=row(norm_ffn[0]),
        router_w=router_w, router_b=row(router_b),
        w_gate=expert_w_gate[0].astype(BF16), w_up=expert_w_up[0].astype(BF16),
        w_down=expert_w_down[0].astype(BF16), norm_final=row(norm_final))


def _finish(x, o_ret, o_rwkv, proj, wts):
    x1, xn2, comb, sizes = _merge(x, o_ret, o_rwkv, proj, wts)
    counts = sizes[:, 0, :N_GROUPS].reshape(-1)
    return _moe(xn2, x1, comb, counts, wts)


def kernel(x_prompt, x_sample, state_ret, state_rwkv, state_shift, norm_mix, w_in, mu_shift, ret_gn_w, rwkv_w0, rwkv_w2, rwkv_a0, rwkv_a2, rwkv_g2, rwkv_k_k, rwkv_k_a, rwkv_r_k, rwkv_lnx_w, rwkv_lnx_b, w_out, norm_ffn, router_group_w, router_group_b, router_expert_w, router_expert_b, expert_w_gate, expert_w_up, expert_w_down, norm_final):
    assert norm_mix.shape[0] == 1, "single-layer step"
    wts = _prepare_weights(norm_mix, w_in, mu_shift, ret_gn_w, rwkv_w0, rwkv_w2, rwkv_a0, rwkv_a2,
                           rwkv_g2, rwkv_k_k, rwkv_k_a, rwkv_r_k, rwkv_lnx_w, rwkv_lnx_b, w_out,
                           norm_ffn, router_group_w, router_group_b, router_expert_w,
                           router_expert_b, expert_w_gate, expert_w_up, expert_w_down, norm_final)

    bp, lp, _ = x_prompt.shape
    xp = x_prompt.reshape(bp * lp, D_MODEL)
    tile_p = math.gcd(lp, 512)
    xb_p, tail_p = _mixer_norm(xp, wts["norm_mix"], tile_p, 8)
    proj_p = _inproj(xb_p, wts["w_in"])
    o_ret_p, ret_p = _retention_prompt(proj_p, wts["ret_gn_w"], bp, lp)
    o_rwkv_p, rwkv_p = _rwkv_prompt(proj_p, wts, bp, lp)
    y_prompt = _finish(xp, o_ret_p, o_rwkv_p, proj_p, wts).reshape(bp, lp, D_MODEL)
    shift_p = tail_p.reshape(bp, lp // tile_p, 8, D_MODEL)[:, -1, -1]

    bs, ls, _ = x_sample.shape
    assert ls == 1, "sample group advances one token"
    xs = x_sample.reshape(bs, D_MODEL)
    xb_s, xn_s = _mixer_norm(xs, wts["norm_mix"], bs, bs)
    proj_s = _inproj(xb_s, wts["w_in"])
    proj_prev = _inproj(state_shift[0].astype(BF16), wts["w_in"])
    pos_s = PAST_LEN + jnp.arange(ls, dtype=F32)
    o_ret_s, ret_s = _retention_step(proj_s, state_ret[0], wts["ret_gn_w"], pos_s)
    o_rwkv_s, rwkv_s = _rwkv_step(proj_s, proj_prev, state_rwkv[0], wts)
    y_sample = _finish(xs, o_ret_s, o_rwkv_s, proj_s, wts).reshape(bs, ls, D_MODEL)

    return (y_prompt, y_sample, ret_p[None], rwkv_p[None], shift_p[None],
            ret_s[None], rwkv_s[None], xn_s)
```

```python
import functools
import math

import jax
import jax.numpy as jnp
from jax import lax
from jax.experimental import pallas as pl
from jax.experimental.pallas import tpu as pltpu

F32 = jnp.float32
BF16 = jnp.bfloat16

D_MODEL = 1024
LANES = 128
PAST_LEN = 16384
RET_HEADS = 8
RET_D = D_MODEL // RET_HEADS
RET_CHUNK = 128
RET_CHUNKS_PER_STEP = 2
ROPE_BASE = 10000.0
RET_GN_EPS = 1e-5
RWKV_HEAD = 64
RWKV_HEADS = D_MODEL // RWKV_HEAD
RWKV_PAIRS = RWKV_HEADS // 2
RWKV_CHUNK = 64
LORA_DECAY = 64
LORA_A = 64
LORA_G = 160
RWKV_GN_EPS = 64e-5
N_GROUPS = 4
EXPERTS_PER_GROUP = 8
N_EXPERTS = N_GROUPS * EXPERTS_PER_GROUP
D_EXPERT = 256
RMS_EPS = 1e-6

SRC_SHIFT = 4 * D_MODEL
SRC_WLO = SRC_SHIFT + 3 * D_MODEL
SRC_ALO = SRC_WLO + LORA_DECAY
SRC_GLO = SRC_ALO + LORA_A
SRC_GATE_A = SRC_GLO + LORA_G
COL_Q, COL_K, COL_V, COL_GSW = 0, 1024, 2048, 3072
COL_R, COL_KK, COL_VV = 4096, 5120, 6144
COL_GATE_A, COL_GATE_B = 7168, 8192
COL_LORA = 9216
LORA_W = 512
N_PROJ = COL_LORA + LORA_W

VMEM_LIMIT = 48 * 1024 * 1024

_NN = (((1,), (0,)), ((), ()))
_NT = (((1,), (1,)), ((), ()))
_TN = (((0,), (0,)), ((), ()))


def _mm(a, b, dims=_NN, exact=False):
    if exact:
        return lax.dot_general(a.astype(F32), b.astype(F32), dims,
                               precision=lax.Precision.HIGHEST,
                               preferred_element_type=F32)
    return lax.dot_general(a.astype(BF16), b.astype(BF16), dims,
                           preferred_element_type=F32)


def _params(sem):
    return pltpu.CompilerParams(dimension_semantics=sem,
                                vmem_limit_bytes=VMEM_LIMIT)


def _rms(x, gain):
    return x * lax.rsqrt(jnp.mean(x * x, axis=-1, keepdims=True) + RMS_EPS) * gain


def _split_bf16(x, terms):
    parts = []
    for _ in range(terms - 1):
        hi = x.astype(BF16)
        parts.append(hi)
        x = x - hi.astype(F32)
    parts.append(x.astype(BF16))
    return parts


def _lane_sum(x, ones):
    hi, lo = _split_bf16(x, 2)
    return (jnp.dot(hi, ones, preferred_element_type=F32)
            + jnp.dot(lo, ones, preferred_element_type=F32))


def _norm_body(x_ref, g_ref, xb_ref, tail_ref, *, keep):
    x = _rms(x_ref[...], g_ref[...])
    xb_ref[...] = x.astype(BF16)
    tail_ref[0] = x[x.shape[0] - keep:, :]


def _mixer_norm(x, gain, tm, keep):
    t = x.shape[0]
    return pl.pallas_call(
        functools.partial(_norm_body, keep=keep),
        grid=(t // tm,),
        in_specs=[pl.BlockSpec((tm, D_MODEL), lambda i: (i, 0)),
                  pl.BlockSpec((1, D_MODEL), lambda i: (0, 0))],
        out_specs=[pl.BlockSpec((tm, D_MODEL), lambda i: (i, 0)),
                   pl.BlockSpec((1, keep, D_MODEL), lambda i: (i, 0, 0))],
        out_shape=[jax.ShapeDtypeStruct((t, D_MODEL), BF16),
                   jax.ShapeDtypeStruct((t // tm, keep, D_MODEL), F32)],
        compiler_params=_params(("parallel",)),
        name="mixer_norm",
    )(x, gain)


def _inproj_body(x_ref, w_ref, o_ref):
    o_ref[...] = jnp.dot(x_ref[...], w_ref[...], preferred_element_type=F32)


def _inproj(xb, w):
    t = xb.shape[0]
    n = w.shape[1]
    tm = math.gcd(t, 1024)
    tn = n // 4
    return pl.pallas_call(
        _inproj_body,
        grid=(n // tn, t // tm),
        in_specs=[pl.BlockSpec((tm, D_MODEL), lambda j, i: (i, 0)),
                  pl.BlockSpec((D_MODEL, tn), lambda j, i: (0, j), pipeline_mode=pl.Buffered(1))],
        out_specs=pl.BlockSpec((tm, tn), lambda j, i: (i, j)),
        out_shape=jax.ShapeDtypeStruct((t, n), F32),
        compiler_params=_params(("arbitrary", "arbitrary")),
        name="inproj",
    )(xb, w)


def _head_norm(x, eps):
    mu = jnp.mean(x, axis=-1, keepdims=True)
    xc = x - mu
    return xc * lax.rsqrt(jnp.mean(xc * xc, axis=-1, keepdims=True) + eps)


def _silu(x):
    return x * jax.nn.sigmoid(x)


def _rope(x, cos, sin_signed):
    return x * cos + pltpu.roll(x, RET_D // 2, 1) * sin_signed


def _ret_tables(chunk):
    log_gamma = jnp.log1p(-jnp.exp2(-5.0 - jnp.arange(RET_HEADS, dtype=F32)))
    idx = jnp.arange(chunk, dtype=F32)
    rel = idx[:, None] - idx[None, :]
    intra = jnp.where(rel[None] >= 0,
                      jnp.exp(jnp.maximum(rel, 0.0)[None] * log_gamma[:, None, None]), 0.0)
    q_dec = jnp.exp((idx[:, None] + 1.0) * log_gamma[None, :])
    k_dec = jnp.exp((chunk - 1.0 - idx)[None, :] * log_gamma[:, None]).T
    chunk_dec = jnp.exp(chunk * log_gamma)[None, :]
    widen = lambda t: jnp.repeat(t, RET_D, axis=1)
    return intra, widen(q_dec), widen(k_dec), widen(chunk_dec)


def _rope_tables(pos):
    half = RET_D // 2
    inv = ROPE_BASE ** (-jnp.arange(half, dtype=F32) / half)
    ang = pos[:, None] * inv[None, :]
    cos, sin = jnp.cos(ang), jnp.sin(ang)
    return jnp.concatenate([cos, cos], axis=1), jnp.concatenate([-sin, sin], axis=1)


def _ret_body(q_ref, k_ref, v_ref, g_ref, cos_ref, sin_ref, qd_ref, kd_ref, cd_ref,
              intra_ref, gnw_ref, o_ref, sout_ref, s_ref):
    c = pl.program_id(1)
    chunk = qd_ref.shape[0]

    @pl.when(c == 0)
    def _():
        s_ref[...] = jnp.zeros_like(s_ref)

    heads = [slice(h * RET_D, (h + 1) * RET_D) for h in range(RET_HEADS)]
    ones = jnp.ones((RET_D, RET_D), BF16)
    state = [s_ref[h] for h in range(RET_HEADS)]
    for r0 in range(0, q_ref.shape[0], chunk):
        rows = slice(r0, r0 + chunk)
        cos = cos_ref[rows, :]
        sin = sin_ref[rows, :]
        qr = [_rope(q_ref[rows, sl], cos, sin).astype(BF16) for sl in heads]
        kr = [_rope(k_ref[rows, sl], cos, sin) * (RET_D ** -0.5) for sl in heads]
        vb = [v_ref[rows, sl].astype(BF16) for sl in heads]
        scores = [_mm(qr[h], kr[h], _NT) * intra_ref[h] for h in range(RET_HEADS)]
        cross = [_mm(qr[h], state[h]) * qd_ref[:, sl] for h, sl in enumerate(heads)]
        adds = [_mm(kr[h] * kd_ref[:, sl], vb[h], _TN) for h, sl in enumerate(heads)]
        out = [_mm(scores[h], vb[h]) + cross[h] for h in range(RET_HEADS)]
        cen = [out[h] - _lane_sum(out[h], ones) * (1.0 / RET_D) for h in range(RET_HEADS)]
        var = [_lane_sum(cen[h] * cen[h], ones) * (1.0 / RET_D) for h in range(RET_HEADS)]
        for h, sl in enumerate(heads):
            state[h] = state[h] * cd_ref[:, sl] + adds[h]
            o_ref[rows, sl] = (cen[h] * lax.rsqrt(var[h] + RET_GN_EPS) * gnw_ref[:, sl]
                               * _silu(g_ref[rows, sl]))
    for h in range(RET_HEADS):
        s_ref[h] = state[h]

    @pl.when(c == pl.num_programs(1) - 1)
    def _():
        sout_ref[0] = s_ref[...]


def _retention_prompt(proj, gn_w, batch, seq):
    chunk = math.gcd(seq, RET_CHUNK)
    tile = math.gcd(seq, RET_CHUNKS_PER_STEP * chunk)
    n_tiles = seq // tile
    intra, q_dec, k_dec, chunk_dec = _ret_tables(chunk)
    cos, sin = _rope_tables(jnp.arange(seq, dtype=F32))
    col = lambda j: pl.BlockSpec((tile, D_MODEL), lambda b, c, j=j: (b * n_tiles + c, j))
    const2 = lambda shape: pl.BlockSpec(shape, lambda b, c: (0, 0))
    state = pl.BlockSpec((1, RET_HEADS, RET_D, RET_D), lambda b, c: (b, 0, 0, 0))
    return pl.pallas_call(
        _ret_body,
        grid=(batch, n_tiles),
        in_specs=[col(COL_Q // D_MODEL), col(COL_K // D_MODEL), col(COL_V // D_MODEL),
                  col(COL_GSW // D_MODEL),
                  pl.BlockSpec((tile, RET_D), lambda b, c: (c, 0)),
                  pl.BlockSpec((tile, RET_D), lambda b, c: (c, 0)),
                  const2((chunk, D_MODEL)), const2((chunk, D_MODEL)), const2((1, D_MODEL)),
                  pl.BlockSpec((RET_HEADS, chunk, chunk), lambda b, c: (0, 0, 0)),
                  const2((1, D_MODEL))],
        out_specs=[pl.BlockSpec((tile, D_MODEL), lambda b, c: (b * n_tiles + c, 0)), state],
        out_shape=[jax.ShapeDtypeStruct((batch * seq, D_MODEL), F32),
                   jax.ShapeDtypeStruct((batch, RET_HEADS, RET_D, RET_D), F32)],
        scratch_shapes=[pltpu.VMEM((RET_HEADS, RET_D, RET_D), F32)],
        compiler_params=_params(("parallel", "arbitrary")),
        name="retention_prompt",
    )(proj, proj, proj, proj, cos, sin, q_dec, k_dec, chunk_dec, intra, gn_w)


def _ret_step_body(q_ref, k_ref, v_ref, g_ref, cos_ref, sin_ref, qd_ref, kd_ref, cd_ref,
                   gnw_ref, s_ref, o_ref, sout_ref, *, rows):
    cos = cos_ref[...]
    sin = sin_ref[...]
    row = lax.broadcasted_iota(jnp.int32, (rows, RET_D), 0)
    for h in range(RET_HEADS):
        sl = slice(h * RET_D, (h + 1) * RET_D)
        qr = _rope(q_ref[:, sl], cos, sin)
        kr = _rope(k_ref[:, sl], cos, sin) * (RET_D ** -0.5)
        v = v_ref[:, sl]
        kd = kr * kd_ref[:, sl]
        cd = cd_ref[:, sl]

        reads = [_mm(qr, s_ref[b, h]) for b in range(rows)]
        adds = [_mm(jnp.where(row == b, kd, 0.0), v, _TN) for b in range(rows)]
        cross = jnp.zeros((rows, RET_D), F32)
        for b in range(rows):
            sout_ref[b, h] = s_ref[b, h] * cd + adds[b]
            cross = jnp.where(row == b, reads[b], cross)
        out = jnp.sum(qr * kr, axis=-1, keepdims=True) * v + cross * qd_ref[:, sl]
        o_ref[:, sl] = _head_norm(out, RET_GN_EPS) * gnw_ref[:, sl] * _silu(g_ref[:, sl])


def _retention_step(proj, s0, gn_w, pos):
    batch = proj.shape[0]
    rows = 8
    _, q_dec, k_dec, chunk_dec = _ret_tables(1)
    cos, sin = _rope_tables(pos)
    col = lambda j: pl.BlockSpec((rows, D_MODEL), lambda i, j=j: (i, j))
    const2 = lambda shape: pl.BlockSpec(shape, lambda i: (0, 0))
    state = pl.BlockSpec((rows, RET_HEADS, RET_D, RET_D), lambda i: (i, 0, 0, 0))
    return pl.pallas_call(
        functools.partial(_ret_step_body, rows=rows),
        grid=(batch // rows,),
        in_specs=[col(COL_Q // D_MODEL), col(COL_K // D_MODEL), col(COL_V // D_MODEL),
                  col(COL_GSW // D_MODEL),
                  const2((1, RET_D)), const2((1, RET_D)),
                  const2((1, D_MODEL)), const2((1, D_MODEL)), const2((1, D_MODEL)),
                  const2((1, D_MODEL)), state],
        out_specs=[pl.BlockSpec((rows, D_MODEL), lambda i: (i, 0)), state],
        out_shape=[jax.ShapeDtypeStruct((batch, D_MODEL), F32),
                   jax.ShapeDtypeStruct(s0.shape, F32)],
        compiler_params=_params(("parallel",)),
        name="retention_step",
    )(proj, proj, proj, proj, cos, sin, q_dec, k_dec, chunk_dec, gn_w, s0)


def _head_ones():
    row = lax.broadcasted_iota(jnp.int32, (LANES, LANES), 0)
    col = lax.broadcasted_iota(jnp.int32, (LANES, LANES), 1)
    return jnp.where((row // RWKV_HEAD) == (col // RWKV_HEAD), 1.0, 0.0).astype(BF16)


def _head_sum(x, ones_bd):
    blocks = [_lane_sum(x[:, j:j + LANES], ones_bd) for j in range(0, x.shape[1], LANES)]
    return blocks[0] if len(blocks) == 1 else jnp.concatenate(blocks, axis=1)


def _softplus(x):
    return jnp.maximum(x, 0.0) + jnp.log1p(jnp.exp(-jnp.abs(x)))


def _lerp(cur, prev, mu):
    return cur + (prev - cur) * mu


def _rwkv_prep(r_, k_, v_, lora, w0, w2, a0, a2, g2, k_k, k_a, r_k, ones_bd):
    w_lo, a_lo, g_lo = lora[:, 0:128], lora[:, 128:256], lora[:, 256:512]
    w_log = -_softplus(-(w0 + _mm(jnp.tanh(w_lo), w2))) - 0.5
    log_decay = -jnp.exp(w_log)
    a_sig = jax.nn.sigmoid(a0 + _mm(a_lo, a2))
    g = _mm(jax.nn.sigmoid(g_lo), g2)
    kk = k_ * k_k
    kk = kk / jnp.maximum(jnp.sqrt(_head_sum(kk * kk, ones_bd)), 1e-12)
    kmod = k_ * (1.0 + (a_sig - 1.0) * k_a)
    bonus = _head_sum(r_ * kmod * r_k, ones_bd) * v_
    return log_decay, a_sig, g, kk, kmod, bonus


def _rwkv_out(y, bonus, g, lnx_w, lnx_b, ones_bd):
    inv_n = 1.0 / RWKV_HEAD
    yc = y - _head_sum(y, ones_bd) * inv_n
    var = _head_sum(yc * yc, ones_bd) * inv_n
    return (yc * lax.rsqrt(var + RWKV_GN_EPS) * lnx_w + lnx_b + bonus) * g


def _block_diag(x, lane_first):
    return jnp.concatenate([jnp.where(lane_first, x, 0.0), jnp.where(lane_first, 0.0, x)], axis=0)


def _pair_masks():
    n = 2 * RWKV_CHUNK
    row = lax.broadcasted_iota(jnp.int32, (n, n), 0)
    col = lax.broadcasted_iota(jnp.int32, (n, n), 1)
    same = (row // RWKV_CHUNK) == (col // RWKV_CHUNK)
    lane_first = lax.broadcasted_iota(jnp.int32, (1, LANES), 1) < RWKV_HEAD
    return lane_first, same & (col < row), same & (col <= row), row == col


def _rwkv_chunks(chunks, masks):
    lane_first, strict, incl, eye = masks
    n = 2 * RWKV_CHUNK
    rows = lambda x, y: jnp.concatenate([x, y], axis=0)
    cols = lambda x, y: jnp.concatenate([x, y], axis=1)
    bd = lambda x: _block_diag(x, lane_first)

    ops = []
    for r, lw, k, v, a, b, cum in chunks:
        p_inc = jnp.exp(cum)
        p_inv = jnp.exp(-cum)
        p_exc = jnp.exp(cum - lw)
        p_end = p_inc[RWKV_CHUNK - 1:RWKV_CHUNK, :]
        b_t = b * p_inv
        k_t = k * p_inv
        ops.append(dict(a_t=bd(a * p_exc), r_t=bd(r * p_inc), b_t=bd(b_t), k_t=bd(k_t),
                        b_hat=bd(b_t * p_end), k_hat=bd(k_t * p_end), v_bd=bd(v), p_end=p_end))

    for o in ops:
        prod = _mm(rows(o["a_t"], o["r_t"]), rows(o["b_t"], o["k_t"]), _NT)
        o["a_ab"] = jnp.where(strict, prod[:n, :n], 0.0)
        o["a_ak"] = jnp.where(strict, prod[:n, n:], 0.0)
        o["r_b"] = jnp.where(incl, prod[n:, :n], 0.0)
        o["r_k"] = jnp.where(incl, prod[n:, n:], 0.0)

    for o in ops:
        o["t_inv"] = jnp.where(eye, 1.0, 0.0) + o["a_ab"]
        o["power"] = _mm(o["a_ab"], o["a_ab"])
        o["av_rv"] = _mm(rows(o["a_ak"], o["r_k"]), o["v_bd"])
    for _ in range(int(math.log2(RWKV_CHUNK)) - 2):
        for o in ops:
            both = _mm(rows(o["power"], o["t_inv"]), o["power"])
            o["power"] = both[:n]
            o["t_inv"] = o["t_inv"] + both[n:]
    for o in ops:
        o["t_inv"] = o["t_inv"] + _mm(o["t_inv"], o["power"])
    for o in ops:
        o["wu"] = _mm(o["t_inv"], cols(o["a_t"], o["av_rv"][:n]))
    out = []
    for o in ops:
        wu = o["wu"]
        ry_y0 = _mm(o["r_b"], wu)
        r_y = o["r_t"] + ry_y0[:, :LANES]
        y0 = ry_y0[:, LANES:] + o["av_rv"][n:]
        psi = _mm(wu[:, :LANES], o["b_hat"], _TN) + jnp.where(eye, o["p_end"], 0.0)
        s_add = _mm(rows(wu[:, LANES:], o["v_bd"]), rows(o["b_hat"], o["k_hat"]), _TN)
        out.append((r_y, y0, psi, s_add))
    return out


def _shift_rows(cur, first_row):
    rolled = pltpu.roll(cur, 1, 0)
    row = lax.broadcasted_iota(jnp.int32, cur.shape, 0)
    return jnp.where(row == 0, first_row, rolled)


def _rwkv_body(r_ref, k_ref, v_ref, lora_ref, mu_ref, mul_ref,
               w0_ref, w2_ref, a0_ref, a2_ref, g2_ref, kk_ref, ka_ref, rk_ref,
               lnw_ref, lnb_ref, o_ref, sout_ref,
               s_ref, carry_ref, carryl_ref, *, tile):
    t = pl.program_id(1)
    p = pl.program_id(2)
    masks = _pair_masks()
    ones_bd = _head_ones()
    pairs = r_ref.shape[1] // LANES

    @pl.when(t == 0)
    def _():
        for q in range(pairs):
            s_ref[p * pairs + q] = jnp.zeros((LANES, LANES), F32)
        carry_ref[p] = jnp.zeros(carry_ref.shape[1:], F32)

        @pl.when(p == 0)
        def _():
            carryl_ref[...] = jnp.zeros_like(carryl_ref)

    prev = carry_ref[p]
    cur_r, cur_k, cur_v, cur_l = r_ref[...], k_ref[...], v_ref[...], lora_ref[...]
    mu = mu_ref[0]
    r_ = _lerp(cur_r, _shift_rows(cur_r, prev[0:1]), mu[0:1])
    k_ = _lerp(cur_k, _shift_rows(cur_k, prev[1:2]), mu[1:2])
    v_ = _lerp(cur_v, _shift_rows(cur_v, prev[2:3]), mu[2:3])
    lo = _lerp(cur_l, _shift_rows(cur_l, carryl_ref[0:1, :]), mul_ref[...])
    carry_ref[p, 0:1, :] = cur_r[tile - 1:tile]
    carry_ref[p, 1:2, :] = cur_k[tile - 1:tile]
    carry_ref[p, 2:3, :] = cur_v[tile - 1:tile]

    @pl.when(p == pl.num_programs(2) - 1)
    def _():
        carryl_ref[0:1, :] = cur_l[tile - 1:tile]

    log_decay, a_sig, g, kk, kmod, bonus = _rwkv_prep(
        r_, k_, v_, lo, w0_ref[...], w2_ref[...], a0_ref[...], a2_ref[...], g2_ref[...],
        kk_ref[...], ka_ref[...], rk_ref[...], ones_bd)

    row = lax.broadcasted_iota(jnp.int32, (tile, tile), 0)
    col = lax.broadcasted_iota(jnp.int32, (tile, tile), 1)
    tri = jnp.where(((row // RWKV_CHUNK) == (col // RWKV_CHUNK)) & (col <= row), 1.0, 0.0).astype(BF16)
    cum = sum(jnp.dot(tri, part, preferred_element_type=F32) for part in _split_bf16(log_decay, 3))

    neg_kk = -kk
    kk_a = kk * a_sig
    n_chunks = tile // RWKV_CHUNK
    chunks = []
    for c in range(n_chunks):
        for q in range(pairs):
            at = (slice(c * RWKV_CHUNK, (c + 1) * RWKV_CHUNK), slice(q * LANES, (q + 1) * LANES))
            chunks.append((r_[at], log_decay[at], kmod[at], v_[at], neg_kk[at], kk_a[at], cum[at]))
    parts = _rwkv_chunks(chunks, masks)

    states = [s_ref[p * pairs + q] for q in range(pairs)]
    ys = [[] for _ in range(pairs)]
    for c in range(n_chunks):
        for q in range(pairs):
            r_y, y0, psi, s_add = parts[c * pairs + q]
            y_bd = _mm(r_y, states[q], _NT) + y0
            states[q] = _mm(states[q], psi) + s_add
            ys[q].append(y_bd[:RWKV_CHUNK] + y_bd[RWKV_CHUNK:])
    for q in range(pairs):
        s_ref[p * pairs + q] = states[q]

    @pl.when(t == pl.num_programs(1) - 1)
    def _():
        for q in range(pairs):
            head = 2 * (p * pairs + q)
            sout_ref[0, head] = states[q][:RWKV_HEAD, :RWKV_HEAD]
            sout_ref[0, head + 1] = states[q][RWKV_HEAD:, RWKV_HEAD:]

    y = jnp.concatenate([jnp.concatenate(yq, axis=0) for yq in ys], axis=1)
    o_ref[...] = _rwkv_out(y, bonus, g, lnw_ref[...], lnb_ref[...], ones_bd)


RWKV_PAIRS_PER_STEP = 8


def _rwkv_prompt(proj, wts, batch, seq):
    tile = math.gcd(seq, 256)
    n_tiles = seq // tile
    width = RWKV_PAIRS_PER_STEP * LANES
    blk = lambda base: pl.BlockSpec((tile, width),
                                    lambda b, t, p, base=base: (b * n_tiles + t, base // width + p))
    per_pair = lambda rows: pl.BlockSpec((rows, width), lambda b, t, p: (0, p))
    in_specs = [
        blk(COL_R), blk(COL_KK), blk(COL_VV),
        pl.BlockSpec((tile, LORA_W), lambda b, t, p: (b * n_tiles + t, COL_LORA // LORA_W)),
        pl.BlockSpec((1, 3, width), lambda b, t, p: (0, 0, p)),
        pl.BlockSpec((1, LORA_W), lambda b, t, p: (0, 0)),
        per_pair(1), per_pair(LANES),
        per_pair(1), per_pair(LANES),
        per_pair(2 * LANES),
        per_pair(1), per_pair(1), per_pair(1),
        per_pair(1), per_pair(1),
    ]
    return pl.pallas_call(
        functools.partial(_rwkv_body, tile=tile),
        grid=(batch, n_tiles, RWKV_PAIRS // RWKV_PAIRS_PER_STEP),
        in_specs=in_specs,
        out_specs=[pl.BlockSpec((tile, width), lambda b, t, p: (b * n_tiles + t, p)),
                   pl.BlockSpec((1, RWKV_HEADS, RWKV_HEAD, RWKV_HEAD), lambda b, t, p: (b, 0, 0, 0))],
        out_shape=[jax.ShapeDtypeStruct((batch * seq, D_MODEL), F32),
                   jax.ShapeDtypeStruct((batch, RWKV_HEADS, RWKV_HEAD, RWKV_HEAD), F32)],
        scratch_shapes=[pltpu.VMEM((RWKV_PAIRS, LANES, LANES), F32),
                        pltpu.VMEM((RWKV_PAIRS // RWKV_PAIRS_PER_STEP, 8, width), F32),
                        pltpu.VMEM((8, LORA_W), F32)],
        compiler_params=_params(("parallel", "arbitrary", "arbitrary")),
        name="rwkv_prompt",
    )(proj, proj, proj, proj, wts["mu3"], wts["mu_lora"],
      wts["w0"], wts["w2"], wts["a0"], wts["a2"], wts["g2"],
      wts["k_k"], wts["k_a"], wts["r_k"], wts["lnx_w"], wts["lnx_b"])


def _rwkv_step_body(r_ref, k_ref, v_ref, lora_ref, pr_ref, pk_ref, pv_ref, plora_ref,
                    mu_ref, mul_ref, w0_ref, w2_ref, a0_ref, a2_ref, g2_ref,
                    kk_ref, ka_ref, rk_ref, lnw_ref, lnb_ref, s_ref,
                    o_ref, sout_ref, *, rows):
    ones_bd = _head_ones()
    mu = mu_ref[0]
    r_ = _lerp(r_ref[...], pr_ref[...], mu[0:1])
    k_ = _lerp(k_ref[...], pk_ref[...], mu[1:2])
    v_ = _lerp(v_ref[...], pv_ref[...], mu[2:3])
    lo = _lerp(lora_ref[...], plora_ref[...], mul_ref[...])
    log_decay, a_sig, g, kk, kmod, bonus = _rwkv_prep(
        r_, k_, v_, lo, w0_ref[...], w2_ref[...], a0_ref[...], a2_ref[...], g2_ref[...],
        kk_ref[...], ka_ref[...], rk_ref[...], ones_bd)
    decay = jnp.exp(log_decay)
    neg_kk = -kk
    kk_a = kk * a_sig
    decay_r = decay * r_
    row = lax.broadcasted_iota(jnp.int32, (rows, RWKV_HEAD), 0)

    heads = [slice(h * RWKV_HEAD, (h + 1) * RWKV_HEAD) for h in range(RWKV_HEADS)]
    reads = []
    for h, sl in enumerate(heads):
        lhs = jnp.concatenate([neg_kk[:, sl], decay_r[:, sl]], axis=0)
        reads.append([_mm(lhs, s_ref[b, h], _NT) for b in range(rows)])
    ys = []
    for h, sl in enumerate(heads):
        w_h, r_h, v_h, k_h, b_h = decay[:, sl], r_[:, sl], v_[:, sl], kmod[:, sl], kk_a[:, sl]
        rhs = jnp.concatenate([b_h, k_h], axis=0)
        sa_all = swr_all = jnp.zeros((rows, RWKV_HEAD), F32)
        adds = []
        for b in range(rows):
            mine = row == b
            sa, swr = reads[h][b][:rows], reads[h][b][rows:]
            left = jnp.concatenate([jnp.where(mine, sa, 0.0), jnp.where(mine, v_h, 0.0)], axis=0)
            adds.append(_mm(left, rhs, _TN))
            sa_all = jnp.where(mine, sa, sa_all)
            swr_all = jnp.where(mine, swr, swr_all)
        for b in range(rows):
            sout_ref[b, h] = s_ref[b, h] * w_h[b:b + 1] + adds[b]
        ys.append(swr_all + sa_all * jnp.sum(b_h * r_h, axis=-1, keepdims=True)
                  + v_h * jnp.sum(k_h * r_h, axis=-1, keepdims=True))
    y = jnp.concatenate(ys, axis=1)
    o_ref[...] = _rwkv_out(y, bonus, g, lnw_ref[...], lnb_ref[...], ones_bd)


def _rwkv_step(proj, proj_prev, s0, wts):
    batch = proj.shape[0]
    rows = 8
    wide = lambda base: pl.BlockSpec((rows, D_MODEL), lambda i, base=base: (i, base // D_MODEL))
    lora = pl.BlockSpec((rows, LORA_W), lambda i: (i, COL_LORA // LORA_W))
    full = lambda shape: pl.BlockSpec(shape, lambda i: (0,) * len(shape))
    state = pl.BlockSpec((rows, RWKV_HEADS, RWKV_HEAD, RWKV_HEAD), lambda i: (i, 0, 0, 0))
    vec = full((1, D_MODEL))
    return pl.pallas_call(
        functools.partial(_rwkv_step_body, rows=rows),
        grid=(batch // rows,),
        in_specs=[wide(COL_R), wide(COL_KK), wide(COL_VV), lora,
                  wide(COL_R), wide(COL_KK), wide(COL_VV), lora,
                  full((1, 3, D_MODEL)), full((1, LORA_W)),
                  vec, full((LANES, D_MODEL)), vec, full((LANES, D_MODEL)),
                  full((2 * LANES, D_MODEL)), vec, vec, vec, vec, vec, state],
        out_specs=[pl.BlockSpec((rows, D_MODEL), lambda i: (i, 0)), state],
        out_shape=[jax.ShapeDtypeStruct((batch, D_MODEL), F32),
                   jax.ShapeDtypeStruct(s0.shape, F32)],
        compiler_params=_params(("parallel",)),
        name="rwkv_step",
    )(proj, proj, proj, proj, proj_prev, proj_prev, proj_prev, proj_prev,
      wts["mu3"], wts["mu_lora"], wts["w0"], wts["w2"], wts["a0"], wts["a2"], wts["g2"],
      wts["k_k"], wts["k_a"], wts["r_k"], wts["lnx_w"], wts["lnx_b"], s0)


def _merge_body(x_ref, oret_ref, orwkv_ref, ga_ref, gb_ref, wout_ref, nffn_ref, rw_ref, rb_ref,
                x1_ref, xn_ref, comb_ref, cnt_ref):
    merged = (jax.nn.sigmoid(ga_ref[...]) * oret_ref[...]
              + jax.nn.sigmoid(gb_ref[...]) * orwkv_ref[...])
    x1 = x_ref[...] + _mm(merged, wout_ref[...])
    x1_ref[...] = x1
    xn = _rms(x1, nffn_ref[...])
    xn_ref[...] = xn.astype(BF16)

    logits = _mm(xn, rw_ref[...]) + rb_ref[...]
    lane = lax.broadcasted_iota(jnp.int32, logits.shape, 1)
    neg_inf = -jnp.inf
    is_group = lane < N_GROUPS
    g_max = jnp.max(jnp.where(is_group, logits, neg_inf), axis=-1, keepdims=True)
    g_sel = jnp.min(jnp.where(is_group & (logits == g_max), lane, LANES), axis=-1, keepdims=True)
    g_w = 1.0 / jnp.sum(jnp.where(is_group, jnp.exp(logits - g_max), 0.0), axis=-1, keepdims=True)
    first = N_GROUPS + EXPERTS_PER_GROUP * g_sel
    in_group = (lane >= first) & (lane < first + EXPERTS_PER_GROUP)
    e_max = jnp.max(jnp.where(in_group, logits, neg_inf), axis=-1, keepdims=True)
    e_exp = jnp.where(in_group, jnp.exp(logits - e_max), 0.0)
    prob = e_exp / jnp.sum(e_exp, axis=-1, keepdims=True)
    prob = jnp.where(in_group, prob, -1.0)
    p1 = jnp.max(prob, axis=-1, keepdims=True)
    i1 = jnp.min(jnp.where(prob == p1, lane, LANES), axis=-1, keepdims=True)
    rest = jnp.where(lane == i1, -1.0, prob)
    p2 = jnp.max(rest, axis=-1, keepdims=True)
    i2 = jnp.min(jnp.where(rest == p2, lane, LANES), axis=-1, keepdims=True)
    denom = p1 + p2
    comb = (jnp.where(lane == i1 - N_GROUPS, g_w * p1 / denom, 0.0)
            + jnp.where(lane == i2 - N_GROUPS, g_w * p2 / denom, 0.0))
    comb_ref[...] = jnp.where(lane == N_EXPERTS, g_sel.astype(F32), comb)
    lane8 = lax.broadcasted_iota(jnp.int32, (8, LANES), 1)
    sizes = jnp.zeros((8, LANES), jnp.int32)
    for g in range(N_GROUPS):
        n_g = jnp.sum(jnp.where(g_sel == g, 1, 0), axis=0, keepdims=True)
        sizes = jnp.where(lane8 == g, n_g, sizes)
    cnt_ref[0] = sizes


MERGE_TILE = 512


def _merge(x, o_ret, o_rwkv, proj, wts):
    t = x.shape[0]
    tm = math.gcd(t, MERGE_TILE)
    row = lambda: pl.BlockSpec((tm, D_MODEL), lambda i: (i, 0))
    full = lambda shape: pl.BlockSpec(shape, lambda i: (0, 0))
    return pl.pallas_call(
        _merge_body,
        grid=(t // tm,),
        in_specs=[row(), row(), row(),
                  pl.BlockSpec((tm, D_MODEL), lambda i: (i, COL_GATE_A // D_MODEL)),
                  pl.BlockSpec((tm, D_MODEL), lambda i: (i, COL_GATE_B // D_MODEL)),
                  full((D_MODEL, D_MODEL)), full((1, D_MODEL)),
                  full((D_MODEL, LANES)), full((1, LANES))],
        out_specs=[row(), row(), pl.BlockSpec((tm, LANES), lambda i: (i, 0)),
                   pl.BlockSpec((1, 8, LANES), lambda i: (i, 0, 0))],
        out_shape=[jax.ShapeDtypeStruct((t, D_MODEL), F32),
                   jax.ShapeDtypeStruct((t, D_MODEL), BF16),
                   jax.ShapeDtypeStruct((t, LANES), F32),
                   jax.ShapeDtypeStruct((t // tm, 8, LANES), jnp.int32)],
        compiler_params=_params(("parallel",)),
        name="merge_router",
    )(x, o_ret, o_rwkv, proj, proj, wts["w_out"], wts["norm_ffn"], wts["router_w"], wts["router_b"])


MOE_TILE = 1024
MOE_BLOCK = 128
EXPERTS_PER_STEP = 4


def _moe_body(cnt_ref, xn_ref, x1_ref, comb_ref, wg_ref, wu_ref, wd_ref, nf_ref, o_ref,
              acc_ref, xs_ref, cs_ref, pt_ref, *, ratio):
    i = pl.program_id(0)
    s = pl.program_id(1)
    tm = xn_ref.shape[0]
    group = s // (EXPERTS_PER_GROUP // EXPERTS_PER_STEP)

    sizes = []
    for g in range(N_GROUPS):
        n = cnt_ref[i * ratio * N_GROUPS + g]
        for k in range(1, ratio):
            n = n + cnt_ref[(i * ratio + k) * N_GROUPS + g]
        sizes.append(n)
    starts = [jnp.int32(0)]
    for g in range(N_GROUPS - 1):
        starts.append(starts[-1] + sizes[g])

    @pl.when(s == 0)
    def _():
        comb = comb_ref[...]
        g_row = comb.T[N_EXPERTS:N_EXPERTS + 1, :]
        gid = lax.broadcasted_iota(jnp.int32, (8, tm), 0)
        onehot = jnp.where(g_row.astype(jnp.int32) == gid, 1.0, 0.0)
        step = math.gcd(tm, 256)
        row = lax.broadcasted_iota(jnp.int32, (step, tm), 0)
        col = lax.broadcasted_iota(jnp.int32, (step, tm), 1)
        rank = jnp.zeros((8, tm), F32)
        for r0 in range(0, tm, step):
            earlier = jnp.where(row + r0 < col, 1.0, 0.0).astype(BF16)
            rank = rank + jnp.dot(onehot[:, r0:r0 + step].astype(BF16), earlier,
                                  preferred_element_type=F32)
        start_col = jnp.zeros((8, 1), F32)
        gid_col = lax.broadcasted_iota(jnp.int32, (8, 1), 0)
        for g in range(1, N_GROUPS):
            start_col = jnp.where(gid_col == g, starts[g].astype(F32), start_col)
        pos_row = jnp.sum(onehot * (rank + start_col), axis=0, keepdims=True)
        pos_row_i = pos_row.astype(jnp.int32)
        xn = xn_ref[...]
        comb_parts = _split_bf16(comb, 3)
        for r0 in range(0, tm, step):
            perm = jnp.where(row + r0 == pos_row_i, 1.0, 0.0).astype(BF16)
            xs_ref[r0:r0 + step, :] = jnp.dot(perm, xn, preferred_element_type=F32).astype(BF16)
            cs_ref[r0:r0 + step, :] = sum(jnp.dot(perm, part, preferred_element_type=F32)
                                          for part in comb_parts)
        pos_col_i = jnp.broadcast_to(pos_row, (LANES, tm)).T.astype(jnp.int32)
        lane = lax.broadcasted_iota(jnp.int32, (tm, LANES), 1)
        for j in range(tm // LANES):
            pt_ref[:, j * LANES:(j + 1) * LANES] = jnp.where(
                lane + j * LANES == pos_col_i, 1.0, 0.0).astype(BF16)
        acc_ref[...] = jnp.zeros_like(acc_ref)

    start, size = starts[N_GROUPS - 1], sizes[N_GROUPS - 1]
    for g in range(N_GROUPS - 2, -1, -1):
        start = jnp.where(group == g, starts[g], start)
        size = jnp.where(group == g, sizes[g], size)
    blk = math.gcd(tm, MOE_BLOCK)
    first = start // blk
    last = jnp.where(size > 0, (start + size + blk - 1) // blk, first)
    lane = lax.broadcasted_iota(jnp.int32, (blk, LANES), 1)

    def block(j, carry):
        r0 = pl.multiple_of(j * blk, blk)
        xb = xs_ref[pl.ds(r0, blk), :]
        cw = cs_ref[pl.ds(r0, blk), :]
        hs = []
        for k in range(EXPERTS_PER_STEP):
            weight = jnp.sum(jnp.where(lane == s * EXPERTS_PER_STEP + k, cw, 0.0),
                             axis=-1, keepdims=True)
            hs.append(_silu(_mm(xb, wg_ref[k])) * _mm(xb, wu_ref[k]) * weight)
        total = _mm(hs[0], wd_ref[0])
        for k in range(1, EXPERTS_PER_STEP):
            total = total + _mm(hs[k], wd_ref[k])
        acc_ref[pl.ds(r0, blk), :] += total
        return carry

    lax.fori_loop(first, last, block, 0)

    @pl.when(s == pl.num_programs(1) - 1)
    def _():
        parts = _split_bf16(acc_ref[...], 2)
        step = math.gcd(tm, 256)
        for r0 in range(0, tm, step):
            moe = sum(jnp.dot(pt_ref[r0:r0 + step, :], part, preferred_element_type=F32)
                      for part in parts)
            o_ref[r0:r0 + step, :] = _rms(x1_ref[r0:r0 + step, :] + moe, nf_ref[...])


def _moe(xn, x1, comb, counts, wts):
    t = xn.shape[0]
    tm = math.gcd(t, MOE_TILE)
    assert tm % LANES == 0, "token count must be a multiple of the lane width"
    ratio = tm // math.gcd(t, MERGE_TILE)
    row = lambda w: pl.BlockSpec((tm, w), lambda i, s, cnt: (i, 0))
    once = lambda w: pl.BlockSpec((tm, w), lambda i, s, cnt: (i, 0), pipeline_mode=pl.Buffered(1))
    expert = lambda a, b: pl.BlockSpec((EXPERTS_PER_STEP, a, b), lambda i, s, cnt: (s, 0, 0))
    grid_spec = pltpu.PrefetchScalarGridSpec(
        num_scalar_prefetch=1,
        grid=(t // tm, N_EXPERTS // EXPERTS_PER_STEP),
        in_specs=[once(D_MODEL), once(D_MODEL), once(LANES),
                  expert(D_MODEL, D_EXPERT), expert(D_MODEL, D_EXPERT), expert(D_EXPERT, D_MODEL),
                  pl.BlockSpec((1, D_MODEL), lambda i, s, cnt: (0, 0))],
        out_specs=row(D_MODEL),
        scratch_shapes=[pltpu.VMEM((tm, D_MODEL), F32),
                        pltpu.VMEM((tm, D_MODEL), BF16),
                        pltpu.VMEM((tm, LANES), F32),
                        pltpu.VMEM((tm, tm), BF16)])
    return pl.pallas_call(
        functools.partial(_moe_body, ratio=ratio),
        grid_spec=grid_spec,
        out_shape=jax.ShapeDtypeStruct((t, D_MODEL), F32),
        compiler_params=_params(("parallel", "arbitrary")),
        name="moe",
    )(counts, xn, x1, comb, wts["w_gate"], wts["w_up"], wts["w_down"], wts["norm_final"])


def _pad_to(x, size, axis):
    pad = [(0, 0)] * x.ndim
    pad[axis] = (0, size - x.shape[axis])
    return jnp.pad(x, pad)


def _prepare_weights(norm_mix, w_in, mu_shift, ret_gn_w, rwkv_w0, rwkv_w2, rwkv_a0, rwkv_a2,
                     rwkv_g2, rwkv_k_k, rwkv_k_a, rwkv_r_k, rwkv_lnx_w, rwkv_lnx_b, w_out,
                     norm_ffn, router_group_w, router_group_b, router_expert_w, router_expert_b,
                     expert_w_gate, expert_w_up, expert_w_down, norm_final):
    def lora_cols(t):
        return jnp.concatenate([_pad_to(t[..., SRC_WLO:SRC_ALO], 128, -1),
                                _pad_to(t[..., SRC_ALO:SRC_GLO], 128, -1),
                                _pad_to(t[..., SRC_GLO:SRC_GATE_A], 256, -1)], axis=-1)

    w = w_in[0]
    w_in_p = jnp.concatenate([w[:, :SRC_WLO], w[:, SRC_GATE_A:], lora_cols(w)], axis=1).astype(BF16)
    mu = jnp.concatenate([jnp.zeros((SRC_SHIFT,), F32), mu_shift[0]])
    row = lambda v: v.reshape(1, -1)
    router_w = _pad_to(jnp.concatenate([router_group_w[0], router_expert_w[0]], axis=1), LANES, 1)
    router_b = _pad_to(jnp.concatenate([router_group_b[0], router_expert_b[0]]), LANES, 0)
    return dict(
        w_in=w_in_p, norm_mix=row(norm_mix[0]), ret_gn_w=row(ret_gn_w[0]),
        mu3=mu[SRC_SHIFT:SRC_WLO].reshape(1, 3, D_MODEL), mu_lora=lora_cols(mu).reshape(1, LORA_W),
        w0=row(rwkv_w0[0]), w2=_pad_to(rwkv_w2[0], 128, 0),
        a0=row(rwkv_a0[0]), a2=_pad_to(rwkv_a2[0], 128, 0),
        g2=_pad_to(rwkv_g2[0], 256, 0),
        k_k=row(rwkv_k_k[0]), k_a=row(rwkv_k_a[0]), r_k=row(rwkv_r_k[0]),
        lnx_w=row(rwkv_lnx_w[0]), lnx_b=row(rwkv_lnx_b[0]),
        w_out=w_out[0].astype(BF16), norm_ffn=row(norm_ffn[0]),
        router_w=router_w, router_b=row(router_b),
        w_gate=expert_w_gate[0].astype(BF16), w_up=expert_w_up[0].astype(BF16),
        w_down=expert_w_down[0].astype(BF16), norm_final=row(norm_final))


def _finish(x, o_ret, o_rwkv, proj, wts):
    x1, xn2, comb, sizes = _merge(x, o_ret, o_rwkv, proj, wts)
    counts = sizes[:, 0, :N_GROUPS].reshape(-1)
    return _moe(xn2, x1, comb, counts, wts)


def kernel(x_prompt, x_sample, state_ret, state_rwkv, state_shift, norm_mix, w_in, mu_shift, ret_gn_w, rwkv_w0, rwkv_w2, rwkv_a0, rwkv_a2, rwkv_g2, rwkv_k_k, rwkv_k_a, rwkv_r_k, rwkv_lnx_w, rwkv_lnx_b, w_out, norm_ffn, router_group_w, router_group_b, router_expert_w, router_expert_b, expert_w_gate, expert_w_up, expert_w_down, norm_final):
    assert norm_mix.shape[0] == 1, "single-layer step"
    wts = _prepare_weights(norm_mix, w_in, mu_shift, ret_gn_w, rwkv_w0, rwkv_w2, rwkv_a0, rwkv_a2,
                           rwkv_g2, rwkv_k_k, rwkv_k_a, rwkv_r_k, rwkv_lnx_w, rwkv_lnx_b, w_out,
                           norm_ffn, router_group_w, router_group_b, router_expert_w,
                           router_expert_b, expert_w_gate, expert_w_up, expert_w_down, norm_final)

    bp, lp, _ = x_prompt.shape
    xp = x_prompt.reshape(bp * lp, D_MODEL)
    tile_p = math.gcd(lp, 512)
    xb_p, tail_p = _mixer_norm(xp, wts["norm_mix"], tile_p, 8)
    proj_p = _inproj(xb_p, wts["w_in"])
    o_ret_p, ret_p = _retention_prompt(proj_p, wts["ret_gn_w"], bp, lp)
    o_rwkv_p, rwkv_p = _rwkv_prompt(proj_p, wts, bp, lp)
    y_prompt = _finish(xp, o_ret_p, o_rwkv_p, proj_p, wts).reshape(bp, lp, D_MODEL)
    shift_p = tail_p.reshape(bp, lp // tile_p, 8, D_MODEL)[:, -1, -1]

    bs, ls, _ = x_sample.shape
    assert ls == 1, "sample group advances one token"
    xs = x_sample.reshape(bs, D_MODEL)
    xb_s, xn_s = _mixer_norm(xs, wts["norm_mix"], bs, bs)
    proj_s = _inproj(xb_s, wts["w_in"])
    proj_prev = _inproj(state_shift[0].astype(BF16), wts["w_in"])
    pos_s = PAST_LEN + jnp.arange(ls, dtype=F32)
    o_ret_s, ret_s = _retention_step(proj_s, state_ret[0], wts["ret_gn_w"], pos_s)
    o_rwkv_s, rwkv_s = _rwkv_step(proj_s, proj_prev, state_rwkv[0], wts)
    y_sample = _finish(xs, o_ret_s, o_rwkv_s, proj_s, wts).reshape(bs, ls, D_MODEL)

    return (y_prompt, y_sample, ret_p[None], rwkv_p[None], shift_p[None],
            ret_s[None], rwkv_s[None], xn_s)
```

```python
import functools
import math

import jax
import jax.numpy as jnp
from jax import lax
from jax.experimental import pallas as pl
from jax.experimental.pallas import tpu as pltpu

F32 = jnp.float32
BF16 = jnp.bfloat16

D_MODEL = 1024
LANES = 128
PAST_LEN = 16384
RET_HEADS = 8
RET_D = D_MODEL // RET_HEADS
RET_CHUNK = 128
RET_CHUNKS_PER_STEP = 2
ROPE_BASE = 10000.0
RET_GN_EPS = 1e-5
RWKV_HEAD = 64
RWKV_HEADS = D_MODEL // RWKV_HEAD
RWKV_PAIRS = RWKV_HEADS // 2
RWKV_CHUNK = 64
LORA_DECAY = 64
LORA_A = 64
LORA_G = 160
RWKV_GN_EPS = 64e-5
N_GROUPS = 4
EXPERTS_PER_GROUP = 8
N_EXPERTS = N_GROUPS * EXPERTS_PER_GROUP
D_EXPERT = 256
RMS_EPS = 1e-6

SRC_SHIFT = 4 * D_MODEL
SRC_WLO = SRC_SHIFT + 3 * D_MODEL
SRC_ALO = SRC_WLO + LORA_DECAY
SRC_GLO = SRC_ALO + LORA_A
SRC_GATE_A = SRC_GLO + LORA_G
COL_Q, COL_K, COL_V, COL_GSW = 0, 1024, 2048, 3072
COL_R, COL_KK, COL_VV = 4096, 5120, 6144
COL_GATE_A, COL_GATE_B = 7168, 8192
COL_LORA = 9216
LORA_W = 512
N_PROJ = COL_LORA + LORA_W

VMEM_LIMIT = 48 * 1024 * 1024

_NN = (((1,), (0,)), ((), ()))
_NT = (((1,), (1,)), ((), ()))
_TN = (((0,), (0,)), ((), ()))


def _mm(a, b, dims=_NN, exact=False):
    if exact:
        return lax.dot_general(a.astype(F32), b.astype(F32), dims,
                               precision=lax.Precision.HIGHEST,
                               preferred_element_type=F32)
    return lax.dot_general(a.astype(BF16), b.astype(BF16), dims,
                           preferred_element_type=F32)


def _params(sem):
    return pltpu.CompilerParams(dimension_semantics=sem,
                                vmem_limit_bytes=VMEM_LIMIT)


def _rms(x, gain):
    return x * lax.rsqrt(jnp.mean(x * x, axis=-1, keepdims=True) + RMS_EPS) * gain


def _split_bf16(x, terms):
    parts = []
    for _ in range(terms - 1):
        hi = x.astype(BF16)
        parts.append(hi)
        x = x - hi.astype(F32)
    parts.append(x.astype(BF16))
    return parts


def _lane_sum(x, ones):
    hi, lo = _split_bf16(x, 2)
    return (jnp.dot(hi, ones, preferred_element_type=F32)
            + jnp.dot(lo, ones, preferred_element_type=F32))


INPROJ_ROWS = 128


def _inproj_body(x_ref, g_ref, w_ref, o_ref, xn_ref, *, normed, keep):
    step = math.gcd(x_ref.shape[0], INPROJ_ROWS)
    for r0 in range(0, x_ref.shape[0], step):
        x = x_ref[r0:r0 + step, :]
        if r0 < normed:
            x = _rms(x, g_ref[...])
            lo, hi = max(r0, normed - keep), r0 + step
            if lo < hi:
                xn_ref[0, 0, lo - (normed - keep):hi - (normed - keep), :] = x[lo - r0:, :]
        o_ref[r0:r0 + step, :] = jnp.dot(x.astype(BF16), w_ref[...], preferred_element_type=F32)


def _inproj(x, gain, w, tm, normed, keep):
    t = x.shape[0]
    n = w.shape[1]
    tn = n // 4
    assert normed % math.gcd(tm, INPROJ_ROWS) == 0 and keep <= normed <= tm
    proj, kept = pl.pallas_call(
        functools.partial(_inproj_body, normed=normed, keep=keep),
        grid=(n // tn, t // tm),
        in_specs=[pl.BlockSpec((tm, D_MODEL), lambda j, i: (i, 0)),
                  pl.BlockSpec((1, D_MODEL), lambda j, i: (0, 0)),
                  pl.BlockSpec((D_MODEL, tn), lambda j, i: (0, j), pipeline_mode=pl.Buffered(1))],
        out_specs=[pl.BlockSpec((tm, tn), lambda j, i: (i, j)),
                   pl.BlockSpec((1, 1, keep, D_MODEL), lambda j, i: (j, i, 0, 0))],
        out_shape=[jax.ShapeDtypeStruct((t, n), F32),
                   jax.ShapeDtypeStruct((n // tn, t // tm, keep, D_MODEL), F32)],
        compiler_params=_params(("arbitrary", "arbitrary")),
        name="inproj",
    )(x, gain, w)
    return proj, kept[0]


def _head_norm(x, eps):
    mu = jnp.mean(x, axis=-1, keepdims=True)
    xc = x - mu
    return xc * lax.rsqrt(jnp.mean(xc * xc, axis=-1, keepdims=True) + eps)


def _silu(x):
    return x * jax.nn.sigmoid(x)


def _rope(x, cos, sin_signed):
    return x * cos + pltpu.roll(x, RET_D // 2, 1) * sin_signed


def _ret_tables(chunk):
    log_gamma = jnp.log1p(-jnp.exp2(-5.0 - jnp.arange(RET_HEADS, dtype=F32)))
    idx = jnp.arange(chunk, dtype=F32)
    rel = idx[:, None] - idx[None, :]
    intra = jnp.where(rel[None] >= 0,
                      jnp.exp(jnp.maximum(rel, 0.0)[None] * log_gamma[:, None, None]), 0.0)
    q_dec = jnp.exp((idx[:, None] + 1.0) * log_gamma[None, :])
    k_dec = jnp.exp((chunk - 1.0 - idx)[None, :] * log_gamma[:, None]).T
    chunk_dec = jnp.exp(chunk * log_gamma)[None, :]
    widen = lambda t: jnp.repeat(t, RET_D, axis=1)
    return intra, widen(q_dec), widen(k_dec), widen(chunk_dec)


def _rope_tables(pos):
    half = RET_D // 2
    inv = ROPE_BASE ** (-jnp.arange(half, dtype=F32) / half)
    ang = pos[:, None] * inv[None, :]
    cos, sin = jnp.cos(ang), jnp.sin(ang)
    return jnp.concatenate([cos, cos], axis=1), jnp.concatenate([-sin, sin], axis=1)


def _ret_body(q_ref, k_ref, v_ref, g_ref, cos_ref, sin_ref, qd_ref, kd_ref, cd_ref,
              intra_ref, gnw_ref, o_ref, sout_ref, s_ref):
    c = pl.program_id(1)
    chunk = qd_ref.shape[0]

    @pl.when(c == 0)
    def _():
        s_ref[...] = jnp.zeros_like(s_ref)

    heads = [slice(h * RET_D, (h + 1) * RET_D) for h in range(RET_HEADS)]
    ones = jnp.ones((RET_D, RET_D), BF16)
    state = [s_ref[h] for h in range(RET_HEADS)]
    for r0 in range(0, q_ref.shape[0], chunk):
        rows = slice(r0, r0 + chunk)
        cos = cos_ref[rows, :]
        sin = sin_ref[rows, :]
        qr = [_rope(q_ref[rows, sl], cos, sin).astype(BF16) for sl in heads]
        kr = [_rope(k_ref[rows, sl], cos, sin) for sl in heads]
        vb = [v_ref[rows, sl].astype(BF16) for sl in heads]
        scores = [_mm(qr[h], kr[h], _NT) * intra_ref[h] for h in range(RET_HEADS)]
        cross = [_mm(qr[h], state[h]) * qd_ref[:, sl] for h, sl in enumerate(heads)]
        adds = [_mm(kr[h] * kd_ref[:, sl], vb[h], _TN) for h, sl in enumerate(heads)]
        out = [_mm(scores[h], vb[h]) + cross[h] for h in range(RET_HEADS)]
        cen = [out[h] - _lane_sum(out[h], ones) * (1.0 / RET_D) for h in range(RET_HEADS)]
        var = [_lane_sum(cen[h] * cen[h], ones) * (1.0 / RET_D) for h in range(RET_HEADS)]
        for h, sl in enumerate(heads):
            state[h] = state[h] * cd_ref[:, sl] + adds[h]
            o_ref[rows, sl] = (cen[h] * lax.rsqrt(var[h] + RET_GN_EPS) * gnw_ref[:, sl]
                               * _silu(g_ref[rows, sl]))
    for h in range(RET_HEADS):
        s_ref[h] = state[h]

    @pl.when(c == pl.num_programs(1) - 1)
    def _():
        sout_ref[0] = s_ref[...]


def _retention_prompt(proj, gn_w, batch, seq):
    chunk = math.gcd(seq, RET_CHUNK)
    tile = math.gcd(seq, RET_CHUNKS_PER_STEP * chunk)
    n_tiles = seq // tile
    intra, q_dec, k_dec, chunk_dec = _ret_tables(chunk)
    intra = intra * (RET_D ** -0.5)
    k_dec = k_dec * (RET_D ** -0.5)
    cos, sin = _rope_tables(jnp.arange(seq, dtype=F32))
    col = lambda j: pl.BlockSpec((tile, D_MODEL), lambda b, c, j=j: (b * n_tiles + c, j))
    const2 = lambda shape: pl.BlockSpec(shape, lambda b, c: (0, 0))
    state = pl.BlockSpec((1, RET_HEADS, RET_D, RET_D), lambda b, c: (b, 0, 0, 0))
    return pl.pallas_call(
        _ret_body,
        grid=(batch, n_tiles),
        in_specs=[col(COL_Q // D_MODEL), col(COL_K // D_MODEL), col(COL_V // D_MODEL),
                  col(COL_GSW // D_MODEL),
                  pl.BlockSpec((tile, RET_D), lambda b, c: (c, 0)),
                  pl.BlockSpec((tile, RET_D), lambda b, c: (c, 0)),
                  const2((chunk, D_MODEL)), const2((chunk, D_MODEL)), const2((1, D_MODEL)),
                  pl.BlockSpec((RET_HEADS, chunk, chunk), lambda b, c: (0, 0, 0)),
                  const2((1, D_MODEL))],
        out_specs=[pl.BlockSpec((tile, D_MODEL), lambda b, c: (b * n_tiles + c, 0)), state],
        out_shape=[jax.ShapeDtypeStruct((batch * seq, D_MODEL), F32),
                   jax.ShapeDtypeStruct((batch, RET_HEADS, RET_D, RET_D), F32)],
        scratch_shapes=[pltpu.VMEM((RET_HEADS, RET_D, RET_D), F32)],
        compiler_params=_params(("parallel", "arbitrary")),
        name="retention_prompt",
    )(proj, proj, proj, proj, cos, sin, q_dec, k_dec, chunk_dec, intra, gn_w)


def _ret_step_body(q_ref, k_ref, v_ref, g_ref, cos_ref, sin_ref, qd_ref, kd_ref, cd_ref,
                   gnw_ref, s_ref, o_ref, sout_ref, *, rows):
    cos = cos_ref[...]
    sin = sin_ref[...]
    row = lax.broadcasted_iota(jnp.int32, (rows, RET_D), 0)
    for h in range(RET_HEADS):
        sl = slice(h * RET_D, (h + 1) * RET_D)
        qr = _rope(q_ref[:, sl], cos, sin)
        kr = _rope(k_ref[:, sl], cos, sin) * (RET_D ** -0.5)
        v = v_ref[:, sl]
        kd = kr * kd_ref[:, sl]
        cd = cd_ref[:, sl]

        reads = [_mm(qr, s_ref[b, h]) for b in range(rows)]
        adds = [_mm(jnp.where(row == b, kd, 0.0), v, _TN) for b in range(rows)]
        cross = jnp.zeros((rows, RET_D), F32)
        for b in range(rows):
            sout_ref[b, h] = s_ref[b, h] * cd + adds[b]
            cross = jnp.where(row == b, reads[b], cross)
        out = jnp.sum(qr * kr, axis=-1, keepdims=True) * v + cross * qd_ref[:, sl]
        o_ref[:, sl] = _head_norm(out, RET_GN_EPS) * gnw_ref[:, sl] * _silu(g_ref[:, sl])


def _retention_step(proj, s0, gn_w, pos):
    batch = s0.shape[0]
    rows = 8
    _, q_dec, k_dec, chunk_dec = _ret_tables(1)
    cos, sin = _rope_tables(pos)
    col = lambda j: pl.BlockSpec((rows, D_MODEL), lambda i, j=j: (i, j))
    const2 = lambda shape: pl.BlockSpec(shape, lambda i: (0, 0))
    state = pl.BlockSpec((rows, RET_HEADS, RET_D, RET_D), lambda i: (i, 0, 0, 0))
    return pl.pallas_call(
        functools.partial(_ret_step_body, rows=rows),
        grid=(batch // rows,),
        in_specs=[col(COL_Q // D_MODEL), col(COL_K // D_MODEL), col(COL_V // D_MODEL),
                  col(COL_GSW // D_MODEL),
                  const2((1, RET_D)), const2((1, RET_D)),
                  const2((1, D_MODEL)), const2((1, D_MODEL)), const2((1, D_MODEL)),
                  const2((1, D_MODEL)), state],
        out_specs=[pl.BlockSpec((rows, D_MODEL), lambda i: (i, 0)), state],
        out_shape=[jax.ShapeDtypeStruct((batch, D_MODEL), F32),
                   jax.ShapeDtypeStruct(s0.shape, F32)],
        compiler_params=_params(("parallel",)),
        name="retention_step",
    )(proj, proj, proj, proj, cos, sin, q_dec, k_dec, chunk_dec, gn_w, s0)


def _head_ones():
    row = lax.broadcasted_iota(jnp.int32, (LANES, LANES), 0)
    col = lax.broadcasted_iota(jnp.int32, (LANES, LANES), 1)
    return jnp.where((row // RWKV_HEAD) == (col // RWKV_HEAD), 1.0, 0.0).astype(BF16)


def _head_sum(x, ones_bd):
    blocks = [_lane_sum(x[:, j:j + LANES], ones_bd) for j in range(0, x.shape[1], LANES)]
    return blocks[0] if len(blocks) == 1 else jnp.concatenate(blocks, axis=1)


def _softplus(x):
    return jnp.maximum(x, 0.0) + jnp.log1p(jnp.exp(-jnp.abs(x)))


def _lerp(cur, prev, mu):
    return cur + (prev - cur) * mu


def _rwkv_prep(r_, k_, v_, lora, w0, w2, a0, a2, g2, k_k, k_a, r_k, ones_bd):
    w_lo, a_lo, g_lo = lora[:, 0:128], lora[:, 128:256], lora[:, 256:512]
    w_log = -_softplus(-(w0 + _mm(jnp.tanh(w_lo), w2))) - 0.5
    log_decay = -jnp.exp(w_log)
    a_sig = jax.nn.sigmoid(a0 + _mm(a_lo, a2))
    g = _mm(jax.nn.sigmoid(g_lo), g2)
    kk = k_ * k_k
    kk = kk / jnp.maximum(jnp.sqrt(_head_sum(kk * kk, ones_bd)), 1e-12)
    kmod = k_ * (1.0 + (a_sig - 1.0) * k_a)
    bonus = _head_sum(r_ * kmod * r_k, ones_bd) * v_
    return log_decay, a_sig, g, kk, kmod, bonus


def _rwkv_out(y, bonus, g, lnx_w, lnx_b, ones_bd):
    inv_n = 1.0 / RWKV_HEAD
    yc = y - _head_sum(y, ones_bd) * inv_n
    var = _head_sum(yc * yc, ones_bd) * inv_n
    return (yc * lax.rsqrt(var + RWKV_GN_EPS) * lnx_w + lnx_b + bonus) * g


def _block_diag(x, lane_first):
    return jnp.concatenate([jnp.where(lane_first, x, 0.0), jnp.where(lane_first, 0.0, x)], axis=0)


def _pair_masks():
    n = 2 * RWKV_CHUNK
    row = lax.broadcasted_iota(jnp.int32, (n, n), 0)
    col = lax.broadcasted_iota(jnp.int32, (n, n), 1)
    same = (row // RWKV_CHUNK) == (col // RWKV_CHUNK)
    lane_first = lax.broadcasted_iota(jnp.int32, (1, LANES), 1) < RWKV_HEAD
    return lane_first, same & (col < row), same & (col <= row), row == col


def _rwkv_chunks(chunks, masks):
    lane_first, strict, incl, eye = masks
    n = 2 * RWKV_CHUNK
    rows = lambda x, y: jnp.concatenate([x, y], axis=0)
    cols = lambda x, y: jnp.concatenate([x, y], axis=1)
    bd = lambda x: _block_diag(x, lane_first)

    ops = []
    for r, lw, k, v, a, b, cum in chunks:
        p_inc = jnp.exp(cum)
        p_inv = jnp.exp(-cum)
        p_exc = jnp.exp(cum - lw)
        p_end = p_inc[RWKV_CHUNK - 1:RWKV_CHUNK, :]
        b_t = b * p_inv
        k_t = k * p_inv
        ops.append(dict(a_t=bd(a * p_exc), r_t=bd(r * p_inc), b_t=bd(b_t), k_t=bd(k_t),
                        b_hat=bd(b_t * p_end), k_hat=bd(k_t * p_end), v_bd=bd(v), p_end=p_end))

    for o in ops:
        prod = _mm(rows(o["a_t"], o["r_t"]), rows(o["b_t"], o["k_t"]), _NT)
        o["a_ab"] = jnp.where(strict, prod[:n, :n], 0.0)
        o["a_ak"] = jnp.where(strict, prod[:n, n:], 0.0)
        o["r_b"] = jnp.where(incl, prod[n:, :n], 0.0)
        o["r_k"] = jnp.where(incl, prod[n:, n:], 0.0)

    for o in ops:
        o["t_inv"] = jnp.where(eye, 1.0, 0.0) + o["a_ab"]
        o["power"] = _mm(o["a_ab"], o["a_ab"])
        o["av_rv"] = _mm(rows(o["a_ak"], o["r_k"]), o["v_bd"])
    for _ in range(int(math.log2(RWKV_CHUNK)) - 2):
        for o in ops:
            both = _mm(rows(o["power"], o["t_inv"]), o["power"])
            o["power"] = both[:n]
            o["t_inv"] = o["t_inv"] + both[n:]
    for o in ops:
        o["t_inv"] = o["t_inv"] + _mm(o["t_inv"], o["power"])
    for o in ops:
        o["wu"] = _mm(o["t_inv"], cols(o["a_t"], o["av_rv"][:n]))
    out = []
    for o in ops:
        wu = o["wu"]
        ry_y0 = _mm(o["r_b"], wu)
        r_y = o["r_t"] + ry_y0[:, :LANES]
        y0 = ry_y0[:, LANES:] + o["av_rv"][n:]
        psi = _mm(wu[:, :LANES], o["b_hat"], _TN) + jnp.where(eye, o["p_end"], 0.0)
        s_add = _mm(rows(wu[:, LANES:], o["v_bd"]), rows(o["b_hat"], o["k_hat"]), _TN)
        out.append((r_y, y0, psi, s_add))
    return out


def _shift_rows(cur, first_row):
    rolled = pltpu.roll(cur, 1, 0)
    row = lax.broadcasted_iota(jnp.int32, cur.shape, 0)
    return jnp.where(row == 0, first_row, rolled)


def _rwkv_body(r_ref, k_ref, v_ref, lora_ref, mu_ref, mul_ref,
               w0_ref, w2_ref, a0_ref, a2_ref, g2_ref, kk_ref, ka_ref, rk_ref,
               lnw_ref, lnb_ref, o_ref, sout_ref,
               s_ref, carry_ref, carryl_ref, *, tile):
    t = pl.program_id(1)
    p = pl.program_id(2)
    masks = _pair_masks()
    ones_bd = _head_ones()
    pairs = r_ref.shape[1] // LANES

    @pl.when(t == 0)
    def _():
        for q in range(pairs):
            s_ref[p * pairs + q] = jnp.zeros((LANES, LANES), F32)
        carry_ref[p] = jnp.zeros(carry_ref.shape[1:], F32)

        @pl.when(p == 0)
        def _():
            carryl_ref[...] = jnp.zeros_like(carryl_ref)

    prev = carry_ref[p]
    cur_r, cur_k, cur_v, cur_l = r_ref[...], k_ref[...], v_ref[...], lora_ref[...]
    mu = mu_ref[0]
    r_ = _lerp(cur_r, _shift_rows(cur_r, prev[0:1]), mu[0:1])
    k_ = _lerp(cur_k, _shift_rows(cur_k, prev[1:2]), mu[1:2])
    v_ = _lerp(cur_v, _shift_rows(cur_v, prev[2:3]), mu[2:3])
    lo = _lerp(cur_l, _shift_rows(cur_l, carryl_ref[0:1, :]), mul_ref[...])
    carry_ref[p, 0:1, :] = cur_r[tile - 1:tile]
    carry_ref[p, 1:2, :] = cur_k[tile - 1:tile]
    carry_ref[p, 2:3, :] = cur_v[tile - 1:tile]

    @pl.when(p == pl.num_programs(2) - 1)
    def _():
        carryl_ref[0:1, :] = cur_l[tile - 1:tile]

    log_decay, a_sig, g, kk, kmod, bonus = _rwkv_prep(
        r_, k_, v_, lo, w0_ref[...], w2_ref[...], a0_ref[...], a2_ref[...], g2_ref[...],
        kk_ref[...], ka_ref[...], rk_ref[...], ones_bd)

    row = lax.broadcasted_iota(jnp.int32, (tile, tile), 0)
    col = lax.broadcasted_iota(jnp.int32, (tile, tile), 1)
    tri = jnp.where(((row // RWKV_CHUNK) == (col // RWKV_CHUNK)) & (col <= row), 1.0, 0.0).astype(BF16)
    cum = sum(jnp.dot(tri, part, preferred_element_type=F32) for part in _split_bf16(log_decay, 3))

    neg_kk = -kk
    kk_a = kk * a_sig
    n_chunks = tile // RWKV_CHUNK
    chunks = []
    for c in range(n_chunks):
        for q in range(pairs):
            at = (slice(c * RWKV_CHUNK, (c + 1) * RWKV_CHUNK), slice(q * LANES, (q + 1) * LANES))
            chunks.append((r_[at], log_decay[at], kmod[at], v_[at], neg_kk[at], kk_a[at], cum[at]))
    parts = _rwkv_chunks(chunks, masks)

    states = [s_ref[p * pairs + q] for q in range(pairs)]
    ys = [[] for _ in range(pairs)]
    for c in range(n_chunks):
        for q in range(pairs):
            r_y, y0, psi, s_add = parts[c * pairs + q]
            y_bd = _mm(r_y, states[q], _NT) + y0
            states[q] = _mm(states[q], psi) + s_add
            ys[q].append(y_bd[:RWKV_CHUNK] + y_bd[RWKV_CHUNK:])
    for q in range(pairs):
        s_ref[p * pairs + q] = states[q]

    @pl.when(t == pl.num_programs(1) - 1)
    def _():
        for q in range(pairs):
            head = 2 * (p * pairs + q)
            sout_ref[0, head] = states[q][:RWKV_HEAD, :RWKV_HEAD]
            sout_ref[0, head + 1] = states[q][RWKV_HEAD:, RWKV_HEAD:]

    y = jnp.concatenate([jnp.concatenate(yq, axis=0) for yq in ys], axis=1)
    o_ref[...] = _rwkv_out(y, bonus, g, lnw_ref[...], lnb_ref[...], ones_bd)


RWKV_PAIRS_PER_STEP = 8


def _rwkv_prompt(proj, wts, batch, seq):
    tile = math.gcd(seq, 256)
    n_tiles = seq // tile
    width = RWKV_PAIRS_PER_STEP * LANES
    blk = lambda base: pl.BlockSpec((tile, width),
                                    lambda b, t, p, base=base: (b * n_tiles + t, base // width + p))
    per_pair = lambda rows: pl.BlockSpec((rows, width), lambda b, t, p: (0, p))
    in_specs = [
        blk(COL_R), blk(COL_KK), blk(COL_VV),
        pl.BlockSpec((tile, LORA_W), lambda b, t, p: (b * n_tiles + t, COL_LORA // LORA_W)),
        pl.BlockSpec((1, 3, width), lambda b, t, p: (0, 0, p)),
        pl.BlockSpec((1, LORA_W), lambda b, t, p: (0, 0)),
        per_pair(1), per_pair(LANES),
        per_pair(1), per_pair(LANES),
        per_pair(2 * LANES),
        per_pair(1), per_pair(1), per_pair(1),
        per_pair(1), per_pair(1),
    ]
    return pl.pallas_call(
        functools.partial(_rwkv_body, tile=tile),
        grid=(batch, n_tiles, RWKV_PAIRS // RWKV_PAIRS_PER_STEP),
        in_specs=in_specs,
        out_specs=[pl.BlockSpec((tile, width), lambda b, t, p: (b * n_tiles + t, p)),
                   pl.BlockSpec((1, RWKV_HEADS, RWKV_HEAD, RWKV_HEAD), lambda b, t, p: (b, 0, 0, 0))],
        out_shape=[jax.ShapeDtypeStruct((batch * seq, D_MODEL), F32),
                   jax.ShapeDtypeStruct((batch, RWKV_HEADS, RWKV_HEAD, RWKV_HEAD), F32)],
        scratch_shapes=[pltpu.VMEM((RWKV_PAIRS, LANES, LANES), F32),
                        pltpu.VMEM((RWKV_PAIRS // RWKV_PAIRS_PER_STEP, 8, width), F32),
                        pltpu.VMEM((8, LORA_W), F32)],
        compiler_params=_params(("parallel", "arbitrary", "arbitrary")),
        name="rwkv_prompt",
    )(proj, proj, proj, proj, wts["mu3"], wts["mu_lora"],
      wts["w0"], wts["w2"], wts["a0"], wts["a2"], wts["g2"],
      wts["k_k"], wts["k_a"], wts["r_k"], wts["lnx_w"], wts["lnx_b"])


def _rwkv_step_body(r_ref, k_ref, v_ref, lora_ref, pr_ref, pk_ref, pv_ref, plora_ref,
                    mu_ref, mul_ref, w0_ref, w2_ref, a0_ref, a2_ref, g2_ref,
                    kk_ref, ka_ref, rk_ref, lnw_ref, lnb_ref, s_ref,
                    o_ref, sout_ref, *, rows):
    ones_bd = _head_ones()
    mu = mu_ref[0]
    r_ = _lerp(r_ref[...], pr_ref[...], mu[0:1])
    k_ = _lerp(k_ref[...], pk_ref[...], mu[1:2])
    v_ = _lerp(v_ref[...], pv_ref[...], mu[2:3])
    lo = _lerp(lora_ref[...], plora_ref[...], mul_ref[...])
    log_decay, a_sig, g, kk, kmod, bonus = _rwkv_prep(
        r_, k_, v_, lo, w0_ref[...], w2_ref[...], a0_ref[...], a2_ref[...], g2_ref[...],
        kk_ref[...], ka_ref[...], rk_ref[...], ones_bd)
    decay = jnp.exp(log_decay)
    neg_kk = -kk
    kk_a = kk * a_sig
    decay_r = decay * r_
    row = lax.broadcasted_iota(jnp.int32, (rows, RWKV_HEAD), 0)

    heads = [slice(h * RWKV_HEAD, (h + 1) * RWKV_HEAD) for h in range(RWKV_HEADS)]
    reads = []
    for h, sl in enumerate(heads):
        lhs = jnp.concatenate([neg_kk[:, sl], decay_r[:, sl]], axis=0)
        reads.append([_mm(lhs, s_ref[b, h], _NT) for b in range(rows)])
    ys = []
    for h, sl in enumerate(heads):
        w_h, r_h, v_h, k_h, b_h = decay[:, sl], r_[:, sl], v_[:, sl], kmod[:, sl], kk_a[:, sl]
        rhs = jnp.concatenate([b_h, k_h], axis=0)
        sa_all = swr_all = jnp.zeros((rows, RWKV_HEAD), F32)
        adds = []
        for b in range(rows):
            mine = row == b
            sa, swr = reads[h][b][:rows], reads[h][b][rows:]
            left = jnp.concatenate([jnp.where(mine, sa, 0.0), jnp.where(mine, v_h, 0.0)], axis=0)
            adds.append(_mm(left, rhs, _TN))
            sa_all = jnp.where(mine, sa, sa_all)
            swr_all = jnp.where(mine, swr, swr_all)
        for b in range(rows):
            sout_ref[b, h] = s_ref[b, h] * w_h[b:b + 1] + adds[b]
        ys.append(swr_all + sa_all * jnp.sum(b_h * r_h, axis=-1, keepdims=True)
                  + v_h * jnp.sum(k_h * r_h, axis=-1, keepdims=True))
    y = jnp.concatenate(ys, axis=1)
    o_ref[...] = _rwkv_out(y, bonus, g, lnw_ref[...], lnb_ref[...], ones_bd)


def _rwkv_step(proj, s0, wts):
    batch = s0.shape[0]
    rows = 8
    shift = batch // rows
    wide = lambda base, off=0: pl.BlockSpec((rows, D_MODEL),
                                            lambda i, base=base, off=off: (i + off, base // D_MODEL))
    lora = pl.BlockSpec((rows, LORA_W), lambda i: (i, COL_LORA // LORA_W))
    lora_prev = pl.BlockSpec((rows, LORA_W), lambda i: (i + shift, COL_LORA // LORA_W))
    full = lambda shape: pl.BlockSpec(shape, lambda i: (0,) * len(shape))
    state = pl.BlockSpec((rows, RWKV_HEADS, RWKV_HEAD, RWKV_HEAD), lambda i: (i, 0, 0, 0))
    vec = full((1, D_MODEL))
    return pl.pallas_call(
        functools.partial(_rwkv_step_body, rows=rows),
        grid=(batch // rows,),
        in_specs=[wide(COL_R), wide(COL_KK), wide(COL_VV), lora,
                  wide(COL_R, shift), wide(COL_KK, shift), wide(COL_VV, shift), lora_prev,
                  full((1, 3, D_MODEL)), full((1, LORA_W)),
                  vec, full((LANES, D_MODEL)), vec, full((LANES, D_MODEL)),
                  full((2 * LANES, D_MODEL)), vec, vec, vec, vec, vec, state],
        out_specs=[pl.BlockSpec((rows, D_MODEL), lambda i: (i, 0)), state],
        out_shape=[jax.ShapeDtypeStruct((batch, D_MODEL), F32),
                   jax.ShapeDtypeStruct(s0.shape, F32)],
        compiler_params=_params(("parallel",)),
        name="rwkv_step",
    )(proj, proj, proj, proj, proj, proj, proj, proj,
      wts["mu3"], wts["mu_lora"], wts["w0"], wts["w2"], wts["a0"], wts["a2"], wts["g2"],
      wts["k_k"], wts["k_a"], wts["r_k"], wts["lnx_w"], wts["lnx_b"], s0)


def _merge_body(x_ref, oret_ref, orwkv_ref, ga_ref, gb_ref, wout_ref, nffn_ref, rw_ref, rb_ref,
                x1_ref, xn_ref, comb_ref, cnt_ref):
    merged = (jax.nn.sigmoid(ga_ref[...]) * oret_ref[...]
              + jax.nn.sigmoid(gb_ref[...]) * orwkv_ref[...])
    x1 = x_ref[...] + _mm(merged, wout_ref[...])
    x1_ref[...] = x1
    xn = _rms(x1, nffn_ref[...])
    xn_ref[...] = xn.astype(BF16)

    logits = _mm(xn, rw_ref[...]) + rb_ref[...]
    lane = lax.broadcasted_iota(jnp.int32, logits.shape, 1)
    neg_inf = -jnp.inf
    is_group = lane < N_GROUPS
    g_max = jnp.max(jnp.where(is_group, logits, neg_inf), axis=-1, keepdims=True)
    g_sel = jnp.min(jnp.where(is_group & (logits == g_max), lane, LANES), axis=-1, keepdims=True)
    g_w = 1.0 / jnp.sum(jnp.where(is_group, jnp.exp(logits - g_max), 0.0), axis=-1, keepdims=True)
    first = N_GROUPS + EXPERTS_PER_GROUP * g_sel
    in_group = (lane >= first) & (lane < first + EXPERTS_PER_GROUP)
    e_max = jnp.max(jnp.where(in_group, logits, neg_inf), axis=-1, keepdims=True)
    e_exp = jnp.where(in_group, jnp.exp(logits - e_max), 0.0)
    prob = e_exp / jnp.sum(e_exp, axis=-1, keepdims=True)
    prob = jnp.where(in_group, prob, -1.0)
    p1 = jnp.max(prob, axis=-1, keepdims=True)
    i1 = jnp.min(jnp.where(prob == p1, lane, LANES), axis=-1, keepdims=True)
    rest = jnp.where(lane == i1, -1.0, prob)
    p2 = jnp.max(rest, axis=-1, keepdims=True)
    i2 = jnp.min(jnp.where(rest == p2, lane, LANES), axis=-1, keepdims=True)
    denom = p1 + p2
    comb = (jnp.where(lane == i1 - N_GROUPS, g_w * p1 / denom, 0.0)
            + jnp.where(lane == i2 - N_GROUPS, g_w * p2 / denom, 0.0))
    comb_ref[...] = jnp.where(lane == N_EXPERTS, g_sel.astype(F32), comb)
    lane8 = lax.broadcasted_iota(jnp.int32, (8, LANES), 1)
    sizes = jnp.zeros((8, LANES), jnp.int32)
    for g in range(N_GROUPS):
        n_g = jnp.sum(jnp.where(g_sel == g, 1, 0), axis=0, keepdims=True)
        sizes = jnp.where(lane8 == g, n_g, sizes)
    cnt_ref[0] = sizes


MERGE_TILE = 512


def _merge(x, o_ret, o_rwkv, proj, wts):
    t = x.shape[0]
    tm = math.gcd(t, MERGE_TILE)
    row = lambda: pl.BlockSpec((tm, D_MODEL), lambda i: (i, 0))
    full = lambda shape: pl.BlockSpec(shape, lambda i: (0, 0))
    return pl.pallas_call(
        _merge_body,
        grid=(t // tm,),
        in_specs=[row(), row(), row(),
                  pl.BlockSpec((tm, D_MODEL), lambda i: (i, COL_GATE_A // D_MODEL)),
                  pl.BlockSpec((tm, D_MODEL), lambda i: (i, COL_GATE_B // D_MODEL)),
                  full((D_MODEL, D_MODEL)), full((1, D_MODEL)),
                  full((D_MODEL, LANES)), full((1, LANES))],
        out_specs=[row(), row(), pl.BlockSpec((tm, LANES), lambda i: (i, 0)),
                   pl.BlockSpec((1, 8, LANES), lambda i: (i, 0, 0))],
        out_shape=[jax.ShapeDtypeStruct((t, D_MODEL), F32),
                   jax.ShapeDtypeStruct((t, D_MODEL), BF16),
                   jax.ShapeDtypeStruct((t, LANES), F32),
                   jax.ShapeDtypeStruct((t // tm, 8, LANES), jnp.int32)],
        compiler_params=_params(("parallel",)),
        name="merge_router",
    )(x, o_ret, o_rwkv, proj, proj, wts["w_out"], wts["norm_ffn"], wts["router_w"], wts["router_b"])


MOE_TILE = 1024
MOE_BLOCK = 128
EXPERTS_PER_STEP = 4


def _moe_body(cnt_ref, xn_ref, x1_ref, comb_ref, wg_ref, wu_ref, wd_ref, nf_ref, o_ref,
              acc_ref, xs_ref, cs_ref, pt_ref, *, ratio):
    i = pl.program_id(0)
    s = pl.program_id(1)
    tm = xn_ref.shape[0]
    group = s // (EXPERTS_PER_GROUP // EXPERTS_PER_STEP)

    sizes = []
    for g in range(N_GROUPS):
        n = cnt_ref[i * ratio * N_GROUPS + g]
        for k in range(1, ratio):
            n = n + cnt_ref[(i * ratio + k) * N_GROUPS + g]
        sizes.append(n)
    starts = [jnp.int32(0)]
    for g in range(N_GROUPS - 1):
        starts.append(starts[-1] + sizes[g])

    @pl.when(s == 0)
    def _():
        comb = comb_ref[...]
        g_row = comb.T[N_EXPERTS:N_EXPERTS + 1, :]
        gid = lax.broadcasted_iota(jnp.int32, (8, tm), 0)
        onehot = jnp.where(g_row.astype(jnp.int32) == gid, 1.0, 0.0)
        step = math.gcd(tm, 256)
        row = lax.broadcasted_iota(jnp.int32, (step, tm), 0)
        col = lax.broadcasted_iota(jnp.int32, (step, tm), 1)
        rank = jnp.zeros((8, tm), F32)
        for r0 in range(0, tm, step):
            earlier = jnp.where(row + r0 < col, 1.0, 0.0).astype(BF16)
            rank = rank + jnp.dot(onehot[:, r0:r0 + step].astype(BF16), earlier,
                                  preferred_element_type=F32)
        start_col = jnp.zeros((8, 1), F32)
        gid_col = lax.broadcasted_iota(jnp.int32, (8, 1), 0)
        for g in range(1, N_GROUPS):
            start_col = jnp.where(gid_col == g, starts[g].astype(F32), start_col)
        pos_row = jnp.sum(onehot * (rank + start_col), axis=0, keepdims=True)
        pos_row_i = pos_row.astype(jnp.int32)
        xn = xn_ref[...]
        comb_parts = _split_bf16(comb, 3)
        for r0 in range(0, tm, step):
            perm = jnp.where(row + r0 == pos_row_i, 1.0, 0.0).astype(BF16)
            xs_ref[r0:r0 + step, :] = jnp.dot(perm, xn, preferred_element_type=F32).astype(BF16)
            cs_ref[r0:r0 + step, :] = sum(jnp.dot(perm, part, preferred_element_type=F32)
                                          for part in comb_parts)
        pos_col_i = jnp.broadcast_to(pos_row, (LANES, tm)).T.astype(jnp.int32)
        lane = lax.broadcasted_iota(jnp.int32, (tm, LANES), 1)
        for j in range(tm // LANES):
            pt_ref[:, j * LANES:(j + 1) * LANES] = jnp.where(
                lane + j * LANES == pos_col_i, 1.0, 0.0).astype(BF16)
        acc_ref[...] = jnp.zeros_like(acc_ref)

    start, size = starts[N_GROUPS - 1], sizes[N_GROUPS - 1]
    for g in range(N_GROUPS - 2, -1, -1):
        start = jnp.where(group == g, starts[g], start)
        size = jnp.where(group == g, sizes[g], size)
    blk = math.gcd(tm, MOE_BLOCK)
    first = start // blk
    last = jnp.where(size > 0, (start + size + blk - 1) // blk, first)
    lane = lax.broadcasted_iota(jnp.int32, (blk, LANES), 1)

    def block(j, carry):
        r0 = pl.multiple_of(j * blk, blk)
        xb = xs_ref[pl.ds(r0, blk), :]
        cw = cs_ref[pl.ds(r0, blk), :]
        hs = []
        for k in range(EXPERTS_PER_STEP):
            weight = jnp.sum(jnp.where(lane == s * EXPERTS_PER_STEP + k, cw, 0.0),
                             axis=-1, keepdims=True)
            hs.append(_silu(_mm(xb, wg_ref[k])) * _mm(xb, wu_ref[k]) * weight)
        total = _mm(hs[0], wd_ref[0])
        for k in range(1, EXPERTS_PER_STEP):
            total = total + _mm(hs[k], wd_ref[k])
        acc_ref[pl.ds(r0, blk), :] += total
        return carry

    lax.fori_loop(first, last, block, 0)

    @pl.when(s == pl.num_programs(1) - 1)
    def _():
        parts = _split_bf16(acc_ref[...], 2)
        step = math.gcd(tm, 256)
        for r0 in range(0, tm, step):
            moe = sum(jnp.dot(pt_ref[r0:r0 + step, :], part, preferred_element_type=F32)
                      for part in parts)
            o_ref[r0:r0 + step, :] = _rms(x1_ref[r0:r0 + step, :] + moe, nf_ref[...])


def _moe(xn, x1, comb, counts, wts):
    t = xn.shape[0]
    tm = math.gcd(t, MOE_TILE)
    assert tm % LANES == 0, "token count must be a multiple of the lane width"
    ratio = tm // math.gcd(t, MERGE_TILE)
    row = lambda w: pl.BlockSpec((tm, w), lambda i, s, cnt: (i, 0))
    once = lambda w: pl.BlockSpec((tm, w), lambda i, s, cnt: (i, 0), pipeline_mode=pl.Buffered(1))
    expert = lambda a, b: pl.BlockSpec((EXPERTS_PER_STEP, a, b), lambda i, s, cnt: (s, 0, 0))
    grid_spec = pltpu.PrefetchScalarGridSpec(
        num_scalar_prefetch=1,
        grid=(t // tm, N_EXPERTS // EXPERTS_PER_STEP),
        in_specs=[row(D_MODEL), once(D_MODEL), row(LANES),
                  expert(D_MODEL, D_EXPERT), expert(D_MODEL, D_EXPERT), expert(D_EXPERT, D_MODEL),
                  pl.BlockSpec((1, D_MODEL), lambda i, s, cnt: (0, 0))],
        out_specs=row(D_MODEL),
        scratch_shapes=[pltpu.VMEM((tm, D_MODEL), F32),
                        pltpu.VMEM((tm, D_MODEL), BF16),
                        pltpu.VMEM((tm, LANES), F32),
                        pltpu.VMEM((tm, tm), BF16)])
    return pl.pallas_call(
        functools.partial(_moe_body, ratio=ratio),
        grid_spec=grid_spec,
        out_shape=jax.ShapeDtypeStruct((t, D_MODEL), F32),
        compiler_params=_params(("parallel", "arbitrary")),
        name="moe",
    )(counts, xn, x1, comb, wts["w_gate"], wts["w_up"], wts["w_down"], wts["norm_final"])


def _pad_to(x, size, axis):
    pad = [(0, 0)] * x.ndim
    pad[axis] = (0, size - x.shape[axis])
    return jnp.pad(x, pad)


def _prepare_weights(norm_mix, w_in, mu_shift, ret_gn_w, rwkv_w0, rwkv_w2, rwkv_a0, rwkv_a2,
                     rwkv_g2, rwkv_k_k, rwkv_k_a, rwkv_r_k, rwkv_lnx_w, rwkv_lnx_b, w_out,
                     norm_ffn, router_group_w, router_group_b, router_expert_w, router_expert_b,
                     expert_w_gate, expert_w_up, expert_w_down, norm_final):
    def lora_cols(t):
        return jnp.concatenate([_pad_to(t[..., SRC_WLO:SRC_ALO], 128, -1),
                                _pad_to(t[..., SRC_ALO:SRC_GLO], 128, -1),
                                _pad_to(t[..., SRC_GLO:SRC_GATE_A], 256, -1)], axis=-1)

    w = w_in[0]
    w_in_p = jnp.concatenate([w[:, :SRC_WLO], w[:, SRC_GATE_A:], lora_cols(w)], axis=1).astype(BF16)
    mu = jnp.concatenate([jnp.zeros((SRC_SHIFT,), F32), mu_shift[0]])
    row = lambda v: v.reshape(1, -1)
    router_w = _pad_to(jnp.concatenate([router_group_w[0], router_expert_w[0]], axis=1), LANES, 1)
    router_b = _pad_to(jnp.concatenate([router_group_b[0], router_expert_b[0]]), LANES, 0)
    return dict(
        w_in=w_in_p, norm_mix=row(norm_mix[0]), ret_gn_w=row(ret_gn_w[0]),
        mu3=mu[SRC_SHIFT:SRC_WLO].reshape(1, 3, D_MODEL), mu_lora=lora_cols(mu).reshape(1, LORA_W),
        w0=row(rwkv_w0[0]), w2=_pad_to(rwkv_w2[0], 128, 0),
        a0=row(rwkv_a0[0]), a2=_pad_to(rwkv_a2[0], 128, 0),
        g2=_pad_to(rwkv_g2[0], 256, 0),
        k_k=row(rwkv_k_k[0]), k_a=row(rwkv_k_a[0]), r_k=row(rwkv_r_k[0]),
        lnx_w=row(rwkv_lnx_w[0]), lnx_b=row(rwkv_lnx_b[0]),
        w_out=w_out[0].astype(BF16), norm_ffn=row(norm_ffn[0]),
        router_w=router_w, router_b=row(router_b),
        w_gate=expert_w_gate[0].astype(BF16), w_up=expert_w_up[0].astype(BF16),
        w_down=expert_w_down[0].astype(BF16), norm_final=row(norm_final))


def _finish(x, o_ret, o_rwkv, proj, wts):
    x1, xn2, comb, sizes = _merge(x, o_ret, o_rwkv, proj, wts)
    counts = sizes[:, 0, :N_GROUPS].reshape(-1)
    return _moe(xn2, x1, comb, counts, wts)


def kernel(x_prompt, x_sample, state_ret, state_rwkv, state_shift, norm_mix, w_in, mu_shift, ret_gn_w, rwkv_w0, rwkv_w2, rwkv_a0, rwkv_a2, rwkv_g2, rwkv_k_k, rwkv_k_a, rwkv_r_k, rwkv_lnx_w, rwkv_lnx_b, w_out, norm_ffn, router_group_w, router_group_b, router_expert_w, router_expert_b, expert_w_gate, expert_w_up, expert_w_down, norm_final):
    assert norm_mix.shape[0] == 1, "single-layer step"
    wts = _prepare_weights(norm_mix, w_in, mu_shift, ret_gn_w, rwkv_w0, rwkv_w2, rwkv_a0, rwkv_a2,
                           rwkv_g2, rwkv_k_k, rwkv_k_a, rwkv_r_k, rwkv_lnx_w, rwkv_lnx_b, w_out,
                           norm_ffn, router_group_w, router_group_b, router_expert_w,
                           router_expert_b, expert_w_gate, expert_w_up, expert_w_down, norm_final)

    bp, lp, _ = x_prompt.shape
    xp = x_prompt.reshape(bp * lp, D_MODEL)
    tile_p = math.gcd(lp, 1024)
    proj_p, tail_p = _inproj(xp, wts["norm_mix"], wts["w_in"], tile_p, tile_p, 8)
    o_ret_p, ret_p = _retention_prompt(proj_p, wts["ret_gn_w"], bp, lp)
    o_rwkv_p, rwkv_p = _rwkv_prompt(proj_p, wts, bp, lp)
    y_prompt = _finish(xp, o_ret_p, o_rwkv_p, proj_p, wts).reshape(bp, lp, D_MODEL)
    shift_p = tail_p.reshape(bp, lp // tile_p, 8, D_MODEL)[:, -1, -1]

    bs, ls, _ = x_sample.shape
    assert ls == 1, "sample group advances one token"
    xs = x_sample.reshape(bs, D_MODEL)
    both = jnp.concatenate([xs, state_shift[0]], axis=0)
    proj_s, xn_s = _inproj(both, wts["norm_mix"], wts["w_in"], 2 * bs, bs, bs)
    pos_s = PAST_LEN + jnp.arange(ls, dtype=F32)
    o_ret_s, ret_s = _retention_step(proj_s, state_ret[0], wts["ret_gn_w"], pos_s)
    o_rwkv_s, rwkv_s = _rwkv_step(proj_s, state_rwkv[0], wts)
    y_sample = _finish(xs, o_ret_s, o_rwkv_s, proj_s, wts).reshape(bs, ls, D_MODEL)

    return (y_prompt, y_sample, ret_p[None], rwkv_p[None], shift_p[None],
            ret_s[None], rwkv_s[None], xn_s)
```

```python
import functools
import math

import jax
import jax.numpy as jnp
from jax import lax
from jax.experimental import pallas as pl
from jax.experimental.pallas import tpu as pltpu

F32 = jnp.float32
BF16 = jnp.bfloat16

D_MODEL = 1024
LANES = 128
PAST_LEN = 16384
RET_HEADS = 8
RET_D = D_MODEL // RET_HEADS
RET_CHUNK = 128
RET_CHUNKS_PER_STEP = 2
ROPE_BASE = 10000.0
RET_GN_EPS = 1e-5
RWKV_HEAD = 64
RWKV_HEADS = D_MODEL // RWKV_HEAD
RWKV_PAIRS = RWKV_HEADS // 2
RWKV_CHUNK = 64
LORA_DECAY = 64
LORA_A = 64
LORA_G = 160
RWKV_GN_EPS = 64e-5
N_GROUPS = 4
EXPERTS_PER_GROUP = 8
N_EXPERTS = N_GROUPS * EXPERTS_PER_GROUP
D_EXPERT = 256
RMS_EPS = 1e-6

SRC_SHIFT = 4 * D_MODEL
SRC_WLO = SRC_SHIFT + 3 * D_MODEL
SRC_ALO = SRC_WLO + LORA_DECAY
SRC_GLO = SRC_ALO + LORA_A
SRC_GATE_A = SRC_GLO + LORA_G
COL_Q, COL_K, COL_V, COL_GSW = 0, 1024, 2048, 3072
COL_R, COL_KK, COL_VV = 4096, 5120, 6144
COL_GATE_A, COL_GATE_B = 7168, 8192
COL_LORA = 9216
LORA_W = 512
N_PROJ = COL_LORA + LORA_W

VMEM_LIMIT = 48 * 1024 * 1024

_NN = (((1,), (0,)), ((), ()))
_NT = (((1,), (1,)), ((), ()))
_TN = (((0,), (0,)), ((), ()))


def _mm(a, b, dims=_NN, exact=False):
    if exact:
        return lax.dot_general(a.astype(F32), b.astype(F32), dims,
                               precision=lax.Precision.HIGHEST,
                               preferred_element_type=F32)
    return lax.dot_general(a.astype(BF16), b.astype(BF16), dims,
                           preferred_element_type=F32)


def _params(sem):
    return pltpu.CompilerParams(dimension_semantics=sem,
                                vmem_limit_bytes=VMEM_LIMIT)


def _rms(x, gain):
    return x * lax.rsqrt(jnp.mean(x * x, axis=-1, keepdims=True) + RMS_EPS) * gain


def _split_bf16(x, terms):
    parts = []
    for _ in range(terms - 1):
        hi = x.astype(BF16)
        parts.append(hi)
        x = x - hi.astype(F32)
    parts.append(x.astype(BF16))
    return parts


def _lane_sum(x, ones):
    hi, lo = _split_bf16(x, 2)
    return (jnp.dot(hi, ones, preferred_element_type=F32)
            + jnp.dot(lo, ones, preferred_element_type=F32))


INPROJ_ROWS = 128


def _inproj_body(x_ref, g_ref, w_ref, o_ref, xn_ref, *, normed, keep):
    step = math.gcd(x_ref.shape[0], INPROJ_ROWS)
    for r0 in range(0, x_ref.shape[0], step):
        x = x_ref[r0:r0 + step, :]
        if r0 < normed:
            x = _rms(x, g_ref[...])
            lo, hi = max(r0, normed - keep), r0 + step
            if lo < hi:
                xn_ref[0, 0, lo - (normed - keep):hi - (normed - keep), :] = x[lo - r0:, :]
        o_ref[r0:r0 + step, :] = jnp.dot(x.astype(BF16), w_ref[...], preferred_element_type=F32)


def _inproj(x, gain, w, tm, normed, keep):
    t = x.shape[0]
    n = w.shape[1]
    tn = n // 4
    assert normed % math.gcd(tm, INPROJ_ROWS) == 0 and keep <= normed <= tm
    proj, kept = pl.pallas_call(
        functools.partial(_inproj_body, normed=normed, keep=keep),
        grid=(n // tn, t // tm),
        in_specs=[pl.BlockSpec((tm, D_MODEL), lambda j, i: (i, 0)),
                  pl.BlockSpec((1, D_MODEL), lambda j, i: (0, 0)),
                  pl.BlockSpec((D_MODEL, tn), lambda j, i: (0, j), pipeline_mode=pl.Buffered(1))],
        out_specs=[pl.BlockSpec((tm, tn), lambda j, i: (i, j)),
                   pl.BlockSpec((1, 1, keep, D_MODEL), lambda j, i: (j, i, 0, 0))],
        out_shape=[jax.ShapeDtypeStruct((t, n), F32),
                   jax.ShapeDtypeStruct((n // tn, t // tm, keep, D_MODEL), F32)],
        compiler_params=_params(("arbitrary", "arbitrary")),
        name="inproj",
    )(x, gain, w)
    return proj, kept[0]


def _head_norm(x, eps):
    mu = jnp.mean(x, axis=-1, keepdims=True)
    xc = x - mu
    return xc * lax.rsqrt(jnp.mean(xc * xc, axis=-1, keepdims=True) + eps)


def _silu(x):
    return x * jax.nn.sigmoid(x)


def _rope(x, cos, sin_signed):
    return x * cos + pltpu.roll(x, RET_D // 2, 1) * sin_signed


def _ret_tables(chunk):
    log_gamma = jnp.log1p(-jnp.exp2(-5.0 - jnp.arange(RET_HEADS, dtype=F32)))
    idx = jnp.arange(chunk, dtype=F32)
    rel = idx[:, None] - idx[None, :]
    intra = jnp.where(rel[None] >= 0,
                      jnp.exp(jnp.maximum(rel, 0.0)[None] * log_gamma[:, None, None]), 0.0)
    q_dec = jnp.exp((idx[:, None] + 1.0) * log_gamma[None, :])
    k_dec = jnp.exp((chunk - 1.0 - idx)[None, :] * log_gamma[:, None]).T
    chunk_dec = jnp.exp(chunk * log_gamma)[None, :]
    widen = lambda t: jnp.repeat(t, RET_D, axis=1)
    return intra, widen(q_dec), widen(k_dec), widen(chunk_dec)


def _rope_tables(pos):
    half = RET_D // 2
    inv = ROPE_BASE ** (-jnp.arange(half, dtype=F32) / half)
    ang = pos[:, None] * inv[None, :]
    cos, sin = jnp.cos(ang), jnp.sin(ang)
    return jnp.concatenate([cos, cos], axis=1), jnp.concatenate([-sin, sin], axis=1)


def _ret_body(q_ref, k_ref, v_ref, g_ref, cos_ref, sin_ref, qd_ref, kd_ref, cd_ref,
              intra_ref, gnw_ref, o_ref, sout_ref, s_ref):
    c = pl.program_id(1)
    chunk = qd_ref.shape[0]

    @pl.when(c == 0)
    def _():
        s_ref[...] = jnp.zeros_like(s_ref)

    heads = [slice(h * RET_D, (h + 1) * RET_D) for h in range(RET_HEADS)]
    ones = jnp.ones((RET_D, RET_D), BF16)
    state = [s_ref[h] for h in range(RET_HEADS)]
    for r0 in range(0, q_ref.shape[0], chunk):
        rows = slice(r0, r0 + chunk)
        cos = cos_ref[rows, :]
        sin = sin_ref[rows, :]
        qr = [_rope(q_ref[rows, sl], cos, sin).astype(BF16) for sl in heads]
        kr = [_rope(k_ref[rows, sl], cos, sin) for sl in heads]
        vb = [v_ref[rows, sl].astype(BF16) for sl in heads]
        scores = [_mm(qr[h], kr[h], _NT) * intra_ref[h] for h in range(RET_HEADS)]
        cross = [_mm(qr[h], state[h]) * qd_ref[:, sl] for h, sl in enumerate(heads)]
        adds = [_mm(kr[h] * kd_ref[:, sl], vb[h], _TN) for h, sl in enumerate(heads)]
        out = [_mm(scores[h], vb[h]) + cross[h] for h in range(RET_HEADS)]
        cen = [out[h] - _lane_sum(out[h], ones) * (1.0 / RET_D) for h in range(RET_HEADS)]
        var = [_lane_sum(cen[h] * cen[h], ones) * (1.0 / RET_D) for h in range(RET_HEADS)]
        for h, sl in enumerate(heads):
            state[h] = state[h] * cd_ref[:, sl] + adds[h]
            o_ref[rows, sl] = (cen[h] * lax.rsqrt(var[h] + RET_GN_EPS) * gnw_ref[:, sl]
                               * _silu(g_ref[rows, sl]))
    for h in range(RET_HEADS):
        s_ref[h] = state[h]

    @pl.when(c == pl.num_programs(1) - 1)
    def _():
        sout_ref[0] = s_ref[...]


def _retention_prompt(proj, gn_w, batch, seq):
    chunk = math.gcd(seq, RET_CHUNK)
    tile = math.gcd(seq, RET_CHUNKS_PER_STEP * chunk)
    n_tiles = seq // tile
    intra, q_dec, k_dec, chunk_dec = _ret_tables(chunk)
    intra = intra * (RET_D ** -0.5)
    k_dec = k_dec * (RET_D ** -0.5)
    cos, sin = _rope_tables(jnp.arange(seq, dtype=F32))
    col = lambda j: pl.BlockSpec((tile, D_MODEL), lambda b, c, j=j: (b * n_tiles + c, j))
    const2 = lambda shape: pl.BlockSpec(shape, lambda b, c: (0, 0))
    state = pl.BlockSpec((1, RET_HEADS, RET_D, RET_D), lambda b, c: (b, 0, 0, 0))
    return pl.pallas_call(
        _ret_body,
        grid=(batch, n_tiles),
        in_specs=[col(COL_Q // D_MODEL), col(COL_K // D_MODEL), col(COL_V // D_MODEL),
                  col(COL_GSW // D_MODEL),
                  pl.BlockSpec((tile, RET_D), lambda b, c: (c, 0)),
                  pl.BlockSpec((tile, RET_D), lambda b, c: (c, 0)),
                  const2((chunk, D_MODEL)), const2((chunk, D_MODEL)), const2((1, D_MODEL)),
                  pl.BlockSpec((RET_HEADS, chunk, chunk), lambda b, c: (0, 0, 0)),
                  const2((1, D_MODEL))],
        out_specs=[pl.BlockSpec((tile, D_MODEL), lambda b, c: (b * n_tiles + c, 0)), state],
        out_shape=[jax.ShapeDtypeStruct((batch * seq, D_MODEL), F32),
                   jax.ShapeDtypeStruct((batch, RET_HEADS, RET_D, RET_D), F32)],
        scratch_shapes=[pltpu.VMEM((RET_HEADS, RET_D, RET_D), F32)],
        compiler_params=_params(("parallel", "arbitrary")),
        name="retention_prompt",
    )(proj, proj, proj, proj, cos, sin, q_dec, k_dec, chunk_dec, intra, gn_w)


def _ret_step_body(q_ref, k_ref, v_ref, g_ref, cos_ref, sin_ref, qd_ref, kd_ref, cd_ref,
                   gnw_ref, s_ref, o_ref, sout_ref, *, rows):
    cos = cos_ref[...]
    sin = sin_ref[...]
    row = lax.broadcasted_iota(jnp.int32, (rows, RET_D), 0)
    for h in range(RET_HEADS):
        sl = slice(h * RET_D, (h + 1) * RET_D)
        qr = _rope(q_ref[:, sl], cos, sin)
        kr = _rope(k_ref[:, sl], cos, sin) * (RET_D ** -0.5)
        v = v_ref[:, sl]
        kd = kr * kd_ref[:, sl]
        cd = cd_ref[:, sl]

        reads = [_mm(qr, s_ref[b, h]) for b in range(rows)]
        adds = [_mm(jnp.where(row == b, kd, 0.0), v, _TN) for b in range(rows)]
        cross = jnp.zeros((rows, RET_D), F32)
        for b in range(rows):
            sout_ref[b, h] = s_ref[b, h] * cd + adds[b]
            cross = jnp.where(row == b, reads[b], cross)
        out = jnp.sum(qr * kr, axis=-1, keepdims=True) * v + cross * qd_ref[:, sl]
        o_ref[:, sl] = _head_norm(out, RET_GN_EPS) * gnw_ref[:, sl] * _silu(g_ref[:, sl])


def _retention_step(proj, s0, gn_w, pos):
    batch = s0.shape[0]
    rows = 8
    _, q_dec, k_dec, chunk_dec = _ret_tables(1)
    cos, sin = _rope_tables(pos)
    col = lambda j: pl.BlockSpec((rows, D_MODEL), lambda i, j=j: (i, j))
    const2 = lambda shape: pl.BlockSpec(shape, lambda i: (0, 0))
    state = pl.BlockSpec((rows, RET_HEADS, RET_D, RET_D), lambda i: (i, 0, 0, 0))
    return pl.pallas_call(
        functools.partial(_ret_step_body, rows=rows),
        grid=(batch // rows,),
        in_specs=[col(COL_Q // D_MODEL), col(COL_K // D_MODEL), col(COL_V // D_MODEL),
                  col(COL_GSW // D_MODEL),
                  const2((1, RET_D)), const2((1, RET_D)),
                  const2((1, D_MODEL)), const2((1, D_MODEL)), const2((1, D_MODEL)),
                  const2((1, D_MODEL)), state],
        out_specs=[pl.BlockSpec((rows, D_MODEL), lambda i: (i, 0)), state],
        out_shape=[jax.ShapeDtypeStruct((batch, D_MODEL), F32),
                   jax.ShapeDtypeStruct(s0.shape, F32)],
        compiler_params=_params(("parallel",)),
        name="retention_step",
    )(proj, proj, proj, proj, cos, sin, q_dec, k_dec, chunk_dec, gn_w, s0)


def _head_ones():
    row = lax.broadcasted_iota(jnp.int32, (LANES, LANES), 0)
    col = lax.broadcasted_iota(jnp.int32, (LANES, LANES), 1)
    return jnp.where((row // RWKV_HEAD) == (col // RWKV_HEAD), 1.0, 0.0).astype(BF16)


def _head_sum(x, ones_bd):
    blocks = [_lane_sum(x[:, j:j + LANES], ones_bd) for j in range(0, x.shape[1], LANES)]
    return blocks[0] if len(blocks) == 1 else jnp.concatenate(blocks, axis=1)


def _softplus(x):
    return jnp.maximum(x, 0.0) + jnp.log1p(jnp.exp(-jnp.abs(x)))


def _lerp(cur, prev, mu):
    return cur + (prev - cur) * mu


def _rwkv_prep(r_, k_, v_, lora, w0, w2, a0, a2, g2, k_k, k_a, r_k, ones_bd):
    w_lo, a_lo, g_lo = lora[:, 0:128], lora[:, 128:256], lora[:, 256:512]
    w_log = -_softplus(-(w0 + _mm(jnp.tanh(w_lo), w2))) - 0.5
    log_decay = -jnp.exp(w_log)
    a_sig = jax.nn.sigmoid(a0 + _mm(a_lo, a2))
    g = _mm(jax.nn.sigmoid(g_lo), g2)
    kk = k_ * k_k
    kk = kk / jnp.maximum(jnp.sqrt(_head_sum(kk * kk, ones_bd)), 1e-12)
    kmod = k_ * (1.0 + (a_sig - 1.0) * k_a)
    bonus = _head_sum(r_ * kmod * r_k, ones_bd) * v_
    return log_decay, a_sig, g, kk, kmod, bonus


def _rwkv_out(y, bonus, g, lnx_w, lnx_b, ones_bd):
    inv_n = 1.0 / RWKV_HEAD
    yc = y - _head_sum(y, ones_bd) * inv_n
    var = _head_sum(yc * yc, ones_bd) * inv_n
    return (yc * lax.rsqrt(var + RWKV_GN_EPS) * lnx_w + lnx_b + bonus) * g


def _block_diag(x, lane_first):
    return jnp.concatenate([jnp.where(lane_first, x, 0.0), jnp.where(lane_first, 0.0, x)], axis=0)


def _pair_masks():
    n = 2 * RWKV_CHUNK
    row = lax.broadcasted_iota(jnp.int32, (n, n), 0)
    col = lax.broadcasted_iota(jnp.int32, (n, n), 1)
    same = (row // RWKV_CHUNK) == (col // RWKV_CHUNK)
    lane_first = lax.broadcasted_iota(jnp.int32, (1, LANES), 1) < RWKV_HEAD
    return lane_first, same & (col < row), same & (col <= row), row == col


def _rwkv_chunks(chunks, masks):
    lane_first, strict, incl, eye = masks
    n = 2 * RWKV_CHUNK
    rows = lambda x, y: jnp.concatenate([x, y], axis=0)
    cols = lambda x, y: jnp.concatenate([x, y], axis=1)
    bd = lambda x: _block_diag(x, lane_first)

    ops = []
    for r, lw, k, v, a, b, cum in chunks:
        p_inc = jnp.exp(cum)
        p_inv = jnp.exp(-cum)
        p_exc = jnp.exp(cum - lw)
        p_end = p_inc[RWKV_CHUNK - 1:RWKV_CHUNK, :]
        b_t = b * p_inv
        k_t = k * p_inv
        ops.append(dict(a_t=bd(a * p_exc), r_t=bd(r * p_inc), b_t=bd(b_t), k_t=bd(k_t),
                        b_hat=bd(b_t * p_end), k_hat=bd(k_t * p_end), v_bd=bd(v), p_end=p_end))

    for o in ops:
        prod = _mm(rows(o["a_t"], o["r_t"]), rows(o["b_t"], o["k_t"]), _NT)
        o["a_ab"] = jnp.where(strict, prod[:n, :n], 0.0)
        o["a_ak"] = jnp.where(strict, prod[:n, n:], 0.0)
        o["r_b"] = jnp.where(incl, prod[n:, :n], 0.0)
        o["r_k"] = jnp.where(incl, prod[n:, n:], 0.0)

    for o in ops:
        o["t_inv"] = jnp.where(eye, 1.0, 0.0) + o["a_ab"]
        o["power"] = _mm(o["a_ab"], o["a_ab"])
        o["av_rv"] = _mm(rows(o["a_ak"], o["r_k"]), o["v_bd"])
    for _ in range(int(math.log2(RWKV_CHUNK)) - 2):
        for o in ops:
            both = _mm(rows(o["power"], o["t_inv"]), o["power"])
            o["power"] = both[:n]
            o["t_inv"] = o["t_inv"] + both[n:]
    for o in ops:
        o["t_inv"] = o["t_inv"] + _mm(o["t_inv"], o["power"])
    for o in ops:
        o["wu"] = _mm(o["t_inv"], cols(o["a_t"], o["av_rv"][:n]))
    out = []
    for o in ops:
        wu = o["wu"]
        ry_y0 = _mm(o["r_b"], wu)
        r_y = o["r_t"] + ry_y0[:, :LANES]
        y0 = ry_y0[:, LANES:] + o["av_rv"][n:]
        psi = _mm(wu[:, :LANES], o["b_hat"], _TN) + jnp.where(eye, o["p_end"], 0.0)
        s_add = _mm(rows(wu[:, LANES:], o["v_bd"]), rows(o["b_hat"], o["k_hat"]), _TN)
        out.append((r_y, y0, psi, s_add))
    return out


def _shift_rows(cur, first_row):
    rolled = pltpu.roll(cur, 1, 0)
    row = lax.broadcasted_iota(jnp.int32, cur.shape, 0)
    return jnp.where(row == 0, first_row, rolled)


def _rwkv_body(r_ref, k_ref, v_ref, lora_ref, mu_ref, mul_ref,
               w0_ref, w2_ref, a0_ref, a2_ref, g2_ref, kk_ref, ka_ref, rk_ref,
               lnw_ref, lnb_ref, o_ref, sout_ref,
               s_ref, carry_ref, carryl_ref, *, tile):
    t = pl.program_id(1)
    p = pl.program_id(2)
    masks = _pair_masks()
    ones_bd = _head_ones()
    pairs = r_ref.shape[1] // LANES

    @pl.when(t == 0)
    def _():
        for q in range(pairs):
            s_ref[p * pairs + q] = jnp.zeros((LANES, LANES), F32)
        carry_ref[p] = jnp.zeros(carry_ref.shape[1:], F32)

        @pl.when(p == 0)
        def _():
            carryl_ref[...] = jnp.zeros_like(carryl_ref)

    prev = carry_ref[p]
    cur_r, cur_k, cur_v, cur_l = r_ref[...], k_ref[...], v_ref[...], lora_ref[...]
    mu = mu_ref[0]
    r_ = _lerp(cur_r, _shift_rows(cur_r, prev[0:1]), mu[0:1])
    k_ = _lerp(cur_k, _shift_rows(cur_k, prev[1:2]), mu[1:2])
    v_ = _lerp(cur_v, _shift_rows(cur_v, prev[2:3]), mu[2:3])
    lo = _lerp(cur_l, _shift_rows(cur_l, carryl_ref[0:1, :]), mul_ref[...])
    carry_ref[p, 0:1, :] = cur_r[tile - 1:tile]
    carry_ref[p, 1:2, :] = cur_k[tile - 1:tile]
    carry_ref[p, 2:3, :] = cur_v[tile - 1:tile]

    @pl.when(p == pl.num_programs(2) - 1)
    def _():
        carryl_ref[0:1, :] = cur_l[tile - 1:tile]

    log_decay, a_sig, g, kk, kmod, bonus = _rwkv_prep(
        r_, k_, v_, lo, w0_ref[...], w2_ref[...], a0_ref[...], a2_ref[...], g2_ref[...],
        kk_ref[...], ka_ref[...], rk_ref[...], ones_bd)

    row = lax.broadcasted_iota(jnp.int32, (tile, tile), 0)
    col = lax.broadcasted_iota(jnp.int32, (tile, tile), 1)
    tri = jnp.where(((row // RWKV_CHUNK) == (col // RWKV_CHUNK)) & (col <= row), 1.0, 0.0).astype(BF16)
    cum = sum(jnp.dot(tri, part, preferred_element_type=F32) for part in _split_bf16(log_decay, 3))

    neg_kk = -kk
    kk_a = kk * a_sig
    n_chunks = tile // RWKV_CHUNK
    chunks = []
    for c in range(n_chunks):
        for q in range(pairs):
            at = (slice(c * RWKV_CHUNK, (c + 1) * RWKV_CHUNK), slice(q * LANES, (q + 1) * LANES))
            chunks.append((r_[at], log_decay[at], kmod[at], v_[at], neg_kk[at], kk_a[at], cum[at]))
    parts = _rwkv_chunks(chunks, masks)

    states = [s_ref[p * pairs + q] for q in range(pairs)]
    ys = [[] for _ in range(pairs)]
    for c in range(n_chunks):
        for q in range(pairs):
            r_y, y0, psi, s_add = parts[c * pairs + q]
            y_bd = _mm(r_y, states[q], _NT) + y0
            states[q] = _mm(states[q], psi) + s_add
            ys[q].append(y_bd[:RWKV_CHUNK] + y_bd[RWKV_CHUNK:])
    for q in range(pairs):
        s_ref[p * pairs + q] = states[q]

    @pl.when(t == pl.num_programs(1) - 1)
    def _():
        for q in range(pairs):
            head = 2 * (p * pairs + q)
            sout_ref[0, head] = states[q][:RWKV_HEAD, :RWKV_HEAD]
            sout_ref[0, head + 1] = states[q][RWKV_HEAD:, RWKV_HEAD:]

    y = jnp.concatenate([jnp.concatenate(yq, axis=0) for yq in ys], axis=1)
    o_ref[...] = _rwkv_out(y, bonus, g, lnw_ref[...], lnb_ref[...], ones_bd)


RWKV_PAIRS_PER_STEP = 8


def _rwkv_prompt(proj, wts, batch, seq):
    tile = math.gcd(seq, 256)
    n_tiles = seq // tile
    width = RWKV_PAIRS_PER_STEP * LANES
    blk = lambda base: pl.BlockSpec((tile, width),
                                    lambda b, t, p, base=base: (b * n_tiles + t, base // width + p))
    per_pair = lambda rows: pl.BlockSpec((rows, width), lambda b, t, p: (0, p))
    in_specs = [
        blk(COL_R), blk(COL_KK), blk(COL_VV),
        pl.BlockSpec((tile, LORA_W), lambda b, t, p: (b * n_tiles + t, COL_LORA // LORA_W)),
        pl.BlockSpec((1, 3, width), lambda b, t, p: (0, 0, p)),
        pl.BlockSpec((1, LORA_W), lambda b, t, p: (0, 0)),
        per_pair(1), per_pair(LANES),
        per_pair(1), per_pair(LANES),
        per_pair(2 * LANES),
        per_pair(1), per_pair(1), per_pair(1),
        per_pair(1), per_pair(1),
    ]
    return pl.pallas_call(
        functools.partial(_rwkv_body, tile=tile),
        grid=(batch, n_tiles, RWKV_PAIRS // RWKV_PAIRS_PER_STEP),
        in_specs=in_specs,
        out_specs=[pl.BlockSpec((tile, width), lambda b, t, p: (b * n_tiles + t, p)),
                   pl.BlockSpec((1, RWKV_HEADS, RWKV_HEAD, RWKV_HEAD), lambda b, t, p: (b, 0, 0, 0))],
        out_shape=[jax.ShapeDtypeStruct((batch * seq, D_MODEL), F32),
                   jax.ShapeDtypeStruct((batch, RWKV_HEADS, RWKV_HEAD, RWKV_HEAD), F32)],
        scratch_shapes=[pltpu.VMEM((RWKV_PAIRS, LANES, LANES), F32),
                        pltpu.VMEM((RWKV_PAIRS // RWKV_PAIRS_PER_STEP, 8, width), F32),
                        pltpu.VMEM((8, LORA_W), F32)],
        compiler_params=_params(("parallel", "arbitrary", "arbitrary")),
        name="rwkv_prompt",
    )(proj, proj, proj, proj, wts["mu3"], wts["mu_lora"],
      wts["w0"], wts["w2"], wts["a0"], wts["a2"], wts["g2"],
      wts["k_k"], wts["k_a"], wts["r_k"], wts["lnx_w"], wts["lnx_b"])


def _rwkv_step_body(r_ref, k_ref, v_ref, lora_ref, pr_ref, pk_ref, pv_ref, plora_ref,
                    mu_ref, mul_ref, w0_ref, w2_ref, a0_ref, a2_ref, g2_ref,
                    kk_ref, ka_ref, rk_ref, lnw_ref, lnb_ref, s_ref,
                    o_ref, sout_ref, *, rows):
    ones_bd = _head_ones()
    mu = mu_ref[0]
    r_ = _lerp(r_ref[...], pr_ref[...], mu[0:1])
    k_ = _lerp(k_ref[...], pk_ref[...], mu[1:2])
    v_ = _lerp(v_ref[...], pv_ref[...], mu[2:3])
    lo = _lerp(lora_ref[...], plora_ref[...], mul_ref[...])
    log_decay, a_sig, g, kk, kmod, bonus = _rwkv_prep(
        r_, k_, v_, lo, w0_ref[...], w2_ref[...], a0_ref[...], a2_ref[...], g2_ref[...],
        kk_ref[...], ka_ref[...], rk_ref[...], ones_bd)
    decay = jnp.exp(log_decay)
    neg_kk = -kk
    kk_a = kk * a_sig
    decay_r = decay * r_
    row = lax.broadcasted_iota(jnp.int32, (rows, RWKV_HEAD), 0)

    heads = [slice(h * RWKV_HEAD, (h + 1) * RWKV_HEAD) for h in range(RWKV_HEADS)]
    reads = []
    for h, sl in enumerate(heads):
        lhs = jnp.concatenate([neg_kk[:, sl], decay_r[:, sl]], axis=0)
        reads.append([_mm(lhs, s_ref[b, h], _NT) for b in range(rows)])
    ys = []
    for h, sl in enumerate(heads):
        w_h, r_h, v_h, k_h, b_h = decay[:, sl], r_[:, sl], v_[:, sl], kmod[:, sl], kk_a[:, sl]
        rhs = jnp.concatenate([b_h, k_h], axis=0)
        sa_all = swr_all = jnp.zeros((rows, RWKV_HEAD), F32)
        adds = []
        for b in range(rows):
            mine = row == b
            sa, swr = reads[h][b][:rows], reads[h][b][rows:]
            left = jnp.concatenate([jnp.where(mine, sa, 0.0), jnp.where(mine, v_h, 0.0)], axis=0)
            adds.append(_mm(left, rhs, _TN))
            sa_all = jnp.where(mine, sa, sa_all)
            swr_all = jnp.where(mine, swr, swr_all)
        for b in range(rows):
            sout_ref[b, h] = s_ref[b, h] * w_h[b:b + 1] + adds[b]
        ys.append(swr_all + sa_all * jnp.sum(b_h * r_h, axis=-1, keepdims=True)
                  + v_h * jnp.sum(k_h * r_h, axis=-1, keepdims=True))
    y = jnp.concatenate(ys, axis=1)
    o_ref[...] = _rwkv_out(y, bonus, g, lnw_ref[...], lnb_ref[...], ones_bd)


def _rwkv_step(proj, s0, wts):
    batch = s0.shape[0]
    rows = 8
    shift = batch // rows
    wide = lambda base, off=0: pl.BlockSpec((rows, D_MODEL),
                                            lambda i, base=base, off=off: (i + off, base // D_MODEL))
    lora = pl.BlockSpec((rows, LORA_W), lambda i: (i, COL_LORA // LORA_W))
    lora_prev = pl.BlockSpec((rows, LORA_W), lambda i: (i + shift, COL_LORA // LORA_W))
    full = lambda shape: pl.BlockSpec(shape, lambda i: (0,) * len(shape))
    state = pl.BlockSpec((rows, RWKV_HEADS, RWKV_HEAD, RWKV_HEAD), lambda i: (i, 0, 0, 0))
    vec = full((1, D_MODEL))
    return pl.pallas_call(
        functools.partial(_rwkv_step_body, rows=rows),
        grid=(batch // rows,),
        in_specs=[wide(COL_R), wide(COL_KK), wide(COL_VV), lora,
                  wide(COL_R, shift), wide(COL_KK, shift), wide(COL_VV, shift), lora_prev,
                  full((1, 3, D_MODEL)), full((1, LORA_W)),
                  vec, full((LANES, D_MODEL)), vec, full((LANES, D_MODEL)),
                  full((2 * LANES, D_MODEL)), vec, vec, vec, vec, vec, state],
        out_specs=[pl.BlockSpec((rows, D_MODEL), lambda i: (i, 0)), state],
        out_shape=[jax.ShapeDtypeStruct((batch, D_MODEL), F32),
                   jax.ShapeDtypeStruct(s0.shape, F32)],
        compiler_params=_params(("parallel",)),
        name="rwkv_step",
    )(proj, proj, proj, proj, proj, proj, proj, proj,
      wts["mu3"], wts["mu_lora"], wts["w0"], wts["w2"], wts["a0"], wts["a2"], wts["g2"],
      wts["k_k"], wts["k_a"], wts["r_k"], wts["lnx_w"], wts["lnx_b"], s0)


MERGE_ROWS = 128


def _merge_body(x_ref, oret_ref, orwkv_ref, ga_ref, gb_ref, wout_ref, nffn_ref, rw_ref, rb_ref,
                x1_ref, xn_ref, comb_ref, cnt_ref):
    step = math.gcd(x_ref.shape[0], MERGE_ROWS)
    chunks = [slice(r0, r0 + step) for r0 in range(0, x_ref.shape[0], step)]
    each = lambda fn, *cols: [fn(*args) for args in zip(*cols)]
    row_max = lambda v: jnp.max(v, axis=-1, keepdims=True)
    row_min = lambda v: jnp.min(v, axis=-1, keepdims=True)
    row_sum = lambda v: jnp.sum(v, axis=-1, keepdims=True)
    lane = lax.broadcasted_iota(jnp.int32, (step, LANES), 1)
    is_group = lane < N_GROUPS
    neg_inf = -jnp.inf

    merged = [jax.nn.sigmoid(ga_ref[r, :]) * oret_ref[r, :]
              + jax.nn.sigmoid(gb_ref[r, :]) * orwkv_ref[r, :] for r in chunks]
    x1 = [x_ref[r, :] + _mm(m, wout_ref[...]) for r, m in zip(chunks, merged)]
    xn = each(lambda v: _rms(v, nffn_ref[...]), x1)
    for r, a, b in zip(chunks, x1, xn):
        x1_ref[r, :] = a
        xn_ref[r, :] = b.astype(BF16)

    logits = each(lambda v: _mm(v, rw_ref[...]) + rb_ref[...], xn)
    g_max = each(lambda l: row_max(jnp.where(is_group, l, neg_inf)), logits)
    g_sel = each(lambda l, m: row_min(jnp.where(is_group & (l == m), lane, LANES)), logits, g_max)
    g_w = each(lambda l, m: 1.0 / row_sum(jnp.where(is_group, jnp.exp(l - m), 0.0)), logits, g_max)
    first = each(lambda g: N_GROUPS + EXPERTS_PER_GROUP * g, g_sel)
    in_group = each(lambda f: (lane >= f) & (lane < f + EXPERTS_PER_GROUP), first)
    e_max = each(lambda l, ing: row_max(jnp.where(ing, l, neg_inf)), logits, in_group)
    e_exp = each(lambda l, ing, m: jnp.where(ing, jnp.exp(l - m), 0.0), logits, in_group, e_max)
    prob = each(lambda e, ing: jnp.where(ing, e / row_sum(e), -1.0), e_exp, in_group)
    p1 = each(row_max, prob)
    i1 = each(lambda p, m: row_min(jnp.where(p == m, lane, LANES)), prob, p1)
    rest = each(lambda p, i: jnp.where(lane == i, -1.0, p), prob, i1)
    p2 = each(row_max, rest)
    i2 = each(lambda p, m: row_min(jnp.where(p == m, lane, LANES)), rest, p2)
    lane8 = lax.broadcasted_iota(jnp.int32, (8, LANES), 1)
    sizes = jnp.zeros((8, LANES), jnp.int32)
    for c, r in enumerate(chunks):
        denom = p1[c] + p2[c]
        comb = (jnp.where(lane == i1[c] - N_GROUPS, g_w[c] * p1[c] / denom, 0.0)
                + jnp.where(lane == i2[c] - N_GROUPS, g_w[c] * p2[c] / denom, 0.0))
        comb_ref[r, :] = jnp.where(lane == N_EXPERTS, g_sel[c].astype(F32), comb)
        for g in range(N_GROUPS):
            n_g = jnp.sum(jnp.where(g_sel[c] == g, 1, 0), axis=0, keepdims=True)
            sizes = sizes + jnp.where(lane8 == g, n_g, 0)
    cnt_ref[0] = sizes


MERGE_TILE = 512


def _merge(x, o_ret, o_rwkv, proj, wts):
    t = x.shape[0]
    tm = math.gcd(t, MERGE_TILE)
    row = lambda: pl.BlockSpec((tm, D_MODEL), lambda i: (i, 0))
    full = lambda shape: pl.BlockSpec(shape, lambda i: (0, 0))
    return pl.pallas_call(
        _merge_body,
        grid=(t // tm,),
        in_specs=[row(), row(), row(),
                  pl.BlockSpec((tm, D_MODEL), lambda i: (i, COL_GATE_A // D_MODEL)),
                  pl.BlockSpec((tm, D_MODEL), lambda i: (i, COL_GATE_B // D_MODEL)),
                  full((D_MODEL, D_MODEL)), full((1, D_MODEL)),
                  full((D_MODEL, LANES)), full((1, LANES))],
        out_specs=[row(), row(), pl.BlockSpec((tm, LANES), lambda i: (i, 0)),
                   pl.BlockSpec((1, 8, LANES), lambda i: (i, 0, 0))],
        out_shape=[jax.ShapeDtypeStruct((t, D_MODEL), F32),
                   jax.ShapeDtypeStruct((t, D_MODEL), BF16),
                   jax.ShapeDtypeStruct((t, LANES), F32),
                   jax.ShapeDtypeStruct((t // tm, 8, LANES), jnp.int32)],
        compiler_params=_params(("parallel",)),
        name="merge_router",
    )(x, o_ret, o_rwkv, proj, proj, wts["w_out"], wts["norm_ffn"], wts["router_w"], wts["router_b"])


MOE_TILE = 1024
MOE_BLOCK = 128
EXPERTS_PER_STEP = 4


def _moe_body(cnt_ref, xn_ref, x1_ref, comb_ref, wg_ref, wu_ref, wd_ref, nf_ref, o_ref,
              acc_ref, xs_ref, cs_ref, pt_ref, *, ratio):
    i = pl.program_id(0)
    s = pl.program_id(1)
    tm = xn_ref.shape[0]
    group = s // (EXPERTS_PER_GROUP // EXPERTS_PER_STEP)

    sizes = []
    for g in range(N_GROUPS):
        n = cnt_ref[i * ratio * N_GROUPS + g]
        for k in range(1, ratio):
            n = n + cnt_ref[(i * ratio + k) * N_GROUPS + g]
        sizes.append(n)
    starts = [jnp.int32(0)]
    for g in range(N_GROUPS - 1):
        starts.append(starts[-1] + sizes[g])

    @pl.when(s == 0)
    def _():
        comb = comb_ref[...]
        g_row = comb.T[N_EXPERTS:N_EXPERTS + 1, :]
        gid = lax.broadcasted_iota(jnp.int32, (8, tm), 0)
        onehot = jnp.where(g_row.astype(jnp.int32) == gid, 1.0, 0.0)
        step = math.gcd(tm, 256)
        row = lax.broadcasted_iota(jnp.int32, (step, tm), 0)
        col = lax.broadcasted_iota(jnp.int32, (step, tm), 1)
        rank = jnp.zeros((8, tm), F32)
        for r0 in range(0, tm, step):
            earlier = jnp.where(row + r0 < col, 1.0, 0.0).astype(BF16)
            rank = rank + jnp.dot(onehot[:, r0:r0 + step].astype(BF16), earlier,
                                  preferred_element_type=F32)
        start_col = jnp.zeros((8, 1), F32)
        gid_col = lax.broadcasted_iota(jnp.int32, (8, 1), 0)
        for g in range(1, N_GROUPS):
            start_col = jnp.where(gid_col == g, starts[g].astype(F32), start_col)
        pos_row = jnp.sum(onehot * (rank + start_col), axis=0, keepdims=True)
        pos_row_i = pos_row.astype(jnp.int32)
        xn = xn_ref[...]
        comb_parts = _split_bf16(comb, 3)
        for r0 in range(0, tm, step):
            perm = jnp.where(row + r0 == pos_row_i, 1.0, 0.0).astype(BF16)
            xs_ref[r0:r0 + step, :] = jnp.dot(perm, xn, preferred_element_type=F32).astype(BF16)
            cs_ref[r0:r0 + step, :] = sum(jnp.dot(perm, part, preferred_element_type=F32)
                                          for part in comb_parts)
        pos_col_i = jnp.broadcast_to(pos_row, (LANES, tm)).T.astype(jnp.int32)
        lane = lax.broadcasted_iota(jnp.int32, (tm, LANES), 1)
        for j in range(tm // LANES):
            pt_ref[:, j * LANES:(j + 1) * LANES] = jnp.where(
                lane + j * LANES == pos_col_i, 1.0, 0.0).astype(BF16)
        acc_ref[...] = jnp.zeros_like(acc_ref)

    start, size = starts[N_GROUPS - 1], sizes[N_GROUPS - 1]
    for g in range(N_GROUPS - 2, -1, -1):
        start = jnp.where(group == g, starts[g], start)
        size = jnp.where(group == g, sizes[g], size)
    blk = math.gcd(tm, MOE_BLOCK)
    first = start // blk
    last = jnp.where(size > 0, (start + size + blk - 1) // blk, first)
    lane = lax.broadcasted_iota(jnp.int32, (blk, LANES), 1)

    def block(j, carry):
        r0 = pl.multiple_of(j * blk, blk)
        xb = xs_ref[pl.ds(r0, blk), :]
        cw = cs_ref[pl.ds(r0, blk), :]
        hs = []
        for k in range(EXPERTS_PER_STEP):
            weight = jnp.sum(jnp.where(lane == s * EXPERTS_PER_STEP + k, cw, 0.0),
                             axis=-1, keepdims=True)
            hs.append(_silu(_mm(xb, wg_ref[k])) * _mm(xb, wu_ref[k]) * weight)
        total = _mm(hs[0], wd_ref[0])
        for k in range(1, EXPERTS_PER_STEP):
            total = total + _mm(hs[k], wd_ref[k])
        acc_ref[pl.ds(r0, blk), :] += total
        return carry

    lax.fori_loop(first, last, block, 0)

    @pl.when(s == pl.num_programs(1) - 1)
    def _():
        parts = _split_bf16(acc_ref[...], 2)
        step = math.gcd(tm, 256)
        for r0 in range(0, tm, step):
            moe = sum(jnp.dot(pt_ref[r0:r0 + step, :], part, preferred_element_type=F32)
                      for part in parts)
            o_ref[r0:r0 + step, :] = _rms(x1_ref[r0:r0 + step, :] + moe, nf_ref[...])


def _moe(xn, x1, comb, counts, wts):
    t = xn.shape[0]
    tm = math.gcd(t, MOE_TILE)
    assert tm % LANES == 0, "token count must be a multiple of the lane width"
    ratio = tm // math.gcd(t, MERGE_TILE)
    row = lambda w: pl.BlockSpec((tm, w), lambda i, s, cnt: (i, 0))
    once = lambda w: pl.BlockSpec((tm, w), lambda i, s, cnt: (i, 0), pipeline_mode=pl.Buffered(1))
    expert = lambda a, b: pl.BlockSpec((EXPERTS_PER_STEP, a, b), lambda i, s, cnt: (s, 0, 0))
    grid_spec = pltpu.PrefetchScalarGridSpec(
        num_scalar_prefetch=1,
        grid=(t // tm, N_EXPERTS // EXPERTS_PER_STEP),
        in_specs=[row(D_MODEL), once(D_MODEL), row(LANES),
                  expert(D_MODEL, D_EXPERT), expert(D_MODEL, D_EXPERT), expert(D_EXPERT, D_MODEL),
                  pl.BlockSpec((1, D_MODEL), lambda i, s, cnt: (0, 0))],
        out_specs=row(D_MODEL),
        scratch_shapes=[pltpu.VMEM((tm, D_MODEL), F32),
                        pltpu.VMEM((tm, D_MODEL), BF16),
                        pltpu.VMEM((tm, LANES), F32),
                        pltpu.VMEM((tm, tm), BF16)])
    return pl.pallas_call(
        functools.partial(_moe_body, ratio=ratio),
        grid_spec=grid_spec,
        out_shape=jax.ShapeDtypeStruct((t, D_MODEL), F32),
        compiler_params=_params(("parallel", "arbitrary")),
        name="moe",
    )(counts, xn, x1, comb, wts["w_gate"], wts["w_up"], wts["w_down"], wts["norm_final"])


def _pad_to(x, size, axis):
    pad = [(0, 0)] * x.ndim
    pad[axis] = (0, size - x.shape[axis])
    return jnp.pad(x, pad)


def _prepare_weights(norm_mix, w_in, mu_shift, ret_gn_w, rwkv_w0, rwkv_w2, rwkv_a0, rwkv_a2,
                     rwkv_g2, rwkv_k_k, rwkv_k_a, rwkv_r_k, rwkv_lnx_w, rwkv_lnx_b, w_out,
                     norm_ffn, router_group_w, router_group_b, router_expert_w, router_expert_b,
                     expert_w_gate, expert_w_up, expert_w_down, norm_final):
    def lora_cols(t):
        return jnp.concatenate([_pad_to(t[..., SRC_WLO:SRC_ALO], 128, -1),
                                _pad_to(t[..., SRC_ALO:SRC_GLO], 128, -1),
                                _pad_to(t[..., SRC_GLO:SRC_GATE_A], 256, -1)], axis=-1)

    w = w_in[0]
    w_in_p = jnp.concatenate([w[:, :SRC_WLO], w[:, SRC_GATE_A:], lora_cols(w)], axis=1).astype(BF16)
    mu = jnp.concatenate([jnp.zeros((SRC_SHIFT,), F32), mu_shift[0]])
    row = lambda v: v.reshape(1, -1)
    router_w = _pad_to(jnp.concatenate([router_group_w[0], router_expert_w[0]], axis=1), LANES, 1)
    router_b = _pad_to(jnp.concatenate([router_group_b[0], router_expert_b[0]]), LANES, 0)
    return dict(
        w_in=w_in_p, norm_mix=row(norm_mix[0]), ret_gn_w=row(ret_gn_w[0]),
        mu3=mu[SRC_SHIFT:SRC_WLO].reshape(1, 3, D_MODEL), mu_lora=lora_cols(mu).reshape(1, LORA_W),
        w0=row(rwkv_w0[0]), w2=_pad_to(rwkv_w2[0], 128, 0),
        a0=row(rwkv_a0[0]), a2=_pad_to(rwkv_a2[0], 128, 0),
        g2=_pad_to(rwkv_g2[0], 256, 0),
        k_k=row(rwkv_k_k[0]), k_a=row(rwkv_k_a[0]), r_k=row(rwkv_r_k[0]),
        lnx_w=row(rwkv_lnx_w[0]), lnx_b=row(rwkv_lnx_b[0]),
        w_out=w_out[0].astype(BF16), norm_ffn=row(norm_ffn[0]),
        router_w=router_w, router_b=row(router_b),
        w_gate=expert_w_gate[0].astype(BF16), w_up=expert_w_up[0].astype(BF16),
        w_down=expert_w_down[0].astype(BF16), norm_final=row(norm_final))


def _finish(x, o_ret, o_rwkv, proj, wts):
    x1, xn2, comb, sizes = _merge(x, o_ret, o_rwkv, proj, wts)
    counts = sizes[:, 0, :N_GROUPS].reshape(-1)
    return _moe(xn2, x1, comb, counts, wts)


def kernel(x_prompt, x_sample, state_ret, state_rwkv, state_shift, norm_mix, w_in, mu_shift, ret_gn_w, rwkv_w0, rwkv_w2, rwkv_a0, rwkv_a2, rwkv_g2, rwkv_k_k, rwkv_k_a, rwkv_r_k, rwkv_lnx_w, rwkv_lnx_b, w_out, norm_ffn, router_group_w, router_group_b, router_expert_w, router_expert_b, expert_w_gate, expert_w_up, expert_w_down, norm_final):
    assert norm_mix.shape[0] == 1, "single-layer step"
    wts = _prepare_weights(norm_mix, w_in, mu_shift, ret_gn_w, rwkv_w0, rwkv_w2, rwkv_a0, rwkv_a2,
                           rwkv_g2, rwkv_k_k, rwkv_k_a, rwkv_r_k, rwkv_lnx_w, rwkv_lnx_b, w_out,
                           norm_ffn, router_group_w, router_group_b, router_expert_w,
                           router_expert_b, expert_w_gate, expert_w_up, expert_w_down, norm_final)

    bp, lp, _ = x_prompt.shape
    xp = x_prompt.reshape(bp * lp, D_MODEL)
    tile_p = math.gcd(lp, 1024)
    proj_p, tail_p = _inproj(xp, wts["norm_mix"], wts["w_in"], tile_p, tile_p, 8)
    o_ret_p, ret_p = _retention_prompt(proj_p, wts["ret_gn_w"], bp, lp)
    o_rwkv_p, rwkv_p = _rwkv_prompt(proj_p, wts, bp, lp)
    y_prompt = _finish(xp, o_ret_p, o_rwkv_p, proj_p, wts).reshape(bp, lp, D_MODEL)
    shift_p = tail_p.reshape(bp, lp // tile_p, 8, D_MODEL)[:, -1, -1]

    bs, ls, _ = x_sample.shape
    assert ls == 1, "sample group advances one token"
    xs = x_sample.reshape(bs, D_MODEL)
    both = jnp.concatenate([xs, state_shift[0]], axis=0)
    proj_s, xn_s = _inproj(both, wts["norm_mix"], wts["w_in"], 2 * bs, bs, bs)
    pos_s = PAST_LEN + jnp.arange(ls, dtype=F32)
    o_ret_s, ret_s = _retention_step(proj_s, state_ret[0], wts["ret_gn_w"], pos_s)
    o_rwkv_s, rwkv_s = _rwkv_step(proj_s, state_rwkv[0], wts)
    y_sample = _finish(xs, o_ret_s, o_rwkv_s, proj_s, wts).reshape(bs, ls, D_MODEL)

    return (y_prompt, y_sample, ret_p[None], rwkv_p[None], shift_p[None],
            ret_s[None], rwkv_s[None], xn_s)
```

```python
import functools
import math

import jax
import jax.numpy as jnp
from jax import lax
from jax.experimental import pallas as pl
from jax.experimental.pallas import tpu as pltpu

F32 = jnp.float32
BF16 = jnp.bfloat16

D_MODEL = 1024
LANES = 128
PAST_LEN = 16384
RET_HEADS = 8
RET_D = D_MODEL // RET_HEADS
RET_CHUNK = 128
RET_CHUNKS_PER_STEP = 2
ROPE_BASE = 10000.0
RET_GN_EPS = 1e-5
RWKV_HEAD = 64
RWKV_HEADS = D_MODEL // RWKV_HEAD
RWKV_PAIRS = RWKV_HEADS // 2
RWKV_CHUNK = 64
LORA_DECAY = 64
LORA_A = 64
LORA_G = 160
RWKV_GN_EPS = 64e-5
N_GROUPS = 4
EXPERTS_PER_GROUP = 8
N_EXPERTS = N_GROUPS * EXPERTS_PER_GROUP
D_EXPERT = 256
RMS_EPS = 1e-6

SRC_SHIFT = 4 * D_MODEL
SRC_WLO = SRC_SHIFT + 3 * D_MODEL
SRC_ALO = SRC_WLO + LORA_DECAY
SRC_GLO = SRC_ALO + LORA_A
SRC_GATE_A = SRC_GLO + LORA_G
COL_Q, COL_K, COL_V, COL_GSW = 0, 1024, 2048, 3072
COL_R, COL_KK, COL_VV = 4096, 5120, 6144
COL_GATE_A, COL_GATE_B = 7168, 8192
COL_LORA = 9216
LORA_W = 512
N_PROJ = COL_LORA + LORA_W

VMEM_LIMIT = 48 * 1024 * 1024

_NN = (((1,), (0,)), ((), ()))
_NT = (((1,), (1,)), ((), ()))
_TN = (((0,), (0,)), ((), ()))


def _mm(a, b, dims=_NN, exact=False):
    if exact:
        return lax.dot_general(a.astype(F32), b.astype(F32), dims,
                               precision=lax.Precision.HIGHEST,
                               preferred_element_type=F32)
    return lax.dot_general(a.astype(BF16), b.astype(BF16), dims,
                           preferred_element_type=F32)


def _params(sem):
    return pltpu.CompilerParams(dimension_semantics=sem,
                                vmem_limit_bytes=VMEM_LIMIT)


def _rms(x, gain):
    return x * lax.rsqrt(jnp.mean(x * x, axis=-1, keepdims=True) + RMS_EPS) * gain


def _split_bf16(x, terms):
    parts = []
    for _ in range(terms - 1):
        hi = x.astype(BF16)
        parts.append(hi)
        x = x - hi.astype(F32)
    parts.append(x.astype(BF16))
    return parts


def _lane_sum(x, ones):
    hi, lo = _split_bf16(x, 2)
    return (jnp.dot(hi, ones, preferred_element_type=F32)
            + jnp.dot(lo, ones, preferred_element_type=F32))


INPROJ_ROWS = 128


def _inproj_body(x_ref, g_ref, w_ref, o_ref, xn_ref, *, normed, keep):
    step = math.gcd(x_ref.shape[0], INPROJ_ROWS)
    for r0 in range(0, x_ref.shape[0], step):
        x = x_ref[r0:r0 + step, :]
        if r0 < normed:
            x = _rms(x, g_ref[...])
            lo, hi = max(r0, normed - keep), r0 + step
            if lo < hi:
                xn_ref[0, 0, lo - (normed - keep):hi - (normed - keep), :] = x[lo - r0:, :]
        o_ref[r0:r0 + step, :] = jnp.dot(x.astype(BF16), w_ref[...], preferred_element_type=F32)


def _inproj(x, gain, w, tm, normed, keep):
    t = x.shape[0]
    n = w.shape[1]
    tn = n // 4
    assert normed % math.gcd(tm, INPROJ_ROWS) == 0 and keep <= normed <= tm
    proj, kept = pl.pallas_call(
        functools.partial(_inproj_body, normed=normed, keep=keep),
        grid=(n // tn, t // tm),
        in_specs=[pl.BlockSpec((tm, D_MODEL), lambda j, i: (i, 0)),
                  pl.BlockSpec((1, D_MODEL), lambda j, i: (0, 0)),
                  pl.BlockSpec((D_MODEL, tn), lambda j, i: (0, j), pipeline_mode=pl.Buffered(1))],
        out_specs=[pl.BlockSpec((tm, tn), lambda j, i: (i, j)),
                   pl.BlockSpec((1, 1, keep, D_MODEL), lambda j, i: (j, i, 0, 0))],
        out_shape=[jax.ShapeDtypeStruct((t, n), F32),
                   jax.ShapeDtypeStruct((n // tn, t // tm, keep, D_MODEL), F32)],
        compiler_params=_params(("arbitrary", "arbitrary")),
        name="inproj",
    )(x, gain, w)
    return proj, kept[0]


def _head_norm(x, eps):
    mu = jnp.mean(x, axis=-1, keepdims=True)
    xc = x - mu
    return xc * lax.rsqrt(jnp.mean(xc * xc, axis=-1, keepdims=True) + eps)


def _silu(x):
    return x * jax.nn.sigmoid(x)


def _rope(x, cos, sin_signed):
    return x * cos + pltpu.roll(x, RET_D // 2, 1) * sin_signed


def _ret_tables(chunk):
    log_gamma = jnp.log1p(-jnp.exp2(-5.0 - jnp.arange(RET_HEADS, dtype=F32)))
    idx = jnp.arange(chunk, dtype=F32)
    rel = idx[:, None] - idx[None, :]
    intra = jnp.where(rel[None] >= 0,
                      jnp.exp(jnp.maximum(rel, 0.0)[None] * log_gamma[:, None, None]), 0.0)
    q_dec = jnp.exp((idx[:, None] + 1.0) * log_gamma[None, :])
    k_dec = jnp.exp((chunk - 1.0 - idx)[None, :] * log_gamma[:, None]).T
    chunk_dec = jnp.exp(chunk * log_gamma)[None, :]
    widen = lambda t: jnp.repeat(t, RET_D, axis=1)
    return intra, widen(q_dec), widen(k_dec), widen(chunk_dec)


def _rope_tables(pos):
    half = RET_D // 2
    inv = ROPE_BASE ** (-jnp.arange(half, dtype=F32) / half)
    ang = pos[:, None] * inv[None, :]
    cos, sin = jnp.cos(ang), jnp.sin(ang)
    return jnp.concatenate([cos, cos], axis=1), jnp.concatenate([-sin, sin], axis=1)


def _ret_body(q_ref, k_ref, v_ref, g_ref, cos_ref, sin_ref, qd_ref, kd_ref, cd_ref,
              intra_ref, gnw_ref, o_ref, sout_ref, s_ref):
    c = pl.program_id(1)
    chunk = qd_ref.shape[0]

    @pl.when(c == 0)
    def _():
        s_ref[...] = jnp.zeros_like(s_ref)

    heads = [slice(h * RET_D, (h + 1) * RET_D) for h in range(RET_HEADS)]
    ones = jnp.ones((RET_D, RET_D), BF16)
    state = [s_ref[h] for h in range(RET_HEADS)]
    for r0 in range(0, q_ref.shape[0], chunk):
        rows = slice(r0, r0 + chunk)
        cos = cos_ref[rows, :]
        sin = sin_ref[rows, :]
        qr = [_rope(q_ref[rows, sl], cos, sin).astype(BF16) for sl in heads]
        kr = [_rope(k_ref[rows, sl], cos, sin) for sl in heads]
        vb = [v_ref[rows, sl].astype(BF16) for sl in heads]
        scores = [_mm(qr[h], kr[h], _NT) * intra_ref[h] for h in range(RET_HEADS)]
        cross = [_mm(qr[h], state[h]) * qd_ref[:, sl] for h, sl in enumerate(heads)]
        adds = [_mm(kr[h] * kd_ref[:, sl], vb[h], _TN) for h, sl in enumerate(heads)]
        out = [_mm(scores[h], vb[h]) + cross[h] for h in range(RET_HEADS)]
        cen = [out[h] - _lane_sum(out[h], ones) * (1.0 / RET_D) for h in range(RET_HEADS)]
        var = [_lane_sum(cen[h] * cen[h], ones) * (1.0 / RET_D) for h in range(RET_HEADS)]
        for h, sl in enumerate(heads):
            state[h] = state[h] * cd_ref[:, sl] + adds[h]
            o_ref[rows, sl] = (cen[h] * lax.rsqrt(var[h] + RET_GN_EPS) * gnw_ref[:, sl]
                               * _silu(g_ref[rows, sl]))
    for h in range(RET_HEADS):
        s_ref[h] = state[h]

    @pl.when(c == pl.num_programs(1) - 1)
    def _():
        sout_ref[0] = s_ref[...]


def _retention_prompt(proj, gn_w, batch, seq):
    chunk = math.gcd(seq, RET_CHUNK)
    tile = math.gcd(seq, RET_CHUNKS_PER_STEP * chunk)
    n_tiles = seq // tile
    intra, q_dec, k_dec, chunk_dec = _ret_tables(chunk)
    intra = intra * (RET_D ** -0.5)
    k_dec = k_dec * (RET_D ** -0.5)
    cos, sin = _rope_tables(jnp.arange(seq, dtype=F32))
    col = lambda j: pl.BlockSpec((tile, D_MODEL), lambda b, c, j=j: (b * n_tiles + c, j))
    const2 = lambda shape: pl.BlockSpec(shape, lambda b, c: (0, 0))
    state = pl.BlockSpec((1, RET_HEADS, RET_D, RET_D), lambda b, c: (b, 0, 0, 0))
    return pl.pallas_call(
        _ret_body,
        grid=(batch, n_tiles),
        in_specs=[col(COL_Q // D_MODEL), col(COL_K // D_MODEL), col(COL_V // D_MODEL),
                  col(COL_GSW // D_MODEL),
                  pl.BlockSpec((tile, RET_D), lambda b, c: (c, 0)),
                  pl.BlockSpec((tile, RET_D), lambda b, c: (c, 0)),
                  const2((chunk, D_MODEL)), const2((chunk, D_MODEL)), const2((1, D_MODEL)),
                  pl.BlockSpec((RET_HEADS, chunk, chunk), lambda b, c: (0, 0, 0)),
                  const2((1, D_MODEL))],
        out_specs=[pl.BlockSpec((tile, D_MODEL), lambda b, c: (b * n_tiles + c, 0)), state],
        out_shape=[jax.ShapeDtypeStruct((batch * seq, D_MODEL), F32),
                   jax.ShapeDtypeStruct((batch, RET_HEADS, RET_D, RET_D), F32)],
        scratch_shapes=[pltpu.VMEM((RET_HEADS, RET_D, RET_D), F32)],
        compiler_params=_params(("parallel", "arbitrary")),
        name="retention_prompt",
    )(proj, proj, proj, proj, cos, sin, q_dec, k_dec, chunk_dec, intra, gn_w)


def _ret_step_body(q_ref, k_ref, v_ref, g_ref, cos_ref, sin_ref, qd_ref, kd_ref, cd_ref,
                   gnw_ref, s_ref, o_ref, sout_ref, *, rows):
    cos = cos_ref[...]
    sin = sin_ref[...]
    row = lax.broadcasted_iota(jnp.int32, (rows, RET_D), 0)
    for h in range(RET_HEADS):
        sl = slice(h * RET_D, (h + 1) * RET_D)
        qr = _rope(q_ref[:, sl], cos, sin)
        kr = _rope(k_ref[:, sl], cos, sin) * (RET_D ** -0.5)
        v = v_ref[:, sl]
        kd = kr * kd_ref[:, sl]
        cd = cd_ref[:, sl]

        reads = [_mm(qr, s_ref[b, h]) for b in range(rows)]
        adds = [_mm(jnp.where(row == b, kd, 0.0), v, _TN) for b in range(rows)]
        cross = jnp.zeros((rows, RET_D), F32)
        for b in range(rows):
            sout_ref[b, h] = s_ref[b, h] * cd + adds[b]
            cross = jnp.where(row == b, reads[b], cross)
        out = jnp.sum(qr * kr, axis=-1, keepdims=True) * v + cross * qd_ref[:, sl]
        o_ref[:, sl] = _head_norm(out, RET_GN_EPS) * gnw_ref[:, sl] * _silu(g_ref[:, sl])


def _retention_step(proj, s0, gn_w, pos):
    batch = s0.shape[0]
    rows = 8
    _, q_dec, k_dec, chunk_dec = _ret_tables(1)
    cos, sin = _rope_tables(pos)
    col = lambda j: pl.BlockSpec((rows, D_MODEL), lambda i, j=j: (i, j))
    const2 = lambda shape: pl.BlockSpec(shape, lambda i: (0, 0))
    state = pl.BlockSpec((rows, RET_HEADS, RET_D, RET_D), lambda i: (i, 0, 0, 0))
    return pl.pallas_call(
        functools.partial(_ret_step_body, rows=rows),
        grid=(batch // rows,),
        in_specs=[col(COL_Q // D_MODEL), col(COL_K // D_MODEL), col(COL_V // D_MODEL),
                  col(COL_GSW // D_MODEL),
                  const2((1, RET_D)), const2((1, RET_D)),
                  const2((1, D_MODEL)), const2((1, D_MODEL)), const2((1, D_MODEL)),
                  const2((1, D_MODEL)), state],
        out_specs=[pl.BlockSpec((rows, D_MODEL), lambda i: (i, 0)), state],
        out_shape=[jax.ShapeDtypeStruct((batch, D_MODEL), F32),
                   jax.ShapeDtypeStruct(s0.shape, F32)],
        compiler_params=_params(("parallel",)),
        name="retention_step",
    )(proj, proj, proj, proj, cos, sin, q_dec, k_dec, chunk_dec, gn_w, s0)


def _head_ones():
    row = lax.broadcasted_iota(jnp.int32, (LANES, LANES), 0)
    col = lax.broadcasted_iota(jnp.int32, (LANES, LANES), 1)
    return jnp.where((row // RWKV_HEAD) == (col // RWKV_HEAD), 1.0, 0.0).astype(BF16)


def _head_sum(x, ones_bd):
    blocks = [_lane_sum(x[:, j:j + LANES], ones_bd) for j in range(0, x.shape[1], LANES)]
    return blocks[0] if len(blocks) == 1 else jnp.concatenate(blocks, axis=1)


def _softplus(x):
    return jnp.maximum(x, 0.0) + jnp.log(1.0 + jnp.exp(-jnp.abs(x)))


def _lerp(cur, prev, mu):
    return cur + (prev - cur) * mu


def _rwkv_prep(r_, k_, v_, lora, w0, w2, a0, a2, g2, k_k, k_a, r_k, ones_bd):
    w_lo, a_lo, g_lo = lora[:, 0:128], lora[:, 128:256], lora[:, 256:512]
    w_log = -_softplus(-(w0 + _mm(jnp.tanh(w_lo), w2))) - 0.5
    log_decay = -jnp.exp(w_log)
    a_sig = jax.nn.sigmoid(a0 + _mm(a_lo, a2))
    g = _mm(jax.nn.sigmoid(g_lo), g2)
    kk = k_ * k_k
    kk = kk * lax.rsqrt(jnp.maximum(_head_sum(kk * kk, ones_bd), 1e-24))
    kmod = k_ * (1.0 + (a_sig - 1.0) * k_a)
    bonus = _head_sum(r_ * kmod * r_k, ones_bd) * v_
    return log_decay, a_sig, g, kk, kmod, bonus


def _rwkv_out(y, bonus, g, lnx_w, lnx_b, ones_bd):
    inv_n = 1.0 / RWKV_HEAD
    yc = y - _head_sum(y, ones_bd) * inv_n
    var = _head_sum(yc * yc, ones_bd) * inv_n
    return (yc * lax.rsqrt(var + RWKV_GN_EPS) * lnx_w + lnx_b + bonus) * g


def _block_diag(x, lane_first):
    return jnp.concatenate([jnp.where(lane_first, x, 0.0), jnp.where(lane_first, 0.0, x)], axis=0)


def _pair_masks():
    tok = lax.broadcasted_iota(jnp.int32, (RWKV_CHUNK, LANES), 0)
    src = lax.broadcasted_iota(jnp.int32, (RWKV_CHUNK, LANES), 1) % RWKV_CHUNK
    n = 2 * RWKV_CHUNK
    eye = lax.broadcasted_iota(jnp.int32, (n, n), 0) == lax.broadcasted_iota(jnp.int32, (n, n), 1)
    lane_first = lax.broadcasted_iota(jnp.int32, (1, LANES), 1) < RWKV_HEAD
    return lane_first, src < tok, src <= tok, src == tok, eye


def _rwkv_chunks(chunks, masks):
    lane_first, strict, incl, eye_wide, eye = masks
    n = RWKV_CHUNK
    rows = lambda x, y: jnp.concatenate([x, y], axis=0)
    cols = lambda x, y: jnp.concatenate([x, y], axis=1)
    bd = lambda x: _block_diag(x, lane_first)

    ops = []
    for r, lw, k, v, a, b, cum in chunks:
        p_inc = jnp.exp(cum)
        p_inv = jnp.exp(-cum)
        p_exc = jnp.exp(cum - lw)
        p_end = p_inc[RWKV_CHUNK - 1:RWKV_CHUNK, :]
        b_t = b * p_inv
        k_t = k * p_inv
        ops.append(dict(a_t=(a * p_exc).astype(BF16), r_t=(r * p_inc).astype(BF16),
                        b_t=bd(b_t.astype(BF16)), k_t=bd(k_t.astype(BF16)),
                        b_hat=bd(b_t * p_end), k_hat=bd(k_t * p_end),
                        v_bd=bd(v.astype(BF16)), p_end=p_end))

    for o in ops:
        prod = _mm(rows(o["a_t"], o["r_t"]), rows(o["b_t"], o["k_t"]), _NT)
        o["a_ab"] = jnp.where(strict, prod[:n, :LANES], 0.0)
        o["a_ak"] = jnp.where(strict, prod[:n, LANES:], 0.0).astype(BF16)
        o["r_bk"] = cols(jnp.where(incl, prod[n:, :LANES], 0.0),
                         jnp.where(incl, prod[n:, LANES:], 0.0)).astype(BF16)

    for o in ops:
        a_ab = o["a_ab"].astype(BF16)
        o["t_inv"] = jnp.where(eye_wide, 1.0, 0.0) + o["a_ab"]
        o["power"] = _mm(a_ab, bd(a_ab)).astype(BF16)
        o["av"] = _mm(o["a_ak"], o["v_bd"]).astype(BF16)
    for _ in range(int(math.log2(RWKV_CHUNK)) - 2):
        for o in ops:
            both = _mm(rows(o["power"], o["t_inv"].astype(BF16)), bd(o["power"]))
            o["power"] = both[:n].astype(BF16)
            o["t_inv"] = o["t_inv"] + both[n:]
    for o in ops:
        o["t_inv"] = o["t_inv"] + _mm(o["t_inv"], bd(o["power"]))
    out = []
    for o in ops:
        wu = _mm(o["t_inv"], cols(bd(o["a_t"]), bd(o["av"])))
        hat_t = rows(o["b_hat"], o["k_hat"]).T
        p_col = jnp.sum(jnp.where(eye, o["p_end"], 0.0), axis=1, keepdims=True)
        out.append((rows(wu[:, :LANES].astype(BF16), o["r_t"]), wu[:, LANES:],
                    o["r_bk"], o["v_bd"], hat_t.astype(BF16), p_col))
    return out


def _shift_rows(cur, first_row):
    rolled = pltpu.roll(cur, 1, 0)
    row = lax.broadcasted_iota(jnp.int32, cur.shape, 0)
    return jnp.where(row == 0, first_row, rolled)


def _rwkv_body(r_ref, k_ref, v_ref, lora_ref, mu_ref, mul_ref,
               w0_ref, w2_ref, a0_ref, a2_ref, g2_ref, kk_ref, ka_ref, rk_ref,
               lnw_ref, lnb_ref, o_ref, sout_ref,
               s_ref, carry_ref, carryl_ref, *, tile):
    t = pl.program_id(1)
    p = pl.program_id(2)
    masks = _pair_masks()
    ones_bd = _head_ones()
    pairs = r_ref.shape[1] // LANES

    @pl.when(t == 0)
    def _():
        for q in range(pairs):
            s_ref[p * pairs + q] = jnp.zeros((LANES, LANES), F32)
        carry_ref[p] = jnp.zeros(carry_ref.shape[1:], F32)

        @pl.when(p == 0)
        def _():
            carryl_ref[...] = jnp.zeros_like(carryl_ref)

    prev = carry_ref[p]
    cur_r, cur_k, cur_v, cur_l = r_ref[...], k_ref[...], v_ref[...], lora_ref[...]
    mu = mu_ref[0]
    r_ = _lerp(cur_r, _shift_rows(cur_r, prev[0:1]), mu[0:1])
    k_ = _lerp(cur_k, _shift_rows(cur_k, prev[1:2]), mu[1:2])
    v_ = _lerp(cur_v, _shift_rows(cur_v, prev[2:3]), mu[2:3])
    lo = _lerp(cur_l, _shift_rows(cur_l, carryl_ref[0:1, :]), mul_ref[...])
    carry_ref[p, 0:1, :] = cur_r[tile - 1:tile]
    carry_ref[p, 1:2, :] = cur_k[tile - 1:tile]
    carry_ref[p, 2:3, :] = cur_v[tile - 1:tile]

    @pl.when(p == pl.num_programs(2) - 1)
    def _():
        carryl_ref[0:1, :] = cur_l[tile - 1:tile]

    log_decay, a_sig, g, kk, kmod, bonus = _rwkv_prep(
        r_, k_, v_, lo, w0_ref[...], w2_ref[...], a0_ref[...], a2_ref[...], g2_ref[...],
        kk_ref[...], ka_ref[...], rk_ref[...], ones_bd)

    row = lax.broadcasted_iota(jnp.int32, (tile, tile), 0)
    col = lax.broadcasted_iota(jnp.int32, (tile, tile), 1)
    tri = jnp.where(((row // RWKV_CHUNK) == (col // RWKV_CHUNK)) & (col <= row), 1.0, 0.0).astype(BF16)
    cum = sum(jnp.dot(tri, part, preferred_element_type=F32) for part in _split_bf16(log_decay, 3))

    neg_kk = -kk
    kk_a = kk * a_sig
    n_chunks = tile // RWKV_CHUNK
    chunks = []
    for c in range(n_chunks):
        for q in range(pairs):
            at = (slice(c * RWKV_CHUNK, (c + 1) * RWKV_CHUNK), slice(q * LANES, (q + 1) * LANES))
            chunks.append((r_[at], log_decay[at], kmod[at], v_[at], neg_kk[at], kk_a[at], cum[at]))
    parts = _rwkv_chunks(chunks, masks)

    n = RWKV_CHUNK
    lane_first = masks[0]
    states = [s_ref[p * pairs + q] for q in range(pairs)]
    ys = [[] for _ in range(pairs)]
    for c in range(n_chunks):
        reads = [_mm(parts[c * pairs + q][0], states[q]) for q in range(pairs)]
        for q in range(pairs):
            _, u0, r_bk, v_bd, hat_t, p_col = parts[c * pairs + q]
            u_bd = _block_diag(reads[q][:n] + u0, lane_first).astype(BF16)
            uv = jnp.concatenate([u_bd, v_bd], axis=0)
            ys[q].append(reads[q][n:] + _mm(r_bk, uv))
            states[q] = states[q] * p_col + _mm(hat_t, uv)
    for q in range(pairs):
        s_ref[p * pairs + q] = states[q]

    @pl.when(t == pl.num_programs(1) - 1)
    def _():
        for q in range(pairs):
            head = 2 * (p * pairs + q)
            s_bd = states[q].T
            sout_ref[0, head] = s_bd[:RWKV_HEAD, :RWKV_HEAD]
            sout_ref[0, head + 1] = s_bd[RWKV_HEAD:, RWKV_HEAD:]

    y = jnp.concatenate([jnp.concatenate(yq, axis=0) for yq in ys], axis=1)
    o_ref[...] = _rwkv_out(y, bonus, g, lnw_ref[...], lnb_ref[...], ones_bd)


RWKV_PAIRS_PER_STEP = 8


def _rwkv_prompt(proj, wts, batch, seq):
    tile = math.gcd(seq, 256)
    n_tiles = seq // tile
    width = RWKV_PAIRS_PER_STEP * LANES
    blk = lambda base: pl.BlockSpec((tile, width),
                                    lambda b, t, p, base=base: (b * n_tiles + t, base // width + p))
    per_pair = lambda rows: pl.BlockSpec((rows, width), lambda b, t, p: (0, p))
    in_specs = [
        blk(COL_R), blk(COL_KK), blk(COL_VV),
        pl.BlockSpec((tile, LORA_W), lambda b, t, p: (b * n_tiles + t, COL_LORA // LORA_W)),
        pl.BlockSpec((1, 3, width), lambda b, t, p: (0, 0, p)),
        pl.BlockSpec((1, LORA_W), lambda b, t, p: (0, 0)),
        per_pair(1), per_pair(LANES),
        per_pair(1), per_pair(LANES),
        per_pair(2 * LANES),
        per_pair(1), per_pair(1), per_pair(1),
        per_pair(1), per_pair(1),
    ]
    return pl.pallas_call(
        functools.partial(_rwkv_body, tile=tile),
        grid=(batch, n_tiles, RWKV_PAIRS // RWKV_PAIRS_PER_STEP),
        in_specs=in_specs,
        out_specs=[pl.BlockSpec((tile, width), lambda b, t, p: (b * n_tiles + t, p)),
                   pl.BlockSpec((1, RWKV_HEADS, RWKV_HEAD, RWKV_HEAD), lambda b, t, p: (b, 0, 0, 0))],
        out_shape=[jax.ShapeDtypeStruct((batch * seq, D_MODEL), F32),
                   jax.ShapeDtypeStruct((batch, RWKV_HEADS, RWKV_HEAD, RWKV_HEAD), F32)],
        scratch_shapes=[pltpu.VMEM((RWKV_PAIRS, LANES, LANES), F32),
                        pltpu.VMEM((RWKV_PAIRS // RWKV_PAIRS_PER_STEP, 8, width), F32),
                        pltpu.VMEM((8, LORA_W), F32)],
        compiler_params=_params(("parallel", "arbitrary", "arbitrary")),
        name="rwkv_prompt",
    )(proj, proj, proj, proj, wts["mu3"], wts["mu_lora"],
      wts["w0"], wts["w2"], wts["a0"], wts["a2"], wts["g2"],
      wts["k_k"], wts["k_a"], wts["r_k"], wts["lnx_w"], wts["lnx_b"])


def _rwkv_step_body(r_ref, k_ref, v_ref, lora_ref, pr_ref, pk_ref, pv_ref, plora_ref,
                    mu_ref, mul_ref, w0_ref, w2_ref, a0_ref, a2_ref, g2_ref,
                    kk_ref, ka_ref, rk_ref, lnw_ref, lnb_ref, s_ref,
                    o_ref, sout_ref, *, rows):
    ones_bd = _head_ones()
    mu = mu_ref[0]
    r_ = _lerp(r_ref[...], pr_ref[...], mu[0:1])
    k_ = _lerp(k_ref[...], pk_ref[...], mu[1:2])
    v_ = _lerp(v_ref[...], pv_ref[...], mu[2:3])
    lo = _lerp(lora_ref[...], plora_ref[...], mul_ref[...])
    log_decay, a_sig, g, kk, kmod, bonus = _rwkv_prep(
        r_, k_, v_, lo, w0_ref[...], w2_ref[...], a0_ref[...], a2_ref[...], g2_ref[...],
        kk_ref[...], ka_ref[...], rk_ref[...], ones_bd)
    decay = jnp.exp(log_decay)
    neg_kk = -kk
    kk_a = kk * a_sig
    decay_r = decay * r_
    row = lax.broadcasted_iota(jnp.int32, (rows, RWKV_HEAD), 0)

    heads = [slice(h * RWKV_HEAD, (h + 1) * RWKV_HEAD) for h in range(RWKV_HEADS)]
    reads = []
    for h, sl in enumerate(heads):
        lhs = jnp.concatenate([neg_kk[:, sl], decay_r[:, sl]], axis=0)
        reads.append([_mm(lhs, s_ref[b, h], _NT) for b in range(rows)])
    ys = []
    for h, sl in enumerate(heads):
        w_h, r_h, v_h, k_h, b_h = decay[:, sl], r_[:, sl], v_[:, sl], kmod[:, sl], kk_a[:, sl]
        rhs = jnp.concatenate([b_h, k_h], axis=0)
        sa_all = swr_all = jnp.zeros((rows, RWKV_HEAD), F32)
        adds = []
        for b in range(rows):
            mine = row == b
            sa, swr = reads[h][b][:rows], reads[h][b][rows:]
            left = jnp.concatenate([jnp.where(mine, sa, 0.0), jnp.where(mine, v_h, 0.0)], axis=0)
            adds.append(_mm(left, rhs, _TN))
            sa_all = jnp.where(mine, sa, sa_all)
            swr_all = jnp.where(mine, swr, swr_all)
        for b in range(rows):
            sout_ref[b, h] = s_ref[b, h] * w_h[b:b + 1] + adds[b]
        ys.append(swr_all + sa_all * jnp.sum(b_h * r_h, axis=-1, keepdims=True)
                  + v_h * jnp.sum(k_h * r_h, axis=-1, keepdims=True))
    y = jnp.concatenate(ys, axis=1)
    o_ref[...] = _rwkv_out(y, bonus, g, lnw_ref[...], lnb_ref[...], ones_bd)


def _rwkv_step(proj, s0, wts):
    batch = s0.shape[0]
    rows = 8
    shift = batch // rows
    wide = lambda base, off=0: pl.BlockSpec((rows, D_MODEL),
                                            lambda i, base=base, off=off: (i + off, base // D_MODEL))
    lora = pl.BlockSpec((rows, LORA_W), lambda i: (i, COL_LORA // LORA_W))
    lora_prev = pl.BlockSpec((rows, LORA_W), lambda i: (i + shift, COL_LORA // LORA_W))
    full = lambda shape: pl.BlockSpec(shape, lambda i: (0,) * len(shape))
    state = pl.BlockSpec((rows, RWKV_HEADS, RWKV_HEAD, RWKV_HEAD), lambda i: (i, 0, 0, 0))
    vec = full((1, D_MODEL))
    return pl.pallas_call(
        functools.partial(_rwkv_step_body, rows=rows),
        grid=(batch // rows,),
        in_specs=[wide(COL_R), wide(COL_KK), wide(COL_VV), lora,
                  wide(COL_R, shift), wide(COL_KK, shift), wide(COL_VV, shift), lora_prev,
                  full((1, 3, D_MODEL)), full((1, LORA_W)),
                  vec, full((LANES, D_MODEL)), vec, full((LANES, D_MODEL)),
                  full((2 * LANES, D_MODEL)), vec, vec, vec, vec, vec, state],
        out_specs=[pl.BlockSpec((rows, D_MODEL), lambda i: (i, 0)), state],
        out_shape=[jax.ShapeDtypeStruct((batch, D_MODEL), F32),
                   jax.ShapeDtypeStruct(s0.shape, F32)],
        compiler_params=_params(("parallel",)),
        name="rwkv_step",
    )(proj, proj, proj, proj, proj, proj, proj, proj,
      wts["mu3"], wts["mu_lora"], wts["w0"], wts["w2"], wts["a0"], wts["a2"], wts["g2"],
      wts["k_k"], wts["k_a"], wts["r_k"], wts["lnx_w"], wts["lnx_b"], s0)


MERGE_ROWS = 128


def _merge_body(x_ref, oret_ref, orwkv_ref, ga_ref, gb_ref, wout_ref, nffn_ref, rw_ref, rb_ref,
                x1_ref, xn_ref, comb_ref, cnt_ref):
    step = math.gcd(x_ref.shape[0], MERGE_ROWS)
    chunks = [slice(r0, r0 + step) for r0 in range(0, x_ref.shape[0], step)]
    each = lambda fn, *cols: [fn(*args) for args in zip(*cols)]
    row_max = lambda v: jnp.max(v, axis=-1, keepdims=True)
    row_min = lambda v: jnp.min(v, axis=-1, keepdims=True)
    row_sum = lambda v: jnp.sum(v, axis=-1, keepdims=True)
    lane = lax.broadcasted_iota(jnp.int32, (step, LANES), 1)
    is_group = lane < N_GROUPS
    neg_inf = -jnp.inf

    merged = [jax.nn.sigmoid(ga_ref[r, :]) * oret_ref[r, :]
              + jax.nn.sigmoid(gb_ref[r, :]) * orwkv_ref[r, :] for r in chunks]
    x1 = [x_ref[r, :] + _mm(m, wout_ref[...]) for r, m in zip(chunks, merged)]
    xn = each(lambda v: _rms(v, nffn_ref[...]), x1)
    for r, a, b in zip(chunks, x1, xn):
        x1_ref[r, :] = a
        xn_ref[r, :] = b.astype(BF16)

    logits = each(lambda v: _mm(v, rw_ref[...]) + rb_ref[...], xn)
    g_max = each(lambda l: row_max(jnp.where(is_group, l, neg_inf)), logits)
    g_sel = each(lambda l, m: row_min(jnp.where(is_group & (l == m), lane, LANES)), logits, g_max)
    g_w = each(lambda l, m: 1.0 / row_sum(jnp.where(is_group, jnp.exp(l - m), 0.0)), logits, g_max)
    first = each(lambda g: N_GROUPS + EXPERTS_PER_GROUP * g, g_sel)
    in_group = each(lambda f: (lane >= f) & (lane < f + EXPERTS_PER_GROUP), first)
    e_max = each(lambda l, ing: row_max(jnp.where(ing, l, neg_inf)), logits, in_group)
    e_exp = each(lambda l, ing, m: jnp.where(ing, jnp.exp(l - m), 0.0), logits, in_group, e_max)
    prob = each(lambda e, ing: jnp.where(ing, e / row_sum(e), -1.0), e_exp, in_group)
    p1 = each(row_max, prob)
    i1 = each(lambda p, m: row_min(jnp.where(p == m, lane, LANES)), prob, p1)
    rest = each(lambda p, i: jnp.where(lane == i, -1.0, p), prob, i1)
    p2 = each(row_max, rest)
    i2 = each(lambda p, m: row_min(jnp.where(p == m, lane, LANES)), rest, p2)
    lane8 = lax.broadcasted_iota(jnp.int32, (8, LANES), 1)
    sizes = jnp.zeros((8, LANES), jnp.int32)
    for c, r in enumerate(chunks):
        denom = p1[c] + p2[c]
        comb = (jnp.where(lane == i1[c] - N_GROUPS, g_w[c] * p1[c] / denom, 0.0)
                + jnp.where(lane == i2[c] - N_GROUPS, g_w[c] * p2[c] / denom, 0.0))
        comb_ref[r, :] = jnp.where(lane == N_EXPERTS, g_sel[c].astype(F32), comb)
        for g in range(N_GROUPS):
            n_g = jnp.sum(jnp.where(g_sel[c] == g, 1, 0), axis=0, keepdims=True)
            sizes = sizes + jnp.where(lane8 == g, n_g, 0)
    cnt_ref[0] = sizes


MERGE_TILE = 512


def _merge(x, o_ret, o_rwkv, proj, wts):
    t = x.shape[0]
    tm = math.gcd(t, MERGE_TILE)
    row = lambda: pl.BlockSpec((tm, D_MODEL), lambda i: (i, 0))
    full = lambda shape: pl.BlockSpec(shape, lambda i: (0, 0))
    return pl.pallas_call(
        _merge_body,
        grid=(t // tm,),
        in_specs=[row(), row(), row(),
                  pl.BlockSpec((tm, D_MODEL), lambda i: (i, COL_GATE_A // D_MODEL)),
                  pl.BlockSpec((tm, D_MODEL), lambda i: (i, COL_GATE_B // D_MODEL)),
                  full((D_MODEL, D_MODEL)), full((1, D_MODEL)),
                  full((D_MODEL, LANES)), full((1, LANES))],
        out_specs=[row(), row(), pl.BlockSpec((tm, LANES), lambda i: (i, 0)),
                   pl.BlockSpec((1, 8, LANES), lambda i: (i, 0, 0))],
        out_shape=[jax.ShapeDtypeStruct((t, D_MODEL), F32),
                   jax.ShapeDtypeStruct((t, D_MODEL), BF16),
                   jax.ShapeDtypeStruct((t, LANES), F32),
                   jax.ShapeDtypeStruct((t // tm, 8, LANES), jnp.int32)],
        compiler_params=_params(("parallel",)),
        name="merge_router",
    )(x, o_ret, o_rwkv, proj, proj, wts["w_out"], wts["norm_ffn"], wts["router_w"], wts["router_b"])


MOE_TILE = 1024
MOE_BLOCK = 128
EXPERTS_PER_STEP = 4


def _moe_body(cnt_ref, xn_ref, x1_ref, comb_ref, wg_ref, wu_ref, wd_ref, nf_ref, o_ref,
              acc_ref, xs_ref, cs_ref, pt_ref, *, ratio):
    i = pl.program_id(0)
    s = pl.program_id(1)
    tm = xn_ref.shape[0]
    group = s // (EXPERTS_PER_GROUP // EXPERTS_PER_STEP)

    sizes = []
    for g in range(N_GROUPS):
        n = cnt_ref[i * ratio * N_GROUPS + g]
        for k in range(1, ratio):
            n = n + cnt_ref[(i * ratio + k) * N_GROUPS + g]
        sizes.append(n)
    starts = [jnp.int32(0)]
    for g in range(N_GROUPS - 1):
        starts.append(starts[-1] + sizes[g])

    @pl.when(s == 0)
    def _():
        comb = comb_ref[...]
        g_row = comb.T[N_EXPERTS:N_EXPERTS + 1, :]
        gid = lax.broadcasted_iota(jnp.int32, (8, tm), 0)
        onehot = jnp.where(g_row.astype(jnp.int32) == gid, 1.0, 0.0)
        step = math.gcd(tm, 256)
        row = lax.broadcasted_iota(jnp.int32, (step, tm), 0)
        col = lax.broadcasted_iota(jnp.int32, (step, tm), 1)
        rank = jnp.zeros((8, tm), F32)
        for r0 in range(0, tm, step):
            earlier = jnp.where(row + r0 < col, 1.0, 0.0).astype(BF16)
            rank = rank + jnp.dot(onehot[:, r0:r0 + step].astype(BF16), earlier,
                                  preferred_element_type=F32)
        start_col = jnp.zeros((8, 1), F32)
        gid_col = lax.broadcasted_iota(jnp.int32, (8, 1), 0)
        for g in range(1, N_GROUPS):
            start_col = jnp.where(gid_col == g, starts[g].astype(F32), start_col)
        pos_row = jnp.sum(onehot * (rank + start_col), axis=0, keepdims=True)
        pos_row_i = pos_row.astype(jnp.int32)
        xn = xn_ref[...]
        comb_parts = _split_bf16(comb, 3)
        for r0 in range(0, tm, step):
            perm = jnp.where(row + r0 == pos_row_i, 1.0, 0.0).astype(BF16)
            xs_ref[r0:r0 + step, :] = jnp.dot(perm, xn, preferred_element_type=F32).astype(BF16)
            cs_ref[r0:r0 + step, :] = sum(jnp.dot(perm, part, preferred_element_type=F32)
                                          for part in comb_parts)
        pos_col_i = jnp.broadcast_to(pos_row, (LANES, tm)).T.astype(jnp.int32)
        lane = lax.broadcasted_iota(jnp.int32, (tm, LANES), 1)
        for j in range(tm // LANES):
            pt_ref[:, j * LANES:(j + 1) * LANES] = jnp.where(
                lane + j * LANES == pos_col_i, 1.0, 0.0).astype(BF16)
        acc_ref[...] = jnp.zeros_like(acc_ref)

    start, size = starts[N_GROUPS - 1], sizes[N_GROUPS - 1]
    for g in range(N_GROUPS - 2, -1, -1):
        start = jnp.where(group == g, starts[g], start)
        size = jnp.where(group == g, sizes[g], size)
    blk = math.gcd(tm, MOE_BLOCK)
    first = start // blk
    last = jnp.where(size > 0, (start + size + blk - 1) // blk, first)
    lane = lax.broadcasted_iota(jnp.int32, (blk, LANES), 1)

    def block(j, carry):
        r0 = pl.multiple_of(j * blk, blk)
        xb = xs_ref[pl.ds(r0, blk), :]
        cw = cs_ref[pl.ds(r0, blk), :]
        hs = []
        for k in range(EXPERTS_PER_STEP):
            weight = jnp.sum(jnp.where(lane == s * EXPERTS_PER_STEP + k, cw, 0.0),
                             axis=-1, keepdims=True)
            hs.append(_silu(_mm(xb, wg_ref[k])) * _mm(xb, wu_ref[k]) * weight)
        total = _mm(hs[0], wd_ref[0])
        for k in range(1, EXPERTS_PER_STEP):
            total = total + _mm(hs[k], wd_ref[k])
        acc_ref[pl.ds(r0, blk), :] += total
        return carry

    lax.fori_loop(first, last, block, 0)

    @pl.when(s == pl.num_programs(1) - 1)
    def _():
        parts = _split_bf16(acc_ref[...], 2)
        step = math.gcd(tm, 256)
        for r0 in range(0, tm, step):
            moe = sum(jnp.dot(pt_ref[r0:r0 + step, :], part, preferred_element_type=F32)
                      for part in parts)
            o_ref[r0:r0 + step, :] = _rms(x1_ref[r0:r0 + step, :] + moe, nf_ref[...])


def _moe(xn, x1, comb, counts, wts):
    t = xn.shape[0]
    tm = math.gcd(t, MOE_TILE)
    assert tm % LANES == 0, "token count must be a multiple of the lane width"
    ratio = tm // math.gcd(t, MERGE_TILE)
    row = lambda w: pl.BlockSpec((tm, w), lambda i, s, cnt: (i, 0))
    once = lambda w: pl.BlockSpec((tm, w), lambda i, s, cnt: (i, 0), pipeline_mode=pl.Buffered(1))
    expert = lambda a, b: pl.BlockSpec((EXPERTS_PER_STEP, a, b), lambda i, s, cnt: (s, 0, 0))
    grid_spec = pltpu.PrefetchScalarGridSpec(
        num_scalar_prefetch=1,
        grid=(t // tm, N_EXPERTS // EXPERTS_PER_STEP),
        in_specs=[row(D_MODEL), once(D_MODEL), row(LANES),
                  expert(D_MODEL, D_EXPERT), expert(D_MODEL, D_EXPERT), expert(D_EXPERT, D_MODEL),
                  pl.BlockSpec((1, D_MODEL), lambda i, s, cnt: (0, 0))],
        out_specs=row(D_MODEL),
        scratch_shapes=[pltpu.VMEM((tm, D_MODEL), F32),
                        pltpu.VMEM((tm, D_MODEL), BF16),
                        pltpu.VMEM((tm, LANES), F32),
                        pltpu.VMEM((tm, tm), BF16)])
    return pl.pallas_call(
        functools.partial(_moe_body, ratio=ratio),
        grid_spec=grid_spec,
        out_shape=jax.ShapeDtypeStruct((t, D_MODEL), F32),
        compiler_params=_params(("parallel", "arbitrary")),
        name="moe",
    )(counts, xn, x1, comb, wts["w_gate"], wts["w_up"], wts["w_down"], wts["norm_final"])


def _pad_to(x, size, axis):
    pad = [(0, 0)] * x.ndim
    pad[axis] = (0, size - x.shape[axis])
    return jnp.pad(x, pad)


def _prepare_weights(norm_mix, w_in, mu_shift, ret_gn_w, rwkv_w0, rwkv_w2, rwkv_a0, rwkv_a2,
                     rwkv_g2, rwkv_k_k, rwkv_k_a, rwkv_r_k, rwkv_lnx_w, rwkv_lnx_b, w_out,
                     norm_ffn, router_group_w, router_group_b, router_expert_w, router_expert_b,
                     expert_w_gate, expert_w_up, expert_w_down, norm_final):
    def lora_cols(t):
        return jnp.concatenate([_pad_to(t[..., SRC_WLO:SRC_ALO], 128, -1),
                                _pad_to(t[..., SRC_ALO:SRC_GLO], 128, -1),
                                _pad_to(t[..., SRC_GLO:SRC_GATE_A], 256, -1)], axis=-1)

    w = w_in[0]
    w_in_p = jnp.concatenate([w[:, :SRC_WLO], w[:, SRC_GATE_A:], lora_cols(w)], axis=1).astype(BF16)
    mu = jnp.concatenate([jnp.zeros((SRC_SHIFT,), F32), mu_shift[0]])
    row = lambda v: v.reshape(1, -1)
    router_w = _pad_to(jnp.concatenate([router_group_w[0], router_expert_w[0]], axis=1), LANES, 1)
    router_b = _pad_to(jnp.concatenate([router_group_b[0], router_expert_b[0]]), LANES, 0)
    return dict(
        w_in=w_in_p, norm_mix=row(norm_mix[0]), ret_gn_w=row(ret_gn_w[0]),
        mu3=mu[SRC_SHIFT:SRC_WLO].reshape(1, 3, D_MODEL), mu_lora=lora_cols(mu).reshape(1, LORA_W),
        w0=row(rwkv_w0[0]), w2=_pad_to(rwkv_w2[0], 128, 0),
        a0=row(rwkv_a0[0]), a2=_pad_to(rwkv_a2[0], 128, 0),
        g2=_pad_to(rwkv_g2[0], 256, 0),
        k_k=row(rwkv_k_k[0]), k_a=row(rwkv_k_a[0]), r_k=row(rwkv_r_k[0]),
        lnx_w=row(rwkv_lnx_w[0]), lnx_b=row(rwkv_lnx_b[0]),
        w_out=w_out[0].astype(BF16), norm_ffn=row(norm_ffn[0]),
        router_w=router_w, router_b=row(router_b),
        w_gate=expert_w_gate[0].astype(BF16), w_up=expert_w_up[0].astype(BF16),
        w_down=expert_w_down[0].astype(BF16), norm_final=row(norm_final))


def _finish(x, o_ret, o_rwkv, proj, wts):
    x1, xn2, comb, sizes = _merge(x, o_ret, o_rwkv, proj, wts)
    counts = sizes[:, 0, :N_GROUPS].reshape(-1)
    return _moe(xn2, x1, comb, counts, wts)


def kernel(x_prompt, x_sample, state_ret, state_rwkv, state_shift, norm_mix, w_in, mu_shift, ret_gn_w, rwkv_w0, rwkv_w2, rwkv_a0, rwkv_a2, rwkv_g2, rwkv_k_k, rwkv_k_a, rwkv_r_k, rwkv_lnx_w, rwkv_lnx_b, w_out, norm_ffn, router_group_w, router_group_b, router_expert_w, router_expert_b, expert_w_gate, expert_w_up, expert_w_down, norm_final):
    assert norm_mix.shape[0] == 1, "single-layer step"
    wts = _prepare_weights(norm_mix, w_in, mu_shift, ret_gn_w, rwkv_w0, rwkv_w2, rwkv_a0, rwkv_a2,
                           rwkv_g2, rwkv_k_k, rwkv_k_a, rwkv_r_k, rwkv_lnx_w, rwkv_lnx_b, w_out,
                           norm_ffn, router_group_w, router_group_b, router_expert_w,
                           router_expert_b, expert_w_gate, expert_w_up, expert_w_down, norm_final)

    bp, lp, _ = x_prompt.shape
    xp = x_prompt.reshape(bp * lp, D_MODEL)
    tile_p = math.gcd(lp, 1024)
    proj_p, tail_p = _inproj(xp, wts["norm_mix"], wts["w_in"], tile_p, tile_p, 8)
    o_ret_p, ret_p = _retention_prompt(proj_p, wts["ret_gn_w"], bp, lp)
    o_rwkv_p, rwkv_p = _rwkv_prompt(proj_p, wts, bp, lp)
    y_prompt = _finish(xp, o_ret_p, o_rwkv_p, proj_p, wts).reshape(bp, lp, D_MODEL)
    shift_p = tail_p.reshape(bp, lp // tile_p, 8, D_MODEL)[:, -1, -1]

    bs, ls, _ = x_sample.shape
    assert ls == 1, "sample group advances one token"
    xs = x_sample.reshape(bs, D_MODEL)
    both = jnp.concatenate([xs, state_shift[0]], axis=0)
    proj_s, xn_s = _inproj(both, wts["norm_mix"], wts["w_in"], 2 * bs, bs, bs)
    pos_s = PAST_LEN + jnp.arange(ls, dtype=F32)
    o_ret_s, ret_s = _retention_step(proj_s, state_ret[0], wts["ret_gn_w"], pos_s)
    o_rwkv_s, rwkv_s = _rwkv_step(proj_s, state_rwkv[0], wts)
    y_sample = _finish(xs, o_ret_s, o_rwkv_s, proj_s, wts).reshape(bs, ls, D_MODEL)

    return (y_prompt, y_sample, ret_p[None], rwkv_p[None], shift_p[None],
            ret_s[None], rwkv_s[None], xn_s)
```

```python
import functools
import math

import jax
import jax.numpy as jnp
from jax import lax
from jax.experimental import pallas as pl
from jax.experimental.pallas import tpu as pltpu

F32 = jnp.float32
BF16 = jnp.bfloat16

D_MODEL = 1024
LANES = 128
SUBLANES = 8
PAST_LEN = 16384
RET_HEADS = 8
RET_D = D_MODEL // RET_HEADS
RET_CHUNK = 128
ROPE_BASE = 10000.0
RET_GN_EPS = 1e-5
RWKV_HEAD = 64
RWKV_HEADS = D_MODEL // RWKV_HEAD
RWKV_PAIRS = RWKV_HEADS // 2
RWKV_CHUNK = 64
LORA_DECAY = 64
LORA_A = 64
LORA_G = 160
RWKV_GN_EPS = 64e-5
N_GROUPS = 4
EXPERTS_PER_GROUP = 8
N_EXPERTS = N_GROUPS * EXPERTS_PER_GROUP
D_EXPERT = 256
RMS_EPS = 1e-6

SRC_SHIFT = 4 * D_MODEL
SRC_WLO = SRC_SHIFT + 3 * D_MODEL
SRC_ALO = SRC_WLO + LORA_DECAY
SRC_GLO = SRC_ALO + LORA_A
SRC_GATE_A = SRC_GLO + LORA_G
COL_Q, COL_K, COL_V, COL_GSW = 0, 1024, 2048, 3072
COL_R, COL_KK, COL_VV = 4096, 5120, 6144
COL_GATE_A, COL_GATE_B = 7168, 8192
COL_LORA = 9216
LORA_W = 4 * LANES
N_PROJ = COL_LORA + LORA_W

VMEM_LIMIT = 48 * 1024 * 1024

INPROJ_TILE = 1024
INPROJ_SLABS = 4
INPROJ_ROWS = 128
RET_CHUNKS_PER_STEP = 4
RWKV_TILE = 256
RWKV_PAIRS_PER_STEP = 8
STEP_ROWS = 8
MERGE_TILE = 512
MERGE_ROWS = 128
MOE_TILE = 1024
MOE_BLOCK = 128
MOE_PERM_ROWS = 256
EXPERTS_PER_STEP = 4

_NN = (((1,), (0,)), ((), ()))
_NT = (((1,), (1,)), ((), ()))
_TN = (((0,), (0,)), ((), ()))


def _mm(a, b, dims=_NN):
    return lax.dot_general(a.astype(BF16), b.astype(BF16), dims,
                           preferred_element_type=F32)


def _params(sem):
    return pltpu.CompilerParams(dimension_semantics=sem,
                                vmem_limit_bytes=VMEM_LIMIT)


def _rms(x, gain):
    return x * lax.rsqrt(jnp.mean(x * x, axis=-1, keepdims=True) + RMS_EPS) * gain


def _split_bf16(x, terms):
    parts = []
    for _ in range(terms - 1):
        hi = x.astype(BF16)
        parts.append(hi)
        x = x - hi.astype(F32)
    parts.append(x.astype(BF16))
    return parts


def _lane_sum(x, ones):
    hi, lo = _split_bf16(x, 2)
    return (jnp.dot(hi, ones, preferred_element_type=F32)
            + jnp.dot(lo, ones, preferred_element_type=F32))


def _inproj_body(x_ref, g_ref, w_ref, o_ref, xn_ref, *, normed, keep):
    step = math.gcd(x_ref.shape[0], INPROJ_ROWS, normed)
    for r0 in range(0, x_ref.shape[0], step):
        x = x_ref[r0:r0 + step, :]
        if r0 < normed:
            x = _rms(x, g_ref[...])
            lo, hi = max(r0, normed - keep), r0 + step
            if lo < hi:
                xn_ref[0, 0, lo - (normed - keep):hi - (normed - keep), :] = x[lo - r0:, :]
        o_ref[r0:r0 + step, :] = jnp.dot(x.astype(BF16), w_ref[...], preferred_element_type=F32)


def _inproj(x, gain, w, tm, normed, keep):
    t = x.shape[0]
    n = w.shape[1]
    tn = n // INPROJ_SLABS
    assert keep <= normed <= tm
    proj, kept = pl.pallas_call(
        functools.partial(_inproj_body, normed=normed, keep=keep),
        grid=(n // tn, t // tm),
        in_specs=[pl.BlockSpec((tm, D_MODEL), lambda j, i: (i, 0)),
                  pl.BlockSpec((1, D_MODEL), lambda j, i: (0, 0)),
                  pl.BlockSpec((D_MODEL, tn), lambda j, i: (0, j), pipeline_mode=pl.Buffered(1))],
        out_specs=[pl.BlockSpec((tm, tn), lambda j, i: (i, j)),
                   pl.BlockSpec((1, 1, keep, D_MODEL), lambda j, i: (j, i, 0, 0))],
        out_shape=[jax.ShapeDtypeStruct((t, n), F32),
                   jax.ShapeDtypeStruct((n // tn, t // tm, keep, D_MODEL), F32)],
        compiler_params=_params(("arbitrary", "arbitrary")),
        name="inproj",
    )(x, gain, w)
    return proj, kept[0]


def _head_norm(x, eps):
    mu = jnp.mean(x, axis=-1, keepdims=True)
    xc = x - mu
    return xc * lax.rsqrt(jnp.mean(xc * xc, axis=-1, keepdims=True) + eps)


def _silu(x):
    return x * jax.nn.sigmoid(x)


def _rope(x, cos, sin_signed):
    return x * cos + pltpu.roll(x, RET_D // 2, 1) * sin_signed


def _ret_tables(chunk):
    log_gamma = jnp.log1p(-jnp.exp2(-5.0 - jnp.arange(RET_HEADS, dtype=F32)))
    idx = jnp.arange(chunk, dtype=F32)
    rel = idx[:, None] - idx[None, :]
    intra = jnp.where(rel[None] >= 0,
                      jnp.exp(jnp.maximum(rel, 0.0)[None] * log_gamma[:, None, None]), 0.0)
    q_dec = jnp.exp((idx[:, None] + 1.0) * log_gamma[None, :])
    k_dec = jnp.exp((chunk - 1.0 - idx)[None, :] * log_gamma[:, None]).T
    chunk_dec = jnp.exp(chunk * log_gamma)[None, :]
    widen = lambda t: jnp.repeat(t, RET_D, axis=1)
    return intra, widen(q_dec), widen(k_dec), widen(chunk_dec)


def _rope_tables(pos):
    half = RET_D // 2
    inv = ROPE_BASE ** (-jnp.arange(half, dtype=F32) / half)
    ang = pos[:, None] * inv[None, :]
    cos, sin = jnp.cos(ang), jnp.sin(ang)
    return jnp.concatenate([cos, cos], axis=1), jnp.concatenate([-sin, sin], axis=1)


def _ret_body(q_ref, k_ref, v_ref, g_ref, cos_ref, sin_ref, qd_ref, kd_ref, cd_ref,
              intra_ref, gnw_ref, o_ref, sout_ref, s_ref):
    c = pl.program_id(1)
    chunk = qd_ref.shape[0]

    @pl.when(c == 0)
    def _():
        s_ref[...] = jnp.zeros_like(s_ref)

    heads = [slice(h * RET_D, (h + 1) * RET_D) for h in range(RET_HEADS)]
    ones = jnp.ones((RET_D, RET_D), BF16)
    state = [s_ref[h] for h in range(RET_HEADS)]
    for r0 in range(0, q_ref.shape[0], chunk):
        rows = slice(r0, r0 + chunk)
        cos = cos_ref[rows, :]
        sin = sin_ref[rows, :]
        qr = [_rope(q_ref[rows, sl], cos, sin).astype(BF16) for sl in heads]
        kr = [_rope(k_ref[rows, sl], cos, sin) for sl in heads]
        vb = [v_ref[rows, sl].astype(BF16) for sl in heads]
        scores = [_mm(qr[h], kr[h], _NT) * intra_ref[h] for h in range(RET_HEADS)]
        cross = [_mm(qr[h], state[h]) * qd_ref[:, sl] for h, sl in enumerate(heads)]
        adds = [_mm(kr[h] * kd_ref[:, sl], vb[h], _TN) for h, sl in enumerate(heads)]
        out = [_mm(scores[h], vb[h]) + cross[h] for h in range(RET_HEADS)]
        cen = [out[h] - _lane_sum(out[h], ones) * (1.0 / RET_D) for h in range(RET_HEADS)]
        var = [_lane_sum(cen[h] * cen[h], ones) * (1.0 / RET_D) for h in range(RET_HEADS)]
        for h, sl in enumerate(heads):
            state[h] = state[h] * cd_ref[:, sl] + adds[h]
            o_ref[rows, sl] = (cen[h] * lax.rsqrt(var[h] + RET_GN_EPS) * gnw_ref[:, sl]
                               * _silu(g_ref[rows, sl]))
    for h in range(RET_HEADS):
        s_ref[h] = state[h]

    @pl.when(c == pl.num_programs(1) - 1)
    def _():
        sout_ref[0] = s_ref[...]


def _retention_prompt(proj, gn_w, batch, seq):
    chunk = math.gcd(seq, RET_CHUNK)
    tile = math.gcd(seq, RET_CHUNKS_PER_STEP * chunk)
    n_tiles = seq // tile
    intra, q_dec, k_dec, chunk_dec = _ret_tables(chunk)
    intra = intra * (RET_D ** -0.5)
    k_dec = k_dec * (RET_D ** -0.5)
    cos, sin = _rope_tables(jnp.arange(seq, dtype=F32))
    col = lambda j: pl.BlockSpec((tile, D_MODEL), lambda b, c, j=j: (b * n_tiles + c, j))
    const2 = lambda shape: pl.BlockSpec(shape, lambda b, c: (0, 0))
    state = pl.BlockSpec((1, RET_HEADS, RET_D, RET_D), lambda b, c: (b, 0, 0, 0))
    return pl.pallas_call(
        _ret_body,
        grid=(batch, n_tiles),
        in_specs=[col(COL_Q // D_MODEL), col(COL_K // D_MODEL), col(COL_V // D_MODEL),
                  col(COL_GSW // D_MODEL),
                  pl.BlockSpec((tile, RET_D), lambda b, c: (c, 0)),
                  pl.BlockSpec((tile, RET_D), lambda b, c: (c, 0)),
                  const2((chunk, D_MODEL)), const2((chunk, D_MODEL)), const2((1, D_MODEL)),
                  pl.BlockSpec((RET_HEADS, chunk, chunk), lambda b, c: (0, 0, 0)),
                  const2((1, D_MODEL))],
        out_specs=[pl.BlockSpec((tile, D_MODEL), lambda b, c: (b * n_tiles + c, 0)), state],
        out_shape=[jax.ShapeDtypeStruct((batch * seq, D_MODEL), F32),
                   jax.ShapeDtypeStruct((batch, RET_HEADS, RET_D, RET_D), F32)],
        scratch_shapes=[pltpu.VMEM((RET_HEADS, RET_D, RET_D), F32)],
        compiler_params=_params(("parallel", "arbitrary")),
        name="retention_prompt",
    )(proj, proj, proj, proj, cos, sin, q_dec, k_dec, chunk_dec, intra, gn_w)


def _ret_step_body(q_ref, k_ref, v_ref, g_ref, cos_ref, sin_ref, qd_ref, kd_ref, cd_ref,
                   gnw_ref, s_ref, o_ref, sout_ref, *, rows):
    cos = cos_ref[...]
    sin = sin_ref[...]
    row = lax.broadcasted_iota(jnp.int32, (rows, RET_D), 0)
    for h in range(RET_HEADS):
        sl = slice(h * RET_D, (h + 1) * RET_D)
        qr = _rope(q_ref[:, sl], cos, sin)
        kr = _rope(k_ref[:, sl], cos, sin) * (RET_D ** -0.5)
        v = v_ref[:, sl]
        kd = kr * kd_ref[:, sl]
        cd = cd_ref[:, sl]

        reads = [_mm(qr, s_ref[b, h]) for b in range(rows)]
        adds = [_mm(jnp.where(row == b, kd, 0.0), v, _TN) for b in range(rows)]
        cross = jnp.zeros((rows, RET_D), F32)
        for b in range(rows):
            sout_ref[b, h] = s_ref[b, h] * cd + adds[b]
            cross = jnp.where(row == b, reads[b], cross)
        out = jnp.sum(qr * kr, axis=-1, keepdims=True) * v + cross * qd_ref[:, sl]
        o_ref[:, sl] = _head_norm(out, RET_GN_EPS) * gnw_ref[:, sl] * _silu(g_ref[:, sl])


def _retention_step(proj, s0, gn_w, pos):
    batch = s0.shape[0]
    rows = STEP_ROWS
    _, q_dec, k_dec, chunk_dec = _ret_tables(1)
    cos, sin = _rope_tables(pos)
    col = lambda j: pl.BlockSpec((rows, D_MODEL), lambda i, j=j: (i, j))
    const2 = lambda shape: pl.BlockSpec(shape, lambda i: (0, 0))
    state = pl.BlockSpec((rows, RET_HEADS, RET_D, RET_D), lambda i: (i, 0, 0, 0))
    return pl.pallas_call(
        functools.partial(_ret_step_body, rows=rows),
        grid=(batch // rows,),
        in_specs=[col(COL_Q // D_MODEL), col(COL_K // D_MODEL), col(COL_V // D_MODEL),
                  col(COL_GSW // D_MODEL),
                  const2((1, RET_D)), const2((1, RET_D)),
                  const2((1, D_MODEL)), const2((1, D_MODEL)), const2((1, D_MODEL)),
                  const2((1, D_MODEL)), state],
        out_specs=[pl.BlockSpec((rows, D_MODEL), lambda i: (i, 0)), state],
        out_shape=[jax.ShapeDtypeStruct((batch, D_MODEL), F32),
                   jax.ShapeDtypeStruct(s0.shape, F32)],
        compiler_params=_params(("parallel",)),
        name="retention_step",
    )(proj, proj, proj, proj, cos, sin, q_dec, k_dec, chunk_dec, gn_w, s0)


def _head_ones():
    row = lax.broadcasted_iota(jnp.int32, (LANES, LANES), 0)
    col = lax.broadcasted_iota(jnp.int32, (LANES, LANES), 1)
    return jnp.where((row // RWKV_HEAD) == (col // RWKV_HEAD), 1.0, 0.0).astype(BF16)


def _head_sum(x, ones_bd):
    blocks = [_lane_sum(x[:, j:j + LANES], ones_bd) for j in range(0, x.shape[1], LANES)]
    return blocks[0] if len(blocks) == 1 else jnp.concatenate(blocks, axis=1)


def _softplus(x):
    return jnp.maximum(x, 0.0) + jnp.log(1.0 + jnp.exp(-jnp.abs(x)))


def _lerp(cur, prev, mu):
    return cur + (prev - cur) * mu


def _rwkv_prep(r_, k_, v_, lora, w0, w2, a0, a2, g2, k_k, k_a, r_k, ones_bd):
    w_lo, a_lo, g_lo = lora[:, :LANES], lora[:, LANES:2 * LANES], lora[:, 2 * LANES:LORA_W]
    w_log = -_softplus(-(w0 + _mm(jnp.tanh(w_lo), w2))) - 0.5
    log_decay = -jnp.exp(w_log)
    a_sig = jax.nn.sigmoid(a0 + _mm(a_lo, a2))
    g = _mm(jax.nn.sigmoid(g_lo), g2)
    kk = k_ * k_k
    kk = kk * lax.rsqrt(jnp.maximum(_head_sum(kk * kk, ones_bd), 1e-24))
    kmod = k_ * (1.0 + (a_sig - 1.0) * k_a)
    bonus = _head_sum(r_ * kmod * r_k, ones_bd) * v_
    return log_decay, a_sig, g, kk, kmod, bonus


def _rwkv_out(y, bonus, g, lnx_w, lnx_b, ones_bd):
    inv_n = 1.0 / RWKV_HEAD
    yc = y - _head_sum(y, ones_bd) * inv_n
    var = _head_sum(yc * yc, ones_bd) * inv_n
    return (yc * lax.rsqrt(var + RWKV_GN_EPS) * lnx_w + lnx_b + bonus) * g


def _block_diag(x, lane_first):
    return jnp.concatenate([jnp.where(lane_first, x, 0.0), jnp.where(lane_first, 0.0, x)], axis=0)


def _pair_masks():
    tok = lax.broadcasted_iota(jnp.int32, (RWKV_CHUNK, LANES), 0)
    src = lax.broadcasted_iota(jnp.int32, (RWKV_CHUNK, LANES), 1) % RWKV_CHUNK
    n = 2 * RWKV_CHUNK
    eye = lax.broadcasted_iota(jnp.int32, (n, n), 0) == lax.broadcasted_iota(jnp.int32, (n, n), 1)
    lane_first = lax.broadcasted_iota(jnp.int32, (1, LANES), 1) < RWKV_HEAD
    return lane_first, src < tok, src <= tok, src == tok, eye


def _rwkv_chunks(chunks, masks):
    lane_first, strict, incl, eye_wide, eye = masks
    n = RWKV_CHUNK
    rows = lambda x, y: jnp.concatenate([x, y], axis=0)
    cols = lambda x, y: jnp.concatenate([x, y], axis=1)
    bd = lambda x: _block_diag(x, lane_first)

    ops = []
    for r, lw, k, v, a, b, cum in chunks:
        p_inc = jnp.exp(cum)
        p_inv = jnp.exp(-cum)
        p_exc = jnp.exp(cum - lw)
        p_end = p_inc[RWKV_CHUNK - 1:RWKV_CHUNK, :]
        b_t = b * p_inv
        k_t = k * p_inv
        ops.append(dict(a_t=(a * p_exc).astype(BF16), r_t=(r * p_inc).astype(BF16),
                        b_t=bd(b_t.astype(BF16)), k_t=bd(k_t.astype(BF16)),
                        b_hat=bd(b_t * p_end), k_hat=bd(k_t * p_end),
                        v_bd=bd(v.astype(BF16)), p_end=p_end))

    for o in ops:
        prod = _mm(rows(o["a_t"], o["r_t"]), rows(o["b_t"], o["k_t"]), _NT)
        o["a_ab"] = jnp.where(strict, prod[:n, :LANES], 0.0)
        o["a_ak"] = jnp.where(strict, prod[:n, LANES:], 0.0).astype(BF16)
        o["r_bk"] = cols(jnp.where(incl, prod[n:, :LANES], 0.0),
                         jnp.where(incl, prod[n:, LANES:], 0.0)).astype(BF16)

    for o in ops:
        a_ab = o["a_ab"].astype(BF16)
        o["t_inv"] = jnp.where(eye_wide, 1.0, 0.0) + o["a_ab"]
        o["power"] = _mm(a_ab, bd(a_ab)).astype(BF16)
        o["av"] = _mm(o["a_ak"], o["v_bd"]).astype(BF16)
    for _ in range(int(math.log2(RWKV_CHUNK)) - 2):
        for o in ops:
            both = _mm(rows(o["power"], o["t_inv"].astype(BF16)), bd(o["power"]))
            o["power"] = both[:n].astype(BF16)
            o["t_inv"] = o["t_inv"] + both[n:]
    for o in ops:
        o["t_inv"] = o["t_inv"] + _mm(o["t_inv"], bd(o["power"]))
    out = []
    for o in ops:
        wu = _mm(o["t_inv"], cols(bd(o["a_t"]), bd(o["av"])))
        hat_t = rows(o["b_hat"], o["k_hat"]).T
        p_col = jnp.sum(jnp.where(eye, o["p_end"], 0.0), axis=1, keepdims=True)
        out.append((rows(wu[:, :LANES].astype(BF16), o["r_t"]), wu[:, LANES:],
                    o["r_bk"], o["v_bd"], hat_t.astype(BF16), p_col))
    return out


def _shift_rows(cur, first_row):
    rolled = pltpu.roll(cur, 1, 0)
    row = lax.broadcasted_iota(jnp.int32, cur.shape, 0)
    return jnp.where(row == 0, first_row, rolled)


def _rwkv_body(r_ref, k_ref, v_ref, lora_ref, mu_ref, mul_ref,
               w0_ref, w2_ref, a0_ref, a2_ref, g2_ref, kk_ref, ka_ref, rk_ref,
               lnw_ref, lnb_ref, o_ref, sout_ref,
               s_ref, carry_ref, carryl_ref, *, tile):
    t = pl.program_id(1)
    p = pl.program_id(2)
    masks = _pair_masks()
    ones_bd = _head_ones()
    pairs = r_ref.shape[1] // LANES

    @pl.when(t == 0)
    def _():
        for q in range(pairs):
            s_ref[p * pairs + q] = jnp.zeros((LANES, LANES), F32)
        carry_ref[p] = jnp.zeros(carry_ref.shape[1:], F32)

        @pl.when(p == 0)
        def _():
            carryl_ref[...] = jnp.zeros_like(carryl_ref)

    prev = carry_ref[p]
    cur_r, cur_k, cur_v, cur_l = r_ref[...], k_ref[...], v_ref[...], lora_ref[...]
    mu = mu_ref[0]
    r_ = _lerp(cur_r, _shift_rows(cur_r, prev[0:1]), mu[0:1])
    k_ = _lerp(cur_k, _shift_rows(cur_k, prev[1:2]), mu[1:2])
    v_ = _lerp(cur_v, _shift_rows(cur_v, prev[2:3]), mu[2:3])
    lo = _lerp(cur_l, _shift_rows(cur_l, carryl_ref[0:1, :]), mul_ref[...])
    carry_ref[p, 0:1, :] = cur_r[tile - 1:tile]
    carry_ref[p, 1:2, :] = cur_k[tile - 1:tile]
    carry_ref[p, 2:3, :] = cur_v[tile - 1:tile]

    @pl.when(p == pl.num_programs(2) - 1)
    def _():
        carryl_ref[0:1, :] = cur_l[tile - 1:tile]

    log_decay, a_sig, g, kk, kmod, bonus = _rwkv_prep(
        r_, k_, v_, lo, w0_ref[...], w2_ref[...], a0_ref[...], a2_ref[...], g2_ref[...],
        kk_ref[...], ka_ref[...], rk_ref[...], ones_bd)

    row = lax.broadcasted_iota(jnp.int32, (tile, tile), 0)
    col = lax.broadcasted_iota(jnp.int32, (tile, tile), 1)
    tri = jnp.where(((row // RWKV_CHUNK) == (col // RWKV_CHUNK)) & (col <= row), 1.0, 0.0).astype(BF16)
    cum = sum(jnp.dot(tri, part, preferred_element_type=F32) for part in _split_bf16(log_decay, 3))

    neg_kk = -kk
    kk_a = kk * a_sig
    n_chunks = tile // RWKV_CHUNK
    chunks = []
    for c in range(n_chunks):
        for q in range(pairs):
            at = (slice(c * RWKV_CHUNK, (c + 1) * RWKV_CHUNK), slice(q * LANES, (q + 1) * LANES))
            chunks.append((r_[at], log_decay[at], kmod[at], v_[at], neg_kk[at], kk_a[at], cum[at]))
    parts = _rwkv_chunks(chunks, masks)

    n = RWKV_CHUNK
    lane_first = masks[0]
    states = [s_ref[p * pairs + q] for q in range(pairs)]
    ys = [[] for _ in range(pairs)]
    for c in range(n_chunks):
        reads = [_mm(parts[c * pairs + q][0], states[q]) for q in range(pairs)]
        for q in range(pairs):
            _, u0, r_bk, v_bd, hat_t, p_col = parts[c * pairs + q]
            u_bd = _block_diag(reads[q][:n] + u0, lane_first).astype(BF16)
            uv = jnp.concatenate([u_bd, v_bd], axis=0)
            ys[q].append(reads[q][n:] + _mm(r_bk, uv))
            states[q] = states[q] * p_col + _mm(hat_t, uv)
    for q in range(pairs):
        s_ref[p * pairs + q] = states[q]

    @pl.when(t == pl.num_programs(1) - 1)
    def _():
        for q in range(pairs):
            head = 2 * (p * pairs + q)
            s_bd = states[q].T
            sout_ref[0, head] = s_bd[:RWKV_HEAD, :RWKV_HEAD]
            sout_ref[0, head + 1] = s_bd[RWKV_HEAD:, RWKV_HEAD:]

    y = jnp.concatenate([jnp.concatenate(yq, axis=0) for yq in ys], axis=1)
    o_ref[...] = _rwkv_out(y, bonus, g, lnw_ref[...], lnb_ref[...], ones_bd)


def _rwkv_prompt(proj, wts, batch, seq):
    tile = math.gcd(seq, RWKV_TILE)
    n_tiles = seq // tile
    width = RWKV_PAIRS_PER_STEP * LANES
    blk = lambda base: pl.BlockSpec((tile, width),
                                    lambda b, t, p, base=base: (b * n_tiles + t, base // width + p))
    per_pair = lambda rows: pl.BlockSpec((rows, width), lambda b, t, p: (0, p))
    in_specs = [
        blk(COL_R), blk(COL_KK), blk(COL_VV),
        pl.BlockSpec((tile, LORA_W), lambda b, t, p: (b * n_tiles + t, COL_LORA // LORA_W)),
        pl.BlockSpec((1, 3, width), lambda b, t, p: (0, 0, p)),
        pl.BlockSpec((1, LORA_W), lambda b, t, p: (0, 0)),
        per_pair(1), per_pair(LANES),
        per_pair(1), per_pair(LANES),
        per_pair(2 * LANES),
        per_pair(1), per_pair(1), per_pair(1),
        per_pair(1), per_pair(1),
    ]
    return pl.pallas_call(
        functools.partial(_rwkv_body, tile=tile),
        grid=(batch, n_tiles, RWKV_PAIRS // RWKV_PAIRS_PER_STEP),
        in_specs=in_specs,
        out_specs=[pl.BlockSpec((tile, width), lambda b, t, p: (b * n_tiles + t, p)),
                   pl.BlockSpec((1, RWKV_HEADS, RWKV_HEAD, RWKV_HEAD), lambda b, t, p: (b, 0, 0, 0))],
        out_shape=[jax.ShapeDtypeStruct((batch * seq, D_MODEL), F32),
                   jax.ShapeDtypeStruct((batch, RWKV_HEADS, RWKV_HEAD, RWKV_HEAD), F32)],
        scratch_shapes=[pltpu.VMEM((RWKV_PAIRS, LANES, LANES), F32),
                        pltpu.VMEM((RWKV_PAIRS // RWKV_PAIRS_PER_STEP, SUBLANES, width), F32),
                        pltpu.VMEM((SUBLANES, LORA_W), F32)],
        compiler_params=_params(("parallel", "arbitrary", "arbitrary")),
        name="rwkv_prompt",
    )(proj, proj, proj, proj, wts["mu3"], wts["mu_lora"],
      wts["w0"], wts["w2"], wts["a0"], wts["a2"], wts["g2"],
      wts["k_k"], wts["k_a"], wts["r_k"], wts["lnx_w"], wts["lnx_b"])


def _rwkv_step_body(r_ref, k_ref, v_ref, lora_ref, pr_ref, pk_ref, pv_ref, plora_ref,
                    mu_ref, mul_ref, w0_ref, w2_ref, a0_ref, a2_ref, g2_ref,
                    kk_ref, ka_ref, rk_ref, lnw_ref, lnb_ref, s_ref,
                    o_ref, sout_ref, *, rows):
    ones_bd = _head_ones()
    mu = mu_ref[0]
    r_ = _lerp(r_ref[...], pr_ref[...], mu[0:1])
    k_ = _lerp(k_ref[...], pk_ref[...], mu[1:2])
    v_ = _lerp(v_ref[...], pv_ref[...], mu[2:3])
    lo = _lerp(lora_ref[...], plora_ref[...], mul_ref[...])
    log_decay, a_sig, g, kk, kmod, bonus = _rwkv_prep(
        r_, k_, v_, lo, w0_ref[...], w2_ref[...], a0_ref[...], a2_ref[...], g2_ref[...],
        kk_ref[...], ka_ref[...], rk_ref[...], ones_bd)
    decay = jnp.exp(log_decay)
    neg_kk = -kk
    kk_a = kk * a_sig
    decay_r = decay * r_
    row = lax.broadcasted_iota(jnp.int32, (rows, RWKV_HEAD), 0)

    heads = [slice(h * RWKV_HEAD, (h + 1) * RWKV_HEAD) for h in range(RWKV_HEADS)]
    reads = []
    for h, sl in enumerate(heads):
        lhs = jnp.concatenate([neg_kk[:, sl], decay_r[:, sl]], axis=0)
        reads.append([_mm(lhs, s_ref[b, h], _NT) for b in range(rows)])
    ys = []
    for h, sl in enumerate(heads):
        w_h, r_h, v_h, k_h, b_h = decay[:, sl], r_[:, sl], v_[:, sl], kmod[:, sl], kk_a[:, sl]
        rhs = jnp.concatenate([b_h, k_h], axis=0)
        sa_all = swr_all = jnp.zeros((rows, RWKV_HEAD), F32)
        adds = []
        for b in range(rows):
            mine = row == b
            sa, swr = reads[h][b][:rows], reads[h][b][rows:]
            left = jnp.concatenate([jnp.where(mine, sa, 0.0), jnp.where(mine, v_h, 0.0)], axis=0)
            adds.append(_mm(left, rhs, _TN))
            sa_all = jnp.where(mine, sa, sa_all)
            swr_all = jnp.where(mine, swr, swr_all)
        for b in range(rows):
            sout_ref[b, h] = s_ref[b, h] * w_h[b:b + 1] + adds[b]
        ys.append(swr_all + sa_all * jnp.sum(b_h * r_h, axis=-1, keepdims=True)
                  + v_h * jnp.sum(k_h * r_h, axis=-1, keepdims=True))
    y = jnp.concatenate(ys, axis=1)
    o_ref[...] = _rwkv_out(y, bonus, g, lnw_ref[...], lnb_ref[...], ones_bd)


def _rwkv_step(proj, s0, wts):
    batch = s0.shape[0]
    rows = STEP_ROWS
    shift = batch // rows
    wide = lambda base, off=0: pl.BlockSpec((rows, D_MODEL),
                                            lambda i, base=base, off=off: (i + off, base // D_MODEL))
    lora = pl.BlockSpec((rows, LORA_W), lambda i: (i, COL_LORA // LORA_W))
    lora_prev = pl.BlockSpec((rows, LORA_W), lambda i: (i + shift, COL_LORA // LORA_W))
    full = lambda shape: pl.BlockSpec(shape, lambda i: (0,) * len(shape))
    state = pl.BlockSpec((rows, RWKV_HEADS, RWKV_HEAD, RWKV_HEAD), lambda i: (i, 0, 0, 0))
    vec = full((1, D_MODEL))
    return pl.pallas_call(
        functools.partial(_rwkv_step_body, rows=rows),
        grid=(batch // rows,),
        in_specs=[wide(COL_R), wide(COL_KK), wide(COL_VV), lora,
                  wide(COL_R, shift), wide(COL_KK, shift), wide(COL_VV, shift), lora_prev,
                  full((1, 3, D_MODEL)), full((1, LORA_W)),
                  vec, full((LANES, D_MODEL)), vec, full((LANES, D_MODEL)),
                  full((2 * LANES, D_MODEL)), vec, vec, vec, vec, vec, state],
        out_specs=[pl.BlockSpec((rows, D_MODEL), lambda i: (i, 0)), state],
        out_shape=[jax.ShapeDtypeStruct((batch, D_MODEL), F32),
                   jax.ShapeDtypeStruct(s0.shape, F32)],
        compiler_params=_params(("parallel",)),
        name="rwkv_step",
    )(proj, proj, proj, proj, proj, proj, proj, proj,
      wts["mu3"], wts["mu_lora"], wts["w0"], wts["w2"], wts["a0"], wts["a2"], wts["g2"],
      wts["k_k"], wts["k_a"], wts["r_k"], wts["lnx_w"], wts["lnx_b"], s0)


def _merge_body(x_ref, oret_ref, orwkv_ref, ga_ref, gb_ref, wout_ref, nffn_ref, rw_ref, rb_ref,
                x1_ref, xn_ref, comb_ref, cnt_ref):
    step = math.gcd(x_ref.shape[0], MERGE_ROWS)
    chunks = [slice(r0, r0 + step) for r0 in range(0, x_ref.shape[0], step)]
    each = lambda fn, *cols: [fn(*args) for args in zip(*cols)]
    row_max = lambda v: jnp.max(v, axis=-1, keepdims=True)
    row_min = lambda v: jnp.min(v, axis=-1, keepdims=True)
    row_sum = lambda v: jnp.sum(v, axis=-1, keepdims=True)
    lane = lax.broadcasted_iota(jnp.int32, (step, LANES), 1)
    is_group = lane < N_GROUPS
    neg_inf = -jnp.inf

    merged = [jax.nn.sigmoid(ga_ref[r, :]) * oret_ref[r, :]
              + jax.nn.sigmoid(gb_ref[r, :]) * orwkv_ref[r, :] for r in chunks]
    x1 = [x_ref[r, :] + _mm(m, wout_ref[...]) for r, m in zip(chunks, merged)]
    xn = each(lambda v: _rms(v, nffn_ref[...]), x1)
    for r, a, b in zip(chunks, x1, xn):
        x1_ref[r, :] = a
        xn_ref[r, :] = b.astype(BF16)

    logits = each(lambda v: _mm(v, rw_ref[...]) + rb_ref[...], xn)
    g_max = each(lambda l: row_max(jnp.where(is_group, l, neg_inf)), logits)
    g_sel = each(lambda l, m: row_min(jnp.where(is_group & (l == m), lane, LANES)), logits, g_max)
    g_w = each(lambda l, m: 1.0 / row_sum(jnp.where(is_group, jnp.exp(l - m), 0.0)), logits, g_max)
    first = each(lambda g: N_GROUPS + EXPERTS_PER_GROUP * g, g_sel)
    in_group = each(lambda f: (lane >= f) & (lane < f + EXPERTS_PER_GROUP), first)
    e_max = each(lambda l, ing: row_max(jnp.where(ing, l, neg_inf)), logits, in_group)
    e_exp = each(lambda l, ing, m: jnp.where(ing, jnp.exp(l - m), 0.0), logits, in_group, e_max)
    prob = each(lambda e, ing: jnp.where(ing, e / row_sum(e), -1.0), e_exp, in_group)
    p1 = each(row_max, prob)
    i1 = each(lambda p, m: row_min(jnp.where(p == m, lane, LANES)), prob, p1)
    rest = each(lambda p, i: jnp.where(lane == i, -1.0, p), prob, i1)
    p2 = each(row_max, rest)
    i2 = each(lambda p, m: row_min(jnp.where(p == m, lane, LANES)), rest, p2)
    lane8 = lax.broadcasted_iota(jnp.int32, (SUBLANES, LANES), 1)
    sizes = jnp.zeros((SUBLANES, LANES), jnp.int32)
    for c, r in enumerate(chunks):
        denom = p1[c] + p2[c]
        comb = (jnp.where(lane == i1[c] - N_GROUPS, g_w[c] * p1[c] / denom, 0.0)
                + jnp.where(lane == i2[c] - N_GROUPS, g_w[c] * p2[c] / denom, 0.0))
        comb_ref[r, :] = jnp.where(lane == N_EXPERTS, g_sel[c].astype(F32), comb)
        for g in range(N_GROUPS):
            n_g = jnp.sum(jnp.where(g_sel[c] == g, 1, 0), axis=0, keepdims=True)
            sizes = sizes + jnp.where(lane8 == g, n_g, 0)
    cnt_ref[0] = sizes


def _merge(x, o_ret, o_rwkv, proj, wts):
    t = x.shape[0]
    tm = math.gcd(t, MERGE_TILE)
    row = lambda: pl.BlockSpec((tm, D_MODEL), lambda i: (i, 0))
    full = lambda shape: pl.BlockSpec(shape, lambda i: (0, 0))
    return pl.pallas_call(
        _merge_body,
        grid=(t // tm,),
        in_specs=[row(), row(), row(),
                  pl.BlockSpec((tm, D_MODEL), lambda i: (i, COL_GATE_A // D_MODEL)),
                  pl.BlockSpec((tm, D_MODEL), lambda i: (i, COL_GATE_B // D_MODEL)),
                  full((D_MODEL, D_MODEL)), full((1, D_MODEL)),
                  full((D_MODEL, LANES)), full((1, LANES))],
        out_specs=[row(), row(), pl.BlockSpec((tm, LANES), lambda i: (i, 0)),
                   pl.BlockSpec((1, SUBLANES, LANES), lambda i: (i, 0, 0))],
        out_shape=[jax.ShapeDtypeStruct((t, D_MODEL), F32),
                   jax.ShapeDtypeStruct((t, D_MODEL), BF16),
                   jax.ShapeDtypeStruct((t, LANES), F32),
                   jax.ShapeDtypeStruct((t // tm, SUBLANES, LANES), jnp.int32)],
        compiler_params=_params(("parallel",)),
        name="merge_router",
    )(x, o_ret, o_rwkv, proj, proj, wts["w_out"], wts["norm_ffn"], wts["router_w"], wts["router_b"])


def _moe_body(cnt_ref, xn_ref, x1_ref, comb_ref, wg_ref, wu_ref, wd_ref, nf_ref, o_ref,
              acc_ref, xs_ref, cs_ref, pt_ref, *, ratio):
    i = pl.program_id(0)
    s = pl.program_id(1)
    tm = xn_ref.shape[0]
    group = s // (EXPERTS_PER_GROUP // EXPERTS_PER_STEP)

    sizes = []
    for g in range(N_GROUPS):
        n = cnt_ref[i * ratio * N_GROUPS + g]
        for k in range(1, ratio):
            n = n + cnt_ref[(i * ratio + k) * N_GROUPS + g]
        sizes.append(n)
    starts = [jnp.int32(0)]
    for g in range(N_GROUPS - 1):
        starts.append(starts[-1] + sizes[g])

    @pl.when(s == 0)
    def _():
        comb = comb_ref[...]
        g_row = comb.T[N_EXPERTS:N_EXPERTS + 1, :]
        gid = lax.broadcasted_iota(jnp.int32, (SUBLANES, tm), 0)
        onehot = jnp.where(g_row.astype(jnp.int32) == gid, 1.0, 0.0)
        step = math.gcd(tm, MOE_PERM_ROWS)
        row = lax.broadcasted_iota(jnp.int32, (step, tm), 0)
        col = lax.broadcasted_iota(jnp.int32, (step, tm), 1)
        rank = jnp.zeros((SUBLANES, tm), F32)
        for r0 in range(0, tm, step):
            earlier = jnp.where(row + r0 < col, 1.0, 0.0).astype(BF16)
            rank = rank + jnp.dot(onehot[:, r0:r0 + step].astype(BF16), earlier,
                                  preferred_element_type=F32)
        start_col = jnp.zeros((SUBLANES, 1), F32)
        gid_col = lax.broadcasted_iota(jnp.int32, (SUBLANES, 1), 0)
        for g in range(1, N_GROUPS):
            start_col = jnp.where(gid_col == g, starts[g].astype(F32), start_col)
        pos_row = jnp.sum(onehot * (rank + start_col), axis=0, keepdims=True)
        pos_row_i = pos_row.astype(jnp.int32)
        xn = xn_ref[...]
        comb_parts = _split_bf16(comb, 3)
        for r0 in range(0, tm, step):
            perm = jnp.where(row + r0 == pos_row_i, 1.0, 0.0).astype(BF16)
            xs_ref[r0:r0 + step, :] = jnp.dot(perm, xn, preferred_element_type=F32).astype(BF16)
            cs_ref[r0:r0 + step, :] = sum(jnp.dot(perm, part, preferred_element_type=F32)
                                          for part in comb_parts)
        pos_col_i = jnp.broadcast_to(pos_row, (LANES, tm)).T.astype(jnp.int32)
        lane = lax.broadcasted_iota(jnp.int32, (tm, LANES), 1)
        for j in range(tm // LANES):
            pt_ref[:, j * LANES:(j + 1) * LANES] = jnp.where(
                lane + j * LANES == pos_col_i, 1.0, 0.0).astype(BF16)
        acc_ref[...] = jnp.zeros_like(acc_ref)

    start, size = starts[N_GROUPS - 1], sizes[N_GROUPS - 1]
    for g in range(N_GROUPS - 2, -1, -1):
        start = jnp.where(group == g, starts[g], start)
        size = jnp.where(group == g, sizes[g], size)
    blk = math.gcd(tm, MOE_BLOCK)
    first = start // blk
    last = jnp.where(size > 0, (start + size + blk - 1) // blk, first)
    lane = lax.broadcasted_iota(jnp.int32, (blk, LANES), 1)

    def block(j, carry):
        r0 = pl.multiple_of(j * blk, blk)
        xb = xs_ref[pl.ds(r0, blk), :]
        cw = cs_ref[pl.ds(r0, blk), :]
        hs = []
        for k in range(EXPERTS_PER_STEP):
            weight = jnp.sum(jnp.where(lane == s * EXPERTS_PER_STEP + k, cw, 0.0),
                             axis=-1, keepdims=True)
            hs.append(_silu(_mm(xb, wg_ref[k])) * _mm(xb, wu_ref[k]) * weight)
        total = _mm(hs[0], wd_ref[0])
        for k in range(1, EXPERTS_PER_STEP):
            total = total + _mm(hs[k], wd_ref[k])
        acc_ref[pl.ds(r0, blk), :] += total
        return carry

    lax.fori_loop(first, last, block, 0)

    @pl.when(s == pl.num_programs(1) - 1)
    def _():
        parts = _split_bf16(acc_ref[...], 2)
        step = math.gcd(tm, MOE_PERM_ROWS)
        for r0 in range(0, tm, step):
            moe = sum(jnp.dot(pt_ref[r0:r0 + step, :], part, preferred_element_type=F32)
                      for part in parts)
            o_ref[r0:r0 + step, :] = _rms(x1_ref[r0:r0 + step, :] + moe, nf_ref[...])


def _moe(xn, x1, comb, counts, wts):
    t = xn.shape[0]
    tm = math.gcd(t, MOE_TILE)
    assert tm % LANES == 0, "token count must be a multiple of the lane width"
    ratio = tm // math.gcd(t, MERGE_TILE)
    row = lambda w: pl.BlockSpec((tm, w), lambda i, s, cnt: (i, 0))
    once = lambda w: pl.BlockSpec((tm, w), lambda i, s, cnt: (i, 0), pipeline_mode=pl.Buffered(1))
    expert = lambda a, b: pl.BlockSpec((EXPERTS_PER_STEP, a, b), lambda i, s, cnt: (s, 0, 0))
    grid_spec = pltpu.PrefetchScalarGridSpec(
        num_scalar_prefetch=1,
        grid=(t // tm, N_EXPERTS // EXPERTS_PER_STEP),
        in_specs=[row(D_MODEL), once(D_MODEL), row(LANES),
                  expert(D_MODEL, D_EXPERT), expert(D_MODEL, D_EXPERT), expert(D_EXPERT, D_MODEL),
                  pl.BlockSpec((1, D_MODEL), lambda i, s, cnt: (0, 0))],
        out_specs=row(D_MODEL),
        scratch_shapes=[pltpu.VMEM((tm, D_MODEL), F32),
                        pltpu.VMEM((tm, D_MODEL), BF16),
                        pltpu.VMEM((tm, LANES), F32),
                        pltpu.VMEM((tm, tm), BF16)])
    return pl.pallas_call(
        functools.partial(_moe_body, ratio=ratio),
        grid_spec=grid_spec,
        out_shape=jax.ShapeDtypeStruct((t, D_MODEL), F32),
        compiler_params=_params(("parallel", "arbitrary")),
        name="moe",
    )(counts, xn, x1, comb, wts["w_gate"], wts["w_up"], wts["w_down"], wts["norm_final"])


def _pad_to(x, size, axis):
    pad = [(0, 0)] * x.ndim
    pad[axis] = (0, size - x.shape[axis])
    return jnp.pad(x, pad)


def _prepare_weights(norm_mix, w_in, mu_shift, ret_gn_w, rwkv_w0, rwkv_w2, rwkv_a0, rwkv_a2,
                     rwkv_g2, rwkv_k_k, rwkv_k_a, rwkv_r_k, rwkv_lnx_w, rwkv_lnx_b, w_out,
                     norm_ffn, router_group_w, router_group_b, router_expert_w, router_expert_b,
                     expert_w_gate, expert_w_up, expert_w_down, norm_final):
    def lora_cols(t):
        return jnp.concatenate([_pad_to(t[..., SRC_WLO:SRC_ALO], LANES, -1),
                                _pad_to(t[..., SRC_ALO:SRC_GLO], LANES, -1),
                                _pad_to(t[..., SRC_GLO:SRC_GATE_A], 2 * LANES, -1)], axis=-1)

    w = w_in[0]
    w_in_p = jnp.concatenate([w[:, :SRC_WLO], w[:, SRC_GATE_A:], lora_cols(w)], axis=1).astype(BF16)
    mu = jnp.concatenate([jnp.zeros((SRC_SHIFT,), F32), mu_shift[0]])
    row = lambda v: v.reshape(1, -1)
    router_w = _pad_to(jnp.concatenate([router_group_w[0], router_expert_w[0]], axis=1), LANES, 1)
    router_b = _pad_to(jnp.concatenate([router_group_b[0], router_expert_b[0]]), LANES, 0)
    return dict(
        w_in=w_in_p, norm_mix=row(norm_mix[0]), ret_gn_w=row(ret_gn_w[0]),
        mu3=mu[SRC_SHIFT:SRC_WLO].reshape(1, 3, D_MODEL), mu_lora=lora_cols(mu).reshape(1, LORA_W),
        w0=row(rwkv_w0[0]), w2=_pad_to(rwkv_w2[0], LANES, 0),
        a0=row(rwkv_a0[0]), a2=_pad_to(rwkv_a2[0], LANES, 0),
        g2=_pad_to(rwkv_g2[0], 2 * LANES, 0),
        k_k=row(rwkv_k_k[0]), k_a=row(rwkv_k_a[0]), r_k=row(rwkv_r_k[0]),
        lnx_w=row(rwkv_lnx_w[0]), lnx_b=row(rwkv_lnx_b[0]),
        w_out=w_out[0].astype(BF16), norm_ffn=row(norm_ffn[0]),
        router_w=router_w, router_b=row(router_b),
        w_gate=expert_w_gate[0].astype(BF16), w_up=expert_w_up[0].astype(BF16),
        w_down=expert_w_down[0].astype(BF16), norm_final=row(norm_final))


def _finish(x, o_ret, o_rwkv, proj, wts):
    x1, xn2, comb, sizes = _merge(x, o_ret, o_rwkv, proj, wts)
    counts = sizes[:, 0, :N_GROUPS].reshape(-1)
    return _moe(xn2, x1, comb, counts, wts)


def kernel(x_prompt, x_sample, state_ret, state_rwkv, state_shift, norm_mix, w_in, mu_shift, ret_gn_w, rwkv_w0, rwkv_w2, rwkv_a0, rwkv_a2, rwkv_g2, rwkv_k_k, rwkv_k_a, rwkv_r_k, rwkv_lnx_w, rwkv_lnx_b, w_out, norm_ffn, router_group_w, router_group_b, router_expert_w, router_expert_b, expert_w_gate, expert_w_up, expert_w_down, norm_final):
    assert norm_mix.shape[0] == 1, "single-layer step"
    wts = _prepare_weights(norm_mix, w_in, mu_shift, ret_gn_w, rwkv_w0, rwkv_w2, rwkv_a0, rwkv_a2,
                           rwkv_g2, rwkv_k_k, rwkv_k_a, rwkv_r_k, rwkv_lnx_w, rwkv_lnx_b, w_out,
                           norm_ffn, router_group_w, router_group_b, router_expert_w,
                           router_expert_b, expert_w_gate, expert_w_up, expert_w_down, norm_final)

    bp, lp, _ = x_prompt.shape
    xp = x_prompt.reshape(bp * lp, D_MODEL)
    tile_p = math.gcd(lp, INPROJ_TILE)
    proj_p, tail_p = _inproj(xp, wts["norm_mix"], wts["w_in"], tile_p, tile_p, SUBLANES)
    o_ret_p, ret_p = _retention_prompt(proj_p, wts["ret_gn_w"], bp, lp)
    o_rwkv_p, rwkv_p = _rwkv_prompt(proj_p, wts, bp, lp)
    y_prompt = _finish(xp, o_ret_p, o_rwkv_p, proj_p, wts).reshape(bp, lp, D_MODEL)
    shift_p = tail_p.reshape(bp, lp // tile_p, SUBLANES, D_MODEL)[:, -1, -1]

    bs, ls, _ = x_sample.shape
    assert ls == 1, "sample group advances one token"
    xs = x_sample.reshape(bs, D_MODEL)
    both = jnp.concatenate([xs, state_shift[0]], axis=0)
    proj_s, xn_s = _inproj(both, wts["norm_mix"], wts["w_in"], 2 * bs, bs, bs)
    pos_s = PAST_LEN + jnp.arange(ls, dtype=F32)
    o_ret_s, ret_s = _retention_step(proj_s, state_ret[0], wts["ret_gn_w"], pos_s)
    o_rwkv_s, rwkv_s = _rwkv_step(proj_s, state_rwkv[0], wts)
    y_sample = _finish(xs, o_ret_s, o_rwkv_s, proj_s, wts).reshape(bs, ls, D_MODEL)

    return (y_prompt, y_sample, ret_p[None], rwkv_p[None], shift_p[None],
            ret_s[None], rwkv_s[None], xn_s)
```

```python
import functools
import math

import jax
import jax.numpy as jnp
from jax import lax
from jax.experimental import pallas as pl
from jax.experimental.pallas import tpu as pltpu

F32 = jnp.float32
BF16 = jnp.bfloat16

D_MODEL = 1024
LANES = 128
SUBLANES = 8
PAST_LEN = 16384
RET_HEADS = 8
RET_D = D_MODEL // RET_HEADS
RET_CHUNK = 128
ROPE_BASE = 10000.0
RET_GN_EPS = 1e-5
RWKV_HEAD = 64
RWKV_HEADS = D_MODEL // RWKV_HEAD
RWKV_PAIRS = RWKV_HEADS // 2
RWKV_CHUNK = 64
LORA_DECAY = 64
LORA_A = 64
LORA_G = 160
RWKV_GN_EPS = 64e-5
N_GROUPS = 4
EXPERTS_PER_GROUP = 8
N_EXPERTS = N_GROUPS * EXPERTS_PER_GROUP
D_EXPERT = 256
RMS_EPS = 1e-6

SRC_SHIFT = 4 * D_MODEL
SRC_WLO = SRC_SHIFT + 3 * D_MODEL
SRC_ALO = SRC_WLO + LORA_DECAY
SRC_GLO = SRC_ALO + LORA_A
SRC_GATE_A = SRC_GLO + LORA_G
COL_Q, COL_K, COL_V, COL_GSW = 0, 1024, 2048, 3072
COL_R, COL_KK, COL_VV = 4096, 5120, 6144
COL_GATE_A, COL_GATE_B = 7168, 8192
COL_LORA = 9216
LORA_W = 4 * LANES
N_PROJ = COL_LORA + LORA_W

VMEM_LIMIT = 48 * 1024 * 1024

INPROJ_TILE = 1024
INPROJ_SLABS = 4
INPROJ_ROWS = 128
RET_CHUNKS_PER_STEP = 4
RWKV_TILE = 256
RWKV_PAIRS_PER_STEP = 8
STEP_ROWS = 8
MERGE_TILE = 512
MERGE_ROWS = 128
MOE_TILE = 1024
MOE_BLOCK = 128
MOE_PERM_ROWS = 256
EXPERTS_PER_STEP = 4

_NN = (((1,), (0,)), ((), ()))
_NT = (((1,), (1,)), ((), ()))
_TN = (((0,), (0,)), ((), ()))


def _mm(a, b, dims=_NN):
    return lax.dot_general(a.astype(BF16), b.astype(BF16), dims,
                           preferred_element_type=F32)


def _params(sem):
    return pltpu.CompilerParams(dimension_semantics=sem,
                                vmem_limit_bytes=VMEM_LIMIT)


def _rms(x, gain):
    return x * lax.rsqrt(jnp.mean(x * x, axis=-1, keepdims=True) + RMS_EPS) * gain


def _split_bf16(x, terms):
    parts = []
    for _ in range(terms - 1):
        hi = x.astype(BF16)
        parts.append(hi)
        x = x - hi.astype(F32)
    parts.append(x.astype(BF16))
    return parts


def _lane_sum(x, ones):
    hi, lo = _split_bf16(x, 2)
    return (jnp.dot(hi, ones, preferred_element_type=F32)
            + jnp.dot(lo, ones, preferred_element_type=F32))


def _inproj_body(x_ref, g_ref, w_ref, o_ref, xn_ref, *, normed, keep):
    step = math.gcd(x_ref.shape[0], INPROJ_ROWS, normed)
    for r0 in range(0, x_ref.shape[0], step):
        x = x_ref[r0:r0 + step, :]
        if r0 < normed:
            x = _rms(x, g_ref[...])
            lo, hi = max(r0, normed - keep), r0 + step
            if lo < hi:
                xn_ref[0, 0, lo - (normed - keep):hi - (normed - keep), :] = x[lo - r0:, :]
        o_ref[r0:r0 + step, :] = jnp.dot(x.astype(BF16), w_ref[...], preferred_element_type=F32)


def _inproj(x, gain, w, tm, normed, keep):
    t = x.shape[0]
    n = w.shape[1]
    tn = n // INPROJ_SLABS
    assert keep <= normed <= tm
    proj, kept = pl.pallas_call(
        functools.partial(_inproj_body, normed=normed, keep=keep),
        grid=(n // tn, t // tm),
        in_specs=[pl.BlockSpec((tm, D_MODEL), lambda j, i: (i, 0)),
                  pl.BlockSpec((1, D_MODEL), lambda j, i: (0, 0)),
                  pl.BlockSpec((D_MODEL, tn), lambda j, i: (0, j), pipeline_mode=pl.Buffered(1))],
        out_specs=[pl.BlockSpec((tm, tn), lambda j, i: (i, j)),
                   pl.BlockSpec((1, 1, keep, D_MODEL), lambda j, i: (j, i, 0, 0))],
        out_shape=[jax.ShapeDtypeStruct((t, n), F32),
                   jax.ShapeDtypeStruct((n // tn, t // tm, keep, D_MODEL), F32)],
        compiler_params=_params(("arbitrary", "arbitrary")),
        name="inproj",
    )(x, gain, w)
    return proj, kept[0]


def _head_norm(x, eps):
    mu = jnp.mean(x, axis=-1, keepdims=True)
    xc = x - mu
    return xc * lax.rsqrt(jnp.mean(xc * xc, axis=-1, keepdims=True) + eps)


def _silu(x):
    return x * jax.nn.sigmoid(x)


def _rope(x, cos, sin_signed):
    return x * cos + pltpu.roll(x, RET_D // 2, 1) * sin_signed


def _ret_tables(chunk):
    log_gamma = jnp.log1p(-jnp.exp2(-5.0 - jnp.arange(RET_HEADS, dtype=F32)))
    idx = jnp.arange(chunk, dtype=F32)
    rel = idx[:, None] - idx[None, :]
    intra = jnp.where(rel[None] >= 0,
                      jnp.exp(jnp.maximum(rel, 0.0)[None] * log_gamma[:, None, None]), 0.0)
    q_dec = jnp.exp((idx[:, None] + 1.0) * log_gamma[None, :])
    k_dec = jnp.exp((chunk - 1.0 - idx)[None, :] * log_gamma[:, None]).T
    chunk_dec = jnp.exp(chunk * log_gamma)[None, :]
    widen = lambda t: jnp.repeat(t, RET_D, axis=1)
    return intra, widen(q_dec), widen(k_dec), widen(chunk_dec)


def _rope_tables(pos):
    half = RET_D // 2
    inv = ROPE_BASE ** (-jnp.arange(half, dtype=F32) / half)
    ang = pos[:, None] * inv[None, :]
    cos, sin = jnp.cos(ang), jnp.sin(ang)
    return jnp.concatenate([cos, cos], axis=1), jnp.concatenate([-sin, sin], axis=1)


def _ret_body(q_ref, k_ref, v_ref, g_ref, cos_ref, sin_ref, qd_ref, kd_ref, cd_ref,
              intra_ref, gnw_ref, o_ref, sout_ref, s_ref):
    c = pl.program_id(1)
    chunk = qd_ref.shape[0]

    @pl.when(c == 0)
    def _():
        s_ref[...] = jnp.zeros_like(s_ref)

    heads = [slice(h * RET_D, (h + 1) * RET_D) for h in range(RET_HEADS)]
    mean_w = jnp.full((RET_D, RET_D), 1.0 / RET_D, BF16)
    state = [s_ref[h] for h in range(RET_HEADS)]
    for r0 in range(0, q_ref.shape[0], chunk):
        rows = slice(r0, r0 + chunk)
        cos = cos_ref[rows, :]
        sin = sin_ref[rows, :]
        qr = [_rope(q_ref[rows, sl], cos, sin).astype(BF16) for sl in heads]
        kr = [_rope(k_ref[rows, sl], cos, sin) for sl in heads]
        vb = [v_ref[rows, sl].astype(BF16) for sl in heads]
        scores = [_mm(qr[h], kr[h], _NT) * intra_ref[h] for h in range(RET_HEADS)]
        cross = [_mm(qr[h], state[h]) * qd_ref[:, sl] for h, sl in enumerate(heads)]
        adds = [_mm(kr[h] * kd_ref[:, sl], vb[h], _TN) for h, sl in enumerate(heads)]
        out = [_mm(scores[h], vb[h]) + cross[h] for h in range(RET_HEADS)]
        cen = [out[h] - _lane_sum(out[h], mean_w) for h in range(RET_HEADS)]
        var = [_lane_sum(cen[h] * cen[h], mean_w) for h in range(RET_HEADS)]
        for h, sl in enumerate(heads):
            state[h] = state[h] * cd_ref[:, sl] + adds[h]
            o_ref[rows, sl] = (cen[h] * lax.rsqrt(var[h] + RET_GN_EPS) * gnw_ref[:, sl]
                               * _silu(g_ref[rows, sl]))
    for h in range(RET_HEADS):
        s_ref[h] = state[h]

    @pl.when(c == pl.num_programs(1) - 1)
    def _():
        sout_ref[0] = s_ref[...]


def _retention_prompt(proj, gn_w, batch, seq):
    chunk = math.gcd(seq, RET_CHUNK)
    tile = math.gcd(seq, RET_CHUNKS_PER_STEP * chunk)
    n_tiles = seq // tile
    intra, q_dec, k_dec, chunk_dec = _ret_tables(chunk)
    intra = intra * (RET_D ** -0.5)
    k_dec = k_dec * (RET_D ** -0.5)
    cos, sin = _rope_tables(jnp.arange(seq, dtype=F32))
    col = lambda j: pl.BlockSpec((tile, D_MODEL), lambda b, c, j=j: (b * n_tiles + c, j))
    const2 = lambda shape: pl.BlockSpec(shape, lambda b, c: (0, 0))
    state = pl.BlockSpec((1, RET_HEADS, RET_D, RET_D), lambda b, c: (b, 0, 0, 0))
    return pl.pallas_call(
        _ret_body,
        grid=(batch, n_tiles),
        in_specs=[col(COL_Q // D_MODEL), col(COL_K // D_MODEL), col(COL_V // D_MODEL),
                  col(COL_GSW // D_MODEL),
                  pl.BlockSpec((tile, RET_D), lambda b, c: (c, 0)),
                  pl.BlockSpec((tile, RET_D), lambda b, c: (c, 0)),
                  const2((chunk, D_MODEL)), const2((chunk, D_MODEL)), const2((1, D_MODEL)),
                  pl.BlockSpec((RET_HEADS, chunk, chunk), lambda b, c: (0, 0, 0)),
                  const2((1, D_MODEL))],
        out_specs=[pl.BlockSpec((tile, D_MODEL), lambda b, c: (b * n_tiles + c, 0)), state],
        out_shape=[jax.ShapeDtypeStruct((batch * seq, D_MODEL), F32),
                   jax.ShapeDtypeStruct((batch, RET_HEADS, RET_D, RET_D), F32)],
        scratch_shapes=[pltpu.VMEM((RET_HEADS, RET_D, RET_D), F32)],
        compiler_params=_params(("parallel", "arbitrary")),
        name="retention_prompt",
    )(proj, proj, proj, proj, cos, sin, q_dec, k_dec, chunk_dec, intra, gn_w)


def _ret_step_body(q_ref, k_ref, v_ref, g_ref, cos_ref, sin_ref, qd_ref, kd_ref, cd_ref,
                   gnw_ref, s_ref, o_ref, sout_ref, *, rows):
    cos = cos_ref[...]
    sin = sin_ref[...]
    row = lax.broadcasted_iota(jnp.int32, (rows, RET_D), 0)
    for h in range(RET_HEADS):
        sl = slice(h * RET_D, (h + 1) * RET_D)
        qr = _rope(q_ref[:, sl], cos, sin)
        kr = _rope(k_ref[:, sl], cos, sin) * (RET_D ** -0.5)
        v = v_ref[:, sl]
        kd = kr * kd_ref[:, sl]
        cd = cd_ref[:, sl]

        reads = [_mm(qr, s_ref[b, h]) for b in range(rows)]
        adds = [_mm(jnp.where(row == b, kd, 0.0), v, _TN) for b in range(rows)]
        cross = jnp.zeros((rows, RET_D), F32)
        for b in range(rows):
            sout_ref[b, h] = s_ref[b, h] * cd + adds[b]
            cross = jnp.where(row == b, reads[b], cross)
        out = jnp.sum(qr * kr, axis=-1, keepdims=True) * v + cross * qd_ref[:, sl]
        o_ref[:, sl] = _head_norm(out, RET_GN_EPS) * gnw_ref[:, sl] * _silu(g_ref[:, sl])


def _retention_step(proj, s0, gn_w, pos):
    batch = s0.shape[0]
    rows = STEP_ROWS
    _, q_dec, k_dec, chunk_dec = _ret_tables(1)
    cos, sin = _rope_tables(pos)
    col = lambda j: pl.BlockSpec((rows, D_MODEL), lambda i, j=j: (i, j))
    const2 = lambda shape: pl.BlockSpec(shape, lambda i: (0, 0))
    state = pl.BlockSpec((rows, RET_HEADS, RET_D, RET_D), lambda i: (i, 0, 0, 0))
    return pl.pallas_call(
        functools.partial(_ret_step_body, rows=rows),
        grid=(batch // rows,),
        in_specs=[col(COL_Q // D_MODEL), col(COL_K // D_MODEL), col(COL_V // D_MODEL),
                  col(COL_GSW // D_MODEL),
                  const2((1, RET_D)), const2((1, RET_D)),
                  const2((1, D_MODEL)), const2((1, D_MODEL)), const2((1, D_MODEL)),
                  const2((1, D_MODEL)), state],
        out_specs=[pl.BlockSpec((rows, D_MODEL), lambda i: (i, 0)), state],
        out_shape=[jax.ShapeDtypeStruct((batch, D_MODEL), F32),
                   jax.ShapeDtypeStruct(s0.shape, F32)],
        compiler_params=_params(("parallel",)),
        name="retention_step",
    )(proj, proj, proj, proj, cos, sin, q_dec, k_dec, chunk_dec, gn_w, s0)


def _head_ones(value=1.0):
    row = lax.broadcasted_iota(jnp.int32, (LANES, LANES), 0)
    col = lax.broadcasted_iota(jnp.int32, (LANES, LANES), 1)
    return jnp.where((row // RWKV_HEAD) == (col // RWKV_HEAD), value, 0.0).astype(BF16)


def _head_sum(x, ones_bd):
    blocks = [_lane_sum(x[:, j:j + LANES], ones_bd) for j in range(0, x.shape[1], LANES)]
    return blocks[0] if len(blocks) == 1 else jnp.concatenate(blocks, axis=1)


def _softplus(x):
    return jnp.maximum(x, 0.0) + jnp.log(1.0 + jnp.exp(-jnp.abs(x)))


def _lerp(cur, prev, mu):
    return cur + (prev - cur) * mu


def _rwkv_prep(r_, k_, v_, lora, w0, w2, a0, a2, g2, k_k, k_a, r_k, ones_bd):
    w_lo, a_lo, g_lo = lora[:, :LANES], lora[:, LANES:2 * LANES], lora[:, 2 * LANES:LORA_W]
    w_log = -_softplus(-(w0 + _mm(jnp.tanh(w_lo), w2))) - 0.5
    log_decay = -jnp.exp(w_log)
    a_sig = jax.nn.sigmoid(a0 + _mm(a_lo, a2))
    g = _mm(jax.nn.sigmoid(g_lo), g2)
    kk = k_ * k_k
    kk = kk * lax.rsqrt(jnp.maximum(_head_sum(kk * kk, ones_bd), 1e-24))
    kmod = k_ * (1.0 + (a_sig - 1.0) * k_a)
    bonus = _head_sum(r_ * kmod * r_k, ones_bd) * v_
    return log_decay, a_sig, g, kk, kmod, bonus


def _rwkv_out(y, bonus, g, lnx_w, lnx_b):
    mean_bd = _head_ones(1.0 / RWKV_HEAD)
    yc = y - _head_sum(y, mean_bd)
    var = _head_sum(yc * yc, mean_bd)
    return (yc * lax.rsqrt(var + RWKV_GN_EPS) * lnx_w + lnx_b + bonus) * g


def _block_diag(x, lane_first):
    return jnp.concatenate([jnp.where(lane_first, x, 0.0), jnp.where(lane_first, 0.0, x)], axis=0)


def _pair_masks():
    tok = lax.broadcasted_iota(jnp.int32, (RWKV_CHUNK, LANES), 0)
    src = lax.broadcasted_iota(jnp.int32, (RWKV_CHUNK, LANES), 1) % RWKV_CHUNK
    n = 2 * RWKV_CHUNK
    eye = lax.broadcasted_iota(jnp.int32, (n, n), 0) == lax.broadcasted_iota(jnp.int32, (n, n), 1)
    lane_first = lax.broadcasted_iota(jnp.int32, (1, LANES), 1) < RWKV_HEAD
    return lane_first, src < tok, src <= tok, src == tok, eye


def _rwkv_chunks(chunks, masks):
    lane_first, strict, incl, eye_wide, eye = masks
    n = RWKV_CHUNK
    rows = lambda x, y: jnp.concatenate([x, y], axis=0)
    cols = lambda x, y: jnp.concatenate([x, y], axis=1)
    bd = lambda x: _block_diag(x, lane_first)

    ops = []
    for r, lw, k, v, a, b, cum in chunks:
        p_inc = jnp.exp(cum)
        p_inv = jnp.exp(-cum)
        p_exc = jnp.exp(cum - lw)
        p_end = p_inc[RWKV_CHUNK - 1:RWKV_CHUNK, :]
        b_t = b * p_inv
        k_t = k * p_inv
        ops.append(dict(a_t=(a * p_exc).astype(BF16), r_t=(r * p_inc).astype(BF16),
                        b_t=bd(b_t.astype(BF16)), k_t=bd(k_t.astype(BF16)),
                        b_hat=bd(b_t * p_end), k_hat=bd(k_t * p_end),
                        v_bd=bd(v.astype(BF16)), p_end=p_end))

    for o in ops:
        prod = _mm(rows(o["a_t"], o["r_t"]), rows(o["b_t"], o["k_t"]), _NT)
        o["a_ab"] = jnp.where(strict, prod[:n, :LANES], 0.0)
        o["a_ak"] = jnp.where(strict, prod[:n, LANES:], 0.0).astype(BF16)
        o["r_bk"] = cols(jnp.where(incl, prod[n:, :LANES], 0.0),
                         jnp.where(incl, prod[n:, LANES:], 0.0)).astype(BF16)

    for o in ops:
        a_ab = o["a_ab"].astype(BF16)
        o["t_inv"] = jnp.where(eye_wide, 1.0, 0.0) + o["a_ab"]
        o["power"] = _mm(a_ab, bd(a_ab)).astype(BF16)
        o["av"] = _mm(o["a_ak"], o["v_bd"]).astype(BF16)
    for _ in range(int(math.log2(RWKV_CHUNK)) - 2):
        for o in ops:
            both = _mm(rows(o["power"], o["t_inv"].astype(BF16)), bd(o["power"]))
            o["power"] = both[:n].astype(BF16)
            o["t_inv"] = o["t_inv"] + both[n:]
    for o in ops:
        o["t_inv"] = o["t_inv"] + _mm(o["t_inv"], bd(o["power"]))
    out = []
    for o in ops:
        wu = _mm(o["t_inv"], cols(bd(o["a_t"]), bd(o["av"])))
        hat_t = rows(o["b_hat"], o["k_hat"]).T
        p_col = jnp.sum(jnp.where(eye, o["p_end"], 0.0), axis=1, keepdims=True)
        out.append((rows(wu[:, :LANES].astype(BF16), o["r_t"]), wu[:, LANES:],
                    o["r_bk"], o["v_bd"], hat_t.astype(BF16), p_col))
    return out


def _shift_rows(cur, first_row):
    rolled = pltpu.roll(cur, 1, 0)
    row = lax.broadcasted_iota(jnp.int32, cur.shape, 0)
    return jnp.where(row == 0, first_row, rolled)


def _rwkv_body(r_ref, k_ref, v_ref, lora_ref, mu_ref, mul_ref,
               w0_ref, w2_ref, a0_ref, a2_ref, g2_ref, kk_ref, ka_ref, rk_ref,
               lnw_ref, lnb_ref, o_ref, sout_ref,
               s_ref, carry_ref, carryl_ref, *, tile):
    t = pl.program_id(1)
    p = pl.program_id(2)
    masks = _pair_masks()
    ones_bd = _head_ones()
    pairs = r_ref.shape[1] // LANES

    @pl.when(t == 0)
    def _():
        for q in range(pairs):
            s_ref[p * pairs + q] = jnp.zeros((LANES, LANES), F32)
        carry_ref[p] = jnp.zeros(carry_ref.shape[1:], F32)

        @pl.when(p == 0)
        def _():
            carryl_ref[...] = jnp.zeros_like(carryl_ref)

    prev = carry_ref[p]
    cur_r, cur_k, cur_v, cur_l = r_ref[...], k_ref[...], v_ref[...], lora_ref[...]
    mu = mu_ref[0]
    r_ = _lerp(cur_r, _shift_rows(cur_r, prev[0:1]), mu[0:1])
    k_ = _lerp(cur_k, _shift_rows(cur_k, prev[1:2]), mu[1:2])
    v_ = _lerp(cur_v, _shift_rows(cur_v, prev[2:3]), mu[2:3])
    lo = _lerp(cur_l, _shift_rows(cur_l, carryl_ref[0:1, :]), mul_ref[...])
    carry_ref[p, 0:1, :] = cur_r[tile - 1:tile]
    carry_ref[p, 1:2, :] = cur_k[tile - 1:tile]
    carry_ref[p, 2:3, :] = cur_v[tile - 1:tile]

    @pl.when(p == pl.num_programs(2) - 1)
    def _():
        carryl_ref[0:1, :] = cur_l[tile - 1:tile]

    log_decay, a_sig, g, kk, kmod, bonus = _rwkv_prep(
        r_, k_, v_, lo, w0_ref[...], w2_ref[...], a0_ref[...], a2_ref[...], g2_ref[...],
        kk_ref[...], ka_ref[...], rk_ref[...], ones_bd)

    row = lax.broadcasted_iota(jnp.int32, (tile, tile), 0)
    col = lax.broadcasted_iota(jnp.int32, (tile, tile), 1)
    tri = jnp.where(((row // RWKV_CHUNK) == (col // RWKV_CHUNK)) & (col <= row), 1.0, 0.0).astype(BF16)
    cum = sum(jnp.dot(tri, part, preferred_element_type=F32) for part in _split_bf16(log_decay, 3))

    neg_kk = -kk
    kk_a = kk * a_sig
    n_chunks = tile // RWKV_CHUNK
    chunks = []
    for c in range(n_chunks):
        for q in range(pairs):
            at = (slice(c * RWKV_CHUNK, (c + 1) * RWKV_CHUNK), slice(q * LANES, (q + 1) * LANES))
            chunks.append((r_[at], log_decay[at], kmod[at], v_[at], neg_kk[at], kk_a[at], cum[at]))
    parts = _rwkv_chunks(chunks, masks)

    n = RWKV_CHUNK
    lane_first = masks[0]
    states = [s_ref[p * pairs + q] for q in range(pairs)]
    ys = [[] for _ in range(pairs)]
    for c in range(n_chunks):
        reads = [_mm(parts[c * pairs + q][0], states[q]) for q in range(pairs)]
        for q in range(pairs):
            _, u0, r_bk, v_bd, hat_t, p_col = parts[c * pairs + q]
            u_bd = _block_diag(reads[q][:n] + u0, lane_first).astype(BF16)
            uv = jnp.concatenate([u_bd, v_bd], axis=0)
            ys[q].append(reads[q][n:] + _mm(r_bk, uv))
            states[q] = states[q] * p_col + _mm(hat_t, uv)
    for q in range(pairs):
        s_ref[p * pairs + q] = states[q]

    @pl.when(t == pl.num_programs(1) - 1)
    def _():
        for q in range(pairs):
            head = 2 * (p * pairs + q)
            s_bd = states[q].T
            sout_ref[0, head] = s_bd[:RWKV_HEAD, :RWKV_HEAD]
            sout_ref[0, head + 1] = s_bd[RWKV_HEAD:, RWKV_HEAD:]

    y = jnp.concatenate([jnp.concatenate(yq, axis=0) for yq in ys], axis=1)
    o_ref[...] = _rwkv_out(y, bonus, g, lnw_ref[...], lnb_ref[...])


def _rwkv_prompt(proj, wts, batch, seq):
    tile = math.gcd(seq, RWKV_TILE)
    n_tiles = seq // tile
    width = RWKV_PAIRS_PER_STEP * LANES
    blk = lambda base: pl.BlockSpec((tile, width),
                                    lambda b, t, p, base=base: (b * n_tiles + t, base // width + p))
    per_pair = lambda rows: pl.BlockSpec((rows, width), lambda b, t, p: (0, p))
    in_specs = [
        blk(COL_R), blk(COL_KK), blk(COL_VV),
        pl.BlockSpec((tile, LORA_W), lambda b, t, p: (b * n_tiles + t, COL_LORA // LORA_W)),
        pl.BlockSpec((1, 3, width), lambda b, t, p: (0, 0, p)),
        pl.BlockSpec((1, LORA_W), lambda b, t, p: (0, 0)),
        per_pair(1), per_pair(LANES),
        per_pair(1), per_pair(LANES),
        per_pair(2 * LANES),
        per_pair(1), per_pair(1), per_pair(1),
        per_pair(1), per_pair(1),
    ]
    return pl.pallas_call(
        functools.partial(_rwkv_body, tile=tile),
        grid=(batch, n_tiles, RWKV_PAIRS // RWKV_PAIRS_PER_STEP),
        in_specs=in_specs,
        out_specs=[pl.BlockSpec((tile, width), lambda b, t, p: (b * n_tiles + t, p)),
                   pl.BlockSpec((1, RWKV_HEADS, RWKV_HEAD, RWKV_HEAD), lambda b, t, p: (b, 0, 0, 0))],
        out_shape=[jax.ShapeDtypeStruct((batch * seq, D_MODEL), F32),
                   jax.ShapeDtypeStruct((batch, RWKV_HEADS, RWKV_HEAD, RWKV_HEAD), F32)],
        scratch_shapes=[pltpu.VMEM((RWKV_PAIRS, LANES, LANES), F32),
                        pltpu.VMEM((RWKV_PAIRS // RWKV_PAIRS_PER_STEP, SUBLANES, width), F32),
                        pltpu.VMEM((SUBLANES, LORA_W), F32)],
        compiler_params=_params(("parallel", "arbitrary", "arbitrary")),
        name="rwkv_prompt",
    )(proj, proj, proj, proj, wts["mu3"], wts["mu_lora"],
      wts["w0"], wts["w2"], wts["a0"], wts["a2"], wts["g2"],
      wts["k_k"], wts["k_a"], wts["r_k"], wts["lnx_w"], wts["lnx_b"])


def _rwkv_step_body(r_ref, k_ref, v_ref, lora_ref, pr_ref, pk_ref, pv_ref, plora_ref,
                    mu_ref, mul_ref, w0_ref, w2_ref, a0_ref, a2_ref, g2_ref,
                    kk_ref, ka_ref, rk_ref, lnw_ref, lnb_ref, s_ref,
                    o_ref, sout_ref, *, rows):
    ones_bd = _head_ones()
    mu = mu_ref[0]
    r_ = _lerp(r_ref[...], pr_ref[...], mu[0:1])
    k_ = _lerp(k_ref[...], pk_ref[...], mu[1:2])
    v_ = _lerp(v_ref[...], pv_ref[...], mu[2:3])
    lo = _lerp(lora_ref[...], plora_ref[...], mul_ref[...])
    log_decay, a_sig, g, kk, kmod, bonus = _rwkv_prep(
        r_, k_, v_, lo, w0_ref[...], w2_ref[...], a0_ref[...], a2_ref[...], g2_ref[...],
        kk_ref[...], ka_ref[...], rk_ref[...], ones_bd)
    decay = jnp.exp(log_decay)
    neg_kk = -kk
    kk_a = kk * a_sig
    decay_r = decay * r_
    row = lax.broadcasted_iota(jnp.int32, (rows, RWKV_HEAD), 0)

    heads = [slice(h * RWKV_HEAD, (h + 1) * RWKV_HEAD) for h in range(RWKV_HEADS)]
    reads = []
    for h, sl in enumerate(heads):
        lhs = jnp.concatenate([neg_kk[:, sl], decay_r[:, sl]], axis=0)
        reads.append([_mm(lhs, s_ref[b, h], _NT) for b in range(rows)])
    ys = []
    for h, sl in enumerate(heads):
        w_h, r_h, v_h, k_h, b_h = decay[:, sl], r_[:, sl], v_[:, sl], kmod[:, sl], kk_a[:, sl]
        rhs = jnp.concatenate([b_h, k_h], axis=0)
        sa_all = swr_all = jnp.zeros((rows, RWKV_HEAD), F32)
        adds = []
        for b in range(rows):
            mine = row == b
            sa, swr = reads[h][b][:rows], reads[h][b][rows:]
            left = jnp.concatenate([jnp.where(mine, sa, 0.0), jnp.where(mine, v_h, 0.0)], axis=0)
            adds.append(_mm(left, rhs, _TN))
            sa_all = jnp.where(mine, sa, sa_all)
            swr_all = jnp.where(mine, swr, swr_all)
        for b in range(rows):
            sout_ref[b, h] = s_ref[b, h] * w_h[b:b + 1] + adds[b]
        ys.append(swr_all + sa_all * jnp.sum(b_h * r_h, axis=-1, keepdims=True)
                  + v_h * jnp.sum(k_h * r_h, axis=-1, keepdims=True))
    y = jnp.concatenate(ys, axis=1)
    o_ref[...] = _rwkv_out(y, bonus, g, lnw_ref[...], lnb_ref[...])


def _rwkv_step(proj, s0, wts):
    batch = s0.shape[0]
    rows = STEP_ROWS
    shift = batch // rows
    wide = lambda base, off=0: pl.BlockSpec((rows, D_MODEL),
                                            lambda i, base=base, off=off: (i + off, base // D_MODEL))
    lora = pl.BlockSpec((rows, LORA_W), lambda i: (i, COL_LORA // LORA_W))
    lora_prev = pl.BlockSpec((rows, LORA_W), lambda i: (i + shift, COL_LORA // LORA_W))
    full = lambda shape: pl.BlockSpec(shape, lambda i: (0,) * len(shape))
    state = pl.BlockSpec((rows, RWKV_HEADS, RWKV_HEAD, RWKV_HEAD), lambda i: (i, 0, 0, 0))
    vec = full((1, D_MODEL))
    return pl.pallas_call(
        functools.partial(_rwkv_step_body, rows=rows),
        grid=(batch // rows,),
        in_specs=[wide(COL_R), wide(COL_KK), wide(COL_VV), lora,
                  wide(COL_R, shift), wide(COL_KK, shift), wide(COL_VV, shift), lora_prev,
                  full((1, 3, D_MODEL)), full((1, LORA_W)),
                  vec, full((LANES, D_MODEL)), vec, full((LANES, D_MODEL)),
                  full((2 * LANES, D_MODEL)), vec, vec, vec, vec, vec, state],
        out_specs=[pl.BlockSpec((rows, D_MODEL), lambda i: (i, 0)), state],
        out_shape=[jax.ShapeDtypeStruct((batch, D_MODEL), F32),
                   jax.ShapeDtypeStruct(s0.shape, F32)],
        compiler_params=_params(("parallel",)),
        name="rwkv_step",
    )(proj, proj, proj, proj, proj, proj, proj, proj,
      wts["mu3"], wts["mu_lora"], wts["w0"], wts["w2"], wts["a0"], wts["a2"], wts["g2"],
      wts["k_k"], wts["k_a"], wts["r_k"], wts["lnx_w"], wts["lnx_b"], s0)


def _merge_body(x_ref, oret_ref, orwkv_ref, ga_ref, gb_ref, wout_ref, nffn_ref, rw_ref, rb_ref,
                x1_ref, xn_ref, comb_ref, cnt_ref):
    step = math.gcd(x_ref.shape[0], MERGE_ROWS)
    chunks = [slice(r0, r0 + step) for r0 in range(0, x_ref.shape[0], step)]
    each = lambda fn, *cols: [fn(*args) for args in zip(*cols)]
    row_max = lambda v: jnp.max(v, axis=-1, keepdims=True)
    row_min = lambda v: jnp.min(v, axis=-1, keepdims=True)
    row_sum = lambda v: jnp.sum(v, axis=-1, keepdims=True)
    lane = lax.broadcasted_iota(jnp.int32, (step, LANES), 1)
    is_group = lane < N_GROUPS
    neg_inf = -jnp.inf

    merged = [jax.nn.sigmoid(ga_ref[r, :]) * oret_ref[r, :]
              + jax.nn.sigmoid(gb_ref[r, :]) * orwkv_ref[r, :] for r in chunks]
    x1 = [x_ref[r, :] + _mm(m, wout_ref[...]) for r, m in zip(chunks, merged)]
    xn = each(lambda v: _rms(v, nffn_ref[...]), x1)
    for r, a, b in zip(chunks, x1, xn):
        x1_ref[r, :] = a
        xn_ref[r, :] = b.astype(BF16)

    logits = each(lambda v: _mm(v, rw_ref[...]) + rb_ref[...], xn)
    g_max = each(lambda l: row_max(jnp.where(is_group, l, neg_inf)), logits)
    g_sel = each(lambda l, m: row_min(jnp.where(is_group & (l == m), lane, LANES)), logits, g_max)
    g_w = each(lambda l, m: 1.0 / row_sum(jnp.where(is_group, jnp.exp(l - m), 0.0)), logits, g_max)
    first = each(lambda g: N_GROUPS + EXPERTS_PER_GROUP * g, g_sel)
    in_group = each(lambda f: (lane >= f) & (lane < f + EXPERTS_PER_GROUP), first)
    e_max = each(lambda l, ing: row_max(jnp.where(ing, l, neg_inf)), logits, in_group)
    e_exp = each(lambda l, ing, m: jnp.where(ing, jnp.exp(l - m), 0.0), logits, in_group, e_max)
    prob = each(lambda e, ing: jnp.where(ing, e / row_sum(e), -1.0), e_exp, in_group)
    p1 = each(row_max, prob)
    i1 = each(lambda p, m: row_min(jnp.where(p == m, lane, LANES)), prob, p1)
    rest = each(lambda p, i: jnp.where(lane == i, -1.0, p), prob, i1)
    p2 = each(row_max, rest)
    i2 = each(lambda p, m: row_min(jnp.where(p == m, lane, LANES)), rest, p2)
    lane8 = lax.broadcasted_iota(jnp.int32, (SUBLANES, LANES), 1)
    sizes = jnp.zeros((SUBLANES, LANES), jnp.int32)
    for c, r in enumerate(chunks):
        denom = p1[c] + p2[c]
        comb = (jnp.where(lane == i1[c] - N_GROUPS, g_w[c] * p1[c] / denom, 0.0)
                + jnp.where(lane == i2[c] - N_GROUPS, g_w[c] * p2[c] / denom, 0.0))
        comb_ref[r, :] = jnp.where(lane == N_EXPERTS, g_sel[c].astype(F32), comb)
        for g in range(N_GROUPS):
            n_g = jnp.sum(jnp.where(g_sel[c] == g, 1, 0), axis=0, keepdims=True)
            sizes = sizes + jnp.where(lane8 == g, n_g, 0)
    cnt_ref[0] = sizes


def _merge(x, o_ret, o_rwkv, proj, wts):
    t = x.shape[0]
    tm = math.gcd(t, MERGE_TILE)
    row = lambda: pl.BlockSpec((tm, D_MODEL), lambda i: (i, 0))
    full = lambda shape: pl.BlockSpec(shape, lambda i: (0, 0))
    return pl.pallas_call(
        _merge_body,
        grid=(t // tm,),
        in_specs=[row(), row(), row(),
                  pl.BlockSpec((tm, D_MODEL), lambda i: (i, COL_GATE_A // D_MODEL)),
                  pl.BlockSpec((tm, D_MODEL), lambda i: (i, COL_GATE_B // D_MODEL)),
                  full((D_MODEL, D_MODEL)), full((1, D_MODEL)),
                  full((D_MODEL, LANES)), full((1, LANES))],
        out_specs=[row(), row(), pl.BlockSpec((tm, LANES), lambda i: (i, 0)),
                   pl.BlockSpec((1, SUBLANES, LANES), lambda i: (i, 0, 0))],
        out_shape=[jax.ShapeDtypeStruct((t, D_MODEL), F32),
                   jax.ShapeDtypeStruct((t, D_MODEL), BF16),
                   jax.ShapeDtypeStruct((t, LANES), F32),
                   jax.ShapeDtypeStruct((t // tm, SUBLANES, LANES), jnp.int32)],
        compiler_params=_params(("parallel",)),
        name="merge_router",
    )(x, o_ret, o_rwkv, proj, proj, wts["w_out"], wts["norm_ffn"], wts["router_w"], wts["router_b"])


def _moe_body(cnt_ref, xn_ref, x1_ref, comb_ref, wg_ref, wu_ref, wd_ref, nf_ref, o_ref,
              acc_ref, xs_ref, cs_ref, pt_ref, *, ratio):
    i = pl.program_id(0)
    s = pl.program_id(1)
    tm = xn_ref.shape[0]
    group = s // (EXPERTS_PER_GROUP // EXPERTS_PER_STEP)

    sizes = []
    for g in range(N_GROUPS):
        n = cnt_ref[i * ratio * N_GROUPS + g]
        for k in range(1, ratio):
            n = n + cnt_ref[(i * ratio + k) * N_GROUPS + g]
        sizes.append(n)
    starts = [jnp.int32(0)]
    for g in range(N_GROUPS - 1):
        starts.append(starts[-1] + sizes[g])

    @pl.when(s == 0)
    def _():
        comb = comb_ref[...]
        g_row = comb.T[N_EXPERTS:N_EXPERTS + 1, :]
        gid = lax.broadcasted_iota(jnp.int32, (SUBLANES, tm), 0)
        onehot = jnp.where(g_row.astype(jnp.int32) == gid, 1.0, 0.0)
        step = math.gcd(tm, MOE_PERM_ROWS)
        row = lax.broadcasted_iota(jnp.int32, (step, tm), 0)
        col = lax.broadcasted_iota(jnp.int32, (step, tm), 1)
        rank = jnp.zeros((SUBLANES, tm), F32)
        for r0 in range(0, tm, step):
            earlier = jnp.where(row + r0 < col, 1.0, 0.0).astype(BF16)
            rank = rank + jnp.dot(onehot[:, r0:r0 + step].astype(BF16), earlier,
                                  preferred_element_type=F32)
        start_col = jnp.zeros((SUBLANES, 1), F32)
        gid_col = lax.broadcasted_iota(jnp.int32, (SUBLANES, 1), 0)
        for g in range(1, N_GROUPS):
            start_col = jnp.where(gid_col == g, starts[g].astype(F32), start_col)
        pos_row = jnp.sum(onehot * (rank + start_col), axis=0, keepdims=True)
        pos_row_i = pos_row.astype(jnp.int32)
        xn = xn_ref[...]
        comb_parts = _split_bf16(comb, 3)
        for r0 in range(0, tm, step):
            perm = jnp.where(row + r0 == pos_row_i, 1.0, 0.0).astype(BF16)
            xs_ref[r0:r0 + step, :] = jnp.dot(perm, xn, preferred_element_type=F32).astype(BF16)
            cs_ref[r0:r0 + step, :] = sum(jnp.dot(perm, part, preferred_element_type=F32)
                                          for part in comb_parts)
        pos_col_i = jnp.broadcast_to(pos_row, (LANES, tm)).T.astype(jnp.int32)
        lane = lax.broadcasted_iota(jnp.int32, (tm, LANES), 1)
        for j in range(tm // LANES):
            pt_ref[:, j * LANES:(j + 1) * LANES] = jnp.where(
                lane + j * LANES == pos_col_i, 1.0, 0.0).astype(BF16)
        acc_ref[...] = jnp.zeros_like(acc_ref)

    start, size = starts[N_GROUPS - 1], sizes[N_GROUPS - 1]
    for g in range(N_GROUPS - 2, -1, -1):
        start = jnp.where(group == g, starts[g], start)
        size = jnp.where(group == g, sizes[g], size)
    blk = math.gcd(tm, MOE_BLOCK)
    first = start // blk
    last = jnp.where(size > 0, (start + size + blk - 1) // blk, first)
    lane = lax.broadcasted_iota(jnp.int32, (blk, LANES), 1)

    def run_blocks(js):
        at = [pl.ds(pl.multiple_of(j * blk, blk), blk) for j in js]
        xb = [xs_ref[r, :] for r in at]
        cw = [cs_ref[r, :] for r in at]
        hs = [[] for _ in js]
        for k in range(EXPERTS_PER_STEP):
            for i in range(len(js)):
                weight = jnp.sum(jnp.where(lane == s * EXPERTS_PER_STEP + k, cw[i], 0.0),
                                 axis=-1, keepdims=True)
                hs[i].append(_silu(_mm(xb[i], wg_ref[k])) * _mm(xb[i], wu_ref[k]) * weight)
        for i, r in enumerate(at):
            total = _mm(hs[i][0], wd_ref[0])
            for k in range(1, EXPERTS_PER_STEP):
                total = total + _mm(hs[i][k], wd_ref[k])
            acc_ref[r, :] += total

    pairs = (last - first) // 2

    def two_blocks(i, carry):
        run_blocks([first + 2 * i, first + 2 * i + 1])
        return carry

    lax.fori_loop(0, pairs, two_blocks, 0)

    @pl.when(first + 2 * pairs < last)
    def _():
        run_blocks([last - 1])

    @pl.when(s == pl.num_programs(1) - 1)
    def _():
        parts = _split_bf16(acc_ref[...], 2)
        step = math.gcd(tm, MOE_PERM_ROWS)
        for r0 in range(0, tm, step):
            moe = sum(jnp.dot(pt_ref[r0:r0 + step, :], part, preferred_element_type=F32)
                      for part in parts)
            o_ref[r0:r0 + step, :] = _rms(x1_ref[r0:r0 + step, :] + moe, nf_ref[...])


def _moe(xn, x1, comb, counts, wts):
    t = xn.shape[0]
    tm = math.gcd(t, MOE_TILE)
    assert tm % LANES == 0, "token count must be a multiple of the lane width"
    ratio = tm // math.gcd(t, MERGE_TILE)
    row = lambda w: pl.BlockSpec((tm, w), lambda i, s, cnt: (i, 0))
    once = lambda w: pl.BlockSpec((tm, w), lambda i, s, cnt: (i, 0), pipeline_mode=pl.Buffered(1))
    expert = lambda a, b: pl.BlockSpec((EXPERTS_PER_STEP, a, b), lambda i, s, cnt: (s, 0, 0))
    grid_spec = pltpu.PrefetchScalarGridSpec(
        num_scalar_prefetch=1,
        grid=(t // tm, N_EXPERTS // EXPERTS_PER_STEP),
        in_specs=[row(D_MODEL), once(D_MODEL), row(LANES),
                  expert(D_MODEL, D_EXPERT), expert(D_MODEL, D_EXPERT), expert(D_EXPERT, D_MODEL),
                  pl.BlockSpec((1, D_MODEL), lambda i, s, cnt: (0, 0))],
        out_specs=row(D_MODEL),
        scratch_shapes=[pltpu.VMEM((tm, D_MODEL), F32),
                        pltpu.VMEM((tm, D_MODEL), BF16),
                        pltpu.VMEM((tm, LANES), F32),
                        pltpu.VMEM((tm, tm), BF16)])
    return pl.pallas_call(
        functools.partial(_moe_body, ratio=ratio),
        grid_spec=grid_spec,
        out_shape=jax.ShapeDtypeStruct((t, D_MODEL), F32),
        compiler_params=_params(("parallel", "arbitrary")),
        name="moe",
    )(counts, xn, x1, comb, wts["w_gate"], wts["w_up"], wts["w_down"], wts["norm_final"])


def _pad_to(x, size, axis):
    pad = [(0, 0)] * x.ndim
    pad[axis] = (0, size - x.shape[axis])
    return jnp.pad(x, pad)


def _prepare_weights(norm_mix, w_in, mu_shift, ret_gn_w, rwkv_w0, rwkv_w2, rwkv_a0, rwkv_a2,
                     rwkv_g2, rwkv_k_k, rwkv_k_a, rwkv_r_k, rwkv_lnx_w, rwkv_lnx_b, w_out,
                     norm_ffn, router_group_w, router_group_b, router_expert_w, router_expert_b,
                     expert_w_gate, expert_w_up, expert_w_down, norm_final):
    def lora_cols(t):
        return jnp.concatenate([_pad_to(t[..., SRC_WLO:SRC_ALO], LANES, -1),
                                _pad_to(t[..., SRC_ALO:SRC_GLO], LANES, -1),
                                _pad_to(t[..., SRC_GLO:SRC_GATE_A], 2 * LANES, -1)], axis=-1)

    w = w_in[0]
    w_in_p = jnp.concatenate([w[:, :SRC_WLO], w[:, SRC_GATE_A:], lora_cols(w)], axis=1).astype(BF16)
    mu = jnp.concatenate([jnp.zeros((SRC_SHIFT,), F32), mu_shift[0]])
    row = lambda v: v.reshape(1, -1)
    router_w = _pad_to(jnp.concatenate([router_group_w[0], router_expert_w[0]], axis=1), LANES, 1)
    router_b = _pad_to(jnp.concatenate([router_group_b[0], router_expert_b[0]]), LANES, 0)
    return dict(
        w_in=w_in_p, norm_mix=row(norm_mix[0]), ret_gn_w=row(ret_gn_w[0]),
        mu3=mu[SRC_SHIFT:SRC_WLO].reshape(1, 3, D_MODEL), mu_lora=lora_cols(mu).reshape(1, LORA_W),
        w0=row(rwkv_w0[0]), w2=_pad_to(rwkv_w2[0], LANES, 0),
        a0=row(rwkv_a0[0]), a2=_pad_to(rwkv_a2[0], LANES, 0),
        g2=_pad_to(rwkv_g2[0], 2 * LANES, 0),
        k_k=row(rwkv_k_k[0]), k_a=row(rwkv_k_a[0]), r_k=row(rwkv_r_k[0]),
        lnx_w=row(rwkv_lnx_w[0]), lnx_b=row(rwkv_lnx_b[0]),
        w_out=w_out[0].astype(BF16), norm_ffn=row(norm_ffn[0]),
        router_w=router_w, router_b=row(router_b),
        w_gate=expert_w_gate[0].astype(BF16), w_up=expert_w_up[0].astype(BF16),
        w_down=expert_w_down[0].astype(BF16), norm_final=row(norm_final))


def _finish(x, o_ret, o_rwkv, proj, wts):
    x1, xn2, comb, sizes = _merge(x, o_ret, o_rwkv, proj, wts)
    counts = sizes[:, 0, :N_GROUPS].reshape(-1)
    return _moe(xn2, x1, comb, counts, wts)


def kernel(x_prompt, x_sample, state_ret, state_rwkv, state_shift, norm_mix, w_in, mu_shift, ret_gn_w, rwkv_w0, rwkv_w2, rwkv_a0, rwkv_a2, rwkv_g2, rwkv_k_k, rwkv_k_a, rwkv_r_k, rwkv_lnx_w, rwkv_lnx_b, w_out, norm_ffn, router_group_w, router_group_b, router_expert_w, router_expert_b, expert_w_gate, expert_w_up, expert_w_down, norm_final):
    assert norm_mix.shape[0] == 1, "single-layer step"
    wts = _prepare_weights(norm_mix, w_in, mu_shift, ret_gn_w, rwkv_w0, rwkv_w2, rwkv_a0, rwkv_a2,
                           rwkv_g2, rwkv_k_k, rwkv_k_a, rwkv_r_k, rwkv_lnx_w, rwkv_lnx_b, w_out,
                           norm_ffn, router_group_w, router_group_b, router_expert_w,
                           router_expert_b, expert_w_gate, expert_w_up, expert_w_down, norm_final)

    bp, lp, _ = x_prompt.shape
    xp = x_prompt.reshape(bp * lp, D_MODEL)
    tile_p = math.gcd(lp, INPROJ_TILE)
    proj_p, tail_p = _inproj(xp, wts["norm_mix"], wts["w_in"], tile_p, tile_p, SUBLANES)
    o_ret_p, ret_p = _retention_prompt(proj_p, wts["ret_gn_w"], bp, lp)
    o_rwkv_p, rwkv_p = _rwkv_prompt(proj_p, wts, bp, lp)
    y_prompt = _finish(xp, o_ret_p, o_rwkv_p, proj_p, wts).reshape(bp, lp, D_MODEL)
    shift_p = tail_p.reshape(bp, lp // tile_p, SUBLANES, D_MODEL)[:, -1, -1]

    bs, ls, _ = x_sample.shape
    assert ls == 1, "sample group advances one token"
    xs = x_sample.reshape(bs, D_MODEL)
    both = jnp.concatenate([xs, state_shift[0]], axis=0)
    proj_s, xn_s = _inproj(both, wts["norm_mix"], wts["w_in"], 2 * bs, bs, bs)
    pos_s = PAST_LEN + jnp.arange(ls, dtype=F32)
    o_ret_s, ret_s = _retention_step(proj_s, state_ret[0], wts["ret_gn_w"], pos_s)
    o_rwkv_s, rwkv_s = _rwkv_step(proj_s, state_rwkv[0], wts)
    y_sample = _finish(xs, o_ret_s, o_rwkv_s, proj_s, wts).reshape(bs, ls, D_MODEL)

    return (y_prompt, y_sample, ret_p[None], rwkv_p[None], shift_p[None],
            ret_s[None], rwkv_s[None], xn_s)
```

```python
import functools
import math

import jax
import jax.numpy as jnp
from jax import lax
from jax.experimental import pallas as pl
from jax.experimental.pallas import tpu as pltpu

F32 = jnp.float32
BF16 = jnp.bfloat16

D_MODEL = 1024
LANES = 128
SUBLANES = 8
PAST_LEN = 16384
RET_HEADS = 8
RET_D = D_MODEL // RET_HEADS
RET_CHUNK = 128
ROPE_BASE = 10000.0
RET_GN_EPS = 1e-5
RWKV_HEAD = 64
RWKV_HEADS = D_MODEL // RWKV_HEAD
RWKV_PAIRS = RWKV_HEADS // 2
RWKV_CHUNK = 64
LORA_DECAY = 64
LORA_A = 64
LORA_G = 160
RWKV_GN_EPS = 64e-5
N_GROUPS = 4
EXPERTS_PER_GROUP = 8
N_EXPERTS = N_GROUPS * EXPERTS_PER_GROUP
D_EXPERT = 256
RMS_EPS = 1e-6

SRC_SHIFT = 4 * D_MODEL
SRC_WLO = SRC_SHIFT + 3 * D_MODEL
SRC_ALO = SRC_WLO + LORA_DECAY
SRC_GLO = SRC_ALO + LORA_A
SRC_GATE_A = SRC_GLO + LORA_G
COL_Q, COL_K, COL_V, COL_GSW = 0, 1024, 2048, 3072
COL_R, COL_KK, COL_VV = 4096, 5120, 6144
COL_GATE_A, COL_GATE_B = 7168, 8192
COL_LORA = 9216
LORA_W = 4 * LANES
N_PROJ = COL_LORA + LORA_W

VMEM_LIMIT = 48 * 1024 * 1024

INPROJ_TILE = 1024
INPROJ_SLABS = 4
INPROJ_ROWS = 128
RET_CHUNKS_PER_STEP = 4
RWKV_TILE = 256
RWKV_PAIRS_PER_STEP = 8
STEP_ROWS = 8
MERGE_TILE = 512
MERGE_ROWS = 128
MOE_TILE = 1024
MOE_BLOCK = 128
MOE_PERM_ROWS = 256
EXPERTS_PER_STEP = 4

_NN = (((1,), (0,)), ((), ()))
_NT = (((1,), (1,)), ((), ()))
_TN = (((0,), (0,)), ((), ()))


def _mm(a, b, dims=_NN):
    return lax.dot_general(a.astype(BF16), b.astype(BF16), dims,
                           preferred_element_type=F32)


def _params(sem):
    return pltpu.CompilerParams(dimension_semantics=sem,
                                vmem_limit_bytes=VMEM_LIMIT)


def _rms(x, gain):
    return x * lax.rsqrt(jnp.mean(x * x, axis=-1, keepdims=True) + RMS_EPS) * gain


def _split_bf16(x, terms):
    parts = []
    for _ in range(terms - 1):
        hi = x.astype(BF16)
        parts.append(hi)
        x = x - hi.astype(F32)
    parts.append(x.astype(BF16))
    return parts


def _lane_sum(x, ones):
    hi, lo = _split_bf16(x, 2)
    return (jnp.dot(hi, ones, preferred_element_type=F32)
            + jnp.dot(lo, ones, preferred_element_type=F32))


def _inproj_body(x_ref, g_ref, w_ref, o_ref, xn_ref, *, normed, keep):
    step = math.gcd(x_ref.shape[0], INPROJ_ROWS, normed)
    for r0 in range(0, x_ref.shape[0], step):
        x = x_ref[r0:r0 + step, :]
        if r0 < normed:
            x = _rms(x, g_ref[...])
            lo, hi = max(r0, normed - keep), r0 + step
            if lo < hi:
                xn_ref[0, 0, lo - (normed - keep):hi - (normed - keep), :] = x[lo - r0:, :]
        o_ref[r0:r0 + step, :] = jnp.dot(x.astype(BF16), w_ref[...], preferred_element_type=F32)


def _inproj(x, gain, w, tm, normed, keep):
    t = x.shape[0]
    n = w.shape[1]
    tn = n // INPROJ_SLABS
    assert keep <= normed <= tm
    proj, kept = pl.pallas_call(
        functools.partial(_inproj_body, normed=normed, keep=keep),
        grid=(n // tn, t // tm),
        in_specs=[pl.BlockSpec((tm, D_MODEL), lambda j, i: (i, 0)),
                  pl.BlockSpec((1, D_MODEL), lambda j, i: (0, 0)),
                  pl.BlockSpec((D_MODEL, tn), lambda j, i: (0, j), pipeline_mode=pl.Buffered(1))],
        out_specs=[pl.BlockSpec((tm, tn), lambda j, i: (i, j)),
                   pl.BlockSpec((1, 1, keep, D_MODEL), lambda j, i: (j, i, 0, 0))],
        out_shape=[jax.ShapeDtypeStruct((t, n), F32),
                   jax.ShapeDtypeStruct((n // tn, t // tm, keep, D_MODEL), F32)],
        compiler_params=_params(("arbitrary", "arbitrary")),
        name="inproj",
    )(x, gain, w)
    return proj, kept[0]


def _head_norm(x, eps):
    mu = jnp.mean(x, axis=-1, keepdims=True)
    xc = x - mu
    return xc * lax.rsqrt(jnp.mean(xc * xc, axis=-1, keepdims=True) + eps)


def _silu(x):
    return x * jax.nn.sigmoid(x)


def _rope(x, cos, sin_signed):
    return x * cos + pltpu.roll(x, RET_D // 2, 1) * sin_signed


def _ret_tables(chunk):
    log_gamma = jnp.log1p(-jnp.exp2(-5.0 - jnp.arange(RET_HEADS, dtype=F32)))
    idx = jnp.arange(chunk, dtype=F32)
    rel = idx[:, None] - idx[None, :]
    intra = jnp.where(rel[None] >= 0,
                      jnp.exp(jnp.maximum(rel, 0.0)[None] * log_gamma[:, None, None]), 0.0)
    q_dec = jnp.exp((idx[:, None] + 1.0) * log_gamma[None, :])
    k_dec = jnp.exp((chunk - 1.0 - idx)[None, :] * log_gamma[:, None]).T
    chunk_dec = jnp.exp(chunk * log_gamma)[None, :]
    widen = lambda t: jnp.repeat(t, RET_D, axis=1)
    return intra, widen(q_dec), widen(k_dec), widen(chunk_dec)


def _rope_tables(pos):
    half = RET_D // 2
    inv = ROPE_BASE ** (-jnp.arange(half, dtype=F32) / half)
    ang = pos[:, None] * inv[None, :]
    cos, sin = jnp.cos(ang), jnp.sin(ang)
    return jnp.concatenate([cos, cos], axis=1), jnp.concatenate([-sin, sin], axis=1)


def _ret_body(q_ref, k_ref, v_ref, g_ref, cos_ref, sin_ref, qd_ref, kd_ref, cd_ref,
              intra_ref, gnw_ref, o_ref, sout_ref, s_ref):
    c = pl.program_id(1)
    chunk = qd_ref.shape[0]

    @pl.when(c == 0)
    def _():
        s_ref[...] = jnp.zeros_like(s_ref)

    heads = [slice(h * RET_D, (h + 1) * RET_D) for h in range(RET_HEADS)]
    mean_w = jnp.full((RET_D, RET_D), 1.0 / RET_D, BF16)
    state = [s_ref[h] for h in range(RET_HEADS)]
    for r0 in range(0, q_ref.shape[0], chunk):
        rows = slice(r0, r0 + chunk)
        cos = cos_ref[rows, :]
        sin = sin_ref[rows, :]
        qr = [_rope(q_ref[rows, sl], cos, sin).astype(BF16) for sl in heads]
        kr = [_rope(k_ref[rows, sl], cos, sin) for sl in heads]
        vb = [v_ref[rows, sl].astype(BF16) for sl in heads]
        scores = [_mm(qr[h], kr[h], _NT) * intra_ref[h] for h in range(RET_HEADS)]
        cross = [_mm(qr[h], state[h]) * qd_ref[:, sl] for h, sl in enumerate(heads)]
        adds = [_mm(kr[h] * kd_ref[:, sl], vb[h], _TN) for h, sl in enumerate(heads)]
        out = [_mm(scores[h], vb[h]) + cross[h] for h in range(RET_HEADS)]
        cen = [out[h] - _lane_sum(out[h], mean_w) for h in range(RET_HEADS)]
        var = [_lane_sum(cen[h] * cen[h], mean_w) for h in range(RET_HEADS)]
        for h, sl in enumerate(heads):
            state[h] = state[h] * cd_ref[:, sl] + adds[h]
            o_ref[rows, sl] = (cen[h] * lax.rsqrt(var[h] + RET_GN_EPS) * gnw_ref[:, sl]
                               * _silu(g_ref[rows, sl]))
    for h in range(RET_HEADS):
        s_ref[h] = state[h]

    @pl.when(c == pl.num_programs(1) - 1)
    def _():
        sout_ref[0] = s_ref[...]


def _retention_prompt(proj, gn_w, batch, seq):
    chunk = math.gcd(seq, RET_CHUNK)
    tile = math.gcd(seq, RET_CHUNKS_PER_STEP * chunk)
    n_tiles = seq // tile
    intra, q_dec, k_dec, chunk_dec = _ret_tables(chunk)
    intra = intra * (RET_D ** -0.5)
    k_dec = k_dec * (RET_D ** -0.5)
    cos, sin = _rope_tables(jnp.arange(seq, dtype=F32))
    col = lambda j: pl.BlockSpec((tile, D_MODEL), lambda b, c, j=j: (b * n_tiles + c, j))
    const2 = lambda shape: pl.BlockSpec(shape, lambda b, c: (0, 0))
    state = pl.BlockSpec((1, RET_HEADS, RET_D, RET_D), lambda b, c: (b, 0, 0, 0))
    return pl.pallas_call(
        _ret_body,
        grid=(batch, n_tiles),
        in_specs=[col(COL_Q // D_MODEL), col(COL_K // D_MODEL), col(COL_V // D_MODEL),
                  col(COL_GSW // D_MODEL),
                  pl.BlockSpec((tile, RET_D), lambda b, c: (c, 0)),
                  pl.BlockSpec((tile, RET_D), lambda b, c: (c, 0)),
                  const2((chunk, D_MODEL)), const2((chunk, D_MODEL)), const2((1, D_MODEL)),
                  pl.BlockSpec((RET_HEADS, chunk, chunk), lambda b, c: (0, 0, 0)),
                  const2((1, D_MODEL))],
        out_specs=[pl.BlockSpec((tile, D_MODEL), lambda b, c: (b * n_tiles + c, 0)), state],
        out_shape=[jax.ShapeDtypeStruct((batch * seq, D_MODEL), F32),
                   jax.ShapeDtypeStruct((batch, RET_HEADS, RET_D, RET_D), F32)],
        scratch_shapes=[pltpu.VMEM((RET_HEADS, RET_D, RET_D), F32)],
        compiler_params=_params(("parallel", "arbitrary")),
        name="retention_prompt",
    )(proj, proj, proj, proj, cos, sin, q_dec, k_dec, chunk_dec, intra, gn_w)


def _ret_step_body(q_ref, k_ref, v_ref, g_ref, cos_ref, sin_ref, qd_ref, kd_ref, cd_ref,
                   gnw_ref, s_ref, o_ref, sout_ref, *, rows):
    cos = cos_ref[...]
    sin = sin_ref[...]
    row = lax.broadcasted_iota(jnp.int32, (rows, RET_D), 0)
    for h in range(RET_HEADS):
        sl = slice(h * RET_D, (h + 1) * RET_D)
        qr = _rope(q_ref[:, sl], cos, sin)
        kr = _rope(k_ref[:, sl], cos, sin) * (RET_D ** -0.5)
        v = v_ref[:, sl]
        kd = kr * kd_ref[:, sl]
        cd = cd_ref[:, sl]

        reads = [_mm(qr, s_ref[b, h]) for b in range(rows)]
        adds = [_mm(jnp.where(row == b, kd, 0.0), v, _TN) for b in range(rows)]
        cross = jnp.zeros((rows, RET_D), F32)
        for b in range(rows):
            sout_ref[b, h] = s_ref[b, h] * cd + adds[b]
            cross = jnp.where(row == b, reads[b], cross)
        out = jnp.sum(qr * kr, axis=-1, keepdims=True) * v + cross * qd_ref[:, sl]
        o_ref[:, sl] = _head_norm(out, RET_GN_EPS) * gnw_ref[:, sl] * _silu(g_ref[:, sl])


def _retention_step(proj, s0, gn_w, pos):
    batch = s0.shape[0]
    rows = STEP_ROWS
    _, q_dec, k_dec, chunk_dec = _ret_tables(1)
    cos, sin = _rope_tables(pos)
    col = lambda j: pl.BlockSpec((rows, D_MODEL), lambda i, j=j: (i, j))
    const2 = lambda shape: pl.BlockSpec(shape, lambda i: (0, 0))
    state = pl.BlockSpec((rows, RET_HEADS, RET_D, RET_D), lambda i: (i, 0, 0, 0))
    return pl.pallas_call(
        functools.partial(_ret_step_body, rows=rows),
        grid=(batch // rows,),
        in_specs=[col(COL_Q // D_MODEL), col(COL_K // D_MODEL), col(COL_V // D_MODEL),
                  col(COL_GSW // D_MODEL),
                  const2((1, RET_D)), const2((1, RET_D)),
                  const2((1, D_MODEL)), const2((1, D_MODEL)), const2((1, D_MODEL)),
                  const2((1, D_MODEL)), state],
        out_specs=[pl.BlockSpec((rows, D_MODEL), lambda i: (i, 0)), state],
        out_shape=[jax.ShapeDtypeStruct((batch, D_MODEL), F32),
                   jax.ShapeDtypeStruct(s0.shape, F32)],
        compiler_params=_params(("parallel",)),
        name="retention_step",
    )(proj, proj, proj, proj, cos, sin, q_dec, k_dec, chunk_dec, gn_w, s0)


def _head_ones(value=1.0):
    row = lax.broadcasted_iota(jnp.int32, (LANES, LANES), 0)
    col = lax.broadcasted_iota(jnp.int32, (LANES, LANES), 1)
    return jnp.where((row // RWKV_HEAD) == (col // RWKV_HEAD), value, 0.0).astype(BF16)


def _head_sum(x, ones_bd):
    blocks = [_lane_sum(x[:, j:j + LANES], ones_bd) for j in range(0, x.shape[1], LANES)]
    return blocks[0] if len(blocks) == 1 else jnp.concatenate(blocks, axis=1)


def _softplus(x):
    return jnp.maximum(x, 0.0) + jnp.log(1.0 + jnp.exp(-jnp.abs(x)))


def _lerp(cur, prev, mu):
    return cur + (prev - cur) * mu


def _rwkv_prep(r_, k_, v_, lora, w0, w2, a0, a2, g2, k_k, k_a, r_k, ones_bd):
    w_lo, a_lo, g_lo = lora[:, :LANES], lora[:, LANES:2 * LANES], lora[:, 2 * LANES:LORA_W]
    w_log = -_softplus(-(w0 + _mm(jnp.tanh(w_lo), w2))) - 0.5
    log_decay = -jnp.exp(w_log)
    a_sig = jax.nn.sigmoid(a0 + _mm(a_lo, a2))
    g = _mm(jax.nn.sigmoid(g_lo), g2)
    kk = k_ * k_k
    kk = kk * lax.rsqrt(jnp.maximum(_head_sum(kk * kk, ones_bd), 1e-24))
    kmod = k_ * (1.0 + (a_sig - 1.0) * k_a)
    bonus = _head_sum(r_ * kmod * r_k, ones_bd) * v_
    return log_decay, a_sig, g, kk, kmod, bonus


def _rwkv_out(y, bonus, g, lnx_w, lnx_b):
    mean_bd = _head_ones(1.0 / RWKV_HEAD)
    yc = y - _head_sum(y, mean_bd)
    var = _head_sum(yc * yc, mean_bd)
    return (yc * lax.rsqrt(var + RWKV_GN_EPS) * lnx_w + lnx_b + bonus) * g


def _block_diag(x, lane_first):
    return jnp.concatenate([jnp.where(lane_first, x, 0.0), jnp.where(lane_first, 0.0, x)], axis=0)


def _pair_masks():
    tok = lax.broadcasted_iota(jnp.int32, (RWKV_CHUNK, LANES), 0)
    src = lax.broadcasted_iota(jnp.int32, (RWKV_CHUNK, LANES), 1) % RWKV_CHUNK
    n = 2 * RWKV_CHUNK
    eye = lax.broadcasted_iota(jnp.int32, (n, n), 0) == lax.broadcasted_iota(jnp.int32, (n, n), 1)
    lane_first = lax.broadcasted_iota(jnp.int32, (1, LANES), 1) < RWKV_HEAD
    return lane_first, src < tok, src <= tok, src == tok, eye


def _rwkv_chunks(chunks, masks):
    lane_first, strict, incl, eye_wide, eye = masks
    n = RWKV_CHUNK
    rows = lambda x, y: jnp.concatenate([x, y], axis=0)
    cols = lambda x, y: jnp.concatenate([x, y], axis=1)
    bd = lambda x: _block_diag(x, lane_first)

    ops = []
    for r, lw, k, v, a, b, cum in chunks:
        p_inc = jnp.exp(cum)
        p_inv = jnp.exp(-cum)
        p_exc = jnp.exp(cum - lw)
        p_end = p_inc[RWKV_CHUNK - 1:RWKV_CHUNK, :]
        b_t = b * p_inv
        k_t = k * p_inv
        ops.append(dict(a_t=(a * p_exc).astype(BF16), r_t=(r * p_inc).astype(BF16),
                        b_t=bd(b_t.astype(BF16)), k_t=bd(k_t.astype(BF16)),
                        b_hat=bd(b_t * p_end), k_hat=bd(k_t * p_end),
                        v_bd=bd(v.astype(BF16)), p_end=p_end))

    for o in ops:
        prod = _mm(rows(o["a_t"], o["r_t"]), rows(o["b_t"], o["k_t"]), _NT)
        o["a_ab"] = jnp.where(strict, prod[:n, :LANES], 0.0)
        o["a_ak"] = jnp.where(strict, prod[:n, LANES:], 0.0).astype(BF16)
        o["r_bk"] = cols(jnp.where(incl, prod[n:, :LANES], 0.0),
                         jnp.where(incl, prod[n:, LANES:], 0.0)).astype(BF16)

    for o in ops:
        a_ab = o["a_ab"].astype(BF16)
        o["t_inv"] = jnp.where(eye_wide, 1.0, 0.0) + o["a_ab"]
        o["power"] = _mm(a_ab, bd(a_ab)).astype(BF16)
        o["av"] = _mm(o["a_ak"], o["v_bd"]).astype(BF16)
    for _ in range(int(math.log2(RWKV_CHUNK)) - 2):
        for o in ops:
            both = _mm(rows(o["power"], o["t_inv"].astype(BF16)), bd(o["power"]))
            o["power"] = both[:n].astype(BF16)
            o["t_inv"] = o["t_inv"] + both[n:]
    for o in ops:
        o["t_inv"] = o["t_inv"] + _mm(o["t_inv"], bd(o["power"]))
    out = []
    for o in ops:
        wu = _mm(o["t_inv"], cols(bd(o["a_t"]), bd(o["av"])))
        hat_t = rows(o["b_hat"], o["k_hat"]).T
        p_col = jnp.sum(jnp.where(eye, o["p_end"], 0.0), axis=1, keepdims=True)
        out.append((rows(wu[:, :LANES].astype(BF16), o["r_t"]), wu[:, LANES:],
                    o["r_bk"], o["v_bd"], hat_t.astype(BF16), p_col))
    return out


def _shift_rows(cur, first_row):
    rolled = pltpu.roll(cur, 1, 0)
    row = lax.broadcasted_iota(jnp.int32, cur.shape, 0)
    return jnp.where(row == 0, first_row, rolled)


def _rwkv_body(r_ref, k_ref, v_ref, lora_ref, mu_ref, mul_ref,
               w0_ref, w2_ref, a0_ref, a2_ref, g2_ref, kk_ref, ka_ref, rk_ref,
               lnw_ref, lnb_ref, o_ref, sout_ref,
               s_ref, carry_ref, carryl_ref, *, tile):
    t = pl.program_id(1)
    p = pl.program_id(2)
    masks = _pair_masks()
    ones_bd = _head_ones()
    pairs = r_ref.shape[1] // LANES

    @pl.when(t == 0)
    def _():
        for q in range(pairs):
            s_ref[p * pairs + q] = jnp.zeros((LANES, LANES), F32)
        carry_ref[p] = jnp.zeros(carry_ref.shape[1:], F32)

        @pl.when(p == 0)
        def _():
            carryl_ref[...] = jnp.zeros_like(carryl_ref)

    prev = carry_ref[p]
    cur_r, cur_k, cur_v, cur_l = r_ref[...], k_ref[...], v_ref[...], lora_ref[...]
    mu = mu_ref[0]
    r_ = _lerp(cur_r, _shift_rows(cur_r, prev[0:1]), mu[0:1])
    k_ = _lerp(cur_k, _shift_rows(cur_k, prev[1:2]), mu[1:2])
    v_ = _lerp(cur_v, _shift_rows(cur_v, prev[2:3]), mu[2:3])
    lo = _lerp(cur_l, _shift_rows(cur_l, carryl_ref[0:1, :]), mul_ref[...])
    carry_ref[p, 0:1, :] = cur_r[tile - 1:tile]
    carry_ref[p, 1:2, :] = cur_k[tile - 1:tile]
    carry_ref[p, 2:3, :] = cur_v[tile - 1:tile]

    @pl.when(p == pl.num_programs(2) - 1)
    def _():
        carryl_ref[0:1, :] = cur_l[tile - 1:tile]

    log_decay, a_sig, g, kk, kmod, bonus = _rwkv_prep(
        r_, k_, v_, lo, w0_ref[...], w2_ref[...], a0_ref[...], a2_ref[...], g2_ref[...],
        kk_ref[...], ka_ref[...], rk_ref[...], ones_bd)

    row = lax.broadcasted_iota(jnp.int32, (tile, tile), 0)
    col = lax.broadcasted_iota(jnp.int32, (tile, tile), 1)
    tri = jnp.where(((row // RWKV_CHUNK) == (col // RWKV_CHUNK)) & (col <= row), 1.0, 0.0).astype(BF16)
    cum = sum(jnp.dot(tri, part, preferred_element_type=F32) for part in _split_bf16(log_decay, 3))

    neg_kk = -kk
    kk_a = kk * a_sig
    n_chunks = tile // RWKV_CHUNK
    chunks = []
    for c in range(n_chunks):
        for q in range(pairs):
            at = (slice(c * RWKV_CHUNK, (c + 1) * RWKV_CHUNK), slice(q * LANES, (q + 1) * LANES))
            chunks.append((r_[at], log_decay[at], kmod[at], v_[at], neg_kk[at], kk_a[at], cum[at]))
    parts = _rwkv_chunks(chunks, masks)

    n = RWKV_CHUNK
    lane_first = masks[0]
    states = [s_ref[p * pairs + q] for q in range(pairs)]
    ys = [[] for _ in range(pairs)]
    for c in range(n_chunks):
        reads = [_mm(parts[c * pairs + q][0], states[q]) for q in range(pairs)]
        for q in range(pairs):
            _, u0, r_bk, v_bd, hat_t, p_col = parts[c * pairs + q]
            u_bd = _block_diag(reads[q][:n] + u0, lane_first).astype(BF16)
            uv = jnp.concatenate([u_bd, v_bd], axis=0)
            ys[q].append(reads[q][n:] + _mm(r_bk, uv))
            states[q] = states[q] * p_col + _mm(hat_t, uv)
    for q in range(pairs):
        s_ref[p * pairs + q] = states[q]

    @pl.when(t == pl.num_programs(1) - 1)
    def _():
        for q in range(pairs):
            head = 2 * (p * pairs + q)
            s_bd = states[q].T
            sout_ref[0, head] = s_bd[:RWKV_HEAD, :RWKV_HEAD]
            sout_ref[0, head + 1] = s_bd[RWKV_HEAD:, RWKV_HEAD:]

    y = jnp.concatenate([jnp.concatenate(yq, axis=0) for yq in ys], axis=1)
    o_ref[...] = _rwkv_out(y, bonus, g, lnw_ref[...], lnb_ref[...])


def _rwkv_prompt(proj, wts, batch, seq):
    tile = math.gcd(seq, RWKV_TILE)
    n_tiles = seq // tile
    width = RWKV_PAIRS_PER_STEP * LANES
    blk = lambda base: pl.BlockSpec((tile, width),
                                    lambda b, t, p, base=base: (b * n_tiles + t, base // width + p))
    per_pair = lambda rows: pl.BlockSpec((rows, width), lambda b, t, p: (0, p))
    in_specs = [
        blk(COL_R), blk(COL_KK), blk(COL_VV),
        pl.BlockSpec((tile, LORA_W), lambda b, t, p: (b * n_tiles + t, COL_LORA // LORA_W)),
        pl.BlockSpec((1, 3, width), lambda b, t, p: (0, 0, p)),
        pl.BlockSpec((1, LORA_W), lambda b, t, p: (0, 0)),
        per_pair(1), per_pair(LANES),
        per_pair(1), per_pair(LANES),
        per_pair(2 * LANES),
        per_pair(1), per_pair(1), per_pair(1),
        per_pair(1), per_pair(1),
    ]
    return pl.pallas_call(
        functools.partial(_rwkv_body, tile=tile),
        grid=(batch, n_tiles, RWKV_PAIRS // RWKV_PAIRS_PER_STEP),
        in_specs=in_specs,
        out_specs=[pl.BlockSpec((tile, width), lambda b, t, p: (b * n_tiles + t, p)),
                   pl.BlockSpec((1, RWKV_HEADS, RWKV_HEAD, RWKV_HEAD), lambda b, t, p: (b, 0, 0, 0))],
        out_shape=[jax.ShapeDtypeStruct((batch * seq, D_MODEL), F32),
                   jax.ShapeDtypeStruct((batch, RWKV_HEADS, RWKV_HEAD, RWKV_HEAD), F32)],
        scratch_shapes=[pltpu.VMEM((RWKV_PAIRS, LANES, LANES), F32),
                        pltpu.VMEM((RWKV_PAIRS // RWKV_PAIRS_PER_STEP, SUBLANES, width), F32),
                        pltpu.VMEM((SUBLANES, LORA_W), F32)],
        compiler_params=_params(("parallel", "arbitrary", "arbitrary")),
        name="rwkv_prompt",
    )(proj, proj, proj, proj, wts["mu3"], wts["mu_lora"],
      wts["w0"], wts["w2"], wts["a0"], wts["a2"], wts["g2"],
      wts["k_k"], wts["k_a"], wts["r_k"], wts["lnx_w"], wts["lnx_b"])


def _rwkv_step_body(r_ref, k_ref, v_ref, lora_ref, pr_ref, pk_ref, pv_ref, plora_ref,
                    mu_ref, mul_ref, w0_ref, w2_ref, a0_ref, a2_ref, g2_ref,
                    kk_ref, ka_ref, rk_ref, lnw_ref, lnb_ref, s_ref,
                    o_ref, sout_ref, *, rows):
    ones_bd = _head_ones()
    mu = mu_ref[0]
    r_ = _lerp(r_ref[...], pr_ref[...], mu[0:1])
    k_ = _lerp(k_ref[...], pk_ref[...], mu[1:2])
    v_ = _lerp(v_ref[...], pv_ref[...], mu[2:3])
    lo = _lerp(lora_ref[...], plora_ref[...], mul_ref[...])
    log_decay, a_sig, g, kk, kmod, bonus = _rwkv_prep(
        r_, k_, v_, lo, w0_ref[...], w2_ref[...], a0_ref[...], a2_ref[...], g2_ref[...],
        kk_ref[...], ka_ref[...], rk_ref[...], ones_bd)
    decay = jnp.exp(log_decay)
    neg_kk = -kk
    kk_a = kk * a_sig
    decay_r = decay * r_
    row = lax.broadcasted_iota(jnp.int32, (rows, RWKV_HEAD), 0)

    heads = [slice(h * RWKV_HEAD, (h + 1) * RWKV_HEAD) for h in range(RWKV_HEADS)]
    reads = []
    for h, sl in enumerate(heads):
        lhs = jnp.concatenate([neg_kk[:, sl], decay_r[:, sl]], axis=0)
        reads.append([_mm(lhs, s_ref[b, h], _NT) for b in range(rows)])
    ys = []
    for h, sl in enumerate(heads):
        w_h, r_h, v_h, k_h, b_h = decay[:, sl], r_[:, sl], v_[:, sl], kmod[:, sl], kk_a[:, sl]
        rhs = jnp.concatenate([b_h, k_h], axis=0)
        sa_all = swr_all = jnp.zeros((rows, RWKV_HEAD), F32)
        adds = []
        for b in range(rows):
            mine = row == b
            sa, swr = reads[h][b][:rows], reads[h][b][rows:]
            left = jnp.concatenate([jnp.where(mine, sa, 0.0), jnp.where(mine, v_h, 0.0)], axis=0)
            adds.append(_mm(left, rhs, _TN))
            sa_all = jnp.where(mine, sa, sa_all)
            swr_all = jnp.where(mine, swr, swr_all)
        for b in range(rows):
            sout_ref[b, h] = s_ref[b, h] * w_h[b:b + 1] + adds[b]
        ys.append(swr_all + sa_all * jnp.sum(b_h * r_h, axis=-1, keepdims=True)
                  + v_h * jnp.sum(k_h * r_h, axis=-1, keepdims=True))
    y = jnp.concatenate(ys, axis=1)
    o_ref[...] = _rwkv_out(y, bonus, g, lnw_ref[...], lnb_ref[...])


def _rwkv_step(proj, s0, wts):
    batch = s0.shape[0]
    rows = STEP_ROWS
    shift = batch // rows
    wide = lambda base, off=0: pl.BlockSpec((rows, D_MODEL),
                                            lambda i, base=base, off=off: (i + off, base // D_MODEL))
    lora = pl.BlockSpec((rows, LORA_W), lambda i: (i, COL_LORA // LORA_W))
    lora_prev = pl.BlockSpec((rows, LORA_W), lambda i: (i + shift, COL_LORA // LORA_W))
    full = lambda shape: pl.BlockSpec(shape, lambda i: (0,) * len(shape))
    state = pl.BlockSpec((rows, RWKV_HEADS, RWKV_HEAD, RWKV_HEAD), lambda i: (i, 0, 0, 0))
    vec = full((1, D_MODEL))
    return pl.pallas_call(
        functools.partial(_rwkv_step_body, rows=rows),
        grid=(batch // rows,),
        in_specs=[wide(COL_R), wide(COL_KK), wide(COL_VV), lora,
                  wide(COL_R, shift), wide(COL_KK, shift), wide(COL_VV, shift), lora_prev,
                  full((1, 3, D_MODEL)), full((1, LORA_W)),
                  vec, full((LANES, D_MODEL)), vec, full((LANES, D_MODEL)),
                  full((2 * LANES, D_MODEL)), vec, vec, vec, vec, vec, state],
        out_specs=[pl.BlockSpec((rows, D_MODEL), lambda i: (i, 0)), state],
        out_shape=[jax.ShapeDtypeStruct((batch, D_MODEL), F32),
                   jax.ShapeDtypeStruct(s0.shape, F32)],
        compiler_params=_params(("parallel",)),
        name="rwkv_step",
    )(proj, proj, proj, proj, proj, proj, proj, proj,
      wts["mu3"], wts["mu_lora"], wts["w0"], wts["w2"], wts["a0"], wts["a2"], wts["g2"],
      wts["k_k"], wts["k_a"], wts["r_k"], wts["lnx_w"], wts["lnx_b"], s0)


def _merge_body(x_ref, oret_ref, orwkv_ref, ga_ref, gb_ref, wout_ref, nffn_ref, rw_ref, rb_ref,
                x1_ref, comb_ref, cnt_ref):
    step = math.gcd(x_ref.shape[0], MERGE_ROWS)
    chunks = [slice(r0, r0 + step) for r0 in range(0, x_ref.shape[0], step)]
    each = lambda fn, *cols: [fn(*args) for args in zip(*cols)]
    row_max = lambda v: jnp.max(v, axis=-1, keepdims=True)
    row_min = lambda v: jnp.min(v, axis=-1, keepdims=True)
    row_sum = lambda v: jnp.sum(v, axis=-1, keepdims=True)
    lane = lax.broadcasted_iota(jnp.int32, (step, LANES), 1)
    is_group = lane < N_GROUPS
    neg_inf = -jnp.inf

    merged = [jax.nn.sigmoid(ga_ref[r, :]) * oret_ref[r, :]
              + jax.nn.sigmoid(gb_ref[r, :]) * orwkv_ref[r, :] for r in chunks]
    x1 = [x_ref[r, :] + _mm(m, wout_ref[...]) for r, m in zip(chunks, merged)]
    xn = each(lambda v: _rms(v, nffn_ref[...]), x1)
    for r, a in zip(chunks, x1):
        x1_ref[r, :] = a

    logits = each(lambda v: _mm(v, rw_ref[...]) + rb_ref[...], xn)
    g_max = each(lambda l: row_max(jnp.where(is_group, l, neg_inf)), logits)
    g_sel = each(lambda l, m: row_min(jnp.where(is_group & (l == m), lane, LANES)), logits, g_max)
    g_w = each(lambda l, m: 1.0 / row_sum(jnp.where(is_group, jnp.exp(l - m), 0.0)), logits, g_max)
    first = each(lambda g: N_GROUPS + EXPERTS_PER_GROUP * g, g_sel)
    in_group = each(lambda f: (lane >= f) & (lane < f + EXPERTS_PER_GROUP), first)
    e_max = each(lambda l, ing: row_max(jnp.where(ing, l, neg_inf)), logits, in_group)
    e_exp = each(lambda l, ing, m: jnp.where(ing, jnp.exp(l - m), 0.0), logits, in_group, e_max)
    prob = each(lambda e, ing: jnp.where(ing, e / row_sum(e), -1.0), e_exp, in_group)
    p1 = each(row_max, prob)
    i1 = each(lambda p, m: row_min(jnp.where(p == m, lane, LANES)), prob, p1)
    rest = each(lambda p, i: jnp.where(lane == i, -1.0, p), prob, i1)
    p2 = each(row_max, rest)
    i2 = each(lambda p, m: row_min(jnp.where(p == m, lane, LANES)), rest, p2)
    lane8 = lax.broadcasted_iota(jnp.int32, (SUBLANES, LANES), 1)
    sizes = jnp.zeros((SUBLANES, LANES), jnp.int32)
    for c, r in enumerate(chunks):
        denom = p1[c] + p2[c]
        comb = (jnp.where(lane == i1[c] - N_GROUPS, g_w[c] * p1[c] / denom, 0.0)
                + jnp.where(lane == i2[c] - N_GROUPS, g_w[c] * p2[c] / denom, 0.0))
        comb_ref[r, :] = jnp.where(lane == N_EXPERTS, g_sel[c].astype(F32), comb)
        for g in range(N_GROUPS):
            n_g = jnp.sum(jnp.where(g_sel[c] == g, 1, 0), axis=0, keepdims=True)
            sizes = sizes + jnp.where(lane8 == g, n_g, 0)
    cnt_ref[0] = sizes


def _merge(x, o_ret, o_rwkv, proj, wts):
    t = x.shape[0]
    tm = math.gcd(t, MERGE_TILE)
    row = lambda: pl.BlockSpec((tm, D_MODEL), lambda i: (i, 0))
    full = lambda shape: pl.BlockSpec(shape, lambda i: (0, 0))
    return pl.pallas_call(
        _merge_body,
        grid=(t // tm,),
        in_specs=[row(), row(), row(),
                  pl.BlockSpec((tm, D_MODEL), lambda i: (i, COL_GATE_A // D_MODEL)),
                  pl.BlockSpec((tm, D_MODEL), lambda i: (i, COL_GATE_B // D_MODEL)),
                  full((D_MODEL, D_MODEL)), full((1, D_MODEL)),
                  full((D_MODEL, LANES)), full((1, LANES))],
        out_specs=[row(), pl.BlockSpec((tm, LANES), lambda i: (i, 0)),
                   pl.BlockSpec((1, SUBLANES, LANES), lambda i: (i, 0, 0))],
        out_shape=[jax.ShapeDtypeStruct((t, D_MODEL), F32),
                   jax.ShapeDtypeStruct((t, LANES), F32),
                   jax.ShapeDtypeStruct((t // tm, SUBLANES, LANES), jnp.int32)],
        compiler_params=_params(("parallel",)),
        name="merge_router",
    )(x, o_ret, o_rwkv, proj, proj, wts["w_out"], wts["norm_ffn"], wts["router_w"], wts["router_b"])


def _moe_body(cnt_ref, x1_ref, comb_ref, wg_ref, wu_ref, wd_ref, nffn_ref, nf_ref, o_ref,
              acc_ref, xs_ref, cs_ref, pt_ref, *, ratio):
    i = pl.program_id(0)
    s = pl.program_id(1)
    tm = x1_ref.shape[0]
    group = s // (EXPERTS_PER_GROUP // EXPERTS_PER_STEP)

    sizes = []
    for g in range(N_GROUPS):
        n = cnt_ref[i * ratio * N_GROUPS + g]
        for k in range(1, ratio):
            n = n + cnt_ref[(i * ratio + k) * N_GROUPS + g]
        sizes.append(n)
    starts = [jnp.int32(0)]
    for g in range(N_GROUPS - 1):
        starts.append(starts[-1] + sizes[g])

    @pl.when(s == 0)
    def _():
        comb = comb_ref[...]
        g_row = comb.T[N_EXPERTS:N_EXPERTS + 1, :]
        gid = lax.broadcasted_iota(jnp.int32, (SUBLANES, tm), 0)
        onehot = jnp.where(g_row.astype(jnp.int32) == gid, 1.0, 0.0)
        step = math.gcd(tm, MOE_PERM_ROWS)
        row = lax.broadcasted_iota(jnp.int32, (step, tm), 0)
        col = lax.broadcasted_iota(jnp.int32, (step, tm), 1)
        rank = jnp.zeros((SUBLANES, tm), F32)
        for r0 in range(0, tm, step):
            earlier = jnp.where(row + r0 < col, 1.0, 0.0).astype(BF16)
            rank = rank + jnp.dot(onehot[:, r0:r0 + step].astype(BF16), earlier,
                                  preferred_element_type=F32)
        start_col = jnp.zeros((SUBLANES, 1), F32)
        gid_col = lax.broadcasted_iota(jnp.int32, (SUBLANES, 1), 0)
        for g in range(1, N_GROUPS):
            start_col = jnp.where(gid_col == g, starts[g].astype(F32), start_col)
        pos_row = jnp.sum(onehot * (rank + start_col), axis=0, keepdims=True)
        pos_row_i = pos_row.astype(jnp.int32)
        xn = _rms(x1_ref[...], nffn_ref[...]).astype(BF16)
        comb_parts = _split_bf16(comb, 3)
        for r0 in range(0, tm, step):
            perm = jnp.where(row + r0 == pos_row_i, 1.0, 0.0).astype(BF16)
            xs_ref[r0:r0 + step, :] = jnp.dot(perm, xn, preferred_element_type=F32).astype(BF16)
            cs_ref[r0:r0 + step, :] = sum(jnp.dot(perm, part, preferred_element_type=F32)
                                          for part in comb_parts)
        pos_col_i = jnp.broadcast_to(pos_row, (LANES, tm)).T.astype(jnp.int32)
        lane = lax.broadcasted_iota(jnp.int32, (tm, LANES), 1)
        for j in range(tm // LANES):
            pt_ref[:, j * LANES:(j + 1) * LANES] = jnp.where(
                lane + j * LANES == pos_col_i, 1.0, 0.0).astype(BF16)
        acc_ref[...] = jnp.zeros_like(acc_ref)

    start, size = starts[N_GROUPS - 1], sizes[N_GROUPS - 1]
    for g in range(N_GROUPS - 2, -1, -1):
        start = jnp.where(group == g, starts[g], start)
        size = jnp.where(group == g, sizes[g], size)
    blk = math.gcd(tm, MOE_BLOCK)
    first = start // blk
    last = jnp.where(size > 0, (start + size + blk - 1) // blk, first)
    lane = lax.broadcasted_iota(jnp.int32, (blk, LANES), 1)

    def run_blocks(js):
        at = [pl.ds(pl.multiple_of(j * blk, blk), blk) for j in js]
        xb = [xs_ref[r, :] for r in at]
        cw = [cs_ref[r, :] for r in at]
        hs = [[] for _ in js]
        for k in range(EXPERTS_PER_STEP):
            for i in range(len(js)):
                weight = jnp.sum(jnp.where(lane == s * EXPERTS_PER_STEP + k, cw[i], 0.0),
                                 axis=-1, keepdims=True)
                hs[i].append(_silu(_mm(xb[i], wg_ref[k])) * _mm(xb[i], wu_ref[k]) * weight)
        for i, r in enumerate(at):
            total = _mm(hs[i][0], wd_ref[0])
            for k in range(1, EXPERTS_PER_STEP):
                total = total + _mm(hs[i][k], wd_ref[k])
            acc_ref[r, :] += total

    pairs = (last - first) // 2

    def two_blocks(i, carry):
        run_blocks([first + 2 * i, first + 2 * i + 1])
        return carry

    lax.fori_loop(0, pairs, two_blocks, 0)

    @pl.when(first + 2 * pairs < last)
    def _():
        run_blocks([last - 1])

    @pl.when(s == pl.num_programs(1) - 1)
    def _():
        parts = _split_bf16(acc_ref[...], 2)
        step = math.gcd(tm, MOE_PERM_ROWS)
        for r0 in range(0, tm, step):
            moe = sum(jnp.dot(pt_ref[r0:r0 + step, :], part, preferred_element_type=F32)
                      for part in parts)
            o_ref[r0:r0 + step, :] = _rms(x1_ref[r0:r0 + step, :] + moe, nf_ref[...])


def _moe(x1, comb, counts, wts):
    t = x1.shape[0]
    tm = math.gcd(t, MOE_TILE)
    assert tm % LANES == 0, "token count must be a multiple of the lane width"
    ratio = tm // math.gcd(t, MERGE_TILE)
    row = lambda w: pl.BlockSpec((tm, w), lambda i, s, cnt: (i, 0))
    expert = lambda a, b: pl.BlockSpec((EXPERTS_PER_STEP, a, b), lambda i, s, cnt: (s, 0, 0))
    vec = pl.BlockSpec((1, D_MODEL), lambda i, s, cnt: (0, 0))
    grid_spec = pltpu.PrefetchScalarGridSpec(
        num_scalar_prefetch=1,
        grid=(t // tm, N_EXPERTS // EXPERTS_PER_STEP),
        in_specs=[row(D_MODEL), row(LANES),
                  expert(D_MODEL, D_EXPERT), expert(D_MODEL, D_EXPERT), expert(D_EXPERT, D_MODEL),
                  vec, vec],
        out_specs=row(D_MODEL),
        scratch_shapes=[pltpu.VMEM((tm, D_MODEL), F32),
                        pltpu.VMEM((tm, D_MODEL), BF16),
                        pltpu.VMEM((tm, LANES), F32),
                        pltpu.VMEM((tm, tm), BF16)])
    return pl.pallas_call(
        functools.partial(_moe_body, ratio=ratio),
        grid_spec=grid_spec,
        out_shape=jax.ShapeDtypeStruct((t, D_MODEL), F32),
        compiler_params=_params(("parallel", "arbitrary")),
        name="moe",
    )(counts, x1, comb, wts["w_gate"], wts["w_up"], wts["w_down"], wts["norm_ffn"], wts["norm_final"])


def _pad_to(x, size, axis):
    pad = [(0, 0)] * x.ndim
    pad[axis] = (0, size - x.shape[axis])
    return jnp.pad(x, pad)


def _prepare_weights(norm_mix, w_in, mu_shift, ret_gn_w, rwkv_w0, rwkv_w2, rwkv_a0, rwkv_a2,
                     rwkv_g2, rwkv_k_k, rwkv_k_a, rwkv_r_k, rwkv_lnx_w, rwkv_lnx_b, w_out,
                     norm_ffn, router_group_w, router_group_b, router_expert_w, router_expert_b,
                     expert_w_gate, expert_w_up, expert_w_down, norm_final):
    def lora_cols(t):
        return jnp.concatenate([_pad_to(t[..., SRC_WLO:SRC_ALO], LANES, -1),
                                _pad_to(t[..., SRC_ALO:SRC_GLO], LANES, -1),
                                _pad_to(t[..., SRC_GLO:SRC_GATE_A], 2 * LANES, -1)], axis=-1)

    w = w_in[0]
    w_in_p = jnp.concatenate([w[:, :SRC_WLO], w[:, SRC_GATE_A:], lora_cols(w)], axis=1).astype(BF16)
    mu = jnp.concatenate([jnp.zeros((SRC_SHIFT,), F32), mu_shift[0]])
    row = lambda v: v.reshape(1, -1)
    router_w = _pad_to(jnp.concatenate([router_group_w[0], router_expert_w[0]], axis=1), LANES, 1)
    router_b = _pad_to(jnp.concatenate([router_group_b[0], router_expert_b[0]]), LANES, 0)
    return dict(
        w_in=w_in_p, norm_mix=row(norm_mix[0]), ret_gn_w=row(ret_gn_w[0]),
        mu3=mu[SRC_SHIFT:SRC_WLO].reshape(1, 3, D_MODEL), mu_lora=lora_cols(mu).reshape(1, LORA_W),
        w0=row(rwkv_w0[0]), w2=_pad_to(rwkv_w2[0], LANES, 0),
        a0=row(rwkv_a0[0]), a2=_pad_to(rwkv_a2[0], LANES, 0),
        g2=_pad_to(rwkv_g2[0], 2 * LANES, 0),
        k_k=row(rwkv_k_k[0]), k_a=row(rwkv_k_a[0]), r_k=row(rwkv_r_k[0]),
        lnx_w=row(rwkv_lnx_w[0]), lnx_b=row(rwkv_lnx_b[0]),
        w_out=w_out[0].astype(BF16), norm_ffn=row(norm_ffn[0]),
        router_w=router_w, router_b=row(router_b),
        w_gate=expert_w_gate[0].astype(BF16), w_up=expert_w_up[0].astype(BF16),
        w_down=expert_w_down[0].astype(BF16), norm_final=row(norm_final))


def _finish(x, o_ret, o_rwkv, proj, wts):
    x1, comb, sizes = _merge(x, o_ret, o_rwkv, proj, wts)
    counts = sizes[:, 0, :N_GROUPS].reshape(-1)
    return _moe(x1, comb, counts, wts)


def kernel(x_prompt, x_sample, state_ret, state_rwkv, state_shift, norm_mix, w_in, mu_shift, ret_gn_w, rwkv_w0, rwkv_w2, rwkv_a0, rwkv_a2, rwkv_g2, rwkv_k_k, rwkv_k_a, rwkv_r_k, rwkv_lnx_w, rwkv_lnx_b, w_out, norm_ffn, router_group_w, router_group_b, router_expert_w, router_expert_b, expert_w_gate, expert_w_up, expert_w_down, norm_final):
    assert norm_mix.shape[0] == 1, "single-layer step"
    wts = _prepare_weights(norm_mix, w_in, mu_shift, ret_gn_w, rwkv_w0, rwkv_w2, rwkv_a0, rwkv_a2,
                           rwkv_g2, rwkv_k_k, rwkv_k_a, rwkv_r_k, rwkv_lnx_w, rwkv_lnx_b, w_out,
                           norm_ffn, router_group_w, router_group_b, router_expert_w,
                           router_expert_b, expert_w_gate, expert_w_up, expert_w_down, norm_final)

    bp, lp, _ = x_prompt.shape
    xp = x_prompt.reshape(bp * lp, D_MODEL)
    tile_p = math.gcd(lp, INPROJ_TILE)
    proj_p, tail_p = _inproj(xp, wts["norm_mix"], wts["w_in"], tile_p, tile_p, SUBLANES)
    o_ret_p, ret_p = _retention_prompt(proj_p, wts["ret_gn_w"], bp, lp)
    o_rwkv_p, rwkv_p = _rwkv_prompt(proj_p, wts, bp, lp)
    y_prompt = _finish(xp, o_ret_p, o_rwkv_p, proj_p, wts).reshape(bp, lp, D_MODEL)
    shift_p = tail_p.reshape(bp, lp // tile_p, SUBLANES, D_MODEL)[:, -1, -1]

    bs, ls, _ = x_sample.shape
    assert ls == 1, "sample group advances one token"
    xs = x_sample.reshape(bs, D_MODEL)
    both = jnp.concatenate([xs, state_shift[0]], axis=0)
    proj_s, xn_s = _inproj(both, wts["norm_mix"], wts["w_in"], 2 * bs, bs, bs)
    pos_s = PAST_LEN + jnp.arange(ls, dtype=F32)
    o_ret_s, ret_s = _retention_step(proj_s, state_ret[0], wts["ret_gn_w"], pos_s)
    o_rwkv_s, rwkv_s = _rwkv_step(proj_s, state_rwkv[0], wts)
    y_sample = _finish(xs, o_ret_s, o_rwkv_s, proj_s, wts).reshape(bs, ls, D_MODEL)

    return (y_prompt, y_sample, ret_p[None], rwkv_p[None], shift_p[None],
            ret_s[None], rwkv_s[None], xn_s)
```

```python
import functools
import math

import jax
import jax.numpy as jnp
from jax import lax
from jax.experimental import pallas as pl
from jax.experimental.pallas import tpu as pltpu

F32 = jnp.float32
BF16 = jnp.bfloat16

D_MODEL = 1024
LANES = 128
SUBLANES = 8
PAST_LEN = 16384
RET_HEADS = 8
RET_D = D_MODEL // RET_HEADS
RET_CHUNK = 128
ROPE_BASE = 10000.0
RET_GN_EPS = 1e-5
RWKV_HEAD = 64
RWKV_HEADS = D_MODEL // RWKV_HEAD
RWKV_PAIRS = RWKV_HEADS // 2
RWKV_CHUNK = 64
LORA_DECAY = 64
LORA_A = 64
LORA_G = 160
RWKV_GN_EPS = 64e-5
N_GROUPS = 4
EXPERTS_PER_GROUP = 8
N_EXPERTS = N_GROUPS * EXPERTS_PER_GROUP
D_EXPERT = 256
RMS_EPS = 1e-6

SRC_SHIFT = 4 * D_MODEL
SRC_WLO = SRC_SHIFT + 3 * D_MODEL
SRC_ALO = SRC_WLO + LORA_DECAY
SRC_GLO = SRC_ALO + LORA_A
SRC_GATE_A = SRC_GLO + LORA_G
COL_Q, COL_K, COL_V, COL_GSW = 0, 1024, 2048, 3072
COL_R, COL_KK, COL_VV = 4096, 5120, 6144
COL_GATE_A, COL_GATE_B = 7168, 8192
COL_LORA = 9216
LORA_W = 4 * LANES
N_PROJ = COL_LORA + LORA_W

VMEM_LIMIT = 48 * 1024 * 1024

INPROJ_TILE = 1024
INPROJ_SLABS = 4
INPROJ_ROWS = 128
RET_CHUNKS_PER_STEP = 4
RWKV_TILE = 256
RWKV_PAIRS_PER_STEP = 8
STEP_ROWS = 8
MERGE_TILE = 512
MERGE_ROWS = 128
MOE_TILE = 1024
MOE_BLOCK = 128
MOE_PERM_ROWS = 256
EXPERTS_PER_STEP = 4

_NN = (((1,), (0,)), ((), ()))
_NT = (((1,), (1,)), ((), ()))
_TN = (((0,), (0,)), ((), ()))


def _mm(a, b, dims=_NN):
    return lax.dot_general(a.astype(BF16), b.astype(BF16), dims,
                           preferred_element_type=F32)


def _params(sem):
    return pltpu.CompilerParams(dimension_semantics=sem,
                                vmem_limit_bytes=VMEM_LIMIT)


def _rms(x, gain):
    return x * lax.rsqrt(jnp.mean(x * x, axis=-1, keepdims=True) + RMS_EPS) * gain


def _split_bf16(x, terms):
    parts = []
    for _ in range(terms - 1):
        hi = x.astype(BF16)
        parts.append(hi)
        x = x - hi.astype(F32)
    parts.append(x.astype(BF16))
    return parts


def _lane_sum(x, ones):
    hi, lo = _split_bf16(x, 2)
    return (jnp.dot(hi, ones, preferred_element_type=F32)
            + jnp.dot(lo, ones, preferred_element_type=F32))


def _inproj_body(x_ref, g_ref, w_ref, o_ref, xn_ref, *, normed, keep):
    step = math.gcd(x_ref.shape[0], INPROJ_ROWS, normed)
    for r0 in range(0, x_ref.shape[0], step):
        x = x_ref[r0:r0 + step, :]
        if r0 < normed:
            x = _rms(x, g_ref[...])
            lo, hi = max(r0, normed - keep), r0 + step
            if lo < hi:
                xn_ref[0, 0, lo - (normed - keep):hi - (normed - keep), :] = x[lo - r0:, :]
        o_ref[r0:r0 + step, :] = jnp.dot(x.astype(BF16), w_ref[...], preferred_element_type=F32)


def _inproj(x, gain, w, tm, normed, keep):
    t = x.shape[0]
    n = w.shape[1]
    tn = n // INPROJ_SLABS
    assert keep <= normed <= tm
    proj, kept = pl.pallas_call(
        functools.partial(_inproj_body, normed=normed, keep=keep),
        grid=(n // tn, t // tm),
        in_specs=[pl.BlockSpec((tm, D_MODEL), lambda j, i: (i, 0)),
                  pl.BlockSpec((1, D_MODEL), lambda j, i: (0, 0)),
                  pl.BlockSpec((D_MODEL, tn), lambda j, i: (0, j))],
        out_specs=[pl.BlockSpec((tm, tn), lambda j, i: (i, j)),
                   pl.BlockSpec((1, 1, keep, D_MODEL), lambda j, i: (j, i, 0, 0))],
        out_shape=[jax.ShapeDtypeStruct((t, n), F32),
                   jax.ShapeDtypeStruct((n // tn, t // tm, keep, D_MODEL), F32)],
        compiler_params=_params(("arbitrary", "arbitrary")),
        name="inproj",
    )(x, gain, w)
    return proj, kept[0]


def _head_norm(x, eps):
    mu = jnp.mean(x, axis=-1, keepdims=True)
    xc = x - mu
    return xc * lax.rsqrt(jnp.mean(xc * xc, axis=-1, keepdims=True) + eps)


def _silu(x):
    return x * jax.nn.sigmoid(x)


def _rope(x, cos, sin_signed):
    return x * cos + pltpu.roll(x, RET_D // 2, 1) * sin_signed


def _ret_tables(chunk):
    log_gamma = jnp.log1p(-jnp.exp2(-5.0 - jnp.arange(RET_HEADS, dtype=F32)))
    idx = jnp.arange(chunk, dtype=F32)
    rel = idx[:, None] - idx[None, :]
    intra = jnp.where(rel[None] >= 0,
                      jnp.exp(jnp.maximum(rel, 0.0)[None] * log_gamma[:, None, None]), 0.0)
    q_dec = jnp.exp((idx[:, None] + 1.0) * log_gamma[None, :])
    k_dec = jnp.exp((chunk - 1.0 - idx)[None, :] * log_gamma[:, None]).T
    chunk_dec = jnp.exp(chunk * log_gamma)[None, :]
    widen = lambda t: jnp.repeat(t, RET_D, axis=1)
    return intra, widen(q_dec), widen(k_dec), widen(chunk_dec)


def _rope_tables(pos):
    half = RET_D // 2
    inv = ROPE_BASE ** (-jnp.arange(half, dtype=F32) / half)
    ang = pos[:, None] * inv[None, :]
    cos, sin = jnp.cos(ang), jnp.sin(ang)
    return jnp.concatenate([cos, cos], axis=1), jnp.concatenate([-sin, sin], axis=1)


def _ret_body(q_ref, k_ref, v_ref, g_ref, cos_ref, sin_ref, qd_ref, kd_ref, cd_ref,
              intra_ref, gnw_ref, o_ref, sout_ref, s_ref):
    c = pl.program_id(1)
    chunk = qd_ref.shape[0]

    @pl.when(c == 0)
    def _():
        s_ref[...] = jnp.zeros_like(s_ref)

    heads = [slice(h * RET_D, (h + 1) * RET_D) for h in range(RET_HEADS)]
    mean_w = jnp.full((RET_D, RET_D), 1.0 / RET_D, BF16)
    state = [s_ref[h] for h in range(RET_HEADS)]
    for r0 in range(0, q_ref.shape[0], chunk):
        rows = slice(r0, r0 + chunk)
        cos = cos_ref[rows, :]
        sin = sin_ref[rows, :]
        qr = [_rope(q_ref[rows, sl], cos, sin).astype(BF16) for sl in heads]
        kr = [_rope(k_ref[rows, sl], cos, sin) for sl in heads]
        vb = [v_ref[rows, sl].astype(BF16) for sl in heads]
        scores = [_mm(qr[h], kr[h], _NT) * intra_ref[h] for h in range(RET_HEADS)]
        cross = [_mm(qr[h], state[h]) * qd_ref[:, sl] for h, sl in enumerate(heads)]
        adds = [_mm(kr[h] * kd_ref[:, sl], vb[h], _TN) for h, sl in enumerate(heads)]
        out = [_mm(scores[h], vb[h]) + cross[h] for h in range(RET_HEADS)]
        cen = [out[h] - _lane_sum(out[h], mean_w) for h in range(RET_HEADS)]
        var = [_lane_sum(cen[h] * cen[h], mean_w) for h in range(RET_HEADS)]
        for h, sl in enumerate(heads):
            state[h] = state[h] * cd_ref[:, sl] + adds[h]
            o_ref[rows, sl] = (cen[h] * lax.rsqrt(var[h] + RET_GN_EPS) * gnw_ref[:, sl]
                               * _silu(g_ref[rows, sl]))
    for h in range(RET_HEADS):
        s_ref[h] = state[h]

    @pl.when(c == pl.num_programs(1) - 1)
    def _():
        sout_ref[0] = s_ref[...]


def _retention_prompt(proj, gn_w, batch, seq):
    chunk = math.gcd(seq, RET_CHUNK)
    tile = math.gcd(seq, RET_CHUNKS_PER_STEP * chunk)
    n_tiles = seq // tile
    intra, q_dec, k_dec, chunk_dec = _ret_tables(chunk)
    intra = intra * (RET_D ** -0.5)
    k_dec = k_dec * (RET_D ** -0.5)
    cos, sin = _rope_tables(jnp.arange(seq, dtype=F32))
    col = lambda j: pl.BlockSpec((tile, D_MODEL), lambda b, c, j=j: (b * n_tiles + c, j))
    const2 = lambda shape: pl.BlockSpec(shape, lambda b, c: (0, 0))
    state = pl.BlockSpec((1, RET_HEADS, RET_D, RET_D), lambda b, c: (b, 0, 0, 0))
    return pl.pallas_call(
        _ret_body,
        grid=(batch, n_tiles),
        in_specs=[col(COL_Q // D_MODEL), col(COL_K // D_MODEL), col(COL_V // D_MODEL),
                  col(COL_GSW // D_MODEL),
                  pl.BlockSpec((tile, RET_D), lambda b, c: (c, 0)),
                  pl.BlockSpec((tile, RET_D), lambda b, c: (c, 0)),
                  const2((chunk, D_MODEL)), const2((chunk, D_MODEL)), const2((1, D_MODEL)),
                  pl.BlockSpec((RET_HEADS, chunk, chunk), lambda b, c: (0, 0, 0)),
                  const2((1, D_MODEL))],
        out_specs=[pl.BlockSpec((tile, D_MODEL), lambda b, c: (b * n_tiles + c, 0)), state],
        out_shape=[jax.ShapeDtypeStruct((batch * seq, D_MODEL), F32),
                   jax.ShapeDtypeStruct((batch, RET_HEADS, RET_D, RET_D), F32)],
        scratch_shapes=[pltpu.VMEM((RET_HEADS, RET_D, RET_D), F32)],
        compiler_params=_params(("parallel", "arbitrary")),
        name="retention_prompt",
    )(proj, proj, proj, proj, cos, sin, q_dec, k_dec, chunk_dec, intra, gn_w)


def _ret_step_body(q_ref, k_ref, v_ref, g_ref, cos_ref, sin_ref, qd_ref, kd_ref, cd_ref,
                   gnw_ref, s_ref, o_ref, sout_ref, *, rows):
    cos = cos_ref[...]
    sin = sin_ref[...]
    row = lax.broadcasted_iota(jnp.int32, (rows, RET_D), 0)
    for h in range(RET_HEADS):
        sl = slice(h * RET_D, (h + 1) * RET_D)
        qr = _rope(q_ref[:, sl], cos, sin)
        kr = _rope(k_ref[:, sl], cos, sin) * (RET_D ** -0.5)
        v = v_ref[:, sl]
        kd = kr * kd_ref[:, sl]
        cd = cd_ref[:, sl]

        reads = [_mm(qr, s_ref[b, h]) for b in range(rows)]
        adds = [_mm(jnp.where(row == b, kd, 0.0), v, _TN) for b in range(rows)]
        cross = jnp.zeros((rows, RET_D), F32)
        for b in range(rows):
            sout_ref[b, h] = s_ref[b, h] * cd + adds[b]
            cross = jnp.where(row == b, reads[b], cross)
        out = jnp.sum(qr * kr, axis=-1, keepdims=True) * v + cross * qd_ref[:, sl]
        o_ref[:, sl] = _head_norm(out, RET_GN_EPS) * gnw_ref[:, sl] * _silu(g_ref[:, sl])


def _retention_step(proj, s0, gn_w, pos):
    batch = s0.shape[0]
    rows = STEP_ROWS
    _, q_dec, k_dec, chunk_dec = _ret_tables(1)
    cos, sin = _rope_tables(pos)
    col = lambda j: pl.BlockSpec((rows, D_MODEL), lambda i, j=j: (i, j))
    const2 = lambda shape: pl.BlockSpec(shape, lambda i: (0, 0))
    state = pl.BlockSpec((rows, RET_HEADS, RET_D, RET_D), lambda i: (i, 0, 0, 0))
    return pl.pallas_call(
        functools.partial(_ret_step_body, rows=rows),
        grid=(batch // rows,),
        in_specs=[col(COL_Q // D_MODEL), col(COL_K // D_MODEL), col(COL_V // D_MODEL),
                  col(COL_GSW // D_MODEL),
                  const2((1, RET_D)), const2((1, RET_D)),
                  const2((1, D_MODEL)), const2((1, D_MODEL)), const2((1, D_MODEL)),
                  const2((1, D_MODEL)), state],
        out_specs=[pl.BlockSpec((rows, D_MODEL), lambda i: (i, 0)), state],
        out_shape=[jax.ShapeDtypeStruct((batch, D_MODEL), F32),
                   jax.ShapeDtypeStruct(s0.shape, F32)],
        compiler_params=_params(("parallel",)),
        name="retention_step",
    )(proj, proj, proj, proj, cos, sin, q_dec, k_dec, chunk_dec, gn_w, s0)


def _head_ones(value=1.0):
    row = lax.broadcasted_iota(jnp.int32, (LANES, LANES), 0)
    col = lax.broadcasted_iota(jnp.int32, (LANES, LANES), 1)
    return jnp.where((row // RWKV_HEAD) == (col // RWKV_HEAD), value, 0.0).astype(BF16)


def _head_sum(x, ones_bd):
    blocks = [_lane_sum(x[:, j:j + LANES], ones_bd) for j in range(0, x.shape[1], LANES)]
    return blocks[0] if len(blocks) == 1 else jnp.concatenate(blocks, axis=1)


def _softplus(x):
    return jnp.maximum(x, 0.0) + jnp.log(1.0 + jnp.exp(-jnp.abs(x)))


def _lerp(cur, prev, mu):
    return cur + (prev - cur) * mu


def _rwkv_prep(r_, k_, v_, lora, w0, w2, a0, a2, g2, k_k, k_a, r_k, ones_bd):
    w_lo, a_lo, g_lo = lora[:, :LANES], lora[:, LANES:2 * LANES], lora[:, 2 * LANES:LORA_W]
    w_log = -_softplus(-(w0 + _mm(jnp.tanh(w_lo), w2))) - 0.5
    log_decay = -jnp.exp(w_log)
    a_sig = jax.nn.sigmoid(a0 + _mm(a_lo, a2))
    g = _mm(jax.nn.sigmoid(g_lo), g2)
    kk = k_ * k_k
    kk = kk * lax.rsqrt(jnp.maximum(_head_sum(kk * kk, ones_bd), 1e-24))
    kmod = k_ * (1.0 + (a_sig - 1.0) * k_a)
    bonus = _head_sum(r_ * kmod * r_k, ones_bd) * v_
    return log_decay, a_sig, g, kk, kmod, bonus


def _rwkv_out(y, bonus, g, lnx_w, lnx_b):
    mean_bd = _head_ones(1.0 / RWKV_HEAD)
    yc = y - _head_sum(y, mean_bd)
    var = _head_sum(yc * yc, mean_bd)
    return (yc * lax.rsqrt(var + RWKV_GN_EPS) * lnx_w + lnx_b + bonus) * g


def _block_diag(x, lane_first):
    return jnp.concatenate([jnp.where(lane_first, x, 0.0), jnp.where(lane_first, 0.0, x)], axis=0)


def _pair_masks():
    tok = lax.broadcasted_iota(jnp.int32, (RWKV_CHUNK, LANES), 0)
    src = lax.broadcasted_iota(jnp.int32, (RWKV_CHUNK, LANES), 1) % RWKV_CHUNK
    n = 2 * RWKV_CHUNK
    eye = lax.broadcasted_iota(jnp.int32, (n, n), 0) == lax.broadcasted_iota(jnp.int32, (n, n), 1)
    lane_first = lax.broadcasted_iota(jnp.int32, (1, LANES), 1) < RWKV_HEAD
    return lane_first, src < tok, src <= tok, src == tok, eye


def _rwkv_chunks(chunks, masks):
    lane_first, strict, incl, eye_wide, eye = masks
    n = RWKV_CHUNK
    rows = lambda x, y: jnp.concatenate([x, y], axis=0)
    cols = lambda x, y: jnp.concatenate([x, y], axis=1)
    bd = lambda x: _block_diag(x, lane_first)

    ops = []
    for r, lw, k, v, a, b, cum in chunks:
        p_inc = jnp.exp(cum)
        p_inv = jnp.exp(-cum)
        p_exc = jnp.exp(cum - lw)
        p_end = p_inc[RWKV_CHUNK - 1:RWKV_CHUNK, :]
        b_t = b * p_inv
        k_t = k * p_inv
        ops.append(dict(a_t=(a * p_exc).astype(BF16), r_t=(r * p_inc).astype(BF16),
                        b_t=bd(b_t.astype(BF16)), k_t=bd(k_t.astype(BF16)),
                        b_hat=bd(b_t * p_end), k_hat=bd(k_t * p_end),
                        v_bd=bd(v.astype(BF16)), p_end=p_end))

    for o in ops:
        prod = _mm(rows(o["a_t"], o["r_t"]), rows(o["b_t"], o["k_t"]), _NT)
        o["a_ab"] = jnp.where(strict, prod[:n, :LANES], 0.0)
        o["a_ak"] = jnp.where(strict, prod[:n, LANES:], 0.0).astype(BF16)
        o["r_bk"] = cols(jnp.where(incl, prod[n:, :LANES], 0.0),
                         jnp.where(incl, prod[n:, LANES:], 0.0)).astype(BF16)

    for o in ops:
        a_ab = o["a_ab"].astype(BF16)
        o["t_inv"] = jnp.where(eye_wide, 1.0, 0.0) + o["a_ab"]
        o["power"] = _mm(a_ab, bd(a_ab)).astype(BF16)
        o["av"] = _mm(o["a_ak"], o["v_bd"]).astype(BF16)
    for _ in range(int(math.log2(RWKV_CHUNK)) - 2):
        for o in ops:
            both = _mm(rows(o["power"], o["t_inv"].astype(BF16)), bd(o["power"]))
            o["power"] = both[:n].astype(BF16)
            o["t_inv"] = o["t_inv"] + both[n:]
    for o in ops:
        o["t_inv"] = o["t_inv"] + _mm(o["t_inv"], bd(o["power"]))
    out = []
    for o in ops:
        wu = _mm(o["t_inv"], cols(bd(o["a_t"]), bd(o["av"])))
        hat_t = rows(o["b_hat"], o["k_hat"]).T
        p_col = jnp.sum(jnp.where(eye, o["p_end"], 0.0), axis=1, keepdims=True)
        out.append((rows(wu[:, :LANES].astype(BF16), o["r_t"]), wu[:, LANES:],
                    o["r_bk"], o["v_bd"], hat_t.astype(BF16), p_col))
    return out


def _shift_rows(cur, first_row):
    rolled = pltpu.roll(cur, 1, 0)
    row = lax.broadcasted_iota(jnp.int32, cur.shape, 0)
    return jnp.where(row == 0, first_row, rolled)


def _rwkv_body(r_ref, k_ref, v_ref, lora_ref, mu_ref, mul_ref,
               w0_ref, w2_ref, a0_ref, a2_ref, g2_ref, kk_ref, ka_ref, rk_ref,
               lnw_ref, lnb_ref, o_ref, sout_ref,
               s_ref, carry_ref, carryl_ref, *, tile):
    t = pl.program_id(1)
    p = pl.program_id(2)
    masks = _pair_masks()
    ones_bd = _head_ones()
    pairs = r_ref.shape[1] // LANES

    @pl.when(t == 0)
    def _():
        for q in range(pairs):
            s_ref[p * pairs + q] = jnp.zeros((LANES, LANES), F32)
        carry_ref[p] = jnp.zeros(carry_ref.shape[1:], F32)

        @pl.when(p == 0)
        def _():
            carryl_ref[...] = jnp.zeros_like(carryl_ref)

    prev = carry_ref[p]
    cur_r, cur_k, cur_v, cur_l = r_ref[...], k_ref[...], v_ref[...], lora_ref[...]
    mu = mu_ref[0]
    r_ = _lerp(cur_r, _shift_rows(cur_r, prev[0:1]), mu[0:1])
    k_ = _lerp(cur_k, _shift_rows(cur_k, prev[1:2]), mu[1:2])
    v_ = _lerp(cur_v, _shift_rows(cur_v, prev[2:3]), mu[2:3])
    lo = _lerp(cur_l, _shift_rows(cur_l, carryl_ref[0:1, :]), mul_ref[...])
    carry_ref[p, 0:1, :] = cur_r[tile - 1:tile]
    carry_ref[p, 1:2, :] = cur_k[tile - 1:tile]
    carry_ref[p, 2:3, :] = cur_v[tile - 1:tile]

    @pl.when(p == pl.num_programs(2) - 1)
    def _():
        carryl_ref[0:1, :] = cur_l[tile - 1:tile]

    log_decay, a_sig, g, kk, kmod, bonus = _rwkv_prep(
        r_, k_, v_, lo, w0_ref[...], w2_ref[...], a0_ref[...], a2_ref[...], g2_ref[...],
        kk_ref[...], ka_ref[...], rk_ref[...], ones_bd)

    row = lax.broadcasted_iota(jnp.int32, (tile, tile), 0)
    col = lax.broadcasted_iota(jnp.int32, (tile, tile), 1)
    tri = jnp.where(((row // RWKV_CHUNK) == (col // RWKV_CHUNK)) & (col <= row), 1.0, 0.0).astype(BF16)
    cum = sum(jnp.dot(tri, part, preferred_element_type=F32) for part in _split_bf16(log_decay, 3))

    neg_kk = -kk
    kk_a = kk * a_sig
    n_chunks = tile // RWKV_CHUNK
    chunks = []
    for c in range(n_chunks):
        for q in range(pairs):
            at = (slice(c * RWKV_CHUNK, (c + 1) * RWKV_CHUNK), slice(q * LANES, (q + 1) * LANES))
            chunks.append((r_[at], log_decay[at], kmod[at], v_[at], neg_kk[at], kk_a[at], cum[at]))
    parts = _rwkv_chunks(chunks, masks)

    n = RWKV_CHUNK
    lane_first = masks[0]
    states = [s_ref[p * pairs + q] for q in range(pairs)]
    ys = [[] for _ in range(pairs)]
    for c in range(n_chunks):
        reads = [_mm(parts[c * pairs + q][0], states[q]) for q in range(pairs)]
        for q in range(pairs):
            _, u0, r_bk, v_bd, hat_t, p_col = parts[c * pairs + q]
            u_bd = _block_diag(reads[q][:n] + u0, lane_first).astype(BF16)
            uv = jnp.concatenate([u_bd, v_bd], axis=0)
            ys[q].append(reads[q][n:] + _mm(r_bk, uv))
            states[q] = states[q] * p_col + _mm(hat_t, uv)
    for q in range(pairs):
        s_ref[p * pairs + q] = states[q]

    @pl.when(t == pl.num_programs(1) - 1)
    def _():
        for q in range(pairs):
            head = 2 * (p * pairs + q)
            s_bd = states[q].T
            sout_ref[0, head] = s_bd[:RWKV_HEAD, :RWKV_HEAD]
            sout_ref[0, head + 1] = s_bd[RWKV_HEAD:, RWKV_HEAD:]

    y = jnp.concatenate([jnp.concatenate(yq, axis=0) for yq in ys], axis=1)
    o_ref[...] = _rwkv_out(y, bonus, g, lnw_ref[...], lnb_ref[...])


def _rwkv_prompt(proj, wts, batch, seq):
    tile = math.gcd(seq, RWKV_TILE)
    n_tiles = seq // tile
    width = RWKV_PAIRS_PER_STEP * LANES
    blk = lambda base: pl.BlockSpec((tile, width),
                                    lambda b, t, p, base=base: (b * n_tiles + t, base // width + p))
    per_pair = lambda rows: pl.BlockSpec((rows, width), lambda b, t, p: (0, p))
    in_specs = [
        blk(COL_R), blk(COL_KK), blk(COL_VV),
        pl.BlockSpec((tile, LORA_W), lambda b, t, p: (b * n_tiles + t, COL_LORA // LORA_W)),
        pl.BlockSpec((1, 3, width), lambda b, t, p: (0, 0, p)),
        pl.BlockSpec((1, LORA_W), lambda b, t, p: (0, 0)),
        per_pair(1), per_pair(LANES),
        per_pair(1), per_pair(LANES),
        per_pair(2 * LANES),
        per_pair(1), per_pair(1), per_pair(1),
        per_pair(1), per_pair(1),
    ]
    return pl.pallas_call(
        functools.partial(_rwkv_body, tile=tile),
        grid=(batch, n_tiles, RWKV_PAIRS // RWKV_PAIRS_PER_STEP),
        in_specs=in_specs,
        out_specs=[pl.BlockSpec((tile, width), lambda b, t, p: (b * n_tiles + t, p)),
                   pl.BlockSpec((1, RWKV_HEADS, RWKV_HEAD, RWKV_HEAD), lambda b, t, p: (b, 0, 0, 0))],
        out_shape=[jax.ShapeDtypeStruct((batch * seq, D_MODEL), F32),
                   jax.ShapeDtypeStruct((batch, RWKV_HEADS, RWKV_HEAD, RWKV_HEAD), F32)],
        scratch_shapes=[pltpu.VMEM((RWKV_PAIRS, LANES, LANES), F32),
                        pltpu.VMEM((RWKV_PAIRS // RWKV_PAIRS_PER_STEP, SUBLANES, width), F32),
                        pltpu.VMEM((SUBLANES, LORA_W), F32)],
        compiler_params=_params(("parallel", "arbitrary", "arbitrary")),
        name="rwkv_prompt",
    )(proj, proj, proj, proj, wts["mu3"], wts["mu_lora"],
      wts["w0"], wts["w2"], wts["a0"], wts["a2"], wts["g2"],
      wts["k_k"], wts["k_a"], wts["r_k"], wts["lnx_w"], wts["lnx_b"])


def _rwkv_step_body(r_ref, k_ref, v_ref, lora_ref, pr_ref, pk_ref, pv_ref, plora_ref,
                    mu_ref, mul_ref, w0_ref, w2_ref, a0_ref, a2_ref, g2_ref,
                    kk_ref, ka_ref, rk_ref, lnw_ref, lnb_ref, s_ref,
                    o_ref, sout_ref, *, rows):
    ones_bd = _head_ones()
    mu = mu_ref[0]
    r_ = _lerp(r_ref[...], pr_ref[...], mu[0:1])
    k_ = _lerp(k_ref[...], pk_ref[...], mu[1:2])
    v_ = _lerp(v_ref[...], pv_ref[...], mu[2:3])
    lo = _lerp(lora_ref[...], plora_ref[...], mul_ref[...])
    log_decay, a_sig, g, kk, kmod, bonus = _rwkv_prep(
        r_, k_, v_, lo, w0_ref[...], w2_ref[...], a0_ref[...], a2_ref[...], g2_ref[...],
        kk_ref[...], ka_ref[...], rk_ref[...], ones_bd)
    decay = jnp.exp(log_decay)
    neg_kk = -kk
    kk_a = kk * a_sig
    decay_r = decay * r_
    row = lax.broadcasted_iota(jnp.int32, (rows, RWKV_HEAD), 0)

    heads = [slice(h * RWKV_HEAD, (h + 1) * RWKV_HEAD) for h in range(RWKV_HEADS)]
    reads = []
    for h, sl in enumerate(heads):
        lhs = jnp.concatenate([neg_kk[:, sl], decay_r[:, sl]], axis=0)
        reads.append([_mm(lhs, s_ref[b, h], _NT) for b in range(rows)])
    ys = []
    for h, sl in enumerate(heads):
        w_h, r_h, v_h, k_h, b_h = decay[:, sl], r_[:, sl], v_[:, sl], kmod[:, sl], kk_a[:, sl]
        rhs = jnp.concatenate([b_h, k_h], axis=0)
        sa_all = swr_all = jnp.zeros((rows, RWKV_HEAD), F32)
        adds = []
        for b in range(rows):
            mine = row == b
            sa, swr = reads[h][b][:rows], reads[h][b][rows:]
            left = jnp.concatenate([jnp.where(mine, sa, 0.0), jnp.where(mine, v_h, 0.0)], axis=0)
            adds.append(_mm(left, rhs, _TN))
            sa_all = jnp.where(mine, sa, sa_all)
            swr_all = jnp.where(mine, swr, swr_all)
        for b in range(rows):
            sout_ref[b, h] = s_ref[b, h] * w_h[b:b + 1] + adds[b]
        ys.append(swr_all + sa_all * jnp.sum(b_h * r_h, axis=-1, keepdims=True)
                  + v_h * jnp.sum(k_h * r_h, axis=-1, keepdims=True))
    y = jnp.concatenate(ys, axis=1)
    o_ref[...] = _rwkv_out(y, bonus, g, lnw_ref[...], lnb_ref[...])


def _rwkv_step(proj, s0, wts):
    batch = s0.shape[0]
    rows = STEP_ROWS
    shift = batch // rows
    wide = lambda base, off=0: pl.BlockSpec((rows, D_MODEL),
                                            lambda i, base=base, off=off: (i + off, base // D_MODEL))
    lora = pl.BlockSpec((rows, LORA_W), lambda i: (i, COL_LORA // LORA_W))
    lora_prev = pl.BlockSpec((rows, LORA_W), lambda i: (i + shift, COL_LORA // LORA_W))
    full = lambda shape: pl.BlockSpec(shape, lambda i: (0,) * len(shape))
    state = pl.BlockSpec((rows, RWKV_HEADS, RWKV_HEAD, RWKV_HEAD), lambda i: (i, 0, 0, 0))
    vec = full((1, D_MODEL))
    return pl.pallas_call(
        functools.partial(_rwkv_step_body, rows=rows),
        grid=(batch // rows,),
        in_specs=[wide(COL_R), wide(COL_KK), wide(COL_VV), lora,
                  wide(COL_R, shift), wide(COL_KK, shift), wide(COL_VV, shift), lora_prev,
                  full((1, 3, D_MODEL)), full((1, LORA_W)),
                  vec, full((LANES, D_MODEL)), vec, full((LANES, D_MODEL)),
                  full((2 * LANES, D_MODEL)), vec, vec, vec, vec, vec, state],
        out_specs=[pl.BlockSpec((rows, D_MODEL), lambda i: (i, 0)), state],
        out_shape=[jax.ShapeDtypeStruct((batch, D_MODEL), F32),
                   jax.ShapeDtypeStruct(s0.shape, F32)],
        compiler_params=_params(("parallel",)),
        name="rwkv_step",
    )(proj, proj, proj, proj, proj, proj, proj, proj,
      wts["mu3"], wts["mu_lora"], wts["w0"], wts["w2"], wts["a0"], wts["a2"], wts["g2"],
      wts["k_k"], wts["k_a"], wts["r_k"], wts["lnx_w"], wts["lnx_b"], s0)


def _merge_body(x_ref, oret_ref, orwkv_ref, ga_ref, gb_ref, wout_ref, nffn_ref, rw_ref, rb_ref,
                x1_ref, comb_ref, cnt_ref):
    step = math.gcd(x_ref.shape[0], MERGE_ROWS)
    chunks = [slice(r0, r0 + step) for r0 in range(0, x_ref.shape[0], step)]
    each = lambda fn, *cols: [fn(*args) for args in zip(*cols)]
    row_max = lambda v: jnp.max(v, axis=-1, keepdims=True)
    row_min = lambda v: jnp.min(v, axis=-1, keepdims=True)
    row_sum = lambda v: jnp.sum(v, axis=-1, keepdims=True)
    lane = lax.broadcasted_iota(jnp.int32, (step, LANES), 1)
    is_group = lane < N_GROUPS
    neg_inf = -jnp.inf

    merged = [jax.nn.sigmoid(ga_ref[r, :]) * oret_ref[r, :]
              + jax.nn.sigmoid(gb_ref[r, :]) * orwkv_ref[r, :] for r in chunks]
    x1 = [x_ref[r, :] + _mm(m, wout_ref[...]) for r, m in zip(chunks, merged)]
    xn = each(lambda v: _rms(v, nffn_ref[...]), x1)
    for r, a in zip(chunks, x1):
        x1_ref[r, :] = a

    logits = each(lambda v: _mm(v, rw_ref[...]) + rb_ref[...], xn)
    g_max = each(lambda l: row_max(jnp.where(is_group, l, neg_inf)), logits)
    g_sel = each(lambda l, m: row_min(jnp.where(is_group & (l == m), lane, LANES)), logits, g_max)
    g_w = each(lambda l, m: 1.0 / row_sum(jnp.where(is_group, jnp.exp(l - m), 0.0)), logits, g_max)
    first = each(lambda g: N_GROUPS + EXPERTS_PER_GROUP * g, g_sel)
    in_group = each(lambda f: (lane >= f) & (lane < f + EXPERTS_PER_GROUP), first)
    e_max = each(lambda l, ing: row_max(jnp.where(ing, l, neg_inf)), logits, in_group)
    e_exp = each(lambda l, ing, m: jnp.where(ing, jnp.exp(l - m), 0.0), logits, in_group, e_max)
    prob = each(lambda e, ing: jnp.where(ing, e / row_sum(e), -1.0), e_exp, in_group)
    p1 = each(row_max, prob)
    i1 = each(lambda p, m: row_min(jnp.where(p == m, lane, LANES)), prob, p1)
    rest = each(lambda p, i: jnp.where(lane == i, -1.0, p), prob, i1)
    p2 = each(row_max, rest)
    i2 = each(lambda p, m: row_min(jnp.where(p == m, lane, LANES)), rest, p2)
    lane8 = lax.broadcasted_iota(jnp.int32, (SUBLANES, LANES), 1)
    sizes = jnp.zeros((SUBLANES, LANES), jnp.int32)
    for c, r in enumerate(chunks):
        denom = p1[c] + p2[c]
        comb = (jnp.where(lane == i1[c] - N_GROUPS, g_w[c] * p1[c] / denom, 0.0)
                + jnp.where(lane == i2[c] - N_GROUPS, g_w[c] * p2[c] / denom, 0.0))
        comb_ref[r, :] = jnp.where(lane == N_EXPERTS, g_sel[c].astype(F32), comb)
        for g in range(N_GROUPS):
            n_g = jnp.sum(jnp.where(g_sel[c] == g, 1, 0), axis=0, keepdims=True)
            sizes = sizes + jnp.where(lane8 == g, n_g, 0)
    cnt_ref[0] = sizes


def _merge(x, o_ret, o_rwkv, proj, wts):
    t = x.shape[0]
    tm = math.gcd(t, MERGE_TILE)
    row = lambda: pl.BlockSpec((tm, D_MODEL), lambda i: (i, 0))
    full = lambda shape: pl.BlockSpec(shape, lambda i: (0, 0))
    return pl.pallas_call(
        _merge_body,
        grid=(t // tm,),
        in_specs=[row(), row(), row(),
                  pl.BlockSpec((tm, D_MODEL), lambda i: (i, COL_GATE_A // D_MODEL)),
                  pl.BlockSpec((tm, D_MODEL), lambda i: (i, COL_GATE_B // D_MODEL)),
                  full((D_MODEL, D_MODEL)), full((1, D_MODEL)),
                  full((D_MODEL, LANES)), full((1, LANES))],
        out_specs=[row(), pl.BlockSpec((tm, LANES), lambda i: (i, 0)),
                   pl.BlockSpec((1, SUBLANES, LANES), lambda i: (i, 0, 0))],
        out_shape=[jax.ShapeDtypeStruct((t, D_MODEL), F32),
                   jax.ShapeDtypeStruct((t, LANES), F32),
                   jax.ShapeDtypeStruct((t // tm, SUBLANES, LANES), jnp.int32)],
        compiler_params=_params(("parallel",)),
        name="merge_router",
    )(x, o_ret, o_rwkv, proj, proj, wts["w_out"], wts["norm_ffn"], wts["router_w"], wts["router_b"])


def _moe_body(cnt_ref, x1_ref, comb_ref, wg_ref, wu_ref, wd_ref, nffn_ref, nf_ref, o_ref,
              acc_ref, xs_ref, cs_ref, pt_ref, *, ratio):
    i = pl.program_id(0)
    s = pl.program_id(1)
    tm = x1_ref.shape[0]
    group = s // (EXPERTS_PER_GROUP // EXPERTS_PER_STEP)

    sizes = []
    for g in range(N_GROUPS):
        n = cnt_ref[i * ratio * N_GROUPS + g]
        for k in range(1, ratio):
            n = n + cnt_ref[(i * ratio + k) * N_GROUPS + g]
        sizes.append(n)
    starts = [jnp.int32(0)]
    for g in range(N_GROUPS - 1):
        starts.append(starts[-1] + sizes[g])

    @pl.when(s == 0)
    def _():
        comb = comb_ref[...]
        g_row = comb.T[N_EXPERTS:N_EXPERTS + 1, :]
        gid = lax.broadcasted_iota(jnp.int32, (SUBLANES, tm), 0)
        onehot = jnp.where(g_row.astype(jnp.int32) == gid, 1.0, 0.0)
        step = math.gcd(tm, MOE_PERM_ROWS)
        row = lax.broadcasted_iota(jnp.int32, (step, tm), 0)
        col = lax.broadcasted_iota(jnp.int32, (step, tm), 1)
        rank = jnp.zeros((SUBLANES, tm), F32)
        for r0 in range(0, tm, step):
            earlier = jnp.where(row + r0 < col, 1.0, 0.0).astype(BF16)
            rank = rank + jnp.dot(onehot[:, r0:r0 + step].astype(BF16), earlier,
                                  preferred_element_type=F32)
        start_col = jnp.zeros((SUBLANES, 1), F32)
        gid_col = lax.broadcasted_iota(jnp.int32, (SUBLANES, 1), 0)
        for g in range(1, N_GROUPS):
            start_col = jnp.where(gid_col == g, starts[g].astype(F32), start_col)
        pos_row = jnp.sum(onehot * (rank + start_col), axis=0, keepdims=True)
        pos_row_i = pos_row.astype(jnp.int32)
        xn = _rms(x1_ref[...], nffn_ref[...]).astype(BF16)
        comb_parts = _split_bf16(comb, 3)
        for r0 in range(0, tm, step):
            perm = jnp.where(row + r0 == pos_row_i, 1.0, 0.0).astype(BF16)
            xs_ref[r0:r0 + step, :] = jnp.dot(perm, xn, preferred_element_type=F32).astype(BF16)
            cs_ref[r0:r0 + step, :] = sum(jnp.dot(perm, part, preferred_element_type=F32)
                                          for part in comb_parts)
        pos_col_i = jnp.broadcast_to(pos_row, (LANES, tm)).T.astype(jnp.int32)
        lane = lax.broadcasted_iota(jnp.int32, (tm, LANES), 1)
        for j in range(tm // LANES):
            pt_ref[:, j * LANES:(j + 1) * LANES] = jnp.where(
                lane + j * LANES == pos_col_i, 1.0, 0.0).astype(BF16)
        acc_ref[...] = jnp.zeros_like(acc_ref)

    start, size = starts[N_GROUPS - 1], sizes[N_GROUPS - 1]
    for g in range(N_GROUPS - 2, -1, -1):
        start = jnp.where(group == g, starts[g], start)
        size = jnp.where(group == g, sizes[g], size)
    blk = math.gcd(tm, MOE_BLOCK)
    first = start // blk
    last = jnp.where(size > 0, (start + size + blk - 1) // blk, first)
    lane = lax.broadcasted_iota(jnp.int32, (blk, LANES), 1)

    def run_blocks(js):
        at = [pl.ds(pl.multiple_of(j * blk, blk), blk) for j in js]
        xb = [xs_ref[r, :] for r in at]
        cw = [cs_ref[r, :] for r in at]
        hs = [[] for _ in js]
        for k in range(EXPERTS_PER_STEP):
            for i in range(len(js)):
                weight = jnp.sum(jnp.where(lane == s * EXPERTS_PER_STEP + k, cw[i], 0.0),
                                 axis=-1, keepdims=True)
                hs[i].append(_silu(_mm(xb[i], wg_ref[k])) * _mm(xb[i], wu_ref[k]) * weight)
        for i, r in enumerate(at):
            total = _mm(hs[i][0], wd_ref[0])
            for k in range(1, EXPERTS_PER_STEP):
                total = total + _mm(hs[i][k], wd_ref[k])
            acc_ref[r, :] += total

    pairs = (last - first) // 2

    def two_blocks(i, carry):
        run_blocks([first + 2 * i, first + 2 * i + 1])
        return carry

    lax.fori_loop(0, pairs, two_blocks, 0)

    @pl.when(first + 2 * pairs < last)
    def _():
        run_blocks([last - 1])

    @pl.when(s == pl.num_programs(1) - 1)
    def _():
        parts = _split_bf16(acc_ref[...], 2)
        step = math.gcd(tm, MOE_PERM_ROWS)
        for r0 in range(0, tm, step):
            moe = sum(jnp.dot(pt_ref[r0:r0 + step, :], part, preferred_element_type=F32)
                      for part in parts)
            o_ref[r0:r0 + step, :] = _rms(x1_ref[r0:r0 + step, :] + moe, nf_ref[...])


def _moe(x1, comb, counts, wts):
    t = x1.shape[0]
    tm = math.gcd(t, MOE_TILE)
    assert tm % LANES == 0, "token count must be a multiple of the lane width"
    ratio = tm // math.gcd(t, MERGE_TILE)
    row = lambda w: pl.BlockSpec((tm, w), lambda i, s, cnt: (i, 0))
    expert = lambda a, b: pl.BlockSpec((EXPERTS_PER_STEP, a, b), lambda i, s, cnt: (s, 0, 0))
    vec = pl.BlockSpec((1, D_MODEL), lambda i, s, cnt: (0, 0))
    grid_spec = pltpu.PrefetchScalarGridSpec(
        num_scalar_prefetch=1,
        grid=(t // tm, N_EXPERTS // EXPERTS_PER_STEP),
        in_specs=[row(D_MODEL), row(LANES),
                  expert(D_MODEL, D_EXPERT), expert(D_MODEL, D_EXPERT), expert(D_EXPERT, D_MODEL),
                  vec, vec],
        out_specs=row(D_MODEL),
        scratch_shapes=[pltpu.VMEM((tm, D_MODEL), F32),
                        pltpu.VMEM((tm, D_MODEL), BF16),
                        pltpu.VMEM((tm, LANES), F32),
                        pltpu.VMEM((tm, tm), BF16)])
    return pl.pallas_call(
        functools.partial(_moe_body, ratio=ratio),
        grid_spec=grid_spec,
        out_shape=jax.ShapeDtypeStruct((t, D_MODEL), F32),
        compiler_params=_params(("parallel", "arbitrary")),
        name="moe",
    )(counts, x1, comb, wts["w_gate"], wts["w_up"], wts["w_down"], wts["norm_ffn"], wts["norm_final"])


def _pad_to(x, size, axis):
    pad = [(0, 0)] * x.ndim
    pad[axis] = (0, size - x.shape[axis])
    return jnp.pad(x, pad)


def _prepare_weights(norm_mix, w_in, mu_shift, ret_gn_w, rwkv_w0, rwkv_w2, rwkv_a0, rwkv_a2,
                     rwkv_g2, rwkv_k_k, rwkv_k_a, rwkv_r_k, rwkv_lnx_w, rwkv_lnx_b, w_out,
                     norm_ffn, router_group_w, router_group_b, router_expert_w, router_expert_b,
                     expert_w_gate, expert_w_up, expert_w_down, norm_final):
    def lora_cols(t):
        return jnp.concatenate([_pad_to(t[..., SRC_WLO:SRC_ALO], LANES, -1),
                                _pad_to(t[..., SRC_ALO:SRC_GLO], LANES, -1),
                                _pad_to(t[..., SRC_GLO:SRC_GATE_A], 2 * LANES, -1)], axis=-1)

    w = w_in[0]
    w_in_p = jnp.concatenate([w[:, :SRC_WLO], w[:, SRC_GATE_A:], lora_cols(w)], axis=1).astype(BF16)
    mu = jnp.concatenate([jnp.zeros((SRC_SHIFT,), F32), mu_shift[0]])
    row = lambda v: v.reshape(1, -1)
    router_w = _pad_to(jnp.concatenate([router_group_w[0], router_expert_w[0]], axis=1), LANES, 1)
    router_b = _pad_to(jnp.concatenate([router_group_b[0], router_expert_b[0]]), LANES, 0)
    return dict(
        w_in=w_in_p, norm_mix=row(norm_mix[0]), ret_gn_w=row(ret_gn_w[0]),
        mu3=mu[SRC_SHIFT:SRC_WLO].reshape(1, 3, D_MODEL), mu_lora=lora_cols(mu).reshape(1, LORA_W),
        w0=row(rwkv_w0[0]), w2=_pad_to(rwkv_w2[0], LANES, 0),
        a0=row(rwkv_a0[0]), a2=_pad_to(rwkv_a2[0], LANES, 0),
        g2=_pad_to(rwkv_g2[0], 2 * LANES, 0),
        k_k=row(rwkv_k_k[0]), k_a=row(rwkv_k_a[0]), r_k=row(rwkv_r_k[0]),
        lnx_w=row(rwkv_lnx_w[0]), lnx_b=row(rwkv_lnx_b[0]),
        w_out=w_out[0].astype(BF16), norm_ffn=row(norm_ffn[0]),
        router_w=router_w, router_b=row(router_b),
        w_gate=expert_w_gate[0].astype(BF16), w_up=expert_w_up[0].astype(BF16),
        w_down=expert_w_down[0].astype(BF16), norm_final=row(norm_final))


def _finish(x, o_ret, o_rwkv, proj, wts):
    x1, comb, sizes = _merge(x, o_ret, o_rwkv, proj, wts)
    counts = sizes[:, 0, :N_GROUPS].reshape(-1)
    return _moe(x1, comb, counts, wts)


def kernel(x_prompt, x_sample, state_ret, state_rwkv, state_shift, norm_mix, w_in, mu_shift, ret_gn_w, rwkv_w0, rwkv_w2, rwkv_a0, rwkv_a2, rwkv_g2, rwkv_k_k, rwkv_k_a, rwkv_r_k, rwkv_lnx_w, rwkv_lnx_b, w_out, norm_ffn, router_group_w, router_group_b, router_expert_w, router_expert_b, expert_w_gate, expert_w_up, expert_w_down, norm_final):
    assert norm_mix.shape[0] == 1, "single-layer step"
    wts = _prepare_weights(norm_mix, w_in, mu_shift, ret_gn_w, rwkv_w0, rwkv_w2, rwkv_a0, rwkv_a2,
                           rwkv_g2, rwkv_k_k, rwkv_k_a, rwkv_r_k, rwkv_lnx_w, rwkv_lnx_b, w_out,
                           norm_ffn, router_group_w, router_group_b, router_expert_w,
                           router_expert_b, expert_w_gate, expert_w_up, expert_w_down, norm_final)

    bp, lp, _ = x_prompt.shape
    xp = x_prompt.reshape(bp * lp, D_MODEL)
    tile_p = math.gcd(lp, INPROJ_TILE)
    proj_p, tail_p = _inproj(xp, wts["norm_mix"], wts["w_in"], tile_p, tile_p, SUBLANES)
    o_ret_p, ret_p = _retention_prompt(proj_p, wts["ret_gn_w"], bp, lp)
    o_rwkv_p, rwkv_p = _rwkv_prompt(proj_p, wts, bp, lp)
    y_prompt = _finish(xp, o_ret_p, o_rwkv_p, proj_p, wts).reshape(bp, lp, D_MODEL)
    shift_p = tail_p.reshape(bp, lp // tile_p, SUBLANES, D_MODEL)[:, -1, -1]

    bs, ls, _ = x_sample.shape
    assert ls == 1, "sample group advances one token"
    xs = x_sample.reshape(bs, D_MODEL)
    both = jnp.concatenate([xs, state_shift[0]], axis=0)
    proj_s, xn_s = _inproj(both, wts["norm_mix"], wts["w_in"], 2 * bs, bs, bs)
    pos_s = PAST_LEN + jnp.arange(ls, dtype=F32)
    o_ret_s, ret_s = _retention_step(proj_s, state_ret[0], wts["ret_gn_w"], pos_s)
    o_rwkv_s, rwkv_s = _rwkv_step(proj_s, state_rwkv[0], wts)
    y_sample = _finish(xs, o_ret_s, o_rwkv_s, proj_s, wts).reshape(bs, ls, D_MODEL)

    return (y_prompt, y_sample, ret_p[None], rwkv_p[None], shift_p[None],
            ret_s[None], rwkv_s[None], xn_s)
```

```python
import functools
import math

import jax
import jax.numpy as jnp
from jax import lax
from jax.experimental import pallas as pl
from jax.experimental.pallas import tpu as pltpu

F32 = jnp.float32
BF16 = jnp.bfloat16

D_MODEL = 1024
LANES = 128
SUBLANES = 8
PAST_LEN = 16384
RET_HEADS = 8
RET_D = D_MODEL // RET_HEADS
RET_CHUNK = 128
ROPE_BASE = 10000.0
RET_GN_EPS = 1e-5
RWKV_HEAD = 64
RWKV_HEADS = D_MODEL // RWKV_HEAD
RWKV_PAIRS = RWKV_HEADS // 2
RWKV_CHUNK = 64
LORA_DECAY = 64
LORA_A = 64
LORA_G = 160
RWKV_GN_EPS = 64e-5
N_GROUPS = 4
EXPERTS_PER_GROUP = 8
N_EXPERTS = N_GROUPS * EXPERTS_PER_GROUP
D_EXPERT = 256
RMS_EPS = 1e-6

SRC_SHIFT = 4 * D_MODEL
SRC_WLO = SRC_SHIFT + 3 * D_MODEL
SRC_ALO = SRC_WLO + LORA_DECAY
SRC_GLO = SRC_ALO + LORA_A
SRC_GATE_A = SRC_GLO + LORA_G
COL_Q, COL_K, COL_V, COL_GSW = 0, 1024, 2048, 3072
COL_R, COL_KK, COL_VV = 4096, 5120, 6144
COL_GATE_A, COL_GATE_B = 7168, 8192
COL_LORA = 9216
LORA_W = 4 * LANES
N_PROJ = COL_LORA + LORA_W

VMEM_LIMIT = 48 * 1024 * 1024

INPROJ_TILE = 1024
INPROJ_SLABS = 4
INPROJ_ROWS = 128
RET_CHUNKS_PER_STEP = 4
RWKV_TILE = 256
RWKV_PAIRS_PER_STEP = 8
STEP_ROWS = 8
MERGE_TILE = 512
MERGE_ROWS = 128
MOE_TILE = 1024
MOE_BLOCK = 128
MOE_PERM_ROWS = 256
EXPERTS_PER_STEP = 4

_NN = (((1,), (0,)), ((), ()))
_NT = (((1,), (1,)), ((), ()))
_TN = (((0,), (0,)), ((), ()))


def _mm(a, b, dims=_NN):
    return lax.dot_general(a.astype(BF16), b.astype(BF16), dims,
                           preferred_element_type=F32)


def _params(sem):
    return pltpu.CompilerParams(dimension_semantics=sem,
                                vmem_limit_bytes=VMEM_LIMIT)


def _rms(x, gain):
    return x * lax.rsqrt(jnp.mean(x * x, axis=-1, keepdims=True) + RMS_EPS) * gain


def _split_bf16(x, terms):
    parts = []
    for _ in range(terms - 1):
        hi = x.astype(BF16)
        parts.append(hi)
        x = x - hi.astype(F32)
    parts.append(x.astype(BF16))
    return parts


def _lane_sum(x, ones):
    hi, lo = _split_bf16(x, 2)
    return (jnp.dot(hi, ones, preferred_element_type=F32)
            + jnp.dot(lo, ones, preferred_element_type=F32))


def _inproj_body(x_ref, g_ref, w_ref, o_ref, xn_ref, *, normed, keep):
    step = math.gcd(x_ref.shape[0], INPROJ_ROWS, normed)
    for r0 in range(0, x_ref.shape[0], step):
        x = x_ref[r0:r0 + step, :]
        if r0 < normed:
            x = _rms(x, g_ref[...])
            lo, hi = max(r0, normed - keep), r0 + step
            if lo < hi:
                xn_ref[0, 0, lo - (normed - keep):hi - (normed - keep), :] = x[lo - r0:, :]
        o_ref[r0:r0 + step, :] = jnp.dot(x.astype(BF16), w_ref[...], preferred_element_type=F32)


def _inproj(x, gain, w, tm, normed, keep):
    t = x.shape[0]
    n = w.shape[1]
    tn = n // INPROJ_SLABS
    assert keep <= normed <= tm
    proj, kept = pl.pallas_call(
        functools.partial(_inproj_body, normed=normed, keep=keep),
        grid=(n // tn, t // tm),
        in_specs=[pl.BlockSpec((tm, D_MODEL), lambda j, i: (i, 0)),
                  pl.BlockSpec((1, D_MODEL), lambda j, i: (0, 0)),
                  pl.BlockSpec((D_MODEL, tn), lambda j, i: (0, j))],
        out_specs=[pl.BlockSpec((tm, tn), lambda j, i: (i, j)),
                   pl.BlockSpec((1, 1, keep, D_MODEL), lambda j, i: (j, i, 0, 0))],
        out_shape=[jax.ShapeDtypeStruct((t, n), F32),
                   jax.ShapeDtypeStruct((n // tn, t // tm, keep, D_MODEL), F32)],
        compiler_params=_params(("arbitrary", "arbitrary")),
        name="inproj",
    )(x, gain, w)
    return proj, kept[0]


def _head_norm(x, eps):
    mu = jnp.mean(x, axis=-1, keepdims=True)
    xc = x - mu
    return xc * lax.rsqrt(jnp.mean(xc * xc, axis=-1, keepdims=True) + eps)


def _silu(x):
    return x * jax.nn.sigmoid(x)


def _rope(x, cos, sin_signed):
    return x * cos + pltpu.roll(x, RET_D // 2, 1) * sin_signed


def _ret_tables(chunk):
    log_gamma = jnp.log1p(-jnp.exp2(-5.0 - jnp.arange(RET_HEADS, dtype=F32)))
    idx = jnp.arange(chunk, dtype=F32)
    rel = idx[:, None] - idx[None, :]
    intra = jnp.where(rel[None] >= 0,
                      jnp.exp(jnp.maximum(rel, 0.0)[None] * log_gamma[:, None, None]), 0.0)
    q_dec = jnp.exp((idx[:, None] + 1.0) * log_gamma[None, :])
    k_dec = jnp.exp((chunk - 1.0 - idx)[None, :] * log_gamma[:, None]).T
    chunk_dec = jnp.exp(chunk * log_gamma)[None, :]
    widen = lambda t: jnp.repeat(t, RET_D, axis=1)
    return intra, widen(q_dec), widen(k_dec), widen(chunk_dec)


def _rope_tables(pos):
    half = RET_D // 2
    inv = ROPE_BASE ** (-jnp.arange(half, dtype=F32) / half)
    ang = pos[:, None] * inv[None, :]
    cos, sin = jnp.cos(ang), jnp.sin(ang)
    return jnp.concatenate([cos, cos], axis=1), jnp.concatenate([-sin, sin], axis=1)


def _ret_body(q_ref, k_ref, v_ref, g_ref, cos_ref, sin_ref, qd_ref, kd_ref, cd_ref,
              intra_ref, gnw_ref, o_ref, sout_ref, s_ref):
    c = pl.program_id(1)
    chunk = qd_ref.shape[0]

    @pl.when(c == 0)
    def _():
        s_ref[...] = jnp.zeros_like(s_ref)

    heads = [slice(h * RET_D, (h + 1) * RET_D) for h in range(RET_HEADS)]
    mean_w = jnp.full((RET_D, RET_D), 1.0 / RET_D, BF16)
    state = [s_ref[h] for h in range(RET_HEADS)]
    for r0 in range(0, q_ref.shape[0], chunk):
        rows = slice(r0, r0 + chunk)
        cos = cos_ref[rows, :]
        sin = sin_ref[rows, :]
        qr = [_rope(q_ref[rows, sl], cos, sin).astype(BF16) for sl in heads]
        kr = [_rope(k_ref[rows, sl], cos, sin) for sl in heads]
        vb = [v_ref[rows, sl].astype(BF16) for sl in heads]
        scores = [_mm(qr[h], kr[h], _NT) * intra_ref[h] for h in range(RET_HEADS)]
        cross = [_mm(qr[h], state[h]) * qd_ref[:, sl] for h, sl in enumerate(heads)]
        adds = [_mm(kr[h] * kd_ref[:, sl], vb[h], _TN) for h, sl in enumerate(heads)]
        out = [_mm(scores[h], vb[h]) + cross[h] for h in range(RET_HEADS)]
        cen = [out[h] - _lane_sum(out[h], mean_w) for h in range(RET_HEADS)]
        var = [_lane_sum(cen[h] * cen[h], mean_w) for h in range(RET_HEADS)]
        for h, sl in enumerate(heads):
            state[h] = state[h] * cd_ref[:, sl] + adds[h]
            o_ref[rows, sl] = (cen[h] * lax.rsqrt(var[h] + RET_GN_EPS) * gnw_ref[:, sl]
                               * _silu(g_ref[rows, sl]))
    for h in range(RET_HEADS):
        s_ref[h] = state[h]

    @pl.when(c == pl.num_programs(1) - 1)
    def _():
        sout_ref[0] = s_ref[...]


def _retention_prompt(proj, gn_w, batch, seq):
    chunk = math.gcd(seq, RET_CHUNK)
    tile = math.gcd(seq, RET_CHUNKS_PER_STEP * chunk)
    n_tiles = seq // tile
    intra, q_dec, k_dec, chunk_dec = _ret_tables(chunk)
    intra = intra * (RET_D ** -0.5)
    k_dec = k_dec * (RET_D ** -0.5)
    cos, sin = _rope_tables(jnp.arange(seq, dtype=F32))
    col = lambda j: pl.BlockSpec((tile, D_MODEL), lambda b, c, j=j: (b * n_tiles + c, j))
    const2 = lambda shape: pl.BlockSpec(shape, lambda b, c: (0, 0))
    state = pl.BlockSpec((1, RET_HEADS, RET_D, RET_D), lambda b, c: (b, 0, 0, 0))
    return pl.pallas_call(
        _ret_body,
        grid=(batch, n_tiles),
        in_specs=[col(COL_Q // D_MODEL), col(COL_K // D_MODEL), col(COL_V // D_MODEL),
                  col(COL_GSW // D_MODEL),
                  pl.BlockSpec((tile, RET_D), lambda b, c: (c, 0)),
                  pl.BlockSpec((tile, RET_D), lambda b, c: (c, 0)),
                  const2((chunk, D_MODEL)), const2((chunk, D_MODEL)), const2((1, D_MODEL)),
                  pl.BlockSpec((RET_HEADS, chunk, chunk), lambda b, c: (0, 0, 0)),
                  const2((1, D_MODEL))],
        out_specs=[pl.BlockSpec((tile, D_MODEL), lambda b, c: (b * n_tiles + c, 0)), state],
        out_shape=[jax.ShapeDtypeStruct((batch * seq, D_MODEL), F32),
                   jax.ShapeDtypeStruct((batch, RET_HEADS, RET_D, RET_D), F32)],
        scratch_shapes=[pltpu.VMEM((RET_HEADS, RET_D, RET_D), F32)],
        compiler_params=_params(("parallel", "arbitrary")),
        name="retention_prompt",
    )(proj, proj, proj, proj, cos, sin, q_dec, k_dec, chunk_dec, intra, gn_w)


def _ret_step_body(q_ref, k_ref, v_ref, g_ref, cos_ref, sin_ref, qd_ref, kd_ref, cd_ref,
                   gnw_ref, s_ref, o_ref, sout_ref, *, rows):
    cos = cos_ref[...]
    sin = sin_ref[...]
    row = lax.broadcasted_iota(jnp.int32, (rows, RET_D), 0)
    for h in range(RET_HEADS):
        sl = slice(h * RET_D, (h + 1) * RET_D)
        qr = _rope(q_ref[:, sl], cos, sin)
        kr = _rope(k_ref[:, sl], cos, sin) * (RET_D ** -0.5)
        v = v_ref[:, sl]
        kd = kr * kd_ref[:, sl]
        cd = cd_ref[:, sl]

        reads = [_mm(qr, s_ref[b, h]) for b in range(rows)]
        adds = [_mm(jnp.where(row == b, kd, 0.0), v, _TN) for b in range(rows)]
        cross = jnp.zeros((rows, RET_D), F32)
        for b in range(rows):
            sout_ref[b, h] = s_ref[b, h] * cd + adds[b]
            cross = jnp.where(row == b, reads[b], cross)
        out = jnp.sum(qr * kr, axis=-1, keepdims=True) * v + cross * qd_ref[:, sl]
        o_ref[:, sl] = _head_norm(out, RET_GN_EPS) * gnw_ref[:, sl] * _silu(g_ref[:, sl])


def _retention_step(proj, s0, gn_w, pos):
    batch = s0.shape[0]
    rows = STEP_ROWS
    _, q_dec, k_dec, chunk_dec = _ret_tables(1)
    cos, sin = _rope_tables(pos)
    col = lambda j: pl.BlockSpec((rows, D_MODEL), lambda i, j=j: (i, j))
    const2 = lambda shape: pl.BlockSpec(shape, lambda i: (0, 0))
    state = pl.BlockSpec((rows, RET_HEADS, RET_D, RET_D), lambda i: (i, 0, 0, 0))
    return pl.pallas_call(
        functools.partial(_ret_step_body, rows=rows),
        grid=(batch // rows,),
        in_specs=[col(COL_Q // D_MODEL), col(COL_K // D_MODEL), col(COL_V // D_MODEL),
                  col(COL_GSW // D_MODEL),
                  const2((1, RET_D)), const2((1, RET_D)),
                  const2((1, D_MODEL)), const2((1, D_MODEL)), const2((1, D_MODEL)),
                  const2((1, D_MODEL)), state],
        out_specs=[pl.BlockSpec((rows, D_MODEL), lambda i: (i, 0)), state],
        out_shape=[jax.ShapeDtypeStruct((batch, D_MODEL), F32),
                   jax.ShapeDtypeStruct(s0.shape, F32)],
        compiler_params=_params(("parallel",)),
        name="retention_step",
    )(proj, proj, proj, proj, cos, sin, q_dec, k_dec, chunk_dec, gn_w, s0)


def _head_ones(value=1.0):
    row = lax.broadcasted_iota(jnp.int32, (LANES, LANES), 0)
    col = lax.broadcasted_iota(jnp.int32, (LANES, LANES), 1)
    return jnp.where((row // RWKV_HEAD) == (col // RWKV_HEAD), value, 0.0).astype(BF16)


def _head_sum(x, ones_bd):
    blocks = [_lane_sum(x[:, j:j + LANES], ones_bd) for j in range(0, x.shape[1], LANES)]
    return blocks[0] if len(blocks) == 1 else jnp.concatenate(blocks, axis=1)


def _softplus(x):
    return jnp.maximum(x, 0.0) + jnp.log(1.0 + jnp.exp(-jnp.abs(x)))


def _lerp(cur, prev, mu):
    return cur + (prev - cur) * mu


def _rwkv_prep(r_, k_, v_, lora, w0, w2, a0, a2, g2, k_k, k_a, r_k, ones_bd):
    w_lo, a_lo, g_lo = lora[:, :LANES], lora[:, LANES:2 * LANES], lora[:, 2 * LANES:LORA_W]
    w_log = -_softplus(-(w0 + _mm(jnp.tanh(w_lo), w2))) - 0.5
    log_decay = -jnp.exp(w_log)
    a_sig = jax.nn.sigmoid(a0 + _mm(a_lo, a2))
    g = _mm(jax.nn.sigmoid(g_lo), g2)
    kk = k_ * k_k
    kk = kk * lax.rsqrt(jnp.maximum(_head_sum(kk * kk, ones_bd), 1e-24))
    kmod = k_ * (1.0 + (a_sig - 1.0) * k_a)
    bonus = _head_sum(r_ * kmod * r_k, ones_bd) * v_
    return log_decay, a_sig, g, kk, kmod, bonus


def _rwkv_out(y, bonus, g, lnx_w, lnx_b):
    mean_bd = _head_ones(1.0 / RWKV_HEAD)
    yc = y - _head_sum(y, mean_bd)
    var = _head_sum(yc * yc, mean_bd)
    return (yc * lax.rsqrt(var + RWKV_GN_EPS) * lnx_w + lnx_b + bonus) * g


def _block_diag(x, lane_first):
    return jnp.concatenate([jnp.where(lane_first, x, 0.0), jnp.where(lane_first, 0.0, x)], axis=0)


def _pair_masks():
    tok = lax.broadcasted_iota(jnp.int32, (RWKV_CHUNK, LANES), 0)
    src = lax.broadcasted_iota(jnp.int32, (RWKV_CHUNK, LANES), 1) % RWKV_CHUNK
    n = 2 * RWKV_CHUNK
    eye = lax.broadcasted_iota(jnp.int32, (n, n), 0) == lax.broadcasted_iota(jnp.int32, (n, n), 1)
    lane_first = lax.broadcasted_iota(jnp.int32, (1, LANES), 1) < RWKV_HEAD
    return lane_first, src < tok, src <= tok, src == tok, eye


def _rwkv_chunks(chunks, masks):
    lane_first, strict, incl, eye_wide, eye = masks
    n = RWKV_CHUNK
    rows = lambda x, y: jnp.concatenate([x, y], axis=0)
    cols = lambda x, y: jnp.concatenate([x, y], axis=1)
    bd = lambda x: _block_diag(x, lane_first)

    ops = []
    for r, lw, k, v, a, b, cum in chunks:
        p_inc = jnp.exp(cum)
        p_inv = jnp.exp(-cum)
        p_exc = jnp.exp(cum - lw)
        p_end = p_inc[RWKV_CHUNK - 1:RWKV_CHUNK, :]
        b_t = b * p_inv
        k_t = k * p_inv
        ops.append(dict(a_t=(a * p_exc).astype(BF16), r_t=(r * p_inc).astype(BF16),
                        b_t=bd(b_t.astype(BF16)), k_t=bd(k_t.astype(BF16)),
                        b_hat=bd(b_t * p_end), k_hat=bd(k_t * p_end),
                        v_bd=bd(v.astype(BF16)), p_end=p_end))

    for o in ops:
        prod = _mm(rows(o["a_t"], o["r_t"]), rows(o["b_t"], o["k_t"]), _NT)
        o["a_ab"] = jnp.where(strict, prod[:n, :LANES], 0.0)
        o["a_ak"] = jnp.where(strict, prod[:n, LANES:], 0.0).astype(BF16)
        o["r_bk"] = cols(jnp.where(incl, prod[n:, :LANES], 0.0),
                         jnp.where(incl, prod[n:, LANES:], 0.0)).astype(BF16)

    for o in ops:
        a_ab = o["a_ab"].astype(BF16)
        o["t_inv"] = jnp.where(eye_wide, 1.0, 0.0) + o["a_ab"]
        o["power"] = _mm(a_ab, bd(a_ab)).astype(BF16)
        o["av"] = _mm(o["a_ak"], o["v_bd"]).astype(BF16)
    for _ in range(int(math.log2(RWKV_CHUNK)) - 2):
        for o in ops:
            both = _mm(rows(o["power"], o["t_inv"].astype(BF16)), bd(o["power"]))
            o["power"] = both[:n].astype(BF16)
            o["t_inv"] = o["t_inv"] + both[n:]
    for o in ops:
        o["t_inv"] = o["t_inv"] + _mm(o["t_inv"], bd(o["power"]))
    out = []
    for o in ops:
        wu = _mm(o["t_inv"], cols(bd(o["a_t"]), bd(o["av"])))
        hat_t = rows(o["b_hat"], o["k_hat"]).T
        p_col = jnp.sum(jnp.where(eye, o["p_end"], 0.0), axis=1, keepdims=True)
        out.append((rows(wu[:, :LANES].astype(BF16), o["r_t"]), wu[:, LANES:],
                    o["r_bk"], o["v_bd"], hat_t.astype(BF16), p_col))
    return out


def _shift_rows(cur, first_row):
    rolled = pltpu.roll(cur, 1, 0)
    row = lax.broadcasted_iota(jnp.int32, cur.shape, 0)
    return jnp.where(row == 0, first_row, rolled)


def _rwkv_body(r_ref, k_ref, v_ref, lora_ref, mu_ref, mul_ref,
               w0_ref, w2_ref, a0_ref, a2_ref, g2_ref, kk_ref, ka_ref, rk_ref,
               lnw_ref, lnb_ref, o_ref, sout_ref,
               s_ref, carry_ref, carryl_ref, *, tile):
    t = pl.program_id(1)
    p = pl.program_id(2)
    masks = _pair_masks()
    ones_bd = _head_ones()
    pairs = r_ref.shape[1] // LANES

    @pl.when(t == 0)
    def _():
        for q in range(pairs):
            s_ref[p * pairs + q] = jnp.zeros((LANES, LANES), F32)
        carry_ref[p] = jnp.zeros(carry_ref.shape[1:], F32)

        @pl.when(p == 0)
        def _():
            carryl_ref[...] = jnp.zeros_like(carryl_ref)

    prev = carry_ref[p]
    cur_r, cur_k, cur_v, cur_l = r_ref[...], k_ref[...], v_ref[...], lora_ref[...]
    mu = mu_ref[0]
    r_ = _lerp(cur_r, _shift_rows(cur_r, prev[0:1]), mu[0:1])
    k_ = _lerp(cur_k, _shift_rows(cur_k, prev[1:2]), mu[1:2])
    v_ = _lerp(cur_v, _shift_rows(cur_v, prev[2:3]), mu[2:3])
    lo = _lerp(cur_l, _shift_rows(cur_l, carryl_ref[0:1, :]), mul_ref[...])
    carry_ref[p, 0:1, :] = cur_r[tile - 1:tile]
    carry_ref[p, 1:2, :] = cur_k[tile - 1:tile]
    carry_ref[p, 2:3, :] = cur_v[tile - 1:tile]

    @pl.when(p == pl.num_programs(2) - 1)
    def _():
        carryl_ref[0:1, :] = cur_l[tile - 1:tile]

    log_decay, a_sig, g, kk, kmod, bonus = _rwkv_prep(
        r_, k_, v_, lo, w0_ref[...], w2_ref[...], a0_ref[...], a2_ref[...], g2_ref[...],
        kk_ref[...], ka_ref[...], rk_ref[...], ones_bd)

    row = lax.broadcasted_iota(jnp.int32, (tile, tile), 0)
    col = lax.broadcasted_iota(jnp.int32, (tile, tile), 1)
    tri = jnp.where(((row // RWKV_CHUNK) == (col // RWKV_CHUNK)) & (col <= row), 1.0, 0.0).astype(BF16)
    cum = sum(jnp.dot(tri, part, preferred_element_type=F32) for part in _split_bf16(log_decay, 3))

    neg_kk = -kk
    kk_a = kk * a_sig
    n_chunks = tile // RWKV_CHUNK
    chunks = []
    for c in range(n_chunks):
        for q in range(pairs):
            at = (slice(c * RWKV_CHUNK, (c + 1) * RWKV_CHUNK), slice(q * LANES, (q + 1) * LANES))
            chunks.append((r_[at], log_decay[at], kmod[at], v_[at], neg_kk[at], kk_a[at], cum[at]))
    parts = _rwkv_chunks(chunks, masks)

    n = RWKV_CHUNK
    lane_first = masks[0]
    states = [s_ref[p * pairs + q] for q in range(pairs)]
    ys = [[] for _ in range(pairs)]
    for c in range(n_chunks):
        reads = [_mm(parts[c * pairs + q][0], states[q]) for q in range(pairs)]
        for q in range(pairs):
            _, u0, r_bk, v_bd, hat_t, p_col = parts[c * pairs + q]
            u_bd = _block_diag(reads[q][:n] + u0, lane_first).astype(BF16)
            uv = jnp.concatenate([u_bd, v_bd], axis=0)
            ys[q].append(reads[q][n:] + _mm(r_bk, uv))
            states[q] = states[q] * p_col + _mm(hat_t, uv)
    for q in range(pairs):
        s_ref[p * pairs + q] = states[q]

    @pl.when(t == pl.num_programs(1) - 1)
    def _():
        for q in range(pairs):
            head = 2 * (p * pairs + q)
            s_bd = states[q].T
            sout_ref[0, head] = s_bd[:RWKV_HEAD, :RWKV_HEAD]
            sout_ref[0, head + 1] = s_bd[RWKV_HEAD:, RWKV_HEAD:]

    y = jnp.concatenate([jnp.concatenate(yq, axis=0) for yq in ys], axis=1)
    o_ref[...] = _rwkv_out(y, bonus, g, lnw_ref[...], lnb_ref[...])


def _rwkv_prompt(proj, wts, batch, seq):
    tile = math.gcd(seq, RWKV_TILE)
    n_tiles = seq // tile
    width = RWKV_PAIRS_PER_STEP * LANES
    blk = lambda base: pl.BlockSpec((tile, width),
                                    lambda b, t, p, base=base: (b * n_tiles + t, base // width + p))
    per_pair = lambda rows: pl.BlockSpec((rows, width), lambda b, t, p: (0, p))
    in_specs = [
        blk(COL_R), blk(COL_KK), blk(COL_VV),
        pl.BlockSpec((tile, LORA_W), lambda b, t, p: (b * n_tiles + t, COL_LORA // LORA_W)),
        pl.BlockSpec((1, 3, width), lambda b, t, p: (0, 0, p)),
        pl.BlockSpec((1, LORA_W), lambda b, t, p: (0, 0)),
        per_pair(1), per_pair(LANES),
        per_pair(1), per_pair(LANES),
        per_pair(2 * LANES),
        per_pair(1), per_pair(1), per_pair(1),
        per_pair(1), per_pair(1),
    ]
    return pl.pallas_call(
        functools.partial(_rwkv_body, tile=tile),
        grid=(batch, n_tiles, RWKV_PAIRS // RWKV_PAIRS_PER_STEP),
        in_specs=in_specs,
        out_specs=[pl.BlockSpec((tile, width), lambda b, t, p: (b * n_tiles + t, p)),
                   pl.BlockSpec((1, RWKV_HEADS, RWKV_HEAD, RWKV_HEAD), lambda b, t, p: (b, 0, 0, 0))],
        out_shape=[jax.ShapeDtypeStruct((batch * seq, D_MODEL), F32),
                   jax.ShapeDtypeStruct((batch, RWKV_HEADS, RWKV_HEAD, RWKV_HEAD), F32)],
        scratch_shapes=[pltpu.VMEM((RWKV_PAIRS, LANES, LANES), F32),
                        pltpu.VMEM((RWKV_PAIRS // RWKV_PAIRS_PER_STEP, SUBLANES, width), F32),
                        pltpu.VMEM((SUBLANES, LORA_W), F32)],
        compiler_params=_params(("parallel", "arbitrary", "arbitrary")),
        name="rwkv_prompt",
    )(proj, proj, proj, proj, wts["mu3"], wts["mu_lora"],
      wts["w0"], wts["w2"], wts["a0"], wts["a2"], wts["g2"],
      wts["k_k"], wts["k_a"], wts["r_k"], wts["lnx_w"], wts["lnx_b"])


def _rwkv_step_body(r_ref, k_ref, v_ref, lora_ref, pr_ref, pk_ref, pv_ref, plora_ref,
                    mu_ref, mul_ref, w0_ref, w2_ref, a0_ref, a2_ref, g2_ref,
                    kk_ref, ka_ref, rk_ref, lnw_ref, lnb_ref, s_ref,
                    o_ref, sout_ref, *, rows):
    ones_bd = _head_ones()
    mu = mu_ref[0]
    r_ = _lerp(r_ref[...], pr_ref[...], mu[0:1])
    k_ = _lerp(k_ref[...], pk_ref[...], mu[1:2])
    v_ = _lerp(v_ref[...], pv_ref[...], mu[2:3])
    lo = _lerp(lora_ref[...], plora_ref[...], mul_ref[...])
    log_decay, a_sig, g, kk, kmod, bonus = _rwkv_prep(
        r_, k_, v_, lo, w0_ref[...], w2_ref[...], a0_ref[...], a2_ref[...], g2_ref[...],
        kk_ref[...], ka_ref[...], rk_ref[...], ones_bd)
    decay = jnp.exp(log_decay)
    neg_kk = -kk
    kk_a = kk * a_sig
    decay_r = decay * r_
    row = lax.broadcasted_iota(jnp.int32, (rows, RWKV_HEAD), 0)

    heads = [slice(h * RWKV_HEAD, (h + 1) * RWKV_HEAD) for h in range(RWKV_HEADS)]
    reads = []
    for h, sl in enumerate(heads):
        lhs = jnp.concatenate([neg_kk[:, sl], decay_r[:, sl]], axis=0)
        reads.append([_mm(lhs, s_ref[b, h], _NT) for b in range(rows)])
    ys = []
    for h, sl in enumerate(heads):
        w_h, r_h, v_h, k_h, b_h = decay[:, sl], r_[:, sl], v_[:, sl], kmod[:, sl], kk_a[:, sl]
        rhs = jnp.concatenate([b_h, k_h], axis=0)
        sa_all = swr_all = jnp.zeros((rows, RWKV_HEAD), F32)
        adds = []
        for b in range(rows):
            mine = row == b
            sa, swr = reads[h][b][:rows], reads[h][b][rows:]
            left = jnp.concatenate([jnp.where(mine, sa, 0.0), jnp.where(mine, v_h, 0.0)], axis=0)
            adds.append(_mm(left, rhs, _TN))
            sa_all = jnp.where(mine, sa, sa_all)
            swr_all = jnp.where(mine, swr, swr_all)
        for b in range(rows):
            sout_ref[b, h] = s_ref[b, h] * w_h[b:b + 1] + adds[b]
        ys.append(swr_all + sa_all * jnp.sum(b_h * r_h, axis=-1, keepdims=True)
                  + v_h * jnp.sum(k_h * r_h, axis=-1, keepdims=True))
    y = jnp.concatenate(ys, axis=1)
    o_ref[...] = _rwkv_out(y, bonus, g, lnw_ref[...], lnb_ref[...])


def _rwkv_step(proj, s0, wts):
    batch = s0.shape[0]
    rows = STEP_ROWS
    shift = batch // rows
    wide = lambda base, off=0: pl.BlockSpec((rows, D_MODEL),
                                            lambda i, base=base, off=off: (i + off, base // D_MODEL))
    lora = pl.BlockSpec((rows, LORA_W), lambda i: (i, COL_LORA // LORA_W))
    lora_prev = pl.BlockSpec((rows, LORA_W), lambda i: (i + shift, COL_LORA // LORA_W))
    full = lambda shape: pl.BlockSpec(shape, lambda i: (0,) * len(shape))
    state = pl.BlockSpec((rows, RWKV_HEADS, RWKV_HEAD, RWKV_HEAD), lambda i: (i, 0, 0, 0))
    vec = full((1, D_MODEL))
    return pl.pallas_call(
        functools.partial(_rwkv_step_body, rows=rows),
        grid=(batch // rows,),
        in_specs=[wide(COL_R), wide(COL_KK), wide(COL_VV), lora,
                  wide(COL_R, shift), wide(COL_KK, shift), wide(COL_VV, shift), lora_prev,
                  full((1, 3, D_MODEL)), full((1, LORA_W)),
                  vec, full((LANES, D_MODEL)), vec, full((LANES, D_MODEL)),
                  full((2 * LANES, D_MODEL)), vec, vec, vec, vec, vec, state],
        out_specs=[pl.BlockSpec((rows, D_MODEL), lambda i: (i, 0)), state],
        out_shape=[jax.ShapeDtypeStruct((batch, D_MODEL), F32),
                   jax.ShapeDtypeStruct(s0.shape, F32)],
        compiler_params=_params(("parallel",)),
        name="rwkv_step",
    )(proj, proj, proj, proj, proj, proj, proj, proj,
      wts["mu3"], wts["mu_lora"], wts["w0"], wts["w2"], wts["a0"], wts["a2"], wts["g2"],
      wts["k_k"], wts["k_a"], wts["r_k"], wts["lnx_w"], wts["lnx_b"], s0)


N_RET_STEP_INPUTS = 11


def _sample_steps_body(*refs, rows):
    n_in = len(refs) - 4
    ret_in, rwkv_in = refs[:N_RET_STEP_INPUTS], refs[N_RET_STEP_INPUTS:n_in]
    _ret_step_body(*ret_in, *refs[n_in:n_in + 2], rows=rows)
    _rwkv_step_body(*rwkv_in, *refs[n_in + 2:], rows=rows)


def _sample_steps(proj, s_ret, s_rwkv, gn_w, pos, wts):
    batch = s_ret.shape[0]
    rows = STEP_ROWS
    shift = batch // rows
    _, q_dec, k_dec, chunk_dec = _ret_tables(1)
    cos, sin = _rope_tables(pos)
    wide = lambda base, off=0: pl.BlockSpec((rows, D_MODEL),
                                            lambda i, base=base, off=off: (i + off, base // D_MODEL))
    lora = lambda off: pl.BlockSpec((rows, LORA_W), lambda i, off=off: (i + off, COL_LORA // LORA_W))
    full = lambda shape: pl.BlockSpec(shape, lambda i: (0,) * len(shape))
    vec = full((1, D_MODEL))
    ret_state = pl.BlockSpec((rows, RET_HEADS, RET_D, RET_D), lambda i: (i, 0, 0, 0))
    rwkv_state = pl.BlockSpec((rows, RWKV_HEADS, RWKV_HEAD, RWKV_HEAD), lambda i: (i, 0, 0, 0))
    out_rows = pl.BlockSpec((rows, D_MODEL), lambda i: (i, 0))
    ret_specs = [wide(COL_Q), wide(COL_K), wide(COL_V), wide(COL_GSW),
                 full((1, RET_D)), full((1, RET_D)), vec, vec, vec, vec, ret_state]
    assert len(ret_specs) == N_RET_STEP_INPUTS
    rwkv_specs = [wide(COL_R), wide(COL_KK), wide(COL_VV), lora(0),
                  wide(COL_R, shift), wide(COL_KK, shift), wide(COL_VV, shift), lora(shift),
                  full((1, 3, D_MODEL)), full((1, LORA_W)),
                  vec, full((LANES, D_MODEL)), vec, full((LANES, D_MODEL)),
                  full((2 * LANES, D_MODEL)), vec, vec, vec, vec, vec, rwkv_state]
    return pl.pallas_call(
        functools.partial(_sample_steps_body, rows=rows),
        grid=(batch // rows,),
        in_specs=ret_specs + rwkv_specs,
        out_specs=[out_rows, ret_state, out_rows, rwkv_state],
        out_shape=[jax.ShapeDtypeStruct((batch, D_MODEL), F32), jax.ShapeDtypeStruct(s_ret.shape, F32),
                   jax.ShapeDtypeStruct((batch, D_MODEL), F32), jax.ShapeDtypeStruct(s_rwkv.shape, F32)],
        compiler_params=_params(("parallel",)),
        name="sample_steps",
    )(proj, proj, proj, proj, cos, sin, q_dec, k_dec, chunk_dec, gn_w, s_ret,
      proj, proj, proj, proj, proj, proj, proj, proj,
      wts["mu3"], wts["mu_lora"], wts["w0"], wts["w2"], wts["a0"], wts["a2"], wts["g2"],
      wts["k_k"], wts["k_a"], wts["r_k"], wts["lnx_w"], wts["lnx_b"], s_rwkv)


def _merge_body(x_ref, oret_ref, orwkv_ref, ga_ref, gb_ref, wout_ref, nffn_ref, rw_ref, rb_ref,
                x1_ref, comb_ref, cnt_ref):
    step = math.gcd(x_ref.shape[0], MERGE_ROWS)
    chunks = [slice(r0, r0 + step) for r0 in range(0, x_ref.shape[0], step)]
    each = lambda fn, *cols: [fn(*args) for args in zip(*cols)]
    row_max = lambda v: jnp.max(v, axis=-1, keepdims=True)
    row_min = lambda v: jnp.min(v, axis=-1, keepdims=True)
    row_sum = lambda v: jnp.sum(v, axis=-1, keepdims=True)
    lane = lax.broadcasted_iota(jnp.int32, (step, LANES), 1)
    is_group = lane < N_GROUPS
    neg_inf = -jnp.inf

    merged = [jax.nn.sigmoid(ga_ref[r, :]) * oret_ref[r, :]
              + jax.nn.sigmoid(gb_ref[r, :]) * orwkv_ref[r, :] for r in chunks]
    x1 = [x_ref[r, :] + _mm(m, wout_ref[...]) for r, m in zip(chunks, merged)]
    xn = each(lambda v: _rms(v, nffn_ref[...]), x1)
    for r, a in zip(chunks, x1):
        x1_ref[r, :] = a

    logits = each(lambda v: _mm(v, rw_ref[...]) + rb_ref[...], xn)
    g_max = each(lambda l: row_max(jnp.where(is_group, l, neg_inf)), logits)
    g_sel = each(lambda l, m: row_min(jnp.where(is_group & (l == m), lane, LANES)), logits, g_max)
    g_w = each(lambda l, m: 1.0 / row_sum(jnp.where(is_group, jnp.exp(l - m), 0.0)), logits, g_max)
    first = each(lambda g: N_GROUPS + EXPERTS_PER_GROUP * g, g_sel)
    in_group = each(lambda f: (lane >= f) & (lane < f + EXPERTS_PER_GROUP), first)
    e_max = each(lambda l, ing: row_max(jnp.where(ing, l, neg_inf)), logits, in_group)
    e_exp = each(lambda l, ing, m: jnp.where(ing, jnp.exp(l - m), 0.0), logits, in_group, e_max)
    prob = each(lambda e, ing: jnp.where(ing, e / row_sum(e), -1.0), e_exp, in_group)
    p1 = each(row_max, prob)
    i1 = each(lambda p, m: row_min(jnp.where(p == m, lane, LANES)), prob, p1)
    rest = each(lambda p, i: jnp.where(lane == i, -1.0, p), prob, i1)
    p2 = each(row_max, rest)
    i2 = each(lambda p, m: row_min(jnp.where(p == m, lane, LANES)), rest, p2)
    lane8 = lax.broadcasted_iota(jnp.int32, (SUBLANES, LANES), 1)
    sizes = jnp.zeros((SUBLANES, LANES), jnp.int32)
    for c, r in enumerate(chunks):
        denom = p1[c] + p2[c]
        comb = (jnp.where(lane == i1[c] - N_GROUPS, g_w[c] * p1[c] / denom, 0.0)
                + jnp.where(lane == i2[c] - N_GROUPS, g_w[c] * p2[c] / denom, 0.0))
        comb_ref[r, :] = jnp.where(lane == N_EXPERTS, g_sel[c].astype(F32), comb)
        for g in range(N_GROUPS):
            n_g = jnp.sum(jnp.where(g_sel[c] == g, 1, 0), axis=0, keepdims=True)
            sizes = sizes + jnp.where(lane8 == g, n_g, 0)
    cnt_ref[0] = sizes


def _merge(x, o_ret, o_rwkv, proj, wts):
    t = x.shape[0]
    tm = math.gcd(t, MERGE_TILE)
    row = lambda: pl.BlockSpec((tm, D_MODEL), lambda i: (i, 0))
    full = lambda shape: pl.BlockSpec(shape, lambda i: (0, 0))
    return pl.pallas_call(
        _merge_body,
        grid=(t // tm,),
        in_specs=[row(), row(), row(),
                  pl.BlockSpec((tm, D_MODEL), lambda i: (i, COL_GATE_A // D_MODEL)),
                  pl.BlockSpec((tm, D_MODEL), lambda i: (i, COL_GATE_B // D_MODEL)),
                  full((D_MODEL, D_MODEL)), full((1, D_MODEL)),
                  full((D_MODEL, LANES)), full((1, LANES))],
        out_specs=[row(), pl.BlockSpec((tm, LANES), lambda i: (i, 0)),
                   pl.BlockSpec((1, SUBLANES, LANES), lambda i: (i, 0, 0))],
        out_shape=[jax.ShapeDtypeStruct((t, D_MODEL), F32),
                   jax.ShapeDtypeStruct((t, LANES), F32),
                   jax.ShapeDtypeStruct((t // tm, SUBLANES, LANES), jnp.int32)],
        compiler_params=_params(("parallel",)),
        name="merge_router",
    )(x, o_ret, o_rwkv, proj, proj, wts["w_out"], wts["norm_ffn"], wts["router_w"], wts["router_b"])


def _moe_body(cnt_ref, x1_ref, comb_ref, wg_ref, wu_ref, wd_ref, nffn_ref, nf_ref, o_ref,
              acc_ref, xs_ref, cs_ref, pt_ref, *, ratio):
    i = pl.program_id(0)
    s = pl.program_id(1)
    tm = x1_ref.shape[0]
    group = s // (EXPERTS_PER_GROUP // EXPERTS_PER_STEP)

    sizes = []
    for g in range(N_GROUPS):
        n = cnt_ref[i * ratio * N_GROUPS + g]
        for k in range(1, ratio):
            n = n + cnt_ref[(i * ratio + k) * N_GROUPS + g]
        sizes.append(n)
    starts = [jnp.int32(0)]
    for g in range(N_GROUPS - 1):
        starts.append(starts[-1] + sizes[g])

    @pl.when(s == 0)
    def _():
        comb = comb_ref[...]
        g_row = comb.T[N_EXPERTS:N_EXPERTS + 1, :]
        gid = lax.broadcasted_iota(jnp.int32, (SUBLANES, tm), 0)
        onehot = jnp.where(g_row.astype(jnp.int32) == gid, 1.0, 0.0)
        step = math.gcd(tm, MOE_PERM_ROWS)
        row = lax.broadcasted_iota(jnp.int32, (step, tm), 0)
        col = lax.broadcasted_iota(jnp.int32, (step, tm), 1)
        rank = jnp.zeros((SUBLANES, tm), F32)
        for r0 in range(0, tm, step):
            earlier = jnp.where(row + r0 < col, 1.0, 0.0).astype(BF16)
            rank = rank + jnp.dot(onehot[:, r0:r0 + step].astype(BF16), earlier,
                                  preferred_element_type=F32)
        start_col = jnp.zeros((SUBLANES, 1), F32)
        gid_col = lax.broadcasted_iota(jnp.int32, (SUBLANES, 1), 0)
        for g in range(1, N_GROUPS):
            start_col = jnp.where(gid_col == g, starts[g].astype(F32), start_col)
        pos_row = jnp.sum(onehot * (rank + start_col), axis=0, keepdims=True)
        pos_row_i = pos_row.astype(jnp.int32)
        xn = _rms(x1_ref[...], nffn_ref[...]).astype(BF16)
        comb_parts = _split_bf16(comb, 3)
        for r0 in range(0, tm, step):
            perm = jnp.where(row + r0 == pos_row_i, 1.0, 0.0).astype(BF16)
            xs_ref[r0:r0 + step, :] = jnp.dot(perm, xn, preferred_element_type=F32).astype(BF16)
            cs_ref[r0:r0 + step, :] = sum(jnp.dot(perm, part, preferred_element_type=F32)
                                          for part in comb_parts)
        pos_col_i = jnp.broadcast_to(pos_row, (LANES, tm)).T.astype(jnp.int32)
        lane = lax.broadcasted_iota(jnp.int32, (tm, LANES), 1)
        for j in range(tm // LANES):
            pt_ref[:, j * LANES:(j + 1) * LANES] = jnp.where(
                lane + j * LANES == pos_col_i, 1.0, 0.0).astype(BF16)
        acc_ref[...] = jnp.zeros_like(acc_ref)

    start, size = starts[N_GROUPS - 1], sizes[N_GROUPS - 1]
    for g in range(N_GROUPS - 2, -1, -1):
        start = jnp.where(group == g, starts[g], start)
        size = jnp.where(group == g, sizes[g], size)
    blk = math.gcd(tm, MOE_BLOCK)
    first = start // blk
    last = jnp.where(size > 0, (start + size + blk - 1) // blk, first)
    lane = lax.broadcasted_iota(jnp.int32, (blk, LANES), 1)

    def run_blocks(js):
        at = [pl.ds(pl.multiple_of(j * blk, blk), blk) for j in js]
        xb = [xs_ref[r, :] for r in at]
        cw = [cs_ref[r, :] for r in at]
        hs = [[] for _ in js]
        for k in range(EXPERTS_PER_STEP):
            for i in range(len(js)):
                weight = jnp.sum(jnp.where(lane == s * EXPERTS_PER_STEP + k, cw[i], 0.0),
                                 axis=-1, keepdims=True)
                hs[i].append(_silu(_mm(xb[i], wg_ref[k])) * _mm(xb[i], wu_ref[k]) * weight)
        for i, r in enumerate(at):
            total = _mm(hs[i][0], wd_ref[0])
            for k in range(1, EXPERTS_PER_STEP):
                total = total + _mm(hs[i][k], wd_ref[k])
            acc_ref[r, :] += total

    pairs = (last - first) // 2

    def two_blocks(i, carry):
        run_blocks([first + 2 * i, first + 2 * i + 1])
        return carry

    lax.fori_loop(0, pairs, two_blocks, 0)

    @pl.when(first + 2 * pairs < last)
    def _():
        run_blocks([last - 1])

    @pl.when(s == pl.num_programs(1) - 1)
    def _():
        parts = _split_bf16(acc_ref[...], 2)
        step = math.gcd(tm, MOE_PERM_ROWS)
        for r0 in range(0, tm, step):
            moe = sum(jnp.dot(pt_ref[r0:r0 + step, :], part, preferred_element_type=F32)
                      for part in parts)
            o_ref[r0:r0 + step, :] = _rms(x1_ref[r0:r0 + step, :] + moe, nf_ref[...])


def _moe(x1, comb, counts, wts):
    t = x1.shape[0]
    tm = math.gcd(t, MOE_TILE)
    assert tm % LANES == 0, "token count must be a multiple of the lane width"
    ratio = tm // math.gcd(t, MERGE_TILE)
    row = lambda w: pl.BlockSpec((tm, w), lambda i, s, cnt: (i, 0))
    expert = lambda a, b: pl.BlockSpec((EXPERTS_PER_STEP, a, b), lambda i, s, cnt: (s, 0, 0))
    vec = pl.BlockSpec((1, D_MODEL), lambda i, s, cnt: (0, 0))
    grid_spec = pltpu.PrefetchScalarGridSpec(
        num_scalar_prefetch=1,
        grid=(t // tm, N_EXPERTS // EXPERTS_PER_STEP),
        in_specs=[row(D_MODEL), row(LANES),
                  expert(D_MODEL, D_EXPERT), expert(D_MODEL, D_EXPERT), expert(D_EXPERT, D_MODEL),
                  vec, vec],
        out_specs=row(D_MODEL),
        scratch_shapes=[pltpu.VMEM((tm, D_MODEL), F32),
                        pltpu.VMEM((tm, D_MODEL), BF16),
                        pltpu.VMEM((tm, LANES), F32),
                        pltpu.VMEM((tm, tm), BF16)])
    return pl.pallas_call(
        functools.partial(_moe_body, ratio=ratio),
        grid_spec=grid_spec,
        out_shape=jax.ShapeDtypeStruct((t, D_MODEL), F32),
        compiler_params=_params(("parallel", "arbitrary")),
        name="moe",
    )(counts, x1, comb, wts["w_gate"], wts["w_up"], wts["w_down"], wts["norm_ffn"], wts["norm_final"])


def _pad_to(x, size, axis):
    pad = [(0, 0)] * x.ndim
    pad[axis] = (0, size - x.shape[axis])
    return jnp.pad(x, pad)


def _prepare_weights(norm_mix, w_in, mu_shift, ret_gn_w, rwkv_w0, rwkv_w2, rwkv_a0, rwkv_a2,
                     rwkv_g2, rwkv_k_k, rwkv_k_a, rwkv_r_k, rwkv_lnx_w, rwkv_lnx_b, w_out,
                     norm_ffn, router_group_w, router_group_b, router_expert_w, router_expert_b,
                     expert_w_gate, expert_w_up, expert_w_down, norm_final):
    def lora_cols(t):
        return jnp.concatenate([_pad_to(t[..., SRC_WLO:SRC_ALO], LANES, -1),
                                _pad_to(t[..., SRC_ALO:SRC_GLO], LANES, -1),
                                _pad_to(t[..., SRC_GLO:SRC_GATE_A], 2 * LANES, -1)], axis=-1)

    w = w_in[0]
    w_in_p = jnp.concatenate([w[:, :SRC_WLO], w[:, SRC_GATE_A:], lora_cols(w)], axis=1).astype(BF16)
    mu = jnp.concatenate([jnp.zeros((SRC_SHIFT,), F32), mu_shift[0]])
    row = lambda v: v.reshape(1, -1)
    router_w = _pad_to(jnp.concatenate([router_group_w[0], router_expert_w[0]], axis=1), LANES, 1)
    router_b = _pad_to(jnp.concatenate([router_group_b[0], router_expert_b[0]]), LANES, 0)
    return dict(
        w_in=w_in_p, norm_mix=row(norm_mix[0]), ret_gn_w=row(ret_gn_w[0]),
        mu3=mu[SRC_SHIFT:SRC_WLO].reshape(1, 3, D_MODEL), mu_lora=lora_cols(mu).reshape(1, LORA_W),
        w0=row(rwkv_w0[0]), w2=_pad_to(rwkv_w2[0], LANES, 0),
        a0=row(rwkv_a0[0]), a2=_pad_to(rwkv_a2[0], LANES, 0),
        g2=_pad_to(rwkv_g2[0], 2 * LANES, 0),
        k_k=row(rwkv_k_k[0]), k_a=row(rwkv_k_a[0]), r_k=row(rwkv_r_k[0]),
        lnx_w=row(rwkv_lnx_w[0]), lnx_b=row(rwkv_lnx_b[0]),
        w_out=w_out[0].astype(BF16), norm_ffn=row(norm_ffn[0]),
        router_w=router_w, router_b=row(router_b),
        w_gate=expert_w_gate[0].astype(BF16), w_up=expert_w_up[0].astype(BF16),
        w_down=expert_w_down[0].astype(BF16), norm_final=row(norm_final))


def _finish(x, o_ret, o_rwkv, proj, wts):
    x1, comb, sizes = _merge(x, o_ret, o_rwkv, proj, wts)
    counts = sizes[:, 0, :N_GROUPS].reshape(-1)
    return _moe(x1, comb, counts, wts)


def kernel(x_prompt, x_sample, state_ret, state_rwkv, state_shift, norm_mix, w_in, mu_shift, ret_gn_w, rwkv_w0, rwkv_w2, rwkv_a0, rwkv_a2, rwkv_g2, rwkv_k_k, rwkv_k_a, rwkv_r_k, rwkv_lnx_w, rwkv_lnx_b, w_out, norm_ffn, router_group_w, router_group_b, router_expert_w, router_expert_b, expert_w_gate, expert_w_up, expert_w_down, norm_final):
    assert norm_mix.shape[0] == 1, "single-layer step"
    wts = _prepare_weights(norm_mix, w_in, mu_shift, ret_gn_w, rwkv_w0, rwkv_w2, rwkv_a0, rwkv_a2,
                           rwkv_g2, rwkv_k_k, rwkv_k_a, rwkv_r_k, rwkv_lnx_w, rwkv_lnx_b, w_out,
                           norm_ffn, router_group_w, router_group_b, router_expert_w,
                           router_expert_b, expert_w_gate, expert_w_up, expert_w_down, norm_final)

    bp, lp, _ = x_prompt.shape
    xp = x_prompt.reshape(bp * lp, D_MODEL)
    tile_p = math.gcd(lp, INPROJ_TILE)
    proj_p, tail_p = _inproj(xp, wts["norm_mix"], wts["w_in"], tile_p, tile_p, SUBLANES)
    o_ret_p, ret_p = _retention_prompt(proj_p, wts["ret_gn_w"], bp, lp)
    o_rwkv_p, rwkv_p = _rwkv_prompt(proj_p, wts, bp, lp)
    y_prompt = _finish(xp, o_ret_p, o_rwkv_p, proj_p, wts).reshape(bp, lp, D_MODEL)
    shift_p = tail_p.reshape(bp, lp // tile_p, SUBLANES, D_MODEL)[:, -1, -1]

    bs, ls, _ = x_sample.shape
    assert ls == 1, "sample group advances one token"
    xs = x_sample.reshape(bs, D_MODEL)
    both = jnp.concatenate([xs, state_shift[0]], axis=0)
    proj_s, xn_s = _inproj(both, wts["norm_mix"], wts["w_in"], 2 * bs, bs, bs)
    pos_s = PAST_LEN + jnp.arange(ls, dtype=F32)
    o_ret_s, ret_s, o_rwkv_s, rwkv_s = _sample_steps(proj_s, state_ret[0], state_rwkv[0],
                                                     wts["ret_gn_w"], pos_s, wts)
    y_sample = _finish(xs, o_ret_s, o_rwkv_s, proj_s, wts).reshape(bs, ls, D_MODEL)

    return (y_prompt, y_sample, ret_p[None], rwkv_p[None], shift_p[None],
            ret_s[None], rwkv_s[None], xn_s)
```
